```python
import math
import jax, jax.numpy as jnp
from jax import lax
import numpy as np

D_MODEL = 2048
BATCH = 8
SEQ = 4096
DEPTH = 2

GRID_W = 64
CTX_LEN = 256
N_MIXERS = 2
MIXER_S5 = 0
MIXER_POOL = 1
S5_GROUP = 16
S5_GROUPS = D_MODEL // S5_GROUP
S5_STATE = 64
S5_DT_MIN = 0.001
S5_DT_MAX = 0.1
POOL_WINDOWS = (2, 4, 8, 16)
POOL_GROUPS = len(POOL_WINDOWS)
POOL_CH = D_MODEL // POOL_GROUPS
D_FF = ((8 * D_MODEL // 3 + 255) // 256) * 256
N_MOD = 6
RMS_EPS = 1e-6
POS_BASE = 10000.0
N_S5_LAYERS = (DEPTH + 1) // 2
N_POOL_LAYERS = DEPTH // 2

kernel_name = "hybrid_s5_pool_convffn_dit"


def _rmsnorm(x, g):
    xf = x.astype(jnp.float32)
    y = xf * lax.rsqrt(jnp.mean(xf * xf, axis=-1, keepdims=True) + RMS_EPS)
    return (y * g.astype(jnp.float32)).astype(x.dtype)


def _modulate(xn, shift, scale):
    return xn * (1 + scale) + shift


def _grid_pos_emb(n_tokens, dim):
    rows = n_tokens // GRID_W
    r, col = jnp.meshgrid(jnp.arange(rows, dtype=jnp.float32),
                          jnp.arange(GRID_W, dtype=jnp.float32), indexing="ij")
    quarter = dim // 4
    omega = 1.0 / (POS_BASE ** (jnp.arange(quarter, dtype=jnp.float32) / quarter))

    def enc(p):
        ang = p.reshape(-1, 1) * omega[None, :]
        return jnp.concatenate([jnp.sin(ang), jnp.cos(ang)], axis=-1)

    return jnp.concatenate([enc(r), enc(col)], axis=-1)


def _ssm_combine(e1, e2):
    a1r, a1i, b1r, b1i = e1
    a2r, a2i, b2r, b2i = e2
    return (a2r * a1r - a2i * a1i,
            a2r * a1i + a2i * a1r,
            a2r * b1r - a2i * b1i + b2r,
            a2r * b1i + a2i * b1r + b2i)


def _s5_discretize(lam_re, lam_im, log_step, b_re, b_im):
    f32 = jnp.float32
    lam_re, lam_im = lam_re.astype(f32), lam_im.astype(f32)
    b_re, b_im = b_re.astype(f32), b_im.astype(f32)
    dt = jnp.exp(log_step.astype(f32))[:, None]
    mag = jnp.exp(lam_re * dt)
    abar_re = mag * jnp.cos(lam_im * dt)
    abar_im = mag * jnp.sin(lam_im * dt)
    nr, ni = abar_re - 1.0, abar_im
    den = lam_re * lam_re + lam_im * lam_im
    fr = (nr * lam_re + ni * lam_im) / den
    fi = (ni * lam_re - nr * lam_im) / den
    bbar_re = fr[..., None] * b_re - fi[..., None] * b_im
    bbar_im = fr[..., None] * b_im + fi[..., None] * b_re
    return abar_re, abar_im, bbar_re, bbar_im


def _s5_states(u_g, abar_re, abar_im, bbar_re, bbar_im, reverse, h0=None):
    b_re = jnp.einsum("blgc,gpc->blgp", u_g, bbar_re)
    b_im = jnp.einsum("blgc,gpc->blgp", u_g, bbar_im)
    if h0 is not None:
        h0_re, h0_im = h0
        first = -1 if reverse else 0
        b_re = b_re.at[:, first].add(abar_re * h0_re - abar_im * h0_im)
        b_im = b_im.at[:, first].add(abar_re * h0_im + abar_im * h0_re)
    a_re = jnp.broadcast_to(abar_re, b_re.shape)
    a_im = jnp.broadcast_to(abar_im, b_im.shape)
    _, _, h_re, h_im = lax.associative_scan(_ssm_combine, (a_re, a_im, b_re, b_im),
                                            reverse=reverse, axis=1)
    return h_re, h_im


def _s5_readout(h_re, h_im, c_re, c_im):
    return (jnp.einsum("blgp,gcp->blgc", h_re, c_re.astype(jnp.float32))
            - jnp.einsum("blgp,gcp->blgc", h_im, c_im.astype(jnp.float32)))


def _s5_glu(y, glu_w, dtype):
    z = jax.nn.gelu(y).astype(dtype)
    val, gate = jnp.split(z @ glu_w, 2, axis=-1)
    return val * jax.nn.sigmoid(gate)


def _s5_mixer(u, u_c, lam_re, lam_im, log_step, b_re, b_im, c_re, c_im, d_skip, glu_w,
              with_ctx_out):
    bsz, n_lat, dim = u.shape
    n_ctx = u_c.shape[1]
    u_g = u.astype(jnp.float32).reshape(bsz, n_lat, S5_GROUPS, S5_GROUP)
    uc_g = u_c.astype(jnp.float32).reshape(bsz, n_ctx, S5_GROUPS, S5_GROUP)
    dsk = d_skip.astype(jnp.float32).reshape(S5_GROUPS, S5_GROUP)
    y = u_g * dsk
    y_c = uc_g * dsk if with_ctx_out else None
    for direction, rev in ((0, False), (1, True)):
        disc = _s5_discretize(lam_re[direction], lam_im[direction], log_step[direction],
                              b_re[direction], b_im[direction])
        hc_re, hc_im = _s5_states(uc_g, *disc, reverse=rev)
        end = 0 if rev else -1
        h_re, h_im = _s5_states(u_g, *disc, reverse=rev, h0=(hc_re[:, end], hc_im[:, end]))
        y = y + _s5_readout(h_re, h_im, c_re[direction], c_im[direction])
        if with_ctx_out:
            y_c = y_c + _s5_readout(hc_re, hc_im, c_re[direction], c_im[direction])
    out = _s5_glu(y.reshape(bsz, n_lat, dim), glu_w, u.dtype)
    out_c = _s5_glu(y_c.reshape(bsz, n_ctx, dim), glu_w, u.dtype) if with_ctx_out else None
    return out, out_c


def _pool_mixer(u, w, scale):
    bsz, n_tok, dim = u.shape
    uf = u.astype(jnp.float32)
    csum = jnp.concatenate([jnp.zeros((bsz, 1, dim), jnp.float32),
                            jnp.cumsum(uf, axis=1)], axis=1)
    t = jnp.arange(n_tok)
    groups = []
    for g, win in enumerate(POOL_WINDOWS):
        lo = jnp.clip(t - win // 2, 0, n_tok - 1)
        hi = jnp.clip(t + win // 2 - 1, 0, n_tok - 1)
        ch = slice(g * POOL_CH, (g + 1) * POOL_CH)
        cs = csum[:, :, ch]
        cnt = (hi - lo + 1).astype(jnp.float32)[None, :, None]
        groups.append((cs[:, hi + 1] - cs[:, lo]) / cnt - uf[:, :, ch])
    p = jnp.stack(groups, axis=2).astype(u.dtype)
    y = jnp.einsum("blgc,gcd->blgd", p, w).reshape(bsz, n_tok, dim)
    return y * scale


def _conv_ffn(u, up, conv, conv_b, down):
    h = u @ up
    hp = jnp.pad(h, ((0, 0), (1, 1), (0, 0)))
    h = hp[:, :-2] * conv[0] + hp[:, 1:-1] * conv[1] + hp[:, 2:] * conv[2] + conv_b
    val, gate = jnp.split(h, 2, axis=-1)
    return (jax.nn.silu(gate) * val) @ down


def _fwd_setup_inputs(seed: int = 0) -> dict:
    key = jax.random.key(seed)
    ks = jax.random.split(key, 24)
    f32 = jnp.float32
    nrm = lambda k, shp: jax.random.normal(k, shp, f32)
    G, P, C = S5_GROUPS, S5_STATE, S5_GROUP
    s5_lam_re = -0.5 + 0.01 * nrm(ks[5], (N_S5_LAYERS, 2, G, P))
    s5_lam_im = math.pi * jnp.arange(P, dtype=f32) + 0.01 * nrm(ks[6], (N_S5_LAYERS, 2, G, P))
    s5_log_step = jax.random.uniform(ks[7], (N_S5_LAYERS, 2, G), f32,
                                     math.log(S5_DT_MIN), math.log(S5_DT_MAX))
    ffn_conv = 0.3 * nrm(ks[18], (DEPTH, 3, 2 * D_FF))
    ffn_conv = ffn_conv.at[:, 1].add(1.0)
    return {
        "x": nrm(ks[0], (BATCH, SEQ, D_MODEL)),
        "c": nrm(ks[1], (BATCH, D_MODEL)),
        "ctx": nrm(ks[2], (BATCH, CTX_LEN, D_MODEL)),
        "c_ctx": nrm(ks[3], (D_MODEL,)),
        "ada_w": 0.5 * D_MODEL ** -0.5 * nrm(ks[4], (DEPTH, D_MODEL, N_MOD * D_MODEL)),
        "ada_b": 0.01 * nrm(ks[8], (DEPTH, N_MOD * D_MODEL)),
        "norm_g": 1.0 + 0.05 * nrm(ks[9], (DEPTH, 4, D_MODEL)),
        "s5_lam_re": s5_lam_re,
        "s5_lam_im": s5_lam_im,
        "s5_log_step": s5_log_step,
        "s5_b_re": (2 * C) ** -0.5 * nrm(ks[10], (N_S5_LAYERS, 2, G, P, C)),
        "s5_b_im": (2 * C) ** -0.5 * nrm(ks[11], (N_S5_LAYERS, 2, G, P, C)),
        "s5_c_re": P ** -0.5 * nrm(ks[12], (N_S5_LAYERS, 2, G, C, P)),
        "s5_c_im": P ** -0.5 * nrm(ks[13], (N_S5_LAYERS, 2, G, C, P)),
        "s5_d": nrm(ks[14], (N_S5_LAYERS, D_MODEL)),
        "s5_glu_w": D_MODEL ** -0.5 * nrm(ks[15], (N_S5_LAYERS, D_MODEL, 2 * D_MODEL)),
        "pool_w": POOL_CH ** -0.5 * nrm(ks[16], (N_POOL_LAYERS, POOL_GROUPS, POOL_CH, POOL_CH)),
        "pool_scale": 1.0 + 0.1 * nrm(ks[17], (N_POOL_LAYERS, D_MODEL)),
        "ffn_up": D_MODEL ** -0.5 * nrm(ks[19], (DEPTH, D_MODEL, 2 * D_FF)),
        "ffn_conv": ffn_conv,
        "ffn_conv_b": 0.01 * nrm(ks[20], (DEPTH, 2 * D_FF)),
        "ffn_down": D_FF ** -0.5 * nrm(ks[21], (DEPTH, D_FF, D_MODEL)),
    }


def _fwd_reference(x, c, ctx, c_ctx, ada_w, ada_b, norm_g, s5_lam_re, s5_lam_im, s5_log_step,
              s5_b_re, s5_b_im, s5_c_re, s5_c_im, s5_d, s5_glu_w, pool_w, pool_scale,
              ffn_up, ffn_conv, ffn_conv_b, ffn_down):
    n_lat = x.shape[1]
    x = x + _grid_pos_emb(n_lat, D_MODEL).astype(x.dtype)[None]
    cond = jax.nn.silu(c)
    cond_ctx = jax.nn.silu(c_ctx)
    h_ctx = ctx
    for i in range(DEPTH):
        kind = i % N_MIXERS
        k = i // N_MIXERS
        ctx_read = kind == MIXER_S5
        ctx_later = any(j % N_MIXERS == MIXER_S5 for j in range(i + 1, DEPTH))
        mod = jnp.split((cond @ ada_w[i] + ada_b[i])[:, None, :], N_MOD, axis=-1)
        u = _modulate(_rmsnorm(x, norm_g[i, 0]), mod[0], mod[1])
        if ctx_read or ctx_later:
            mod_c = jnp.split(cond_ctx @ ada_w[i] + ada_b[i], N_MOD)
            u_c = _modulate(_rmsnorm(h_ctx, norm_g[i, 0]), mod_c[0], mod_c[1])
        if kind == MIXER_S5:
            y, y_c = _s5_mixer(u, u_c, s5_lam_re[k], s5_lam_im[k], s5_log_step[k],
                               s5_b_re[k], s5_b_im[k], s5_c_re[k], s5_c_im[k],
                               s5_d[k], s5_glu_w[k], ctx_later)
        else:
            y = _pool_mixer(u, pool_w[k], pool_scale[k])
            y_c = _pool_mixer(u_c, pool_w[k], pool_scale[k]) if ctx_later else None
        x = x + mod[2] * _rmsnorm(y, norm_g[i, 1])
        f = _conv_ffn(_modulate(_rmsnorm(x, norm_g[i, 2]), mod[3], mod[4]),
                      ffn_up[i], ffn_conv[i], ffn_conv_b[i], ffn_down[i])
        x = x + mod[5] * _rmsnorm(f, norm_g[i, 3])
        if ctx_later:
            h_ctx = h_ctx + mod_c[2] * _rmsnorm(y_c, norm_g[i, 1])
            fc = _conv_ffn(_modulate(_rmsnorm(h_ctx, norm_g[i, 2]), mod_c[3], mod_c[4]),
                           ffn_up[i], ffn_conv[i], ffn_conv_b[i], ffn_down[i])
            h_ctx = h_ctx + mod_c[5] * _rmsnorm(fc, norm_g[i, 3])
    return x


import jax as _jax
import jax.numpy as _jnp

TWIN_FORMAT = 'train_step'
FWD_PARAMS = ['x', 'c', 'ctx', 'c_ctx', 'ada_w', 'ada_b', 'norm_g', 's5_lam_re', 's5_lam_im', 's5_log_step', 's5_b_re', 's5_b_im', 's5_c_re', 's5_c_im', 's5_d', 's5_glu_w', 'pool_w', 'pool_scale', 'ffn_up', 'ffn_conv', 'ffn_conv_b', 'ffn_down']
TWIN_WEIGHTS = ['c_ctx', 'ada_w', 'ada_b', 'norm_g', 's5_lam_re', 's5_lam_im', 's5_log_step', 's5_b_re', 's5_b_im', 's5_c_re', 's5_c_im', 's5_d', 's5_glu_w', 'pool_w', 'pool_scale', 'ffn_up', 'ffn_conv', 'ffn_conv_b', 'ffn_down']
TWIN_DIFF_INPUT = 'x'
TWIN_INPUTS = ['x', 'c', 'ctx', 'c_ctx', 'ada_w', 'ada_b', 'norm_g', 's5_lam_re', 's5_lam_im', 's5_log_step', 's5_b_re', 's5_b_im', 's5_c_re', 's5_c_im', 's5_d', 's5_glu_w', 'pool_w', 'pool_scale', 'ffn_up', 'ffn_conv', 'ffn_conv_b', 'ffn_down', 'loss_target', 'm_c_ctx', 'm_ada_w', 'm_ada_b', 'm_norm_g', 'm_s5_lam_re', 'm_s5_lam_im', 'm_s5_log_step', 'm_s5_b_re', 'm_s5_b_im', 'm_s5_c_re', 'm_s5_c_im', 'm_s5_d', 'm_s5_glu_w', 'm_pool_w', 'm_pool_scale', 'm_ffn_up', 'm_ffn_conv', 'm_ffn_conv_b', 'm_ffn_down', 'v_c_ctx', 'v_ada_w', 'v_ada_b', 'v_norm_g', 'v_s5_lam_re', 'v_s5_lam_im', 'v_s5_log_step', 'v_s5_b_re', 'v_s5_b_im', 'v_s5_c_re', 'v_s5_c_im', 'v_s5_d', 'v_s5_glu_w', 'v_pool_w', 'v_pool_scale', 'v_ffn_up', 'v_ffn_conv', 'v_ffn_conv_b', 'v_ffn_down']
TWIN_OUTPUTS = ['loss', 'grad_x', 'grad_c_ctx', 'grad_ada_w', 'grad_ada_b', 'grad_norm_g', 'grad_s5_lam_re', 'grad_s5_lam_im', 'grad_s5_log_step', 'grad_s5_b_re', 'grad_s5_b_im', 'grad_s5_c_re', 'grad_s5_c_im', 'grad_s5_d', 'grad_s5_glu_w', 'grad_pool_w', 'grad_pool_scale', 'grad_ffn_up', 'grad_ffn_conv', 'grad_ffn_conv_b', 'grad_ffn_down', 'delta_c_ctx', 'delta_ada_w', 'delta_ada_b', 'delta_norm_g', 'delta_s5_lam_re', 'delta_s5_lam_im', 'delta_s5_log_step', 'delta_s5_b_re', 'delta_s5_b_im', 'delta_s5_c_re', 'delta_s5_c_im', 'delta_s5_d', 'delta_s5_glu_w', 'delta_pool_w', 'delta_pool_scale', 'delta_ffn_up', 'delta_ffn_conv', 'delta_ffn_conv_b', 'delta_ffn_down', 'new_m_c_ctx', 'new_m_ada_w', 'new_m_ada_b', 'new_m_norm_g', 'new_m_s5_lam_re', 'new_m_s5_lam_im', 'new_m_s5_log_step', 'new_m_s5_b_re', 'new_m_s5_b_im', 'new_m_s5_c_re', 'new_m_s5_c_im', 'new_m_s5_d', 'new_m_s5_glu_w', 'new_m_pool_w', 'new_m_pool_scale', 'new_m_ffn_up', 'new_m_ffn_conv', 'new_m_ffn_conv_b', 'new_m_ffn_down', 'new_v_c_ctx', 'new_v_ada_w', 'new_v_ada_b', 'new_v_norm_g', 'new_v_s5_lam_re', 'new_v_s5_lam_im', 'new_v_s5_log_step', 'new_v_s5_b_re', 'new_v_s5_b_im', 'new_v_s5_c_re', 'new_v_s5_c_im', 'new_v_s5_d', 'new_v_s5_glu_w', 'new_v_pool_w', 'new_v_pool_scale', 'new_v_ffn_up', 'new_v_ffn_conv', 'new_v_ffn_conv_b', 'new_v_ffn_down']
TWIN_LEAF_KINDS = {'loss': 'loss', 'grad_x': 'grad_x', 'grad_c_ctx': 'grad_w', 'grad_ada_w': 'grad_w', 'grad_ada_b': 'grad_w', 'grad_norm_g': 'grad_w', 'grad_s5_lam_re': 'grad_w', 'grad_s5_lam_im': 'grad_w', 'grad_s5_log_step': 'grad_w', 'grad_s5_b_re': 'grad_w', 'grad_s5_b_im': 'grad_w', 'grad_s5_c_re': 'grad_w', 'grad_s5_c_im': 'grad_w', 'grad_s5_d': 'grad_w', 'grad_s5_glu_w': 'grad_w', 'grad_pool_w': 'grad_w', 'grad_pool_scale': 'grad_w', 'grad_ffn_up': 'grad_w', 'grad_ffn_conv': 'grad_w', 'grad_ffn_conv_b': 'grad_w', 'grad_ffn_down': 'grad_w', 'delta_c_ctx': 'delta_w', 'delta_ada_w': 'delta_w', 'delta_ada_b': 'delta_w', 'delta_norm_g': 'delta_w', 'delta_s5_lam_re': 'delta_w', 'delta_s5_lam_im': 'delta_w', 'delta_s5_log_step': 'delta_w', 'delta_s5_b_re': 'delta_w', 'delta_s5_b_im': 'delta_w', 'delta_s5_c_re': 'delta_w', 'delta_s5_c_im': 'delta_w', 'delta_s5_d': 'delta_w', 'delta_s5_glu_w': 'delta_w', 'delta_pool_w': 'delta_w', 'delta_pool_scale': 'delta_w', 'delta_ffn_up': 'delta_w', 'delta_ffn_conv': 'delta_w', 'delta_ffn_conv_b': 'delta_w', 'delta_ffn_down': 'delta_w', 'new_m_c_ctx': 'new_m', 'new_m_ada_w': 'new_m', 'new_m_ada_b': 'new_m', 'new_m_norm_g': 'new_m', 'new_m_s5_lam_re': 'new_m', 'new_m_s5_lam_im': 'new_m', 'new_m_s5_log_step': 'new_m', 'new_m_s5_b_re': 'new_m', 'new_m_s5_b_im': 'new_m', 'new_m_s5_c_re': 'new_m', 'new_m_s5_c_im': 'new_m', 'new_m_s5_d': 'new_m', 'new_m_s5_glu_w': 'new_m', 'new_m_pool_w': 'new_m', 'new_m_pool_scale': 'new_m', 'new_m_ffn_up': 'new_m', 'new_m_ffn_conv': 'new_m', 'new_m_ffn_conv_b': 'new_m', 'new_m_ffn_down': 'new_m', 'new_v_c_ctx': 'new_v', 'new_v_ada_w': 'new_v', 'new_v_ada_b': 'new_v', 'new_v_norm_g': 'new_v', 'new_v_s5_lam_re': 'new_v', 'new_v_s5_lam_im': 'new_v', 'new_v_s5_log_step': 'new_v', 'new_v_s5_b_re': 'new_v', 'new_v_s5_b_im': 'new_v', 'new_v_s5_c_re': 'new_v', 'new_v_s5_c_im': 'new_v', 'new_v_s5_d': 'new_v', 'new_v_s5_glu_w': 'new_v', 'new_v_pool_w': 'new_v', 'new_v_pool_scale': 'new_v', 'new_v_ffn_up': 'new_v', 'new_v_ffn_conv': 'new_v', 'new_v_ffn_conv_b': 'new_v', 'new_v_ffn_down': 'new_v'}


def _forward(args):
    return _fwd_reference(*[args[k] for k in FWD_PARAMS])


def _output_shape():
    def fwd():
        inp = _fwd_setup_inputs(0)
        return _fwd_reference(*[inp[k] for k in FWD_PARAMS])
    out = _jax.eval_shape(fwd)
    return out.shape, out.dtype

N_MICROBATCH = 1
ADAM_LR = 0.001
ADAM_B1 = 0.9
ADAM_B2 = 0.999
ADAM_EPS = 1e-08
ADAM_WD = 0.01
ADAM_STEP = 10
PER_EXAMPLE_BATCH_AXIS = {'x': 0, 'c': 0, 'ctx': 0, 'loss_target': 0}
SHARED_INPUTS = []
_WEIGHT_DTYPES = {'c_ctx': _jnp.float32, 'ada_w': _jnp.float32, 'ada_b': _jnp.float32, 'norm_g': _jnp.float32, 's5_lam_re': _jnp.float32, 's5_lam_im': _jnp.float32, 's5_log_step': _jnp.float32, 's5_b_re': _jnp.float32, 's5_b_im': _jnp.float32, 's5_c_re': _jnp.float32, 's5_c_im': _jnp.float32, 's5_d': _jnp.float32, 's5_glu_w': _jnp.float32, 'pool_w': _jnp.float32, 'pool_scale': _jnp.float32, 'ffn_up': _jnp.float32, 'ffn_conv': _jnp.float32, 'ffn_conv_b': _jnp.float32, 'ffn_down': _jnp.float32}
MOMENT_SCALE = {'c_ctx': 8.789694e-03, 'ada_w': 8.996148e-01, 'ada_b': 2.299912e+00, 'norm_g': 1.309093e+00, 's5_lam_re': 7.553779e-02, 's5_lam_im': 6.520767e-02, 's5_log_step': 2.108902e+00, 's5_b_re': 5.392097e-02, 's5_b_im': 5.576906e-02, 's5_c_re': 7.093846e-02, 's5_c_im': 8.090876e-02, 's5_d': 8.807266e-01, 's5_glu_w': 6.076239e-01, 'pool_w': 9.798312e-02, 'pool_scale': 8.121637e-01, 'ffn_up': 2.034221e-01, 'ffn_conv': 2.073490e-01, 'ffn_conv_b': 3.427723e-01, 'ffn_down': 3.771242e-01}


def _to_microbatches(a, axis):
    t = _jnp.moveaxis(a, axis, 0)
    t = t.reshape((N_MICROBATCH, t.shape[0] // N_MICROBATCH) + t.shape[1:])
    return _jnp.moveaxis(t, 1, axis + 1)


def setup_inputs(seed: int = 0) -> dict:
    inp = _fwd_setup_inputs(seed)
    key = _jax.random.fold_in(_jax.random.key(seed), 7919)
    shape, _ = _output_shape()
    out = dict(inp)
    out["loss_target"] = _jax.random.normal(_jax.random.fold_in(key, 0), shape, _jnp.float32)
    for i, name in enumerate(TWIN_WEIGHTS):
        w = inp[name].astype(_jnp.float32)
        if MOMENT_SCALE is None:
            s = _jnp.sqrt(_jnp.mean(_jnp.square(w)) + 1e-30)
        else:
            s = MOMENT_SCALE[name]
        km, kv = _jax.random.split(_jax.random.fold_in(key, i + 1))
        out[name] = w
        out["m_" + name] = s * _jax.random.normal(km, w.shape, _jnp.float32)
        out["v_" + name] = (s * s) * _jax.random.uniform(kv, w.shape, _jnp.float32, 0.5, 1.5)
    if N_MICROBATCH > 1:
        for name, axis in PER_EXAMPLE_BATCH_AXIS.items():
            out[name] = _to_microbatches(out[name], axis)
    return {'x': out['x'], 'c': out['c'], 'ctx': out['ctx'], 'c_ctx': out['c_ctx'], 'ada_w': out['ada_w'], 'ada_b': out['ada_b'], 'norm_g': out['norm_g'], 's5_lam_re': out['s5_lam_re'], 's5_lam_im': out['s5_lam_im'], 's5_log_step': out['s5_log_step'], 's5_b_re': out['s5_b_re'], 's5_b_im': out['s5_b_im'], 's5_c_re': out['s5_c_re'], 's5_c_im': out['s5_c_im'], 's5_d': out['s5_d'], 's5_glu_w': out['s5_glu_w'], 'pool_w': out['pool_w'], 'pool_scale': out['pool_scale'], 'ffn_up': out['ffn_up'], 'ffn_conv': out['ffn_conv'], 'ffn_conv_b': out['ffn_conv_b'], 'ffn_down': out['ffn_down'], 'loss_target': out['loss_target'], 'm_c_ctx': out['m_c_ctx'], 'm_ada_w': out['m_ada_w'], 'm_ada_b': out['m_ada_b'], 'm_norm_g': out['m_norm_g'], 'm_s5_lam_re': out['m_s5_lam_re'], 'm_s5_lam_im': out['m_s5_lam_im'], 'm_s5_log_step': out['m_s5_log_step'], 'm_s5_b_re': out['m_s5_b_re'], 'm_s5_b_im': out['m_s5_b_im'], 'm_s5_c_re': out['m_s5_c_re'], 'm_s5_c_im': out['m_s5_c_im'], 'm_s5_d': out['m_s5_d'], 'm_s5_glu_w': out['m_s5_glu_w'], 'm_pool_w': out['m_pool_w'], 'm_pool_scale': out['m_pool_scale'], 'm_ffn_up': out['m_ffn_up'], 'm_ffn_conv': out['m_ffn_conv'], 'm_ffn_conv_b': out['m_ffn_conv_b'], 'm_ffn_down': out['m_ffn_down'], 'v_c_ctx': out['v_c_ctx'], 'v_ada_w': out['v_ada_w'], 'v_ada_b': out['v_ada_b'], 'v_norm_g': out['v_norm_g'], 'v_s5_lam_re': out['v_s5_lam_re'], 'v_s5_lam_im': out['v_s5_lam_im'], 'v_s5_log_step': out['v_s5_log_step'], 'v_s5_b_re': out['v_s5_b_re'], 'v_s5_b_im': out['v_s5_b_im'], 'v_s5_c_re': out['v_s5_c_re'], 'v_s5_c_im': out['v_s5_c_im'], 'v_s5_d': out['v_s5_d'], 'v_s5_glu_w': out['v_s5_glu_w'], 'v_pool_w': out['v_pool_w'], 'v_pool_scale': out['v_pool_scale'], 'v_ffn_up': out['v_ffn_up'], 'v_ffn_conv': out['v_ffn_conv'], 'v_ffn_conv_b': out['v_ffn_conv_b'], 'v_ffn_down': out['v_ffn_down']}


def _loss(weights, diff, rest, loss_target):
    with _jax.named_scope("forward"):
        args = {**rest, TWIN_DIFF_INPUT: diff, **{k: w.astype(_WEIGHT_DTYPES[k]) for k, w in weights.items()}}
        y = _forward(args)
    with _jax.named_scope("loss_head"):
        err = _jnp.square(y.astype(_jnp.float32) - loss_target)
        return 0.5 * _jnp.sum(_jnp.mean(err, axis=-1)) if err.ndim else 0.5 * err


def _adamw(w, g, m, v):
    m = ADAM_B1 * m + (1.0 - ADAM_B1) * g
    v = ADAM_B2 * v + (1.0 - ADAM_B2) * _jnp.square(g)
    m_hat = m / (1.0 - ADAM_B1 ** ADAM_STEP)
    v_hat = v / (1.0 - ADAM_B2 ** ADAM_STEP)
    delta = -ADAM_LR * (m_hat / (_jnp.sqrt(v_hat) + ADAM_EPS) + ADAM_WD * w)
    return delta, m, v


def reference(x, c, ctx, c_ctx, ada_w, ada_b, norm_g, s5_lam_re, s5_lam_im, s5_log_step, s5_b_re, s5_b_im, s5_c_re, s5_c_im, s5_d, s5_glu_w, pool_w, pool_scale, ffn_up, ffn_conv, ffn_conv_b, ffn_down, loss_target, m_c_ctx, m_ada_w, m_ada_b, m_norm_g, m_s5_lam_re, m_s5_lam_im, m_s5_log_step, m_s5_b_re, m_s5_b_im, m_s5_c_re, m_s5_c_im, m_s5_d, m_s5_glu_w, m_pool_w, m_pool_scale, m_ffn_up, m_ffn_conv, m_ffn_conv_b, m_ffn_down, v_c_ctx, v_ada_w, v_ada_b, v_norm_g, v_s5_lam_re, v_s5_lam_im, v_s5_log_step, v_s5_b_re, v_s5_b_im, v_s5_c_re, v_s5_c_im, v_s5_d, v_s5_glu_w, v_pool_w, v_pool_scale, v_ffn_up, v_ffn_conv, v_ffn_conv_b, v_ffn_down):
    given = dict(x=x, c=c, ctx=ctx, c_ctx=c_ctx, ada_w=ada_w, ada_b=ada_b, norm_g=norm_g, s5_lam_re=s5_lam_re, s5_lam_im=s5_lam_im, s5_log_step=s5_log_step, s5_b_re=s5_b_re, s5_b_im=s5_b_im, s5_c_re=s5_c_re, s5_c_im=s5_c_im, s5_d=s5_d, s5_glu_w=s5_glu_w, pool_w=pool_w, pool_scale=pool_scale, ffn_up=ffn_up, ffn_conv=ffn_conv, ffn_conv_b=ffn_conv_b, ffn_down=ffn_down, loss_target=loss_target, m_c_ctx=m_c_ctx, m_ada_w=m_ada_w, m_ada_b=m_ada_b, m_norm_g=m_norm_g, m_s5_lam_re=m_s5_lam_re, m_s5_lam_im=m_s5_lam_im, m_s5_log_step=m_s5_log_step, m_s5_b_re=m_s5_b_re, m_s5_b_im=m_s5_b_im, m_s5_c_re=m_s5_c_re, m_s5_c_im=m_s5_c_im, m_s5_d=m_s5_d, m_s5_glu_w=m_s5_glu_w, m_pool_w=m_pool_w, m_pool_scale=m_pool_scale, m_ffn_up=m_ffn_up, m_ffn_conv=m_ffn_conv, m_ffn_conv_b=m_ffn_conv_b, m_ffn_down=m_ffn_down, v_c_ctx=v_c_ctx, v_ada_w=v_ada_w, v_ada_b=v_ada_b, v_norm_g=v_norm_g, v_s5_lam_re=v_s5_lam_re, v_s5_lam_im=v_s5_lam_im, v_s5_log_step=v_s5_log_step, v_s5_b_re=v_s5_b_re, v_s5_b_im=v_s5_b_im, v_s5_c_re=v_s5_c_re, v_s5_c_im=v_s5_c_im, v_s5_d=v_s5_d, v_s5_glu_w=v_s5_glu_w, v_pool_w=v_pool_w, v_pool_scale=v_pool_scale, v_ffn_up=v_ffn_up, v_ffn_conv=v_ffn_conv, v_ffn_conv_b=v_ffn_conv_b, v_ffn_down=v_ffn_down)
    weights = {n: given[n] for n in TWIN_WEIGHTS}
    shared = {n: given[n] for n in SHARED_INPUTS}
    per_example = {n: given[n] for n in ['x', 'c', 'ctx']}
    grad_fn = _jax.value_and_grad(_loss, argnums=(0, 1))

    def one_microbatch(ex, loss_target):
        ex = dict(ex)
        diff = ex.pop(TWIN_DIFF_INPUT)
        return grad_fn(weights, diff, {**shared, **ex}, loss_target)

    if N_MICROBATCH == 1:
        loss, (grad_w, grad_x) = one_microbatch(per_example, given["loss_target"])
    else:
        def body(carry, xs):
            loss_sum, grad_sum = carry
            l_k, (gw_k, gx_k) = one_microbatch(xs[0], xs[1])
            with _jax.named_scope("update"):
                return (loss_sum + l_k, _jax.tree.map(_jnp.add, grad_sum, gw_k)), gx_k

        init = (_jnp.zeros((), _jnp.float32), _jax.tree.map(_jnp.zeros_like, weights))
        (loss, grad_w), grad_x = _jax.lax.scan(body, init, (per_example, given["loss_target"]))
    with _jax.named_scope("update"):
        delta_w, new_m, new_v = {}, {}, {}
        for n in TWIN_WEIGHTS:
            delta_w[n], new_m[n], new_v[n] = _adamw(weights[n], grad_w[n], given["m_" + n], given["v_" + n])
    return (loss, grad_x, *[grad_w[n] for n in TWIN_WEIGHTS], *[delta_w[n] for n in TWIN_WEIGHTS],
            *[new_m[n] for n in TWIN_WEIGHTS], *[new_v[n] for n in TWIN_WEIGHTS])
```

```python
import math

import jax
import jax.numpy as jnp
from jax import lax
from jax.experimental import pallas as pl
from jax.experimental.pallas import tpu as pltpu

F32 = jnp.float32
MXU_DTYPE = jnp.bfloat16
ACT_DTYPE = jnp.bfloat16

LANE = 128
SUB = 8
PACK = 16
VMEM_LIMIT = 56 * 1024 * 1024
ELEMWISE_BLOCK = 1 << 18

N_DEV = 8
N_CHIP = 4
N_SEG = SUB
S5_GROUP = 16
S5_STATE = 64
S5_CB = LANE
S5_H = (S5_CB // S5_GROUP) * S5_STATE
S5_NS = 2 * S5_H
POOL_WINDOWS = (2, 4, 8, 16)
POOL_HALO = 16
GRID_W = 64
POS_BASE = 10000.0
RMS_EPS = 1e-6
N_MOD = 6

ADAM_LR = 0.001
ADAM_B1 = 0.9
ADAM_B2 = 0.999
ADAM_EPS = 1e-08
ADAM_WD = 0.01
ADAM_STEP = 10

NN = (((1,), (0,)), ((), ()))
NT = (((1,), (1,)), ((), ()))
TN = (((0,), (0,)), ((), ()))
MESH = pl.DeviceIdType.MESH


def _tile(n, cap, align=LANE):
    best = None
    for t in range(align, min(n, cap) + 1, align):
        if n % t == 0:
            best = t
    return n if best is None else best


def _params(sem=None):
    return pltpu.CompilerParams(dimension_semantics=sem, vmem_limit_bytes=VMEM_LIMIT)


def _dot(a, b, dims=NN):
    return lax.dot_general(a.astype(MXU_DTYPE), b.astype(MXU_DTYPE), dims, preferred_element_type=F32)


def _sigmoid(x):
    return 1.0 / (1.0 + jnp.exp(-x))


_GELU_C = math.sqrt(2.0 / math.pi)
_GELU_K = 0.044715


def _gelu(x):
    return 0.5 * x * (1.0 + jnp.tanh(_GELU_C * (x + _GELU_K * x * x * x)))


def _gelu_grad(x):
    t = jnp.tanh(_GELU_C * (x + _GELU_K * x * x * x))
    return 0.5 * (1.0 + t) + 0.5 * x * (1.0 - t * t) * _GELU_C * (1.0 + 3.0 * _GELU_K * x * x)


def _rstd(x):
    return lax.rsqrt(jnp.mean(x * x, axis=-1, keepdims=True) + RMS_EPS)


def _norm_bwd(dxh, xh, r):
    return r * (dxh - xh * jnp.mean(dxh * xh, axis=-1, keepdims=True))


def _rowwise(name, fn, rows, tiled, vecs, outs, accs=()):
    seg = rows // N_SEG
    tm = _tile(seg, 256, SUB)
    nt, ntp = rows // tm, seg // tm
    n_t, n_v, n_o, n_a = len(tiled), len(vecs), len(outs), len(accs)

    def spec(width, perm):
        if perm:
            return pl.BlockSpec((tm, width), lambda i: (i % ntp, i // ntp))
        return pl.BlockSpec((tm, width), lambda i: (i, 0))

    args, in_specs = [], []
    for arr, perm in tiled:
        width = arr.shape[-1]
        args.append(arr.reshape(seg, N_SEG * width) if perm else arr)
        in_specs.append(spec(width, perm))
    for v in vecs:
        args.append(v)
        in_specs.append(pl.BlockSpec(v.shape, lambda i: (0, 0)))
    out_shape, out_specs = [], []
    for width, dtype, perm in outs:
        out_shape.append(jax.ShapeDtypeStruct((seg, N_SEG * width) if perm else (rows, width), dtype))
        out_specs.append(spec(width, perm))
    for width in accs:
        out_shape.append(jax.ShapeDtypeStruct((SUB, width), F32))
        out_specs.append(pl.BlockSpec((SUB, width), lambda i: (0, 0)))

    def body(*refs):
        vals = [r[...] for r in refs[:n_t + n_v]]
        o_refs = refs[n_t + n_v:n_t + n_v + n_o]
        a_refs = refs[n_t + n_v + n_o:]
        o_vals, a_vals = fn(*vals)
        for r, v in zip(o_refs, o_vals):
            r[...] = v.astype(r.dtype)
        if n_a:
            @pl.when(pl.program_id(0) == 0)
            def _():
                for r in a_refs:
                    r[...] = jnp.zeros_like(r)
            for r, v in zip(a_refs, a_vals):
                r[...] += v.reshape(tm // SUB, SUB, v.shape[-1]).sum(axis=0)

    res = pl.pallas_call(
        body, name=name, grid=(nt,), in_specs=in_specs, out_specs=out_specs, out_shape=out_shape,
        compiler_params=_params(("arbitrary",)),
    )(*args)
    res = list(res)
    for k, (width, _, perm) in enumerate(outs):
        if perm:
            res[k] = res[k].reshape(rows, width)
    return res[:n_o], [jnp.sum(a, axis=0, keepdims=True) for a in res[n_o:]]


def _prenorm(x, gam, shift, scale1):
    r = _rstd(x)
    return (x * r) * gam * scale1 + shift


def _prenorm_bwd(du, x, gam, scale1):
    r = _rstd(x)
    xh = x * r
    dxn = du * scale1
    dx = _norm_bwd(dxn * gam, xh, r)
    return dx, du, du * (xh * gam), dxn * xh


def _postnorm_bwd(dxo, y, gam, gate):
    r = _rstd(y)
    yh = y * r
    dyn = dxo * gate
    dy = _norm_bwd(dyn * gam, yh, r)
    return dy, dxo * (yh * gam), dyn * yh


def _matmul(name, a, b, dims, grid, a_spec, b_spec, o_spec, out_shape, out_dtype, acc_shape):
    nk = grid[2]

    def body(a_ref, b_ref, o_ref, *scratch):
        part = _dot(a_ref[...], b_ref[...], dims)
        if nk == 1:
            o_ref[...] = part.astype(o_ref.dtype)
        else:
            acc_ref, = scratch
            k = pl.program_id(2)

            @pl.when(k == 0)
            def _():
                acc_ref[...] = part

            @pl.when(k > 0)
            def _():
                acc_ref[...] += part

            @pl.when(k == nk - 1)
            def _():
                o_ref[...] = acc_ref[...].astype(o_ref.dtype)

    return pl.pallas_call(
        body, name=name, grid=grid, in_specs=[a_spec, b_spec], out_specs=o_spec,
        out_shape=jax.ShapeDtypeStruct(out_shape, out_dtype),
        scratch_shapes=[] if nk == 1 else [pltpu.VMEM(acc_shape, F32)],
        compiler_params=_params(("parallel", "parallel", "arbitrary")),
    )(a, b)


def _mm_cols(name, a, w4, layer, out_dtype):
    m, k = a.shape
    ns = w4.shape[-1]
    tm, tn = _tile(m, 1024), _tile(ns, 1536)
    nps = ns // tn
    return _matmul(
        name, a, w4, NN, (m // tm, N_CHIP * nps, 1),
        pl.BlockSpec((tm, k), lambda i, n, kk: (i, 0)),
        pl.BlockSpec((None, None, k, tn), lambda i, n, kk: (n // nps, layer, 0, n % nps)),
        pl.BlockSpec((tm, tn), lambda i, n, kk: (i, n)),
        (m, N_CHIP * ns), out_dtype, None)


def _mm_cols_nt(name, g, w4, layer, out_dtype):
    m = g.shape[0]
    k, ns = w4.shape[-2:]
    tm, tk = _tile(m, 512), _tile(ns, 1536)
    kps = ns // tk
    return _matmul(
        name, g, w4, NT, (m // tm, 1, N_CHIP * kps),
        pl.BlockSpec((tm, tk), lambda i, n, kk: (i, kk)),
        pl.BlockSpec((None, None, k, tk), lambda i, n, kk: (kk // kps, layer, 0, kk % kps)),
        pl.BlockSpec((tm, k), lambda i, n, kk: (i, 0)),
        (m, k), out_dtype, (tm, k))


def _mm_cols_tn(name, a, g, ns, out_dtype):
    m, k = a.shape
    tkm, tmk, tn = _tile(m, 512), _tile(k, 1024), _tile(ns, 1536)
    nps = ns // tn
    return _matmul(
        name, a, g, TN, (k // tmk, N_CHIP * nps, m // tkm),
        pl.BlockSpec((tkm, tmk), lambda i, n, kk: (kk, i)),
        pl.BlockSpec((tkm, tn), lambda i, n, kk: (kk, n)),
        pl.BlockSpec((None, tmk, tn), lambda i, n, kk: (n // nps, i, n % nps)),
        (N_CHIP, k, ns), out_dtype, (tmk, tn))


def _mm_rows(name, a, w4, layer, out_dtype):
    m = a.shape[0]
    rs, n = w4.shape[-2:]
    tm, tk = _tile(m, 512), _tile(rs, 1536)
    kps = rs // tk
    return _matmul(
        name, a, w4, NN, (m // tm, 1, N_CHIP * kps),
        pl.BlockSpec((tm, tk), lambda i, j, kk: (i, kk)),
        pl.BlockSpec((None, None, tk, n), lambda i, j, kk: (kk // kps, layer, kk % kps, 0)),
        pl.BlockSpec((tm, n), lambda i, j, kk: (i, 0)),
        (m, n), out_dtype, (tm, n))


def _mm_rows_nt(name, g, w4, layer, out_dtype):
    m, n = g.shape
    rs = w4.shape[-2]
    tm, tn = _tile(m, 1024), _tile(rs, 1536)
    nps = rs // tn
    return _matmul(
        name, g, w4, NT, (m // tm, N_CHIP * nps, 1),
        pl.BlockSpec((tm, n), lambda i, j, kk: (i, 0)),
        pl.BlockSpec((None, None, tn, n), lambda i, j, kk: (j // nps, layer, j % nps, 0)),
        pl.BlockSpec((tm, tn), lambda i, j, kk: (i, j)),
        (m, N_CHIP * rs), out_dtype, None)


def _mm_rows_tn(name, a, g, rs, out_dtype):
    m = a.shape[0]
    n = g.shape[1]
    tkm, tmr, tn = _tile(m, 512), _tile(rs, 1536), _tile(n, 1024)
    mps = rs // tmr
    return _matmul(
        name, a, g, TN, (N_CHIP * mps, n // tn, m // tkm),
        pl.BlockSpec((tkm, tmr), lambda i, j, kk: (kk, i)),
        pl.BlockSpec((tkm, tn), lambda i, j, kk: (kk, j)),
        pl.BlockSpec((None, tmr, tn), lambda i, j, kk: (i // mps, i % mps, j)),
        (N_CHIP, rs, n), out_dtype, (tmr, tn))


def _mm_grp(name, a, w, dims, out_dtype):
    m = a.shape[0]
    ng, ch = w.shape[:2]
    tm = _tile(m, 1024)
    return _matmul(
        name, a, w, dims, (m // tm, ng, 1),
        pl.BlockSpec((tm, ch), lambda i, g, kk: (i, g)),
        pl.BlockSpec((None, ch, ch), lambda i, g, kk: (g, 0, 0)),
        pl.BlockSpec((tm, ch), lambda i, g, kk: (i, g)),
        (m, ng * ch), out_dtype, None)


def _mm_grp_tn(name, a, g, ng, out_dtype):
    m = a.shape[0]
    ch = a.shape[1] // ng
    tk = _tile(m, 1024)
    return _matmul(
        name, a, g, TN, (ng, 1, m // tk),
        pl.BlockSpec((tk, ch), lambda i, j, kk: (kk, i)),
        pl.BlockSpec((tk, ch), lambda i, j, kk: (kk, i)),
        pl.BlockSpec((None, ch, ch), lambda i, j, kk: (i, 0, 0)),
        (ng, ch, ch), out_dtype, (ch, ch))


def _view2d(a):
    return a.reshape(-1, a.shape[-1])


def _elementwise(name, fn, ins, out_dtypes):
    r, c = ins[0].shape
    tr = _tile(r, max(PACK, ELEMWISE_BLOCK // c), PACK)
    spec = pl.BlockSpec((tr, c), lambda i: (i, 0))

    def body(*refs):
        outs = fn(*[x[...] for x in refs[:len(ins)]])
        for o_ref, o in zip(refs[len(ins):], outs):
            o_ref[...] = o.astype(o_ref.dtype)

    return pl.pallas_call(
        body, name=name, grid=(r // tr,), in_specs=[spec] * len(ins), out_specs=[spec] * len(out_dtypes),
        out_shape=[jax.ShapeDtypeStruct((r, c), d) for d in out_dtypes],
        compiler_params=_params(("parallel",)),
    )(*ins)


def _adamw_math(w, g, m, v):
    m = ADAM_B1 * m + (1.0 - ADAM_B1) * g
    v = ADAM_B2 * v + (1.0 - ADAM_B2) * (g * g)
    m_hat = m / (1.0 - ADAM_B1 ** ADAM_STEP)
    v_hat = v / (1.0 - ADAM_B2 ** ADAM_STEP)
    delta = -ADAM_LR * (m_hat / (jnp.sqrt(v_hat) + ADAM_EPS) + ADAM_WD * w)
    return delta, m, v


def _adamw(name, w, g, m, v):
    shape = w.shape
    outs = _elementwise(name, _adamw_math, [_view2d(w), _view2d(g), _view2d(m), _view2d(v)], [F32, F32, F32])
    return [o.reshape(shape) for o in outs]


def _coords():
    return lax.axis_index("x"), lax.axis_index("y"), lax.axis_index("c")


def _peer(x, y, c, k):
    return (x ^ (k >> 2), y ^ ((k >> 1) & 1), c ^ (k & 1))


def _all_gather8(name, block):
    r = block.shape[0]

    def body(x_ref, out_ref, send_sems, recv_sems):
        x, y, c = _coords()
        me = 4 * x + 2 * y + c
        out_ref[me] = x_ref[...]
        copies = []
        for k in range(1, N_DEV):
            cp = pltpu.make_async_remote_copy(
                src_ref=x_ref, dst_ref=out_ref.at[me], send_sem=send_sems.at[k], recv_sem=recv_sems.at[k],
                device_id=_peer(x, y, c, k), device_id_type=MESH)
            cp.start()
            copies.append(cp)
        for cp in copies:
            cp.wait()

    return pl.pallas_call(
        body, name=name,
        in_specs=[pl.BlockSpec(memory_space=pltpu.VMEM)], out_specs=pl.BlockSpec(memory_space=pltpu.VMEM),
        out_shape=jax.ShapeDtypeStruct((N_DEV, r, LANE), F32),
        scratch_shapes=[pltpu.SemaphoreType.DMA((N_DEV,)), pltpu.SemaphoreType.DMA((N_DEV,))],
        compiler_params=pltpu.CompilerParams(vmem_limit_bytes=VMEM_LIMIT),
    )(block)


def _all_reduce8(name, parts):
    r = parts.shape[1]

    def body(p_ref, out_ref, rbuf, send1, recv1, send2, recv2):
        x, y, c = _coords()
        me = 4 * x + 2 * y + c
        first = []
        for k in range(1, N_DEV):
            px, py, pc = _peer(x, y, c, k)
            cp = pltpu.make_async_remote_copy(
                src_ref=p_ref.at[4 * px + 2 * py + pc], dst_ref=rbuf.at[me], send_sem=send1.at[k],
                recv_sem=recv1.at[k], device_id=(px, py, pc), device_id_type=MESH)
            cp.start()
            first.append(cp)
        rbuf[me] = p_ref[me]
        for cp in first:
            cp.wait()
        acc = rbuf[0]
        for d in range(1, N_DEV):
            acc = acc + rbuf[d]
        out_ref[me] = acc
        second = []
        for k in range(1, N_DEV):
            cp = pltpu.make_async_remote_copy(
                src_ref=out_ref.at[me], dst_ref=out_ref.at[me], send_sem=send2.at[k], recv_sem=recv2.at[k],
                device_id=_peer(x, y, c, k), device_id_type=MESH)
            cp.start()
            second.append(cp)
        for cp in second:
            cp.wait()

    return pl.pallas_call(
        body, name=name,
        in_specs=[pl.BlockSpec(memory_space=pltpu.VMEM)], out_specs=pl.BlockSpec(memory_space=pltpu.VMEM),
        out_shape=jax.ShapeDtypeStruct((N_DEV, r, LANE), F32),
        scratch_shapes=[pltpu.VMEM((N_DEV, r, LANE), F32)] + [pltpu.SemaphoreType.DMA((N_DEV,))] * 4,
        compiler_params=pltpu.CompilerParams(vmem_limit_bytes=VMEM_LIMIT),
    )(parts)


def _hbm_exchange(name, ins, out_shapes, plan):
    n_in = len(ins)

    def body(*refs):
        in_refs, out_refs = refs[:n_in], refs[n_in:n_in + len(out_shapes)]
        send_sems, recv_sems, local_sems = refs[n_in + len(out_shapes):]
        started = []
        for i, (src, dst, dev) in enumerate(plan(in_refs, out_refs, _coords())):
            if dev is None:
                cp = pltpu.make_async_copy(src, dst, local_sems.at[i])
            else:
                cp = pltpu.make_async_remote_copy(
                    src_ref=src, dst_ref=dst, send_sem=send_sems.at[i], recv_sem=recv_sems.at[i],
                    device_id=dev, device_id_type=MESH)
            cp.start()
            started.append(cp)
        for cp in started:
            cp.wait()

    n_sem = 8 * max(len(ins), len(out_shapes))
    any_spec = pl.BlockSpec(memory_space=pl.ANY)
    return pl.pallas_call(
        body, name=name, in_specs=[any_spec] * n_in, out_specs=[any_spec] * len(out_shapes),
        out_shape=out_shapes,
        scratch_shapes=[pltpu.SemaphoreType.DMA((n_sem,))] * 3,
        compiler_params=pltpu.CompilerParams(has_side_effects=True),
    )(*ins)


def _gather_chips(shards):
    def plan(in_refs, out_refs, xyc):
        x, y, c = xyc
        chip = 2 * x + y
        copies = []
        for src, dst in zip(in_refs, out_refs):
            copies.append((src, dst.at[chip], None))
            for k in (1, 2, 3):
                copies.append((src, dst.at[chip], (x ^ (k >> 1), y ^ (k & 1), c)))
        return copies

    return _hbm_exchange(
        "gather_weights", shards, [jax.ShapeDtypeStruct((N_CHIP,) + s.shape, s.dtype) for s in shards], plan)


def _reduce_grads(grads, full_shapes, targets):
    halves = [(g.shape[1] // 2) for g in grads]

    def plan_a(in_refs, out_refs, xyc):
        x, y, c = xyc
        copies = []
        for i, g in enumerate(in_refs):
            h = halves[i]
            own, got = out_refs[2 * i], out_refs[2 * i + 1]
            copies.append((g.at[:, pl.ds(c * h, h)], own, None))
            copies.append((g.at[:, pl.ds((1 - c) * h, h)], got, (x, y, 1 - c)))
        return copies

    shapes_a = []
    for g, h in zip(grads, halves):
        shapes_a += [jax.ShapeDtypeStruct((N_CHIP, h, g.shape[2]), g.dtype)] * 2
    res_a = _hbm_exchange("reduce_pair", grads, shapes_a, plan_a)
    pair = [
        _elementwise(f"pair_sum_{i}", lambda a, b: (a.astype(F32) + b.astype(F32),),
                     [_view2d(res_a[2 * i]), _view2d(res_a[2 * i + 1])], [ACT_DTYPE])[0].reshape(res_a[2 * i].shape)
        for i in range(len(grads))
    ]

    def plan_b(in_refs, out_refs, xyc):
        x, y, c = xyc
        chip = 2 * x + y
        copies = []
        for src, dst in zip(in_refs, out_refs):
            copies.append((src.at[chip], dst.at[chip], None))
            for k in (1, 2, 3):
                px, py = x ^ (k >> 1), y ^ (k & 1)
                copies.append((src.at[2 * px + py], dst.at[chip], (px, py, c)))
        return copies

    res_b = _hbm_exchange("reduce_chips", pair, [jax.ShapeDtypeStruct(p.shape, p.dtype) for p in pair], plan_b)
    red = []
    for i, rb in enumerate(res_b):
        n, h, cdim = rb.shape
        flat = rb.reshape(n * h, cdim)
        tr = _tile(h, max(PACK, ELEMWISE_BLOCK // cdim), PACK)
        nb = h // tr

        def body(a_ref, b_ref, c_ref, d_ref, o_ref):
            o_ref[...] = ((a_ref[...].astype(F32) + b_ref[...].astype(F32)) + c_ref[...].astype(F32)) + d_ref[...].astype(F32)

        red.append(pl.pallas_call(
            body, name=f"chip_sum_{i}", grid=(nb,),
            in_specs=[pl.BlockSpec((tr, cdim), (lambda j, s=s: (s * nb + j, 0))) for s in range(N_CHIP)],
            out_specs=pl.BlockSpec((tr, cdim), lambda j: (j, 0)),
            out_shape=jax.ShapeDtypeStruct((h, cdim), F32),
            compiler_params=_params(("parallel",)),
        )(flat, flat, flat, flat))

    def plan_c(in_refs, out_refs, xyc):
        x, y, c = xyc
        copies = []
        for i, src in enumerate(in_refs):
            h = halves[i]
            where, layer = targets[i]
            dst = out_refs[where] if layer is None else out_refs[where].at[layer]
            dst = dst.at[pl.ds(c * h, h)]
            copies.append((src, dst, None))
            copies.append((src, dst, (x, y, 1 - c)))
        return copies

    return _hbm_exchange("reduce_swap", red, [jax.ShapeDtypeStruct(s, F32) for s in full_shapes], plan_c)


def _cmul(ar, ai, br, bi):
    return ar * br - ai * bi, ar * bi + ai * br


def _cpow(ar, ai, n):
    rr, ri = jnp.ones_like(ar), jnp.zeros_like(ai)
    br, bi = ar, ai
    while n:
        if n & 1:
            rr, ri = _cmul(rr, ri, br, bi)
        br, bi = _cmul(br, bi, br, bi)
        n >>= 1
    return rr, ri


def _tile_rows(t):
    return pl.ds(pl.multiple_of(t * SUB, SUB), SUB)


def _scan_setup(buf, steps, ar, ai, h0r, h0i, rev):
    def total(i, carry):
        sr, si = carry
        rows = _tile_rows(steps - 1 - i if rev else i)
        pr, pi = _cmul(ar, ai, sr, si)
        return pr + buf[rows, 0:S5_H], pi + buf[rows, S5_H:S5_NS]

    zero = jnp.zeros((SUB, S5_H), F32)
    tot_r, tot_i = lax.fori_loop(0, steps, total, (zero, zero))
    pw_r, pw_i = _cpow(ar[0:1], ai[0:1], steps)
    row = lax.broadcasted_iota(jnp.int32, (SUB, S5_H), 0)
    cur_r, cur_i = h0r, h0i
    init_r, init_i = zero, zero
    for s in (range(N_SEG - 1, -1, -1) if rev else range(N_SEG)):
        init_r = jnp.where(row == s, cur_r, init_r)
        init_i = jnp.where(row == s, cur_i, init_i)
        nr, ni = _cmul(pw_r, pw_i, cur_r, cur_i)
        cur_r, cur_i = nr + tot_r[s:s + 1], ni + tot_i[s:s + 1]
    return init_r, init_i, cur_r, cur_i


def _scan(buf, steps, ar, ai, h0r, h0i, rev, store):
    init_r, init_i, fin_r, fin_i = _scan_setup(buf, steps, ar, ai, h0r, h0i, rev)
    if store:
        def step(i, carry):
            hr, hi = carry
            rows = _tile_rows(steps - 1 - i if rev else i)
            pr, pi = _cmul(ar, ai, hr, hi)
            hr, hi = pr + buf[rows, 0:S5_H], pi + buf[rows, S5_H:S5_NS]
            buf[rows, 0:S5_H] = hr
            buf[rows, S5_H:S5_NS] = hi
            return hr, hi

        lax.fori_loop(0, steps, step, (init_r, init_i))
    return fin_r, fin_i


def _adjoint_scan(gbuf, hbuf, steps, ar, ai, l0r, l0i, hin_r, hin_i, rev):
    ci = -ai
    arev = not rev
    init_r, init_i, fin_r, fin_i = _scan_setup(gbuf, steps, ar, ci, l0r, l0i, arev)
    zero = jnp.zeros((SUB, S5_H), F32)

    def update(t, hp_r, hp_i, carry):
        lr, li, dr, di = carry
        rows = _tile_rows(t)
        pr, pi = _cmul(ar, ci, lr, li)
        lr, li = pr + gbuf[rows, 0:S5_H], pi + gbuf[rows, S5_H:S5_NS]
        gbuf[rows, 0:S5_H] = lr
        gbuf[rows, S5_H:S5_NS] = li
        return lr, li, dr + lr * hp_r + li * hp_i, di + li * hp_r - lr * hp_i

    def step(i, carry):
        t = steps - 1 - i if rev is False else i
        prev = _tile_rows(t - 1 if rev is False else t + 1)
        return update(t, hbuf[prev, 0:S5_H], hbuf[prev, S5_H:S5_NS], carry)

    carry = lax.fori_loop(0, steps - 1, step, (init_r, init_i, zero, zero))
    row = lax.broadcasted_iota(jnp.int32, (SUB, S5_H), 0)
    if rev:
        last, edge, shift, t = _tile_rows(0), N_SEG - 1, SUB - 1, steps - 1
    else:
        last, edge, shift, t = _tile_rows(steps - 1), 0, 1, 0
    hp_r = jnp.where(row == edge, hin_r, pltpu.roll(hbuf[last, 0:S5_H], shift, 0))
    hp_i = jnp.where(row == edge, hin_i, pltpu.roll(hbuf[last, S5_H:S5_NS], shift, 0))
    _, _, dr, di = update(t, hp_r, hp_i, carry)
    return fin_r, fin_i, dr, di


def _s5_chunk(rows):
    return _tile(rows, 512, PACK)


def _s5_forward(u, uc, bblk, cblk, atile, dsk):
    rows, d = u.shape
    rows_c = uc.shape[0]
    nj = d // S5_CB
    steps, steps_c = rows // N_SEG, rows_c // N_SEG
    rc = _s5_chunk(rows)

    def body(u_ref, uc_ref, b_ref, c_ref, a_ref, d_ref, y_ref, buf, bufc):
        zero = jnp.zeros((1, S5_H), F32)
        for dr in (0, 1):
            rev = dr == 1
            ar, ai = a_ref[dr, :, 0:S5_H], a_ref[dr, :, S5_H:S5_NS]
            bm, cm = b_ref[dr].astype(MXU_DTYPE), c_ref[dr].astype(MXU_DTYPE)
            bufc[...] = _dot(uc_ref[...], bm)
            fin_r, fin_i = _scan(bufc, steps_c, ar, ai, zero, zero, rev, False)

            def project(r, _):
                rs = pl.ds(pl.multiple_of(r * rc, rc), rc)
                buf[rs, :] = _dot(u_ref[rs, :], bm)
                return 0

            lax.fori_loop(0, rows // rc, project, 0)
            _scan(buf, steps, ar, ai, fin_r, fin_i, rev, True)

            def readout(r, _):
                rs = pl.ds(pl.multiple_of(r * rc, rc), rc)
                yv = _dot(buf[rs, :], cm)
                if dr == 0:
                    y_ref[rs, :] = u_ref[rs, :].astype(F32) * d_ref[...] + yv
                else:
                    y_ref[rs, :] += yv
                return 0

            lax.fori_loop(0, rows // rc, readout, 0)

    return pl.pallas_call(
        body, name="s5_forward", grid=(nj,),
        in_specs=[
            pl.BlockSpec((rows, S5_CB), lambda j: (0, j)),
            pl.BlockSpec((rows_c, S5_CB), lambda j: (0, j)),
            pl.BlockSpec((2, None, S5_CB, S5_NS), lambda j: (0, j, 0, 0)),
            pl.BlockSpec((2, None, S5_NS, S5_CB), lambda j: (0, j, 0, 0)),
            pl.BlockSpec((2, None, SUB, S5_NS), lambda j: (0, j, 0, 0)),
            pl.BlockSpec((1, S5_CB), lambda j: (0, j)),
        ],
        out_specs=pl.BlockSpec((rows, S5_CB), lambda j: (0, j)),
        out_shape=jax.ShapeDtypeStruct((rows, d), F32),
        scratch_shapes=[pltpu.VMEM((rows, S5_NS), F32), pltpu.VMEM((rows_c, S5_NS), F32)],
        compiler_params=_params(("parallel",)),
    )(u, uc, bblk, cblk, atile, dsk)


def _s5_backward(u, uc, dy, bblk, cblk, atile, dsk):
    rows, d = u.shape
    rows_c = uc.shape[0]
    nj = d // S5_CB
    steps, steps_c = rows // N_SEG, rows_c // N_SEG
    rc = _s5_chunk(rows)
    nchunk = rows // rc

    def body(u_ref, uc_ref, dy_ref, b_ref, c_ref, a_ref, d_ref,
             du_ref, duc_ref, db_ref, dc_ref, da_ref, dd_ref, hbuf, gbuf, hcbuf, gcbuf):
        zero = jnp.zeros((1, S5_H), F32)
        db_ref[...] = jnp.zeros_like(db_ref)
        dc_ref[...] = jnp.zeros_like(dc_ref)
        dd_ref[...] = jnp.zeros_like(dd_ref)
        for dr in (0, 1):
            rev = dr == 1
            ar, ai = a_ref[dr, :, 0:S5_H], a_ref[dr, :, S5_H:S5_NS]
            bm, cm = b_ref[dr].astype(MXU_DTYPE), c_ref[dr].astype(MXU_DTYPE)
            hcbuf[...] = _dot(uc_ref[...], bm)
            hin_r, hin_i = _scan(hcbuf, steps_c, ar, ai, zero, zero, rev, True)

            def project(r, _):
                rs = pl.ds(pl.multiple_of(r * rc, rc), rc)
                hbuf[rs, :] = _dot(u_ref[rs, :], bm)
                return 0

            lax.fori_loop(0, nchunk, project, 0)
            _scan(hbuf, steps, ar, ai, hin_r, hin_i, rev, True)

            def readout_bwd(r, _):
                rs = pl.ds(pl.multiple_of(r * rc, rc), rc)
                dyv = dy_ref[rs, :]
                gbuf[rs, :] = _dot(dyv, cm, NT)
                dc_ref[dr] += _dot(hbuf[rs, :], dyv, TN)
                return 0

            lax.fori_loop(0, nchunk, readout_bwd, 0)
            lf_r, lf_i, dar, dai = _adjoint_scan(gbuf, hbuf, steps, ar, ai, zero, zero, hin_r, hin_i, rev)
            gcbuf[...] = jnp.zeros_like(gcbuf)
            _, _, dar_c, dai_c = _adjoint_scan(gcbuf, hcbuf, steps_c, ar, ai, lf_r, lf_i, zero, zero, rev)
            da_ref[dr, :, 0:S5_H] = dar + dar_c
            da_ref[dr, :, S5_H:S5_NS] = dai + dai_c

            def project_bwd(r, _):
                rs = pl.ds(pl.multiple_of(r * rc, rc), rc)
                lam = gbuf[rs, :]
                uv = u_ref[rs, :]
                part = _dot(lam, bm, NT)
                db_ref[dr] += _dot(uv, lam, TN)
                if dr == 0:
                    dyv = dy_ref[rs, :].astype(F32)
                    du_ref[rs, :] = part + dyv * d_ref[...]
                    dd_ref[...] += (dyv * uv.astype(F32)).reshape(rc // SUB, SUB, S5_CB).sum(axis=0)
                else:
                    du_ref[rs, :] += part
                return 0

            lax.fori_loop(0, nchunk, project_bwd, 0)
            lam_c = gcbuf[...]
            part_c = _dot(lam_c, bm, NT)
            db_ref[dr] += _dot(uc_ref[...], lam_c, TN)
            if dr == 0:
                duc_ref[...] = part_c
            else:
                duc_ref[...] += part_c

    blk = lambda r: pl.BlockSpec((r, S5_CB), lambda j: (0, j))
    return pl.pallas_call(
        body, name="s5_backward", grid=(nj,),
        in_specs=[
            blk(rows), blk(rows_c), blk(rows),
            pl.BlockSpec((2, None, S5_CB, S5_NS), lambda j: (0, j, 0, 0)),
            pl.BlockSpec((2, None, S5_NS, S5_CB), lambda j: (0, j, 0, 0)),
            pl.BlockSpec((2, None, SUB, S5_NS), lambda j: (0, j, 0, 0)),
            pl.BlockSpec((1, S5_CB), lambda j: (0, j)),
        ],
        out_specs=[
            blk(rows), blk(rows_c),
            pl.BlockSpec((2, None, S5_CB, S5_NS), lambda j: (0, j, 0, 0)),
            pl.BlockSpec((2, None, S5_NS, S5_CB), lambda j: (0, j, 0, 0)),
            pl.BlockSpec((2, None, SUB, S5_NS), lambda j: (0, j, 0, 0)),
            pl.BlockSpec((SUB, S5_CB), lambda j: (0, j)),
        ],
        out_shape=[
            jax.ShapeDtypeStruct((rows, d), F32), jax.ShapeDtypeStruct((rows_c, d), F32),
            jax.ShapeDtypeStruct(bblk.shape, F32), jax.ShapeDtypeStruct(cblk.shape, F32),
            jax.ShapeDtypeStruct(atile.shape, F32), jax.ShapeDtypeStruct((SUB, d), F32),
        ],
        scratch_shapes=[pltpu.VMEM((rows, S5_NS), F32), pltpu.VMEM((rows, S5_NS), F32),
                        pltpu.VMEM((rows_c, S5_NS), F32), pltpu.VMEM((rows_c, S5_NS), F32)],
        compiler_params=_params(("parallel",)),
    )(u, uc, dy, bblk, cblk, atile, dsk)


def _s5_prepare(lam_re, lam_im, log_step, b_re, b_im, c_re, c_im):
    nd, g, p = lam_re.shape
    gb = S5_CB // S5_GROUP
    nj = g // gb
    dt = jnp.exp(log_step)[..., None]
    mag = jnp.exp(lam_re * dt)
    abar_re = mag * jnp.cos(lam_im * dt)
    abar_im = mag * jnp.sin(lam_im * dt)
    nr, ni = abar_re - 1.0, abar_im
    den = lam_re * lam_re + lam_im * lam_im
    fr = (nr * lam_re + ni * lam_im) / den
    fi = (ni * lam_re - nr * lam_im) / den
    bbar_re = fr[..., None] * b_re - fi[..., None] * b_im
    bbar_im = fr[..., None] * b_im + fi[..., None] * b_re
    eye = jnp.eye(gb, dtype=bool)

    def diag_in(w):
        w = w.reshape(nd, nj, gb, p, S5_GROUP).transpose(0, 1, 2, 4, 3)
        w = jnp.where(eye[None, None, :, None, :, None], w[:, :, :, :, None, :], 0.0)
        return w.reshape(nd, nj, gb * S5_GROUP, gb * p)

    def diag_out(w):
        w = w.reshape(nd, nj, gb, S5_GROUP, p).transpose(0, 1, 2, 4, 3)
        w = jnp.where(eye[None, None, :, None, :, None], w[:, :, :, :, None, :], 0.0)
        return w.reshape(nd, nj, gb * p, gb * S5_GROUP)

    bblk = jnp.concatenate([diag_in(bbar_re), diag_in(bbar_im)], axis=-1)
    cblk = jnp.concatenate([diag_out(c_re), -diag_out(c_im)], axis=-2)
    a2 = jnp.concatenate([abar_re.reshape(nd, nj, gb * p), abar_im.reshape(nd, nj, gb * p)], axis=-1)
    atile = jnp.broadcast_to(a2[:, :, None, :], (nd, nj, SUB, 2 * gb * p))
    return bblk, cblk, atile


def _shifted(x, prev_row, next_row):
    n = x.shape[0]
    row = lax.broadcasted_iota(jnp.int32, x.shape, 0)
    xp = jnp.where(row == 0, prev_row, pltpu.roll(x, 1, 0))
    xn = jnp.where(row == n - 1, next_row, pltpu.roll(x, n - 1, 0))
    return xp, xn


def _edge_rows(ref, r0, n, total, group):
    lo = pl.multiple_of(jnp.maximum(r0 - group, 0), group)
    hi = pl.multiple_of(jnp.minimum(r0 + n, total - group), group)
    prev_row = ref[pl.ds(lo, group), :].astype(F32)[group - 1:group] * (r0 > 0).astype(F32)
    next_row = ref[pl.ds(hi, group), :].astype(F32)[0:1] * (r0 + n < total).astype(F32)
    return prev_row, next_row


def _conv_rows(ref, r0, n, total, w_ref, b_ref):
    x = ref[pl.ds(r0, n), :].astype(F32)
    xp, xn = _shifted(x, *_edge_rows(ref, r0, n, total, PACK))
    hc = w_ref[0:1, :] * xp + w_ref[1:2, :] * x + w_ref[2:3, :] * xn + b_ref[...]
    return hc, xp, x, xn


def _conv_specs(rows, f, tc):
    nt = f // tc
    val = lambda r: pl.BlockSpec((r, tc), lambda j: (0, j))
    gate = lambda r: pl.BlockSpec((r, tc), lambda j: (0, j + nt))
    return val, gate


def _conv_swiglu_fwd(name, h, cw, cb):
    rows, f2 = h.shape
    f = f2 // 2
    tc = _tile(f, 256)
    rc = _tile(rows, 256, PACK)
    val, gate = _conv_specs(rows, f, tc)

    def body(hv_ref, hg_ref, wv_ref, wg_ref, bv_ref, bg_ref, a_ref):
        def chunk(r, _):
            r0 = pl.multiple_of(r * rc, rc)
            hv = _conv_rows(hv_ref, r0, rc, rows, wv_ref, bv_ref)[0]
            hg = _conv_rows(hg_ref, r0, rc, rows, wg_ref, bg_ref)[0]
            a_ref[pl.ds(r0, rc), :] = (hg * _sigmoid(hg) * hv).astype(a_ref.dtype)
            return 0

        lax.fori_loop(0, rows // rc, chunk, 0)

    return pl.pallas_call(
        body, name=name, grid=(f // tc,),
        in_specs=[val(rows), gate(rows), val(3), gate(3), val(1), gate(1)],
        out_specs=val(rows), out_shape=jax.ShapeDtypeStruct((rows, f), ACT_DTYPE),
        compiler_params=_params(("parallel",)),
    )(h, h, cw, cw, cb, cb)


def _conv_swiglu_bwd(name, da, h, cw, cb):
    rows, f2 = h.shape
    f = f2 // 2
    tc = _tile(f, 256)
    rc = _tile(rows, 256, PACK)
    val, gate = _conv_specs(rows, f, tc)

    def body(da_ref, hv_ref, hg_ref, wv_ref, wg_ref, bv_ref, bg_ref,
             dhv_ref, dhg_ref, dwv_ref, dwg_ref, dbv_ref, dbg_ref, sv, sg):
        def first(r, carry):
            r0 = pl.multiple_of(r * rc, rc)
            rs = pl.ds(r0, rc)
            hv, vp, vx, vn = _conv_rows(hv_ref, r0, rc, rows, wv_ref, bv_ref)
            hg, gp, gx, gn = _conv_rows(hg_ref, r0, rc, rows, wg_ref, bg_ref)
            d = da_ref[rs, :].astype(F32)
            s = _sigmoid(hg)
            dv = d * (hg * s)
            dg = d * hv * (s * (1.0 + hg * (1.0 - s)))
            sv[rs, :] = dv
            sg[rs, :] = dg
            sums = [dv * vp, dv * vx, dv * vn, dv, dg * gp, dg * gx, dg * gn, dg]
            return tuple(c + jnp.sum(x, axis=0, keepdims=True) for c, x in zip(carry, sums))

        zero = jnp.zeros((1, tc), F32)
        acc = lax.fori_loop(0, rows // rc, first, (zero,) * 8)
        for k in range(3):
            dwv_ref[k:k + 1, :] = acc[k]
            dwg_ref[k:k + 1, :] = acc[4 + k]
        dbv_ref[...] = acc[3]
        dbg_ref[...] = acc[7]

        def second(r, _):
            r0 = pl.multiple_of(r * rc, rc)
            rs = pl.ds(r0, rc)
            for s_ref, w_ref, o_ref in ((sv, wv_ref, dhv_ref), (sg, wg_ref, dhg_ref)):
                x = s_ref[rs, :]
                xp, xn = _shifted(x, *_edge_rows(s_ref, r0, rc, rows, SUB))
                o_ref[rs, :] = (w_ref[0:1, :] * xn + w_ref[1:2, :] * x + w_ref[2:3, :] * xp).astype(o_ref.dtype)
            return 0

        lax.fori_loop(0, rows // rc, second, 0)

    res = pl.pallas_call(
        body, name=name, grid=(f // tc,),
        in_specs=[val(rows), val(rows), gate(rows), val(3), gate(3), val(1), gate(1)],
        out_specs=[val(rows), val(rows), val(3), val(3), val(1), val(1)],
        out_shape=[jax.ShapeDtypeStruct((rows, f), ACT_DTYPE)] * 2
        + [jax.ShapeDtypeStruct((3, f), F32)] * 2 + [jax.ShapeDtypeStruct((1, f), F32)] * 2,
        scratch_shapes=[pltpu.VMEM((rows, tc), F32), pltpu.VMEM((rows, tc), F32)],
        compiler_params=_params(("parallel",)),
    )(da, h, h, cw, cw, cb, cb)
    return res


def _pool_band(name, x, transpose, out_dtype):
    rows, d = x.shape
    ng = len(POOL_WINDOWS)
    ch = d // ng
    tm = _tile(rows, 256, PACK)
    win = tm + 2 * POOL_HALO
    assert win <= rows

    def body(x_ref, o_ref):
        half = lax.shift_left(jnp.int32(1), pl.program_id(0))
        t0 = pl.program_id(1) * tm
        ws = pl.multiple_of(jnp.clip(t0 - POOL_HALO, 0, rows - win), PACK)
        i = t0 + lax.broadcasted_iota(jnp.int32, (tm, win), 0)
        j = ws + lax.broadcasted_iota(jnp.int32, (tm, win), 1)

        def inv_count(t):
            hi = jnp.minimum(t + half - 1, rows - 1)
            lo = jnp.maximum(t - half, 0)
            return 1.0 / (hi - lo + 1).astype(F32)

        xw = x_ref[pl.ds(ws, win), :]
        xt = x_ref[pl.ds(pl.multiple_of(t0, PACK), tm), :].astype(F32)
        if transpose:
            band = (j - half <= i) & (i <= j + half - 1)
            tw = ws + lax.broadcasted_iota(jnp.int32, (win, 1), 0)
            o = _dot(band.astype(MXU_DTYPE), xw.astype(F32) * inv_count(tw)) - xt
        else:
            band = (i - half <= j) & (j <= i + half - 1)
            tt = t0 + lax.broadcasted_iota(jnp.int32, (tm, 1), 0)
            o = _dot(band.astype(MXU_DTYPE), xw) * inv_count(tt) - xt
        o_ref[...] = o.astype(o_ref.dtype)

    return pl.pallas_call(
        body, name=name, grid=(ng, rows // tm),
        in_specs=[pl.BlockSpec((rows, ch), lambda g, i: (0, g))],
        out_specs=pl.BlockSpec((tm, ch), lambda g, i: (i, g)),
        out_shape=jax.ShapeDtypeStruct((rows, d), out_dtype),
        compiler_params=_params(("parallel", "arbitrary")),
    )(x)


def _ada_forward(c16, ada_w, ada_b):
    nl, d, cols = ada_w.shape
    tn = _tile(cols, 512)

    def body(c_ref, w_ref, b_ref, o_ref):
        cv = c_ref[...]
        o_ref[...] = _dot(cv * _sigmoid(cv), w_ref[...]) + b_ref[...]

    return pl.pallas_call(
        body, name="ada_forward", grid=(nl, cols // tn),
        in_specs=[pl.BlockSpec((16, d), lambda l, n: (0, 0)),
                  pl.BlockSpec((None, d, tn), lambda l, n: (l, 0, n)),
                  pl.BlockSpec((None, 1, tn), lambda l, n: (l, 0, n))],
        out_specs=pl.BlockSpec((None, 16, tn), lambda l, n: (l, 0, n)),
        out_shape=jax.ShapeDtypeStruct((nl, 16, cols), F32),
        compiler_params=_params(("parallel", "parallel")),
    )(c16, ada_w, ada_b)


def _ada_backward(c16, dmod, ada_w):
    nl, d, cols = ada_w.shape
    tn = _tile(cols, 512)
    nn = cols // tn

    def body(c_ref, g_ref, w_ref, gw_ref, gc_ref):
        cv = c_ref[...]
        s = _sigmoid(cv)
        gv = g_ref[...]
        gw_ref[...] = _dot(cv * s, gv, TN)
        dcond = _dot(gv, w_ref[...], NT)
        row = lax.broadcasted_iota(jnp.int32, dcond.shape, 0)
        dctx = jnp.sum(jnp.where(row >= 8, dcond * (s * (1.0 + cv * (1.0 - s))), 0.0), axis=0, keepdims=True)

        @pl.when((pl.program_id(0) == 0) & (pl.program_id(1) == 0))
        def _():
            gc_ref[...] = jnp.zeros_like(gc_ref)

        gc_ref[...] += dctx

    return pl.pallas_call(
        body, name="ada_backward", grid=(nl, nn),
        in_specs=[pl.BlockSpec((16, d), lambda l, n: (0, 0)),
                  pl.BlockSpec((None, 16, tn), lambda l, n: (l, 0, n)),
                  pl.BlockSpec((None, d, tn), lambda l, n: (l, 0, n))],
        out_specs=[pl.BlockSpec((None, d, tn), lambda l, n: (l, 0, n)),
                   pl.BlockSpec((1, d), lambda l, n: (0, 0))],
        out_shape=[jax.ShapeDtypeStruct((nl, d, cols), F32), jax.ShapeDtypeStruct((1, d), F32)],
        compiler_params=_params(("arbitrary", "arbitrary")),
    )(c16, dmod, ada_w)


def _row_sum16(name, a):
    nl, _, w = a.shape
    tn = _tile(w, 4096)

    def body(a_ref, o_ref):
        o_ref[...] = jnp.sum(a_ref[...], axis=0, keepdims=True)

    return pl.pallas_call(
        body, name=name, grid=(nl, w // tn),
        in_specs=[pl.BlockSpec((None, 16, tn), lambda l, n: (l, 0, n))],
        out_specs=pl.BlockSpec((None, 1, tn), lambda l, n: (l, 0, n)),
        out_shape=jax.ShapeDtypeStruct((nl, 1, w), F32),
        compiler_params=_params(("parallel", "parallel")),
    )(a)


def _pack(arrays, row_align):
    flat = jnp.concatenate([a.reshape(-1).astype(F32) for a in arrays])
    quantum = row_align * LANE
    padded = -(-flat.shape[0] // quantum) * quantum
    return jnp.pad(flat, (0, padded - flat.shape[0])).reshape(-1, LANE)


def _unpack(packed, shapes):
    flat = packed.reshape(-1)
    out, off = [], 0
    for s in shapes:
        n = math.prod(s)
        out.append(flat[off:off + n].reshape(s))
        off += n
    return out


def _grid_pos_emb(n_tokens, dim):
    rows = n_tokens // GRID_W
    r, col = jnp.meshgrid(jnp.arange(rows, dtype=F32), jnp.arange(GRID_W, dtype=F32), indexing="ij")
    quarter = dim // 4
    omega = 1.0 / (POS_BASE ** (jnp.arange(quarter, dtype=F32) / quarter))

    def enc(p):
        ang = p.reshape(-1, 1) * omega[None, :]
        return jnp.concatenate([jnp.sin(ang), jnp.cos(ang)], axis=-1)

    return jnp.concatenate([enc(r), enc(col)], axis=-1)


def _ffn_forward(tag, v, x_in, up4, dn4, layer, cw, cb):
    h = _mm_cols(f"ffn_up_{tag}", v, up4, layer, ACT_DTYPE)
    a = _conv_swiglu_fwd(f"ffn_conv_{tag}", h, cw, cb)
    f = _mm_rows(f"ffn_down_{tag}", a, dn4, layer, F32)
    return h, a, f


def _ffn_backward(tag, dx_out, fb, x_mid, v, h, a, up4, dn4, layer, cw, cb, gate5, gam3, gam2, scale4p1):
    def post(dxo, f, gam, gate):
        dy, dgate, dgam = _postnorm_bwd(dxo, f.astype(F32), gam, gate)
        return (dy,), (dgate, dgam)

    (df,), (dgate5, dgam3) = _rowwise(f"ffn_post_bwd_{tag}", post, dx_out.shape[0],
                                      [(dx_out, False), (fb, False)], [gam3, gate5],
                                      [(dx_out.shape[1], ACT_DTYPE, False)], [dx_out.shape[1]] * 2)
    da = _mm_rows_nt(f"ffn_down_dx_{tag}", df, dn4, layer, ACT_DTYPE)
    g_dn = _mm_rows_tn(f"ffn_down_dw_{tag}", a, df, dn4.shape[-2], ACT_DTYPE)
    dhv, dhg, dwv, dwg, dbv, dbg = _conv_swiglu_bwd(f"ffn_conv_bwd_{tag}", da, h, cw, cb)
    dh = jnp.concatenate([dhv, dhg], axis=1)
    dcw = jnp.concatenate([dwv, dwg], axis=1)
    dcb = jnp.concatenate([dbv, dbg], axis=1)
    dv = _mm_cols_nt(f"ffn_up_dx_{tag}", dh, up4, layer, F32)
    g_up = _mm_cols_tn(f"ffn_up_dw_{tag}", v, dh, up4.shape[-1], ACT_DTYPE)

    def pre(dvv, x, dxo, gam, s1):
        dx, dshift, dscale, dgam = _prenorm_bwd(dvv, x, gam, s1)
        return (dxo + dx,), (dshift, dscale, dgam)

    d = dx_out.shape[1]
    (dx_mid,), (dshift3, dscale4, dgam2) = _rowwise(
        f"ffn_pre_bwd_{tag}", pre, dx_out.shape[0], [(dv, False), (x_mid, False), (dx_out, False)],
        [gam2, scale4p1], [(d, F32, False)], [d] * 3)
    return dx_mid, g_up, g_dn, dcw, dcb, (dshift3, dscale4, dgate5), (dgam2, dgam3)


def kernel(x, c, ctx, c_ctx, ada_w, ada_b, norm_g, s5_lam_re, s5_lam_im, s5_log_step, s5_b_re, s5_b_im, s5_c_re, s5_c_im, s5_d, s5_glu_w, pool_w, pool_scale, ffn_up, ffn_conv, ffn_conv_b, ffn_down, loss_target, m_c_ctx, m_ada_w, m_ada_b, m_norm_g, m_s5_lam_re, m_s5_lam_im, m_s5_log_step, m_s5_b_re, m_s5_b_im, m_s5_c_re, m_s5_c_im, m_s5_d, m_s5_glu_w, m_pool_w, m_pool_scale, m_ffn_up, m_ffn_conv, m_ffn_conv_b, m_ffn_down, v_c_ctx, v_ada_w, v_ada_b, v_norm_g, v_s5_lam_re, v_s5_lam_im, v_s5_log_step, v_s5_b_re, v_s5_b_im, v_s5_c_re, v_s5_c_im, v_s5_d, v_s5_glu_w, v_pool_w, v_pool_scale, v_ffn_up, v_ffn_conv, v_ffn_conv_b, v_ffn_down):
    ix, iy, ic = _coords()
    chip = 2 * ix + iy
    me = 2 * chip + ic
    _, rows, d = x.shape
    rows_c = ctx.shape[1]
    nl = ada_w.shape[0]
    assert nl == 2 and s5_glu_w.shape[0] == 1 and pool_w.shape[0] == 1
    a_cols = ada_w.shape[2]
    f2s = ffn_up.shape[2]
    f2 = N_CHIP * f2s
    ds = d // N_CHIP
    ng = len(POOL_WINDOWS)
    ch = d // ng
    ps = pool_w.shape[2]

    def to_bf16(name, w):
        return _elementwise(name, lambda a: (a,), [_view2d(w)], [MXU_DTYPE])[0].reshape(w.shape)

    up4, dn4, glu4, pool4 = _gather_chips([
        to_bf16("cast_up", ffn_up), to_bf16("cast_down", ffn_down),
        to_bf16("cast_glu", s5_glu_w[0]), to_bf16("cast_pool", pool_w[0])])
    pool_full = pool4.transpose(1, 0, 2, 3).reshape(ng, ch, ch)

    c_all = _all_gather8("gather_cond", _pack([c], SUB))
    c_all = c_all.reshape(N_DEV, -1)[:, :d]
    c16 = jnp.concatenate([c_all, jnp.broadcast_to(c_ctx[None, :], (N_DEV, d))], axis=0)
    ada_b_mine = lax.dynamic_slice_in_dim(ada_b, chip * a_cols, a_cols, axis=1)
    mod_part = _ada_forward(c16, ada_w, ada_b_mine[:, None, :])
    narrow_shapes = [mod_part.shape, norm_g.shape, pool_scale.shape, ffn_conv.shape]
    narrow = _all_gather8("gather_narrow", _pack([mod_part, norm_g, pool_scale, ffn_conv], SUB))
    per_chip = [_unpack(narrow[2 * s], narrow_shapes) for s in range(N_CHIP)]
    mod_all = jnp.concatenate([p[0] for p in per_chip], axis=-1)
    gam = jnp.concatenate([p[1] for p in per_chip], axis=-1)
    pscale = jnp.concatenate([p[2] for p in per_chip], axis=-1)
    conv_w = jnp.concatenate([p[3] for p in per_chip], axis=-1)
    mod_mine = lax.dynamic_index_in_dim(mod_all, me, axis=1, keepdims=False)
    mod_ctx = mod_all[0, N_DEV]

    def mods(vec):
        s0, s1, g2, s3, s4, g5 = [vec[k * d:(k + 1) * d][None, :] for k in range(N_MOD)]
        return s0, 1.0 + s1, g2, s3, 1.0 + s4, g5

    m0, m1, mc = mods(mod_mine[0]), mods(mod_mine[1]), mods(mod_ctx)
    gains = [[gam[l, k][None, :] for k in range(4)] for l in range(nl)]
    conv_b = ffn_conv_b[:, None, :]

    pos = _grid_pos_emb(rows, d)

    def init(xv, pv, g0, shift, s1):
        x0 = xv + pv
        return (x0, _prenorm(x0, g0, shift, s1)), ()

    (x0, u), _ = _rowwise("init", init, rows, [(x[0], False), (pos, False)], [gains[0][0], m0[0], m0[1]],
                          [(d, F32, False), (d, ACT_DTYPE, True)])
    (uc,), _ = _rowwise("ctx_prenorm", lambda cv, g0, shift, s1: ((_prenorm(cv, g0, shift, s1),), ()),
                        rows_c, [(ctx[0], False)], [gains[0][0], mc[0], mc[1]], [(d, ACT_DTYPE, True)])
    s5_params = (s5_lam_re[0], s5_lam_im[0], s5_log_step[0], s5_b_re[0], s5_b_im[0], s5_c_re[0], s5_c_im[0])
    (bblk, cblk, atile), s5_vjp = jax.vjp(_s5_prepare, *s5_params)
    y = _s5_forward(u, uc, bblk, cblk, atile, s5_d)
    (z,), _ = _rowwise("gelu", lambda yv: ((_gelu(yv),), ()), rows, [(y, True)], [], [(d, ACT_DTYPE, False)])
    zz = _mm_cols("glu_proj", z, glu4[:, None], 0, ACT_DTYPE)

    def glu_out(zzv, xv, gate2, g1, g2, shift3, s4):
        zf = zzv.astype(F32)
        o = zf[:, :d] * _sigmoid(zf[:, d:])
        x1 = xv + gate2 * (o * _rstd(o) * g1)
        return (x1, _prenorm(x1, g2, shift3, s4)), ()

    (x1, v0), _ = _rowwise("glu_resid", glu_out, rows, [(zz, False), (x0, False)],
                           [m0[2], gains[0][1], gains[0][2], m0[3], m0[4]], [(d, F32, False), (d, ACT_DTYPE, False)])
    h0, a0, f0 = _ffn_forward("l0", v0, x1, up4, dn4, 0, conv_w[0], conv_b[0])

    def ffn_out(fv, xv, gate5, g3, g0n, shift0, s1):
        x2 = xv + gate5 * (fv * _rstd(fv) * g3)
        return (x2, _prenorm(x2, g0n, shift0, s1), fv), ()

    (x2, u1, fb0), _ = _rowwise("ffn_resid_l0", ffn_out, rows, [(f0, False), (x1, False)],
                                [m0[5], gains[0][3], gains[1][0], m1[0], m1[1]],
                                [(d, F32, False), (d, ACT_DTYPE, False), (d, ACT_DTYPE, False)])

    p1 = _pool_band("pool_band", u1, False, ACT_DTYPE)
    yr = _mm_grp("pool_proj", p1, pool_full, NN, F32)

    def pool_out(yv, xv, ps_, gate2, g1, g2, shift3, s4):
        o = yv * ps_
        x1n = xv + gate2 * (o * _rstd(o) * g1)
        return (x1n, _prenorm(x1n, g2, shift3, s4), yv), ()

    (x3, v1, yb), _ = _rowwise("pool_resid", pool_out, rows, [(yr, False), (x2, False)],
                               [pscale, m1[2], gains[1][1], gains[1][2], m1[3], m1[4]],
                               [(d, F32, False), (d, ACT_DTYPE, False), (d, ACT_DTYPE, False)])
    h1, a1, f1 = _ffn_forward("l1", v1, x3, up4, dn4, 1, conv_w[1], conv_b[1])

    def loss_head(fv, xv, tv, gate5, g3):
        err = xv + gate5 * (fv * _rstd(fv) * g3) - tv
        return (err * (1.0 / d), fv), (err * err,)

    (dx4, fb1), (sq,) = _rowwise("loss_head", loss_head, rows, [(f1, False), (x3, False), (loss_target[0], False)],
                                 [m1[5], gains[1][3]], [(d, F32, False), (d, ACT_DTYPE, False)], [d])
    loss = lax.psum(0.5 * jnp.sum(sq) / d, ("x", "y", "c"))

    dx3, g_up1, g_dn1, dcw1, dcb1, dmod_ffn1, (dgam12, dgam13) = _ffn_backward(
        "l1", dx4, fb1, x3, v1, h1, a1, up4, dn4, 1, conv_w[1], conv_b[1], m1[5], gains[1][3], gains[1][2], m1[4])

    def pool_post(dxo, yv, ps_, g1, gate2):
        yraw = yv.astype(F32)
        dy, dgate, dgam = _postnorm_bwd(dxo, yraw * ps_, g1, gate2)
        return (dy * ps_,), (dgate, dgam, dy * yraw)

    (dyr,), (dgate2_1, dgam11, dpscale) = _rowwise("pool_post_bwd", pool_post, rows, [(dx3, False), (yb, False)],
                                                   [pscale, gains[1][1], m1[2]], [(d, ACT_DTYPE, False)], [d] * 3)
    dp1 = _mm_grp("pool_proj_dx", dyr, pool_full, NT, ACT_DTYPE)
    g_pool = _mm_grp_tn("pool_proj_dw", p1, dyr, ng, ACT_DTYPE)
    du1 = _pool_band("pool_band_bwd", dp1, True, F32)

    def pre_bwd(duv, xv, dxo, g0, s1):
        dx, dshift, dscale, dgam = _prenorm_bwd(duv, xv, g0, s1)
        return (dxo + dx,), (dshift, dscale, dgam)

    (dx2,), (dshift0_1, dscale1_1, dgam10) = _rowwise(
        "pool_pre_bwd", pre_bwd, rows, [(du1, False), (x2, False), (dx3, False)],
        [gains[1][0], m1[1]], [(d, F32, False)], [d] * 3)

    dx1, g_up0, g_dn0, dcw0, dcb0, dmod_ffn0, (dgam02, dgam03) = _ffn_backward(
        "l0", dx2, fb0, x1, v0, h0, a0, up4, dn4, 0, conv_w[0], conv_b[0], m0[5], gains[0][3], gains[0][2], m0[4])

    def glu_post(dxo, zzv, g1, gate2):
        zf = zzv.astype(F32)
        val, s = zf[:, :d], _sigmoid(zf[:, d:])
        do, dgate, dgam = _postnorm_bwd(dxo, val * s, g1, gate2)
        return (jnp.concatenate([do * s, do * val * (s * (1.0 - s))], axis=1),), (dgate, dgam)

    (dzz,), (dgate2_0, dgam01) = _rowwise("glu_post_bwd", glu_post, rows, [(dx1, False), (zz, False)],
                                          [gains[0][1], m0[2]], [(2 * d, ACT_DTYPE, False)], [d] * 2)
    dz = _mm_cols_nt("glu_proj_dx", dzz, glu4[:, None], 0, F32)
    g_glu = _mm_cols_tn("glu_proj_dw", z, dzz, glu4.shape[-1], ACT_DTYPE)
    (dy,), _ = _rowwise("gelu_bwd", lambda dzv, yv: ((dzv * _gelu_grad(yv),), ()), rows,
                        [(dz, False), (y, True)], [], [(d, ACT_DTYPE, True)])
    du0, duc, d_bblk, d_cblk, d_atile, d_dsk = _s5_backward(u, uc, dy, bblk, cblk, atile, s5_d)
    (gx,), (dshift0_0, dscale1_0, dgam00) = _rowwise(
        "s5_pre_bwd", pre_bwd, rows, [(du0, True), (x0, False), (dx1, False)],
        [gains[0][0], m0[1]], [(d, F32, False)], [d] * 3)

    def ctx_bwd(duv, cv, g0, s1):
        _, dshift, dscale, dgam = _prenorm_bwd(duv, cv, g0, s1)
        return (), (dshift, dscale, dgam)

    _, (dshift_c, dscale_c, dgam00c) = _rowwise("ctx_pre_bwd", ctx_bwd, rows_c, [(duc, True), (ctx[0], False)],
                                                [gains[0][0], mc[1]], [], [d] * 3)
    g_s5 = s5_vjp((d_bblk, d_cblk, d_atile))

    zero_d = jnp.zeros((1, d), F32)
    dmod_lat = jnp.stack([
        jnp.concatenate([dshift0_0, dscale1_0, dgate2_0, *dmod_ffn0], axis=1),
        jnp.concatenate([dshift0_1, dscale1_1, dgate2_1, *dmod_ffn1], axis=1)])
    dmod_ctx = jnp.stack([jnp.concatenate([dshift_c, dscale_c] + [zero_d] * 4, axis=1),
                          jnp.zeros((1, N_MOD * d), F32)])
    dmod_shape = (nl, 2, N_MOD * d)
    dmod_all = _all_gather8("gather_dmod", _pack([jnp.concatenate([dmod_lat, dmod_ctx], axis=1)], SUB))
    dmod_all = jnp.stack([_unpack(dmod_all[k], [dmod_shape])[0] for k in range(N_DEV)])
    dmod16 = jnp.concatenate([dmod_all[:, :, 0], dmod_all[:, :, 1]], axis=0).transpose(1, 0, 2)
    dmod16_mine = lax.dynamic_slice_in_dim(dmod16, chip * a_cols, a_cols, axis=2)
    g_ada_w, g_cctx_part = _ada_backward(c16, dmod16_mine, ada_w)
    g_ada_b = _row_sum16("ada_bias_grad", dmod16)[:, 0]

    d_gam = jnp.stack([jnp.concatenate([dgam00 + dgam00c, dgam01, dgam02, dgam03], axis=0),
                       jnp.concatenate([dgam10, dgam11, dgam12, dgam13], axis=0)])
    small = [d_gam, 0.5 * g_cctx_part, *g_s5, jnp.sum(d_dsk, axis=0, keepdims=True), dpscale,
             jnp.stack([dcw0, dcw1]), jnp.stack([dcb0[0], dcb1[0]])]
    small_shapes = [s.shape for s in small]
    packed = _pack(small, N_DEV * SUB)
    summed = _all_reduce8("reduce_small", packed.reshape(N_DEV, -1, LANE)).reshape(-1, LANE)
    (r_gam, r_cctx, r_lam_re, r_lam_im, r_log_step, r_b_re, r_b_im, r_c_re, r_c_im,
     r_dsk, r_pscale, r_conv, r_convb) = _unpack(summed, small_shapes)
    g_norm = lax.dynamic_slice_in_dim(r_gam, chip * ds, ds, axis=2)
    g_pscale = lax.dynamic_slice_in_dim(r_pscale, chip * ds, ds, axis=1)
    g_conv = lax.dynamic_slice_in_dim(r_conv, chip * f2s, f2s, axis=2)

    g_pool4 = g_pool.reshape(ng, N_CHIP, ps, ch).transpose(1, 0, 2, 3).reshape(N_CHIP, ng * ps, ch)
    g_up, g_dn, g_glu_f, g_pool_f = _reduce_grads(
        [g_up0, g_up1, g_dn0, g_dn1, g_glu, g_pool4],
        [ffn_up.shape, ffn_down.shape, s5_glu_w.shape[1:], (ng * ps, ch)],
        [(0, 0), (0, 1), (1, 0), (1, 1), (2, None), (3, None)])

    grads = {
        "c_ctx": r_cctx[0], "ada_w": g_ada_w, "ada_b": g_ada_b, "norm_g": g_norm,
        "s5_lam_re": r_lam_re[None], "s5_lam_im": r_lam_im[None], "s5_log_step": r_log_step[None],
        "s5_b_re": r_b_re[None], "s5_b_im": r_b_im[None], "s5_c_re": r_c_re[None], "s5_c_im": r_c_im[None],
        "s5_d": r_dsk, "s5_glu_w": g_glu_f[None], "pool_w": g_pool_f.reshape(pool_w.shape),
        "pool_scale": g_pscale, "ffn_up": g_up, "ffn_conv": g_conv, "ffn_conv_b": r_convb, "ffn_down": g_dn,
    }
    weights = {
        "c_ctx": (c_ctx, m_c_ctx, v_c_ctx), "ada_w": (ada_w, m_ada_w, v_ada_w), "ada_b": (ada_b, m_ada_b, v_ada_b),
        "norm_g": (norm_g, m_norm_g, v_norm_g), "s5_lam_re": (s5_lam_re, m_s5_lam_re, v_s5_lam_re),
        "s5_lam_im": (s5_lam_im, m_s5_lam_im, v_s5_lam_im), "s5_log_step": (s5_log_step, m_s5_log_step, v_s5_log_step),
        "s5_b_re": (s5_b_re, m_s5_b_re, v_s5_b_re), "s5_b_im": (s5_b_im, m_s5_b_im, v_s5_b_im),
        "s5_c_re": (s5_c_re, m_s5_c_re, v_s5_c_re), "s5_c_im": (s5_c_im, m_s5_c_im, v_s5_c_im),
        "s5_d": (s5_d, m_s5_d, v_s5_d), "s5_glu_w": (s5_glu_w, m_s5_glu_w, v_s5_glu_w),
        "pool_w": (pool_w, m_pool_w, v_pool_w), "pool_scale": (pool_scale, m_pool_scale, v_pool_scale),
        "ffn_up": (ffn_up, m_ffn_up, v_ffn_up), "ffn_conv": (ffn_conv, m_ffn_conv, v_ffn_conv),
        "ffn_conv_b": (ffn_conv_b, m_ffn_conv_b, v_ffn_conv_b), "ffn_down": (ffn_down, m_ffn_down, v_ffn_down),
    }
    names = list(weights)
    large = ("ada_w", "s5_glu_w", "pool_w", "ffn_up", "ffn_down")
    delta, new_m, new_v = {}, {}, {}
    for n in large:
        w, m, v = weights[n]
        delta[n], new_m[n], new_v[n] = _adamw(f"adamw_{n}", w, grads[n], m, v)
    rest = [n for n in names if n not in large]
    rest_shapes = [weights[n][0].shape for n in rest]
    packs = [_pack([weights[n][k] for n in rest], 1024) for k in range(3)]
    packs.insert(1, _pack([grads[n] for n in rest], 1024))
    for out, res in zip((delta, new_m, new_v), _adamw("adamw_small", *packs)):
        for n, val in zip(rest, _unpack(res, rest_shapes)):
            out[n] = val

    return (loss, gx[None], *[grads[n] for n in names], *[delta[n] for n in names],
            *[new_m[n] for n in names], *[new_v[n] for n in names])
```

```python
import math

import jax
import jax.numpy as jnp
from jax import lax
from jax.experimental import pallas as pl
from jax.experimental.pallas import tpu as pltpu

F32 = jnp.float32
MXU_DTYPE = jnp.bfloat16
ACT_DTYPE = jnp.bfloat16

LANE = 128
SUB = 8
PACK = 16
VMEM_LIMIT = 56 * 1024 * 1024
ELEMWISE_BLOCK = 1 << 18

N_DEV = 8
N_CHIP = 4
N_SEG = SUB
S5_GROUP = 16
S5_STATE = 64
S5_CB = LANE
S5_H = (S5_CB // S5_GROUP) * S5_STATE
S5_NS = 2 * S5_H
POOL_WINDOWS = (2, 4, 8, 16)
POOL_HALO = 16
GRID_W = 64
POS_BASE = 10000.0
RMS_EPS = 1e-6
N_MOD = 6

ADAM_LR = 0.001
ADAM_B1 = 0.9
ADAM_B2 = 0.999
ADAM_EPS = 1e-08
ADAM_WD = 0.01
ADAM_STEP = 10

NN = (((1,), (0,)), ((), ()))
NT = (((1,), (1,)), ((), ()))
TN = (((0,), (0,)), ((), ()))
MESH = pl.DeviceIdType.MESH


def _tile(n, cap, align=LANE):
    best = None
    for t in range(align, min(n, cap) + 1, align):
        if n % t == 0:
            best = t
    return n if best is None else best


def _params(sem=None):
    return pltpu.CompilerParams(dimension_semantics=sem, vmem_limit_bytes=VMEM_LIMIT)


def _dot(a, b, dims=NN):
    return lax.dot_general(a.astype(MXU_DTYPE), b.astype(MXU_DTYPE), dims, preferred_element_type=F32)


def _sigmoid(x):
    return 1.0 / (1.0 + jnp.exp(-x))


_GELU_C = math.sqrt(2.0 / math.pi)
_GELU_K = 0.044715


def _gelu(x):
    return 0.5 * x * (1.0 + jnp.tanh(_GELU_C * (x + _GELU_K * x * x * x)))


def _gelu_grad(x):
    t = jnp.tanh(_GELU_C * (x + _GELU_K * x * x * x))
    return 0.5 * (1.0 + t) + 0.5 * x * (1.0 - t * t) * _GELU_C * (1.0 + 3.0 * _GELU_K * x * x)


def _rstd(x):
    return lax.rsqrt(jnp.mean(x * x, axis=-1, keepdims=True) + RMS_EPS)


def _norm_bwd(dxh, xh, r):
    return r * (dxh - xh * jnp.mean(dxh * xh, axis=-1, keepdims=True))


def _rowwise(name, fn, rows, tiled, vecs, outs, accs=()):
    seg = rows // N_SEG
    tm = _tile(seg, 256, SUB)
    nt, ntp = rows // tm, seg // tm
    n_t, n_v, n_o, n_a = len(tiled), len(vecs), len(outs), len(accs)

    def spec(width, perm):
        if perm:
            return pl.BlockSpec((tm, width), lambda i: (i % ntp, i // ntp))
        return pl.BlockSpec((tm, width), lambda i: (i, 0))

    args, in_specs = [], []
    for arr, perm in tiled:
        width = arr.shape[-1]
        args.append(arr.reshape(seg, N_SEG * width) if perm else arr)
        in_specs.append(spec(width, perm))
    for v in vecs:
        args.append(v)
        in_specs.append(pl.BlockSpec(v.shape, lambda i: (0, 0)))
    out_shape, out_specs = [], []
    for width, dtype, perm in outs:
        out_shape.append(jax.ShapeDtypeStruct((seg, N_SEG * width) if perm else (rows, width), dtype))
        out_specs.append(spec(width, perm))
    for width in accs:
        out_shape.append(jax.ShapeDtypeStruct((SUB, width), F32))
        out_specs.append(pl.BlockSpec((SUB, width), lambda i: (0, 0)))

    def body(*refs):
        vals = [r[...] for r in refs[:n_t + n_v]]
        o_refs = refs[n_t + n_v:n_t + n_v + n_o]
        a_refs = refs[n_t + n_v + n_o:]
        o_vals, a_vals = fn(*vals)
        for r, v in zip(o_refs, o_vals):
            r[...] = v.astype(r.dtype)
        if n_a:
            @pl.when(pl.program_id(0) == 0)
            def _():
                for r in a_refs:
                    r[...] = jnp.zeros_like(r)
            for r, v in zip(a_refs, a_vals):
                r[...] += v.reshape(tm // SUB, SUB, v.shape[-1]).sum(axis=0)

    res = pl.pallas_call(
        body, name=name, grid=(nt,), in_specs=in_specs, out_specs=out_specs, out_shape=out_shape,
        compiler_params=_params(("arbitrary",)),
    )(*args)
    res = list(res)
    for k, (width, _, perm) in enumerate(outs):
        if perm:
            res[k] = res[k].reshape(rows, width)
    return res[:n_o], [jnp.sum(a, axis=0, keepdims=True) for a in res[n_o:]]


def _prenorm(x, gam, shift, scale1):
    r = _rstd(x)
    return (x * r) * gam * scale1 + shift


def _prenorm_bwd(du, x, gam, scale1):
    r = _rstd(x)
    xh = x * r
    dxn = du * scale1
    dx = _norm_bwd(dxn * gam, xh, r)
    return dx, du, du * (xh * gam), dxn * xh


def _postnorm_bwd(dxo, y, gam, gate):
    r = _rstd(y)
    yh = y * r
    dyn = dxo * gate
    dy = _norm_bwd(dyn * gam, yh, r)
    return dy, dxo * (yh * gam), dyn * yh


def _matmul(name, a, b, dims, grid, a_spec, b_spec, o_spec, out_shape, out_dtype, acc_shape):
    nk = grid[2]

    def body(a_ref, b_ref, o_ref, *scratch):
        part = _dot(a_ref[...], b_ref[...], dims)
        if nk == 1:
            o_ref[...] = part.astype(o_ref.dtype)
        else:
            acc_ref, = scratch
            k = pl.program_id(2)

            @pl.when(k == 0)
            def _():
                acc_ref[...] = part

            @pl.when(k > 0)
            def _():
                acc_ref[...] += part

            @pl.when(k == nk - 1)
            def _():
                o_ref[...] = acc_ref[...].astype(o_ref.dtype)

    return pl.pallas_call(
        body, name=name, grid=grid, in_specs=[a_spec, b_spec], out_specs=o_spec,
        out_shape=jax.ShapeDtypeStruct(out_shape, out_dtype),
        scratch_shapes=[] if nk == 1 else [pltpu.VMEM(acc_shape, F32)],
        compiler_params=_params(("parallel", "parallel", "arbitrary")),
    )(a, b)


def _mm_cols(name, a, w4, layer, out_dtype):
    m, k = a.shape
    ns = w4.shape[-1]
    tm, tn = _tile(m, 1024), _tile(ns, 1536)
    nps = ns // tn
    return _matmul(
        name, a, w4, NN, (m // tm, N_CHIP * nps, 1),
        pl.BlockSpec((tm, k), lambda i, n, kk: (i, 0)),
        pl.BlockSpec((None, None, k, tn), lambda i, n, kk: (n // nps, layer, 0, n % nps)),
        pl.BlockSpec((tm, tn), lambda i, n, kk: (i, n)),
        (m, N_CHIP * ns), out_dtype, None)


def _mm_cols_nt(name, g, w4, layer, out_dtype):
    m = g.shape[0]
    k, ns = w4.shape[-2:]
    tm, tk = _tile(m, 512), _tile(ns, 1536)
    kps = ns // tk
    return _matmul(
        name, g, w4, NT, (m // tm, 1, N_CHIP * kps),
        pl.BlockSpec((tm, tk), lambda i, n, kk: (i, kk)),
        pl.BlockSpec((None, None, k, tk), lambda i, n, kk: (kk // kps, layer, 0, kk % kps)),
        pl.BlockSpec((tm, k), lambda i, n, kk: (i, 0)),
        (m, k), out_dtype, (tm, k))


def _mm_cols_tn(name, a, g, ns, out_dtype):
    m, k = a.shape
    tkm, tmk, tn = _tile(m, 512), _tile(k, 1024), _tile(ns, 1536)
    nps = ns // tn
    return _matmul(
        name, a.T, g, NN, (k // tmk, N_CHIP * nps, m // tkm),
        pl.BlockSpec((tmk, tkm), lambda i, n, kk: (i, kk)),
        pl.BlockSpec((tkm, tn), lambda i, n, kk: (kk, n)),
        pl.BlockSpec((None, tmk, tn), lambda i, n, kk: (n // nps, i, n % nps)),
        (N_CHIP, k, ns), out_dtype, (tmk, tn))


def _mm_rows(name, a, w4, layer, out_dtype):
    m = a.shape[0]
    rs, n = w4.shape[-2:]
    tm, tk = _tile(m, 512), _tile(rs, 1536)
    kps = rs // tk
    return _matmul(
        name, a, w4, NN, (m // tm, 1, N_CHIP * kps),
        pl.BlockSpec((tm, tk), lambda i, j, kk: (i, kk)),
        pl.BlockSpec((None, None, tk, n), lambda i, j, kk: (kk // kps, layer, kk % kps, 0)),
        pl.BlockSpec((tm, n), lambda i, j, kk: (i, 0)),
        (m, n), out_dtype, (tm, n))


def _mm_rows_nt(name, g, w4, layer, out_dtype):
    m, n = g.shape
    rs = w4.shape[-2]
    tm, tn = _tile(m, 1024), _tile(rs, 1536)
    nps = rs // tn
    return _matmul(
        name, g, w4, NT, (m // tm, N_CHIP * nps, 1),
        pl.BlockSpec((tm, n), lambda i, j, kk: (i, 0)),
        pl.BlockSpec((None, None, tn, n), lambda i, j, kk: (j // nps, layer, j % nps, 0)),
        pl.BlockSpec((tm, tn), lambda i, j, kk: (i, j)),
        (m, N_CHIP * rs), out_dtype, None)


def _mm_rows_tn(name, a, g, rs, out_dtype):
    m = a.shape[0]
    n = g.shape[1]
    tkm, tmr, tn = _tile(m, 512), _tile(rs, 1536), _tile(n, 1024)
    mps = rs // tmr
    return _matmul(
        name, a.T, g, NN, (N_CHIP * mps, n // tn, m // tkm),
        pl.BlockSpec((tmr, tkm), lambda i, j, kk: (i, kk)),
        pl.BlockSpec((tkm, tn), lambda i, j, kk: (kk, j)),
        pl.BlockSpec((None, tmr, tn), lambda i, j, kk: (i // mps, i % mps, j)),
        (N_CHIP, rs, n), out_dtype, (tmr, tn))


def _mm_grp(name, a, w, dims, out_dtype):
    m = a.shape[0]
    ng, ch = w.shape[:2]
    tm = _tile(m, 1024)
    return _matmul(
        name, a, w, dims, (m // tm, ng, 1),
        pl.BlockSpec((tm, ch), lambda i, g, kk: (i, g)),
        pl.BlockSpec((None, ch, ch), lambda i, g, kk: (g, 0, 0)),
        pl.BlockSpec((tm, ch), lambda i, g, kk: (i, g)),
        (m, ng * ch), out_dtype, None)


def _mm_grp_tn(name, a, g, ng, out_dtype):
    m = a.shape[0]
    ch = a.shape[1] // ng
    tk = _tile(m, 1024)
    return _matmul(
        name, a.T, g, NN, (ng, 1, m // tk),
        pl.BlockSpec((ch, tk), lambda i, j, kk: (i, kk)),
        pl.BlockSpec((tk, ch), lambda i, j, kk: (kk, i)),
        pl.BlockSpec((None, ch, ch), lambda i, j, kk: (i, 0, 0)),
        (ng, ch, ch), out_dtype, (ch, ch))


def _view2d(a):
    return a.reshape(-1, a.shape[-1])


def _elementwise(name, fn, ins, out_dtypes):
    r, c = ins[0].shape
    tr = _tile(r, max(PACK, ELEMWISE_BLOCK // c), PACK)
    spec = pl.BlockSpec((tr, c), lambda i: (i, 0))

    def body(*refs):
        outs = fn(*[x[...] for x in refs[:len(ins)]])
        for o_ref, o in zip(refs[len(ins):], outs):
            o_ref[...] = o.astype(o_ref.dtype)

    return pl.pallas_call(
        body, name=name, grid=(r // tr,), in_specs=[spec] * len(ins), out_specs=[spec] * len(out_dtypes),
        out_shape=[jax.ShapeDtypeStruct((r, c), d) for d in out_dtypes],
        compiler_params=_params(("parallel",)),
    )(*ins)


def _adamw_math(w, g, m, v):
    m = ADAM_B1 * m + (1.0 - ADAM_B1) * g
    v = ADAM_B2 * v + (1.0 - ADAM_B2) * (g * g)
    m_hat = m / (1.0 - ADAM_B1 ** ADAM_STEP)
    v_hat = v / (1.0 - ADAM_B2 ** ADAM_STEP)
    delta = -ADAM_LR * (m_hat / (jnp.sqrt(v_hat) + ADAM_EPS) + ADAM_WD * w)
    return delta, m, v


def _adamw(name, w, g, m, v):
    shape = w.shape
    outs = _elementwise(name, _adamw_math, [_view2d(w), _view2d(g), _view2d(m), _view2d(v)], [F32, F32, F32])
    return [o.reshape(shape) for o in outs]


def _coords():
    return lax.axis_index("x"), lax.axis_index("y"), lax.axis_index("c")


def _peer(x, y, c, k):
    return (x ^ (k >> 2), y ^ ((k >> 1) & 1), c ^ (k & 1))


def _all_gather8(name, block):
    r = block.shape[0]

    def body(x_ref, out_ref, send_sems, recv_sems):
        x, y, c = _coords()
        me = 4 * x + 2 * y + c
        out_ref[me] = x_ref[...]
        copies = []
        for k in range(1, N_DEV):
            cp = pltpu.make_async_remote_copy(
                src_ref=x_ref, dst_ref=out_ref.at[me], send_sem=send_sems.at[k], recv_sem=recv_sems.at[k],
                device_id=_peer(x, y, c, k), device_id_type=MESH)
            cp.start()
            copies.append(cp)
        for cp in copies:
            cp.wait()

    return pl.pallas_call(
        body, name=name,
        in_specs=[pl.BlockSpec(memory_space=pltpu.VMEM)], out_specs=pl.BlockSpec(memory_space=pltpu.VMEM),
        out_shape=jax.ShapeDtypeStruct((N_DEV, r, LANE), F32),
        scratch_shapes=[pltpu.SemaphoreType.DMA((N_DEV,)), pltpu.SemaphoreType.DMA((N_DEV,))],
        compiler_params=pltpu.CompilerParams(vmem_limit_bytes=VMEM_LIMIT),
    )(block)


def _all_reduce8(name, parts):
    r = parts.shape[1]

    def body(p_ref, out_ref, rbuf, send1, recv1, send2, recv2):
        x, y, c = _coords()
        me = 4 * x + 2 * y + c
        first = []
        for k in range(1, N_DEV):
            px, py, pc = _peer(x, y, c, k)
            cp = pltpu.make_async_remote_copy(
                src_ref=p_ref.at[4 * px + 2 * py + pc], dst_ref=rbuf.at[me], send_sem=send1.at[k],
                recv_sem=recv1.at[k], device_id=(px, py, pc), device_id_type=MESH)
            cp.start()
            first.append(cp)
        rbuf[me] = p_ref[me]
        for cp in first:
            cp.wait()
        acc = rbuf[0]
        for d in range(1, N_DEV):
            acc = acc + rbuf[d]
        out_ref[me] = acc
        second = []
        for k in range(1, N_DEV):
            cp = pltpu.make_async_remote_copy(
                src_ref=out_ref.at[me], dst_ref=out_ref.at[me], send_sem=send2.at[k], recv_sem=recv2.at[k],
                device_id=_peer(x, y, c, k), device_id_type=MESH)
            cp.start()
            second.append(cp)
        for cp in second:
            cp.wait()

    return pl.pallas_call(
        body, name=name,
        in_specs=[pl.BlockSpec(memory_space=pltpu.VMEM)], out_specs=pl.BlockSpec(memory_space=pltpu.VMEM),
        out_shape=jax.ShapeDtypeStruct((N_DEV, r, LANE), F32),
        scratch_shapes=[pltpu.VMEM((N_DEV, r, LANE), F32)] + [pltpu.SemaphoreType.DMA((N_DEV,))] * 4,
        compiler_params=pltpu.CompilerParams(vmem_limit_bytes=VMEM_LIMIT),
    )(parts)


ANY_SPEC = pl.BlockSpec(memory_space=pl.ANY)
COMM_BLOCK_BYTES = 2 << 20


def _staged_call(name, body, core, grid, in_specs, ins, out_shape, scratch, aliases=None):
    return pl.pallas_call(
        body, name=name,
        grid_spec=pltpu.PrefetchScalarGridSpec(
            num_scalar_prefetch=1, grid=grid, in_specs=in_specs, out_specs=ANY_SPEC, scratch_shapes=scratch),
        out_shape=out_shape, input_output_aliases=aliases or {},
        compiler_params=pltpu.CompilerParams(
            dimension_semantics=("arbitrary",) * len(grid), vmem_limit_bytes=VMEM_LIMIT, has_side_effects=True),
    )(core, *ins)


def _rows_tile(rows, cols, itemsize):
    return _tile(rows, max(PACK, COMM_BLOCK_BYTES // (cols * itemsize)), PACK)


def _chip_peer(x, y, c, k):
    return (x ^ (k >> 1), y ^ (k & 1), c)


def _gather_weight(name, w, core):
    r, cols = w.shape
    h = r // 2
    tr = _rows_tile(h, cols, 4)
    nb = h // tr
    full = jax.ShapeDtypeStruct((N_CHIP * r, cols), MXU_DTYPE)

    def send_body(c_ref, w_ref, full_ref, buf, send_sems, recv_sems, local_sem):
        j = pl.program_id(0)
        x, y, c = _coords()
        buf[...] = w_ref[...].astype(buf.dtype)
        dst = full_ref.at[pl.ds(pl.multiple_of((2 * x + y) * r + c * h + j * tr, PACK), tr)]
        local = pltpu.make_async_copy(buf, dst, local_sem)
        local.start()
        remote = [pltpu.make_async_remote_copy(
            src_ref=buf, dst_ref=dst, send_sem=send_sems.at[k], recv_sem=recv_sems.at[k],
            device_id=_chip_peer(x, y, c, k), device_id_type=MESH) for k in (1, 2, 3)]
        for cp in remote:
            cp.start()
        for cp in remote:
            cp.wait_send()
        local.wait()

        @pl.when(j == nb - 1)
        def _():
            landed = full_ref.at[pl.ds(0, h)]
            for k in (1, 2, 3):
                pltpu.make_async_remote_copy(
                    src_ref=landed, dst_ref=landed, send_sem=send_sems.at[k], recv_sem=recv_sems.at[k],
                    device_id=_chip_peer(x, y, c, k), device_id_type=MESH).wait_recv()

    part = _staged_call(
        name + "_ici", send_body, core, (nb,),
        [pl.BlockSpec((tr, cols), lambda j, c_ref: (c_ref[0] * nb + j, 0))], [w], full,
        [pltpu.VMEM((tr, cols), MXU_DTYPE), pltpu.SemaphoreType.DMA((4,)), pltpu.SemaphoreType.DMA((4,)),
         pltpu.SemaphoreType.DMA(())])

    tr2 = _rows_tile(h, cols, 2)
    nb2 = h // tr2

    def swap_body(c_ref, mine_ref, full_ref, send_sem, recv_sem):
        s, j = pl.program_id(0), pl.program_id(1)
        x, y, c = _coords()
        dst = full_ref.at[pl.ds(pl.multiple_of(s * r + c * h + j * tr2, PACK), tr2)]
        cp = pltpu.make_async_remote_copy(
            src_ref=mine_ref, dst_ref=dst, send_sem=send_sem, recv_sem=recv_sem,
            device_id=(x, y, 1 - c), device_id_type=MESH)
        cp.start()
        cp.wait_send()

        @pl.when((s == N_CHIP - 1) & (j == nb2 - 1))
        def _():
            landed = full_ref.at[pl.ds(0, N_CHIP * h)]
            pltpu.make_async_remote_copy(
                src_ref=landed, dst_ref=landed, send_sem=send_sem, recv_sem=recv_sem,
                device_id=(x, y, 1 - c), device_id_type=MESH).wait_recv()

    return _staged_call(
        name + "_d2d", swap_body, core, (N_CHIP, nb2),
        [pl.BlockSpec((tr2, cols), lambda s, j, c_ref: ((2 * s + c_ref[0]) * nb2 + j, 0))], [part], full,
        [pltpu.SemaphoreType.DMA(()), pltpu.SemaphoreType.DMA(())], aliases={1: 0}).reshape(N_CHIP, r, cols)


def _pair_sum(name, g, core):
    n, r, cols = g.shape
    h = r // 2
    tr = _rows_tile(h, cols, g.dtype.itemsize)
    nb = h // tr
    half = jax.ShapeDtypeStruct((n, h, cols), g.dtype)

    def send_body(c_ref, g_ref, got_ref, send_sem, recv_sem):
        s, j = pl.program_id(0), pl.program_id(1)
        x, y, c = _coords()
        dst = got_ref.at[pl.ds(pl.multiple_of(s * h + j * tr, PACK), tr)]
        cp = pltpu.make_async_remote_copy(
            src_ref=g_ref, dst_ref=dst, send_sem=send_sem, recv_sem=recv_sem,
            device_id=(x, y, 1 - c), device_id_type=MESH)
        cp.start()
        cp.wait_send()

        @pl.when((s == n - 1) & (j == nb - 1))
        def _():
            pltpu.make_async_remote_copy(
                src_ref=got_ref, dst_ref=got_ref, send_sem=send_sem, recv_sem=recv_sem,
                device_id=(x, y, 1 - c), device_id_type=MESH).wait_recv()

    got = _staged_call(
        name + "_send", send_body, core, (n, nb),
        [pl.BlockSpec((tr, cols), lambda s, j, c_ref: ((2 * s + 1 - c_ref[0]) * nb + j, 0))],
        [g.reshape(n * r, cols)], jax.ShapeDtypeStruct((n * h, cols), g.dtype),
        [pltpu.SemaphoreType.DMA(()), pltpu.SemaphoreType.DMA(())]).reshape(n, h, cols)

    def add_body(c_ref, own_ref, got_ref, o_ref):
        o_ref[...] = (own_ref[...].astype(F32) + got_ref[...].astype(F32)).astype(o_ref.dtype)

    blk = pl.BlockSpec((None, tr, cols), lambda s, j, c_ref: (s, j, 0))
    return pl.pallas_call(
        add_body, name=name + "_add",
        grid_spec=pltpu.PrefetchScalarGridSpec(
            num_scalar_prefetch=1, grid=(n, nb),
            in_specs=[pl.BlockSpec((None, tr, cols), lambda s, j, c_ref: (s, c_ref[0] * nb + j, 0)), blk],
            out_specs=blk),
        out_shape=half, compiler_params=_params(("parallel", "parallel")),
    )(core, g, got)


def _chip_exchange_plan(pairs):
    shapes = [jax.ShapeDtypeStruct((N_CHIP - 1,) + p.shape[1:], p.dtype) for p in pairs]

    def copies(in_refs, out_refs, send_sems, recv_sems):
        x, y, c = _coords()
        out = []
        for i, (src, dst) in enumerate(zip(in_refs, out_refs)):
            for k in (1, 2, 3):
                px, py, pc = _chip_peer(x, y, c, k)
                n = 3 * i + k - 1
                out.append(pltpu.make_async_remote_copy(
                    src_ref=src.at[2 * px + py], dst_ref=dst.at[k - 1], send_sem=send_sems.at[n],
                    recv_sem=recv_sems.at[n], device_id=(px, py, pc), device_id_type=MESH))
        return out

    return shapes, copies


def _chip_sum(name, pair, got, chip):
    _, h, cols = pair.shape
    tr = _tile(h, max(PACK, ELEMWISE_BLOCK // cols), PACK)

    def body(chip_ref, own_ref, a_ref, b_ref, c_ref, o_ref):
        o_ref[...] = ((own_ref[...].astype(F32) + a_ref[...].astype(F32)) + b_ref[...].astype(F32)) + c_ref[...].astype(F32)

    return pl.pallas_call(
        body, name=name,
        grid_spec=pltpu.PrefetchScalarGridSpec(
            num_scalar_prefetch=1, grid=(h // tr,),
            in_specs=[pl.BlockSpec((None, tr, cols), lambda j, chip_ref: (chip_ref[0], j, 0))]
            + [pl.BlockSpec((None, tr, cols), (lambda j, chip_ref, k=k: (k, j, 0))) for k in range(N_CHIP - 1)],
            out_specs=pl.BlockSpec((tr, cols), lambda j, chip_ref: (j, 0))),
        out_shape=jax.ShapeDtypeStruct((h, cols), F32), compiler_params=_params(("parallel",)),
    )(chip, pair, got, got, got)


def _swap_halves(name, red, core, into, total_rows, base):
    h, cols = red.shape
    tr = _rows_tile(h, cols, 4)
    nb = h // tr

    def body(c_ref, red_ref, *rest):
        full_ref, send_sem, recv_sem, local_sem = rest[-4:]
        j = pl.program_id(0)
        x, y, c = _coords()
        dst = full_ref.at[pl.ds(pl.multiple_of(base + c * h + j * tr, SUB), tr)]
        local = pltpu.make_async_copy(red_ref, dst, local_sem)
        remote = pltpu.make_async_remote_copy(
            src_ref=red_ref, dst_ref=dst, send_sem=send_sem, recv_sem=recv_sem,
            device_id=(x, y, 1 - c), device_id_type=MESH)
        local.start()
        remote.start()
        remote.wait_send()
        local.wait()

        @pl.when(j == nb - 1)
        def _():
            landed = full_ref.at[pl.ds(0, h)]
            pltpu.make_async_remote_copy(
                src_ref=landed, dst_ref=landed, send_sem=send_sem, recv_sem=recv_sem,
                device_id=(x, y, 1 - c), device_id_type=MESH).wait_recv()

    in_specs = [pl.BlockSpec((tr, cols), lambda j, c_ref: (j, 0))]
    ins = [red]
    aliases = None
    if into is not None:
        in_specs.append(ANY_SPEC)
        ins.append(into)
        aliases = {2: 0}
    return _staged_call(
        name, body, core, (nb,), in_specs, ins, jax.ShapeDtypeStruct((total_rows, cols), F32),
        [pltpu.SemaphoreType.DMA(())] * 3, aliases)


def _cmul(ar, ai, br, bi):
    return ar * br - ai * bi, ar * bi + ai * br


def _cpow(ar, ai, n):
    rr, ri = jnp.ones_like(ar), jnp.zeros_like(ai)
    br, bi = ar, ai
    while n:
        if n & 1:
            rr, ri = _cmul(rr, ri, br, bi)
        br, bi = _cmul(br, bi, br, bi)
        n >>= 1
    return rr, ri


def _tile_rows(t):
    return pl.ds(pl.multiple_of(t * SUB, SUB), SUB)


def _scan_setup(buf, steps, ar, ai, h0r, h0i, rev):
    def total(i, carry):
        sr, si = carry
        rows = _tile_rows(steps - 1 - i if rev else i)
        pr, pi = _cmul(ar, ai, sr, si)
        return pr + buf[rows, 0:S5_H], pi + buf[rows, S5_H:S5_NS]

    zero = jnp.zeros((SUB, S5_H), F32)
    tot_r, tot_i = lax.fori_loop(0, steps, total, (zero, zero))
    pw_r, pw_i = _cpow(ar[0:1], ai[0:1], steps)
    row = lax.broadcasted_iota(jnp.int32, (SUB, S5_H), 0)
    cur_r, cur_i = h0r, h0i
    init_r, init_i = zero, zero
    for s in (range(N_SEG - 1, -1, -1) if rev else range(N_SEG)):
        init_r = jnp.where(row == s, cur_r, init_r)
        init_i = jnp.where(row == s, cur_i, init_i)
        nr, ni = _cmul(pw_r, pw_i, cur_r, cur_i)
        cur_r, cur_i = nr + tot_r[s:s + 1], ni + tot_i[s:s + 1]
    return init_r, init_i, cur_r, cur_i


def _scan(buf, steps, ar, ai, h0r, h0i, rev, store):
    init_r, init_i, fin_r, fin_i = _scan_setup(buf, steps, ar, ai, h0r, h0i, rev)
    if store:
        def step(i, carry):
            hr, hi = carry
            rows = _tile_rows(steps - 1 - i if rev else i)
            pr, pi = _cmul(ar, ai, hr, hi)
            hr, hi = pr + buf[rows, 0:S5_H], pi + buf[rows, S5_H:S5_NS]
            buf[rows, 0:S5_H] = hr
            buf[rows, S5_H:S5_NS] = hi
            return hr, hi

        lax.fori_loop(0, steps, step, (init_r, init_i))
    return fin_r, fin_i


def _adjoint_scan(gbuf, hbuf, steps, ar, ai, l0r, l0i, hin_r, hin_i, rev):
    ci = -ai
    arev = not rev
    init_r, init_i, fin_r, fin_i = _scan_setup(gbuf, steps, ar, ci, l0r, l0i, arev)
    zero = jnp.zeros((SUB, S5_H), F32)

    def update(t, hp_r, hp_i, carry):
        lr, li, dr, di = carry
        rows = _tile_rows(t)
        pr, pi = _cmul(ar, ci, lr, li)
        lr, li = pr + gbuf[rows, 0:S5_H], pi + gbuf[rows, S5_H:S5_NS]
        gbuf[rows, 0:S5_H] = lr
        gbuf[rows, S5_H:S5_NS] = li
        return lr, li, dr + lr * hp_r + li * hp_i, di + li * hp_r - lr * hp_i

    def step(i, carry):
        t = steps - 1 - i if rev is False else i
        prev = _tile_rows(t - 1 if rev is False else t + 1)
        return update(t, hbuf[prev, 0:S5_H], hbuf[prev, S5_H:S5_NS], carry)

    carry = lax.fori_loop(0, steps - 1, step, (init_r, init_i, zero, zero))
    row = lax.broadcasted_iota(jnp.int32, (SUB, S5_H), 0)
    if rev:
        last, edge, shift, t = _tile_rows(0), N_SEG - 1, SUB - 1, steps - 1
    else:
        last, edge, shift, t = _tile_rows(steps - 1), 0, 1, 0
    hp_r = jnp.where(row == edge, hin_r, pltpu.roll(hbuf[last, 0:S5_H], shift, 0))
    hp_i = jnp.where(row == edge, hin_i, pltpu.roll(hbuf[last, S5_H:S5_NS], shift, 0))
    _, _, dr, di = update(t, hp_r, hp_i, carry)
    return fin_r, fin_i, dr, di


def _s5_chunk(rows):
    return _tile(rows, 512, PACK)


def _s5_forward(u, uc, bblk, cblk, atile, dsk):
    rows, d = u.shape
    rows_c = uc.shape[0]
    nj = d // S5_CB
    steps, steps_c = rows // N_SEG, rows_c // N_SEG
    rc = _s5_chunk(rows)

    def body(u_ref, uc_ref, b_ref, c_ref, a_ref, d_ref, y_ref, buf, bufc):
        zero = jnp.zeros((1, S5_H), F32)
        for dr in (0, 1):
            rev = dr == 1
            ar, ai = a_ref[dr, :, 0:S5_H], a_ref[dr, :, S5_H:S5_NS]
            bm, cm = b_ref[dr].astype(MXU_DTYPE), c_ref[dr].astype(MXU_DTYPE)
            bufc[...] = _dot(uc_ref[...], bm)
            fin_r, fin_i = _scan(bufc, steps_c, ar, ai, zero, zero, rev, False)

            def project(r, _):
                rs = pl.ds(pl.multiple_of(r * rc, rc), rc)
                buf[rs, :] = _dot(u_ref[rs, :], bm)
                return 0

            lax.fori_loop(0, rows // rc, project, 0)
            _scan(buf, steps, ar, ai, fin_r, fin_i, rev, True)

            def readout(r, _):
                rs = pl.ds(pl.multiple_of(r * rc, rc), rc)
                yv = _dot(buf[rs, :], cm)
                if dr == 0:
                    y_ref[rs, :] = u_ref[rs, :].astype(F32) * d_ref[...] + yv
                else:
                    y_ref[rs, :] += yv
                return 0

            lax.fori_loop(0, rows // rc, readout, 0)

    return pl.pallas_call(
        body, name="s5_forward", grid=(nj,),
        in_specs=[
            pl.BlockSpec((rows, S5_CB), lambda j: (0, j)),
            pl.BlockSpec((rows_c, S5_CB), lambda j: (0, j)),
            pl.BlockSpec((2, None, S5_CB, S5_NS), lambda j: (0, j, 0, 0)),
            pl.BlockSpec((2, None, S5_NS, S5_CB), lambda j: (0, j, 0, 0)),
            pl.BlockSpec((2, None, SUB, S5_NS), lambda j: (0, j, 0, 0)),
            pl.BlockSpec((1, S5_CB), lambda j: (0, j)),
        ],
        out_specs=pl.BlockSpec((rows, S5_CB), lambda j: (0, j)),
        out_shape=jax.ShapeDtypeStruct((rows, d), F32),
        scratch_shapes=[pltpu.VMEM((rows, S5_NS), F32), pltpu.VMEM((rows_c, S5_NS), F32)],
        compiler_params=_params(("parallel",)),
    )(u, uc, bblk, cblk, atile, dsk)


def _s5_backward(u, uc, dy, bblk, cblk, atile, dsk, ride_ins, ride_shapes, ride_copies):
    rows, d = u.shape
    rows_c = uc.shape[0]
    nj = d // S5_CB
    steps, steps_c = rows // N_SEG, rows_c // N_SEG
    rc = _s5_chunk(rows)
    nchunk = rows // rc
    n_ride = len(ride_ins)
    n_sem = 3 * n_ride

    def body(u_ref, uc_ref, dy_ref, b_ref, c_ref, a_ref, d_ref, *rest):
        ride_in = rest[:n_ride]
        du_ref, duc_ref, db_ref, dc_ref, da_ref, dd_ref = rest[n_ride:n_ride + 6]
        ride_out = rest[n_ride + 6:2 * n_ride + 6]
        hbuf, gbuf, hcbuf, gcbuf, send_sems, recv_sems = rest[2 * n_ride + 6:]

        @pl.when(pl.program_id(0) == 0)
        def _():
            for cp in ride_copies(ride_in, ride_out, send_sems, recv_sems):
                cp.start()

        zero = jnp.zeros((1, S5_H), F32)
        db_ref[...] = jnp.zeros_like(db_ref)
        dc_ref[...] = jnp.zeros_like(dc_ref)
        dd_ref[...] = jnp.zeros_like(dd_ref)
        for dr in (0, 1):
            rev = dr == 1
            ar, ai = a_ref[dr, :, 0:S5_H], a_ref[dr, :, S5_H:S5_NS]
            bm, cm = b_ref[dr].astype(MXU_DTYPE), c_ref[dr].astype(MXU_DTYPE)
            hcbuf[...] = _dot(uc_ref[...], bm)
            hin_r, hin_i = _scan(hcbuf, steps_c, ar, ai, zero, zero, rev, True)

            def project(r, _):
                rs = pl.ds(pl.multiple_of(r * rc, rc), rc)
                hbuf[rs, :] = _dot(u_ref[rs, :], bm)
                return 0

            lax.fori_loop(0, nchunk, project, 0)
            _scan(hbuf, steps, ar, ai, hin_r, hin_i, rev, True)

            def readout_bwd(r, _):
                rs = pl.ds(pl.multiple_of(r * rc, rc), rc)
                dyv = dy_ref[rs, :]
                gbuf[rs, :] = _dot(dyv, cm, NT)
                dc_ref[dr] += _dot(hbuf[rs, :], dyv, TN)
                return 0

            lax.fori_loop(0, nchunk, readout_bwd, 0)
            lf_r, lf_i, dar, dai = _adjoint_scan(gbuf, hbuf, steps, ar, ai, zero, zero, hin_r, hin_i, rev)
            gcbuf[...] = jnp.zeros_like(gcbuf)
            _, _, dar_c, dai_c = _adjoint_scan(gcbuf, hcbuf, steps_c, ar, ai, lf_r, lf_i, zero, zero, rev)
            da_ref[dr, :, 0:S5_H] = dar + dar_c
            da_ref[dr, :, S5_H:S5_NS] = dai + dai_c

            def project_bwd(r, _):
                rs = pl.ds(pl.multiple_of(r * rc, rc), rc)
                lam = gbuf[rs, :]
                uv = u_ref[rs, :]
                part = _dot(lam, bm, NT)
                db_ref[dr] += _dot(uv, lam, TN)
                if dr == 0:
                    dyv = dy_ref[rs, :].astype(F32)
                    du_ref[rs, :] = part + dyv * d_ref[...]
                    dd_ref[...] += (dyv * uv.astype(F32)).reshape(rc // SUB, SUB, S5_CB).sum(axis=0)
                else:
                    du_ref[rs, :] += part
                return 0

            lax.fori_loop(0, nchunk, project_bwd, 0)
            lam_c = gcbuf[...]
            part_c = _dot(lam_c, bm, NT)
            db_ref[dr] += _dot(uc_ref[...], lam_c, TN)
            if dr == 0:
                duc_ref[...] = part_c
            else:
                duc_ref[...] += part_c

        @pl.when(pl.program_id(0) == nj - 1)
        def _():
            for cp in ride_copies(ride_in, ride_out, send_sems, recv_sems):
                cp.wait()

    blk = lambda r: pl.BlockSpec((r, S5_CB), lambda j: (0, j))
    res = pl.pallas_call(
        body, name="s5_backward", grid=(nj,),
        in_specs=[
            blk(rows), blk(rows_c), blk(rows),
            pl.BlockSpec((2, None, S5_CB, S5_NS), lambda j: (0, j, 0, 0)),
            pl.BlockSpec((2, None, S5_NS, S5_CB), lambda j: (0, j, 0, 0)),
            pl.BlockSpec((2, None, SUB, S5_NS), lambda j: (0, j, 0, 0)),
            pl.BlockSpec((1, S5_CB), lambda j: (0, j)),
        ] + [ANY_SPEC] * n_ride,
        out_specs=[
            blk(rows), blk(rows_c),
            pl.BlockSpec((2, None, S5_CB, S5_NS), lambda j: (0, j, 0, 0)),
            pl.BlockSpec((2, None, S5_NS, S5_CB), lambda j: (0, j, 0, 0)),
            pl.BlockSpec((2, None, SUB, S5_NS), lambda j: (0, j, 0, 0)),
            pl.BlockSpec((SUB, S5_CB), lambda j: (0, j)),
        ] + [ANY_SPEC] * n_ride,
        out_shape=[
            jax.ShapeDtypeStruct((rows, d), F32), jax.ShapeDtypeStruct((rows_c, d), F32),
            jax.ShapeDtypeStruct(bblk.shape, F32), jax.ShapeDtypeStruct(cblk.shape, F32),
            jax.ShapeDtypeStruct(atile.shape, F32), jax.ShapeDtypeStruct((SUB, d), F32),
        ] + list(ride_shapes),
        scratch_shapes=[pltpu.VMEM((rows, S5_NS), F32), pltpu.VMEM((rows, S5_NS), F32),
                        pltpu.VMEM((rows_c, S5_NS), F32), pltpu.VMEM((rows_c, S5_NS), F32),
                        pltpu.SemaphoreType.DMA((n_sem,)), pltpu.SemaphoreType.DMA((n_sem,))],
        compiler_params=pltpu.CompilerParams(
            dimension_semantics=("arbitrary",), vmem_limit_bytes=VMEM_LIMIT, has_side_effects=True),
    )(u, uc, dy, bblk, cblk, atile, dsk, *ride_ins)
    return res[:6], res[6:]


def _s5_prepare(lam_re, lam_im, log_step, b_re, b_im, c_re, c_im):
    nd, g, p = lam_re.shape
    gb = S5_CB // S5_GROUP
    nj = g // gb
    dt = jnp.exp(log_step)[..., None]
    mag = jnp.exp(lam_re * dt)
    abar_re = mag * jnp.cos(lam_im * dt)
    abar_im = mag * jnp.sin(lam_im * dt)
    nr, ni = abar_re - 1.0, abar_im
    den = lam_re * lam_re + lam_im * lam_im
    fr = (nr * lam_re + ni * lam_im) / den
    fi = (ni * lam_re - nr * lam_im) / den
    bbar_re = fr[..., None] * b_re - fi[..., None] * b_im
    bbar_im = fr[..., None] * b_im + fi[..., None] * b_re
    eye = jnp.eye(gb, dtype=bool)

    def diag_in(w):
        w = w.reshape(nd, nj, gb, p, S5_GROUP).transpose(0, 1, 2, 4, 3)
        w = jnp.where(eye[None, None, :, None, :, None], w[:, :, :, :, None, :], 0.0)
        return w.reshape(nd, nj, gb * S5_GROUP, gb * p)

    def diag_out(w):
        w = w.reshape(nd, nj, gb, S5_GROUP, p).transpose(0, 1, 2, 4, 3)
        w = jnp.where(eye[None, None, :, None, :, None], w[:, :, :, :, None, :], 0.0)
        return w.reshape(nd, nj, gb * p, gb * S5_GROUP)

    bblk = jnp.concatenate([diag_in(bbar_re), diag_in(bbar_im)], axis=-1)
    cblk = jnp.concatenate([diag_out(c_re), -diag_out(c_im)], axis=-2)
    a2 = jnp.concatenate([abar_re.reshape(nd, nj, gb * p), abar_im.reshape(nd, nj, gb * p)], axis=-1)
    atile = jnp.broadcast_to(a2[:, :, None, :], (nd, nj, SUB, 2 * gb * p))
    return bblk, cblk, atile


def _shifted(x, prev_row, next_row):
    n = x.shape[0]
    row = lax.broadcasted_iota(jnp.int32, x.shape, 0)
    xp = jnp.where(row == 0, prev_row, pltpu.roll(x, 1, 0))
    xn = jnp.where(row == n - 1, next_row, pltpu.roll(x, n - 1, 0))
    return xp, xn


def _edge_rows(ref, r0, n, total, group):
    lo = pl.multiple_of(jnp.maximum(r0 - group, 0), group)
    hi = pl.multiple_of(jnp.minimum(r0 + n, total - group), group)
    prev_row = ref[pl.ds(lo, group), :].astype(F32)[group - 1:group] * (r0 > 0).astype(F32)
    next_row = ref[pl.ds(hi, group), :].astype(F32)[0:1] * (r0 + n < total).astype(F32)
    return prev_row, next_row


def _conv_rows(ref, r0, n, total, w_ref, b_ref):
    x = ref[pl.ds(r0, n), :].astype(F32)
    xp, xn = _shifted(x, *_edge_rows(ref, r0, n, total, PACK))
    hc = w_ref[0:1, :] * xp + w_ref[1:2, :] * x + w_ref[2:3, :] * xn + b_ref[...]
    return hc, xp, x, xn


def _conv_specs(rows, f, tc):
    nt = f // tc
    val = lambda r: pl.BlockSpec((r, tc), lambda j: (0, j))
    gate = lambda r: pl.BlockSpec((r, tc), lambda j: (0, j + nt))
    return val, gate


def _conv_swiglu_fwd(name, h, cw, cb):
    rows, f2 = h.shape
    f = f2 // 2
    tc = _tile(f, 256)
    rc = _tile(rows, 256, PACK)
    val, gate = _conv_specs(rows, f, tc)

    def body(hv_ref, hg_ref, wv_ref, wg_ref, bv_ref, bg_ref, a_ref):
        def chunk(r, _):
            r0 = pl.multiple_of(r * rc, rc)
            hv = _conv_rows(hv_ref, r0, rc, rows, wv_ref, bv_ref)[0]
            hg = _conv_rows(hg_ref, r0, rc, rows, wg_ref, bg_ref)[0]
            a_ref[pl.ds(r0, rc), :] = (hg * _sigmoid(hg) * hv).astype(a_ref.dtype)
            return 0

        lax.fori_loop(0, rows // rc, chunk, 0)

    return pl.pallas_call(
        body, name=name, grid=(f // tc,),
        in_specs=[val(rows), gate(rows), val(3), gate(3), val(1), gate(1)],
        out_specs=val(rows), out_shape=jax.ShapeDtypeStruct((rows, f), ACT_DTYPE),
        compiler_params=_params(("parallel",)),
    )(h, h, cw, cw, cb, cb)


def _conv_swiglu_bwd(name, da, h, cw, cb):
    rows, f2 = h.shape
    f = f2 // 2
    tc = _tile(f, 256)
    rc = _tile(rows, 256, PACK)
    val, gate = _conv_specs(rows, f, tc)

    def body(da_ref, hv_ref, hg_ref, wv_ref, wg_ref, bv_ref, bg_ref,
             dhv_ref, dhg_ref, dwv_ref, dwg_ref, dbv_ref, dbg_ref, sv, sg):
        def first(r, carry):
            r0 = pl.multiple_of(r * rc, rc)
            rs = pl.ds(r0, rc)
            hv, vp, vx, vn = _conv_rows(hv_ref, r0, rc, rows, wv_ref, bv_ref)
            hg, gp, gx, gn = _conv_rows(hg_ref, r0, rc, rows, wg_ref, bg_ref)
            d = da_ref[rs, :].astype(F32)
            s = _sigmoid(hg)
            dv = d * (hg * s)
            dg = d * hv * (s * (1.0 + hg * (1.0 - s)))
            sv[rs, :] = dv
            sg[rs, :] = dg
            sums = [dv * vp, dv * vx, dv * vn, dv, dg * gp, dg * gx, dg * gn, dg]
            return tuple(c + jnp.sum(x, axis=0, keepdims=True) for c, x in zip(carry, sums))

        zero = jnp.zeros((1, tc), F32)
        acc = lax.fori_loop(0, rows // rc, first, (zero,) * 8)
        for k in range(3):
            dwv_ref[k:k + 1, :] = acc[k]
            dwg_ref[k:k + 1, :] = acc[4 + k]
        dbv_ref[...] = acc[3]
        dbg_ref[...] = acc[7]

        def second(r, _):
            r0 = pl.multiple_of(r * rc, rc)
            rs = pl.ds(r0, rc)
            for s_ref, w_ref, o_ref in ((sv, wv_ref, dhv_ref), (sg, wg_ref, dhg_ref)):
                x = s_ref[rs, :]
                xp, xn = _shifted(x, *_edge_rows(s_ref, r0, rc, rows, SUB))
                o_ref[rs, :] = (w_ref[0:1, :] * xn + w_ref[1:2, :] * x + w_ref[2:3, :] * xp).astype(o_ref.dtype)
            return 0

        lax.fori_loop(0, rows // rc, second, 0)

    res = pl.pallas_call(
        body, name=name, grid=(f // tc,),
        in_specs=[val(rows), val(rows), gate(rows), val(3), gate(3), val(1), gate(1)],
        out_specs=[val(rows), val(rows), val(3), val(3), val(1), val(1)],
        out_shape=[jax.ShapeDtypeStruct((rows, f), ACT_DTYPE)] * 2
        + [jax.ShapeDtypeStruct((3, f), F32)] * 2 + [jax.ShapeDtypeStruct((1, f), F32)] * 2,
        scratch_shapes=[pltpu.VMEM((rows, tc), F32), pltpu.VMEM((rows, tc), F32)],
        compiler_params=_params(("parallel",)),
    )(da, h, h, cw, cw, cb, cb)
    return res


def _pool_band(name, x, transpose, out_dtype):
    rows, d = x.shape
    ng = len(POOL_WINDOWS)
    ch = d // ng
    tm = _tile(rows, 256, PACK)
    win = tm + 2 * POOL_HALO
    assert win <= rows

    def body(x_ref, o_ref):
        half = lax.shift_left(jnp.int32(1), pl.program_id(0))
        t0 = pl.program_id(1) * tm
        ws = pl.multiple_of(jnp.clip(t0 - POOL_HALO, 0, rows - win), PACK)
        i = t0 + lax.broadcasted_iota(jnp.int32, (tm, win), 0)
        j = ws + lax.broadcasted_iota(jnp.int32, (tm, win), 1)

        def inv_count(t):
            hi = jnp.minimum(t + half - 1, rows - 1)
            lo = jnp.maximum(t - half, 0)
            return 1.0 / (hi - lo + 1).astype(F32)

        xw = x_ref[pl.ds(ws, win), :]
        xt = x_ref[pl.ds(pl.multiple_of(t0, PACK), tm), :].astype(F32)
        if transpose:
            band = (j - half <= i) & (i <= j + half - 1)
            tw = ws + lax.broadcasted_iota(jnp.int32, (win, 1), 0)
            o = _dot(band.astype(MXU_DTYPE), xw.astype(F32) * inv_count(tw)) - xt
        else:
            band = (i - half <= j) & (j <= i + half - 1)
            tt = t0 + lax.broadcasted_iota(jnp.int32, (tm, 1), 0)
            o = _dot(band.astype(MXU_DTYPE), xw) * inv_count(tt) - xt
        o_ref[...] = o.astype(o_ref.dtype)

    return pl.pallas_call(
        body, name=name, grid=(ng, rows // tm),
        in_specs=[pl.BlockSpec((rows, ch), lambda g, i: (0, g))],
        out_specs=pl.BlockSpec((tm, ch), lambda g, i: (i, g)),
        out_shape=jax.ShapeDtypeStruct((rows, d), out_dtype),
        compiler_params=_params(("parallel", "arbitrary")),
    )(x)


def _ada_forward(c16, ada_w, ada_b):
    nl, d, cols = ada_w.shape
    tn = _tile(cols, 512)

    def body(c_ref, w_ref, b_ref, o_ref):
        cv = c_ref[...]
        o_ref[...] = _dot(cv * _sigmoid(cv), w_ref[...]) + b_ref[...]

    return pl.pallas_call(
        body, name="ada_forward", grid=(nl, cols // tn),
        in_specs=[pl.BlockSpec((16, d), lambda l, n: (0, 0)),
                  pl.BlockSpec((None, d, tn), lambda l, n: (l, 0, n)),
                  pl.BlockSpec((None, 1, tn), lambda l, n: (l, 0, n))],
        out_specs=pl.BlockSpec((None, 16, tn), lambda l, n: (l, 0, n)),
        out_shape=jax.ShapeDtypeStruct((nl, 16, cols), F32),
        compiler_params=_params(("parallel", "parallel")),
    )(c16, ada_w, ada_b)


def _ada_backward(c16, dmod, ada_w):
    nl, d, cols = ada_w.shape
    tn = _tile(cols, 512)
    nn = cols // tn

    def body(c_ref, g_ref, w_ref, gw_ref, gc_ref):
        cv = c_ref[...]
        s = _sigmoid(cv)
        gv = g_ref[...]
        gw_ref[...] = _dot(cv * s, gv, TN)
        dcond = _dot(gv, w_ref[...], NT)
        row = lax.broadcasted_iota(jnp.int32, dcond.shape, 0)
        dctx = jnp.sum(jnp.where(row >= 8, dcond * (s * (1.0 + cv * (1.0 - s))), 0.0), axis=0, keepdims=True)

        @pl.when((pl.program_id(0) == 0) & (pl.program_id(1) == 0))
        def _():
            gc_ref[...] = jnp.zeros_like(gc_ref)

        gc_ref[...] += dctx

    return pl.pallas_call(
        body, name="ada_backward", grid=(nl, nn),
        in_specs=[pl.BlockSpec((16, d), lambda l, n: (0, 0)),
                  pl.BlockSpec((None, 16, tn), lambda l, n: (l, 0, n)),
                  pl.BlockSpec((None, d, tn), lambda l, n: (l, 0, n))],
        out_specs=[pl.BlockSpec((None, d, tn), lambda l, n: (l, 0, n)),
                   pl.BlockSpec((1, d), lambda l, n: (0, 0))],
        out_shape=[jax.ShapeDtypeStruct((nl, d, cols), F32), jax.ShapeDtypeStruct((1, d), F32)],
        compiler_params=_params(("arbitrary", "arbitrary")),
    )(c16, dmod, ada_w)


def _row_sum16(name, a):
    nl, _, w = a.shape
    tn = _tile(w, 4096)

    def body(a_ref, o_ref):
        o_ref[...] = jnp.sum(a_ref[...], axis=0, keepdims=True)

    return pl.pallas_call(
        body, name=name, grid=(nl, w // tn),
        in_specs=[pl.BlockSpec((None, 16, tn), lambda l, n: (l, 0, n))],
        out_specs=pl.BlockSpec((None, 1, tn), lambda l, n: (l, 0, n)),
        out_shape=jax.ShapeDtypeStruct((nl, 1, w), F32),
        compiler_params=_params(("parallel", "parallel")),
    )(a)


def _pack(arrays, row_align):
    flat = jnp.concatenate([a.reshape(-1).astype(F32) for a in arrays])
    quantum = row_align * LANE
    padded = -(-flat.shape[0] // quantum) * quantum
    return jnp.pad(flat, (0, padded - flat.shape[0])).reshape(-1, LANE)


def _unpack(packed, shapes):
    flat = packed.reshape(-1)
    out, off = [], 0
    for s in shapes:
        n = math.prod(s)
        out.append(flat[off:off + n].reshape(s))
        off += n
    return out


def _grid_pos_emb(n_tokens, dim):
    rows = n_tokens // GRID_W
    r, col = jnp.meshgrid(jnp.arange(rows, dtype=F32), jnp.arange(GRID_W, dtype=F32), indexing="ij")
    quarter = dim // 4
    omega = 1.0 / (POS_BASE ** (jnp.arange(quarter, dtype=F32) / quarter))

    def enc(p):
        ang = p.reshape(-1, 1) * omega[None, :]
        return jnp.concatenate([jnp.sin(ang), jnp.cos(ang)], axis=-1)

    return jnp.concatenate([enc(r), enc(col)], axis=-1)


def _ffn_forward(tag, v, x_in, up4, dn4, layer, cw, cb):
    h = _mm_cols(f"ffn_up_{tag}", v, up4, layer, ACT_DTYPE)
    a = _conv_swiglu_fwd(f"ffn_conv_{tag}", h, cw, cb)
    f = _mm_rows(f"ffn_down_{tag}", a, dn4, layer, F32)
    return h, a, f


def _ffn_backward(tag, dx_out, fb, x_mid, v, h, a, up4, dn4, layer, cw, cb, gate5, gam3, gam2, scale4p1):
    def post(dxo, f, gam, gate):
        dy, dgate, dgam = _postnorm_bwd(dxo, f.astype(F32), gam, gate)
        return (dy,), (dgate, dgam)

    (df,), (dgate5, dgam3) = _rowwise(f"ffn_post_bwd_{tag}", post, dx_out.shape[0],
                                      [(dx_out, False), (fb, False)], [gam3, gate5],
                                      [(dx_out.shape[1], ACT_DTYPE, False)], [dx_out.shape[1]] * 2)
    da = _mm_rows_nt(f"ffn_down_dx_{tag}", df, dn4, layer, ACT_DTYPE)
    g_dn = _mm_rows_tn(f"ffn_down_dw_{tag}", a, df, dn4.shape[-2], ACT_DTYPE)
    dhv, dhg, dwv, dwg, dbv, dbg = _conv_swiglu_bwd(f"ffn_conv_bwd_{tag}", da, h, cw, cb)
    dh = jnp.concatenate([dhv, dhg], axis=1)
    dcw = jnp.concatenate([dwv, dwg], axis=1)
    dcb = jnp.concatenate([dbv, dbg], axis=1)
    dv = _mm_cols_nt(f"ffn_up_dx_{tag}", dh, up4, layer, F32)
    g_up = _mm_cols_tn(f"ffn_up_dw_{tag}", v, dh, up4.shape[-1], ACT_DTYPE)

    def pre(dvv, x, dxo, gam, s1):
        dx, dshift, dscale, dgam = _prenorm_bwd(dvv, x, gam, s1)
        return (dxo + dx,), (dshift, dscale, dgam)

    d = dx_out.shape[1]
    (dx_mid,), (dshift3, dscale4, dgam2) = _rowwise(
        f"ffn_pre_bwd_{tag}", pre, dx_out.shape[0], [(dv, False), (x_mid, False), (dx_out, False)],
        [gam2, scale4p1], [(d, F32, False)], [d] * 3)
    return dx_mid, g_up, g_dn, dcw, dcb, (dshift3, dscale4, dgate5), (dgam2, dgam3)


def kernel(x, c, ctx, c_ctx, ada_w, ada_b, norm_g, s5_lam_re, s5_lam_im, s5_log_step, s5_b_re, s5_b_im, s5_c_re, s5_c_im, s5_d, s5_glu_w, pool_w, pool_scale, ffn_up, ffn_conv, ffn_conv_b, ffn_down, loss_target, m_c_ctx, m_ada_w, m_ada_b, m_norm_g, m_s5_lam_re, m_s5_lam_im, m_s5_log_step, m_s5_b_re, m_s5_b_im, m_s5_c_re, m_s5_c_im, m_s5_d, m_s5_glu_w, m_pool_w, m_pool_scale, m_ffn_up, m_ffn_conv, m_ffn_conv_b, m_ffn_down, v_c_ctx, v_ada_w, v_ada_b, v_norm_g, v_s5_lam_re, v_s5_lam_im, v_s5_log_step, v_s5_b_re, v_s5_b_im, v_s5_c_re, v_s5_c_im, v_s5_d, v_s5_glu_w, v_pool_w, v_pool_scale, v_ffn_up, v_ffn_conv, v_ffn_conv_b, v_ffn_down):
    ix, iy, ic = _coords()
    chip = 2 * ix + iy
    me = 2 * chip + ic
    _, rows, d = x.shape
    rows_c = ctx.shape[1]
    nl = ada_w.shape[0]
    assert nl == 2 and s5_glu_w.shape[0] == 1 and pool_w.shape[0] == 1
    a_cols = ada_w.shape[2]
    f2s = ffn_up.shape[2]
    f2 = N_CHIP * f2s
    ds = d // N_CHIP
    ng = len(POOL_WINDOWS)
    ch = d // ng
    ps = pool_w.shape[2]

    core = jnp.reshape(ic, (1,)).astype(jnp.int32)
    chip_id = jnp.reshape(chip, (1,)).astype(jnp.int32)
    up4 = _gather_weight("gather_up", _view2d(ffn_up), core).reshape((N_CHIP,) + ffn_up.shape)
    dn4 = _gather_weight("gather_down", _view2d(ffn_down), core).reshape((N_CHIP,) + ffn_down.shape)
    glu4 = _gather_weight("gather_glu", s5_glu_w[0], core)
    pool4 = _gather_weight("gather_pool", _view2d(pool_w[0]), core).reshape(N_CHIP, ng, ps, ch)
    pool_full = pool4.transpose(1, 0, 2, 3).reshape(ng, ch, ch)

    c_all = _all_gather8("gather_cond", _pack([c], SUB))
    c_all = c_all.reshape(N_DEV, -1)[:, :d]
    c16 = jnp.concatenate([c_all, jnp.broadcast_to(c_ctx[None, :], (N_DEV, d))], axis=0)
    ada_b_mine = lax.dynamic_slice_in_dim(ada_b, chip * a_cols, a_cols, axis=1)
    mod_part = _ada_forward(c16, ada_w, ada_b_mine[:, None, :])
    narrow_shapes = [mod_part.shape, norm_g.shape, pool_scale.shape, ffn_conv.shape]
    narrow = _all_gather8("gather_narrow", _pack([mod_part, norm_g, pool_scale, ffn_conv], SUB))
    per_chip = [_unpack(narrow[2 * s], narrow_shapes) for s in range(N_CHIP)]
    mod_all = jnp.concatenate([p[0] for p in per_chip], axis=-1)
    gam = jnp.concatenate([p[1] for p in per_chip], axis=-1)
    pscale = jnp.concatenate([p[2] for p in per_chip], axis=-1)
    conv_w = jnp.concatenate([p[3] for p in per_chip], axis=-1)
    mod_mine = lax.dynamic_index_in_dim(mod_all, me, axis=1, keepdims=False)
    mod_ctx = mod_all[0, N_DEV]

    def mods(vec):
        s0, s1, g2, s3, s4, g5 = [vec[k * d:(k + 1) * d][None, :] for k in range(N_MOD)]
        return s0, 1.0 + s1, g2, s3, 1.0 + s4, g5

    m0, m1, mc = mods(mod_mine[0]), mods(mod_mine[1]), mods(mod_ctx)
    gains = [[gam[l, k][None, :] for k in range(4)] for l in range(nl)]
    conv_b = ffn_conv_b[:, None, :]

    pos = _grid_pos_emb(rows, d)

    def init(xv, pv, g0, shift, s1):
        x0 = xv + pv
        return (x0, _prenorm(x0, g0, shift, s1)), ()

    (x0, u), _ = _rowwise("init", init, rows, [(x[0], False), (pos, False)], [gains[0][0], m0[0], m0[1]],
                          [(d, F32, False), (d, ACT_DTYPE, True)])
    (uc,), _ = _rowwise("ctx_prenorm", lambda cv, g0, shift, s1: ((_prenorm(cv, g0, shift, s1),), ()),
                        rows_c, [(ctx[0], False)], [gains[0][0], mc[0], mc[1]], [(d, ACT_DTYPE, True)])
    s5_params = (s5_lam_re[0], s5_lam_im[0], s5_log_step[0], s5_b_re[0], s5_b_im[0], s5_c_re[0], s5_c_im[0])
    (bblk, cblk, atile), s5_vjp = jax.vjp(_s5_prepare, *s5_params)
    y = _s5_forward(u, uc, bblk, cblk, atile, s5_d)
    (z,), _ = _rowwise("gelu", lambda yv: ((_gelu(yv),), ()), rows, [(y, True)], [], [(d, ACT_DTYPE, False)])
    zz = _mm_cols("glu_proj", z, glu4[:, None], 0, ACT_DTYPE)

    def glu_out(zzv, xv, gate2, g1, g2, shift3, s4):
        zf = zzv.astype(F32)
        o = zf[:, :d] * _sigmoid(zf[:, d:])
        x1 = xv + gate2 * (o * _rstd(o) * g1)
        return (x1, _prenorm(x1, g2, shift3, s4)), ()

    (x1, v0), _ = _rowwise("glu_resid", glu_out, rows, [(zz, False), (x0, False)],
                           [m0[2], gains[0][1], gains[0][2], m0[3], m0[4]], [(d, F32, False), (d, ACT_DTYPE, False)])
    h0, a0, f0 = _ffn_forward("l0", v0, x1, up4, dn4, 0, conv_w[0], conv_b[0])

    def ffn_out(fv, xv, gate5, g3, g0n, shift0, s1):
        x2 = xv + gate5 * (fv * _rstd(fv) * g3)
        return (x2, _prenorm(x2, g0n, shift0, s1), fv), ()

    (x2, u1, fb0), _ = _rowwise("ffn_resid_l0", ffn_out, rows, [(f0, False), (x1, False)],
                                [m0[5], gains[0][3], gains[1][0], m1[0], m1[1]],
                                [(d, F32, False), (d, ACT_DTYPE, False), (d, ACT_DTYPE, False)])

    p1 = _pool_band("pool_band", u1, False, ACT_DTYPE)
    yr = _mm_grp("pool_proj", p1, pool_full, NN, F32)

    def pool_out(yv, xv, ps_, gate2, g1, g2, shift3, s4):
        o = yv * ps_
        x1n = xv + gate2 * (o * _rstd(o) * g1)
        return (x1n, _prenorm(x1n, g2, shift3, s4), yv), ()

    (x3, v1, yb), _ = _rowwise("pool_resid", pool_out, rows, [(yr, False), (x2, False)],
                               [pscale, m1[2], gains[1][1], gains[1][2], m1[3], m1[4]],
                               [(d, F32, False), (d, ACT_DTYPE, False), (d, ACT_DTYPE, False)])
    h1, a1, f1 = _ffn_forward("l1", v1, x3, up4, dn4, 1, conv_w[1], conv_b[1])

    def loss_head(fv, xv, tv, gate5, g3):
        err = xv + gate5 * (fv * _rstd(fv) * g3) - tv
        return (err * (1.0 / d), fv), (err * err,)

    (dx4, fb1), (sq,) = _rowwise("loss_head", loss_head, rows, [(f1, False), (x3, False), (loss_target[0], False)],
                                 [m1[5], gains[1][3]], [(d, F32, False), (d, ACT_DTYPE, False)], [d])
    loss = lax.psum(0.5 * jnp.sum(sq) / d, ("x", "y", "c"))

    dx3, g_up1, g_dn1, dcw1, dcb1, dmod_ffn1, (dgam12, dgam13) = _ffn_backward(
        "l1", dx4, fb1, x3, v1, h1, a1, up4, dn4, 1, conv_w[1], conv_b[1], m1[5], gains[1][3], gains[1][2], m1[4])

    def pool_post(dxo, yv, ps_, g1, gate2):
        yraw = yv.astype(F32)
        dy, dgate, dgam = _postnorm_bwd(dxo, yraw * ps_, g1, gate2)
        return (dy * ps_,), (dgate, dgam, dy * yraw)

    (dyr,), (dgate2_1, dgam11, dpscale) = _rowwise("pool_post_bwd", pool_post, rows, [(dx3, False), (yb, False)],
                                                   [pscale, gains[1][1], m1[2]], [(d, ACT_DTYPE, False)], [d] * 3)
    dp1 = _mm_grp("pool_proj_dx", dyr, pool_full, NT, ACT_DTYPE)
    g_pool = _mm_grp_tn("pool_proj_dw", p1, dyr, ng, ACT_DTYPE)
    du1 = _pool_band("pool_band_bwd", dp1, True, F32)

    def pre_bwd(duv, xv, dxo, g0, s1):
        dx, dshift, dscale, dgam = _prenorm_bwd(duv, xv, g0, s1)
        return (dxo + dx,), (dshift, dscale, dgam)

    (dx2,), (dshift0_1, dscale1_1, dgam10) = _rowwise(
        "pool_pre_bwd", pre_bwd, rows, [(du1, False), (x2, False), (dx3, False)],
        [gains[1][0], m1[1]], [(d, F32, False)], [d] * 3)

    dx1, g_up0, g_dn0, dcw0, dcb0, dmod_ffn0, (dgam02, dgam03) = _ffn_backward(
        "l0", dx2, fb0, x1, v0, h0, a0, up4, dn4, 0, conv_w[0], conv_b[0], m0[5], gains[0][3], gains[0][2], m0[4])

    def glu_post(dxo, zzv, g1, gate2):
        zf = zzv.astype(F32)
        val, s = zf[:, :d], _sigmoid(zf[:, d:])
        do, dgate, dgam = _postnorm_bwd(dxo, val * s, g1, gate2)
        return (jnp.concatenate([do * s, do * val * (s * (1.0 - s))], axis=1),), (dgate, dgam)

    (dzz,), (dgate2_0, dgam01) = _rowwise("glu_post_bwd", glu_post, rows, [(dx1, False), (zz, False)],
                                          [gains[0][1], m0[2]], [(2 * d, ACT_DTYPE, False)], [d] * 2)
    dz = _mm_cols_nt("glu_proj_dx", dzz, glu4[:, None], 0, F32)
    g_glu = _mm_cols_tn("glu_proj_dw", z, dzz, glu4.shape[-1], ACT_DTYPE)
    (dy,), _ = _rowwise("gelu_bwd", lambda dzv, yv: ((dzv * _gelu_grad(yv),), ()), rows,
                        [(dz, False), (y, True)], [], [(d, ACT_DTYPE, True)])
    g_pool4 = g_pool.reshape(ng, N_CHIP, ps, ch).transpose(1, 0, 2, 3).reshape(N_CHIP, ng * ps, ch)
    big = {"up0": g_up0, "up1": g_up1, "dn0": g_dn0, "dn1": g_dn1, "glu": g_glu, "pool": g_pool4}
    pairs = [_pair_sum(f"pair_{n}", g, core) for n, g in big.items()]
    ride_shapes, ride_copies = _chip_exchange_plan(pairs)
    (du0, duc, d_bblk, d_cblk, d_atile, d_dsk), others = _s5_backward(
        u, uc, dy, bblk, cblk, atile, s5_d, pairs, ride_shapes, ride_copies)
    (gx,), (dshift0_0, dscale1_0, dgam00) = _rowwise(
        "s5_pre_bwd", pre_bwd, rows, [(du0, True), (x0, False), (dx1, False)],
        [gains[0][0], m0[1]], [(d, F32, False)], [d] * 3)

    def ctx_bwd(duv, cv, g0, s1):
        _, dshift, dscale, dgam = _prenorm_bwd(duv, cv, g0, s1)
        return (), (dshift, dscale, dgam)

    _, (dshift_c, dscale_c, dgam00c) = _rowwise("ctx_pre_bwd", ctx_bwd, rows_c, [(duc, True), (ctx[0], False)],
                                                [gains[0][0], mc[1]], [], [d] * 3)
    g_s5 = s5_vjp((d_bblk, d_cblk, d_atile))

    zero_d = jnp.zeros((1, d), F32)
    dmod_lat = jnp.stack([
        jnp.concatenate([dshift0_0, dscale1_0, dgate2_0, *dmod_ffn0], axis=1),
        jnp.concatenate([dshift0_1, dscale1_1, dgate2_1, *dmod_ffn1], axis=1)])
    dmod_ctx = jnp.stack([jnp.concatenate([dshift_c, dscale_c] + [zero_d] * 4, axis=1),
                          jnp.zeros((1, N_MOD * d), F32)])
    dmod_shape = (nl, 2, N_MOD * d)
    dmod_all = _all_gather8("gather_dmod", _pack([jnp.concatenate([dmod_lat, dmod_ctx], axis=1)], SUB))
    dmod_all = jnp.stack([_unpack(dmod_all[k], [dmod_shape])[0] for k in range(N_DEV)])
    dmod16 = jnp.concatenate([dmod_all[:, :, 0], dmod_all[:, :, 1]], axis=0).transpose(1, 0, 2)
    dmod16_mine = lax.dynamic_slice_in_dim(dmod16, chip * a_cols, a_cols, axis=2)
    g_ada_w, g_cctx_part = _ada_backward(c16, dmod16_mine, ada_w)
    g_ada_b = _row_sum16("ada_bias_grad", dmod16)[:, 0]

    d_gam = jnp.stack([jnp.concatenate([dgam00 + dgam00c, dgam01, dgam02, dgam03], axis=0),
                       jnp.concatenate([dgam10, dgam11, dgam12, dgam13], axis=0)])
    small = [d_gam, 0.5 * g_cctx_part, *g_s5, jnp.sum(d_dsk, axis=0, keepdims=True), dpscale,
             jnp.stack([dcw0, dcw1]), jnp.stack([dcb0[0], dcb1[0]])]
    small_shapes = [s.shape for s in small]
    packed = _pack(small, N_DEV * SUB)
    summed = _all_reduce8("reduce_small", packed.reshape(N_DEV, -1, LANE)).reshape(-1, LANE)
    (r_gam, r_cctx, r_lam_re, r_lam_im, r_log_step, r_b_re, r_b_im, r_c_re, r_c_im,
     r_dsk, r_pscale, r_conv, r_convb) = _unpack(summed, small_shapes)
    g_norm = lax.dynamic_slice_in_dim(r_gam, chip * ds, ds, axis=2)
    g_pscale = lax.dynamic_slice_in_dim(r_pscale, chip * ds, ds, axis=1)
    g_conv = lax.dynamic_slice_in_dim(r_conv, chip * f2s, f2s, axis=2)

    red = {n: _chip_sum(f"chip_sum_{n}", p, o, chip_id) for n, p, o in zip(big, pairs, others)}
    d_rows, dn_rows = ffn_up.shape[1], ffn_down.shape[1]
    g_up = _swap_halves("swap_up0", red["up0"], core, None, nl * d_rows, 0)
    g_up = _swap_halves("swap_up1", red["up1"], core, g_up, nl * d_rows, d_rows).reshape(ffn_up.shape)
    g_dn = _swap_halves("swap_dn0", red["dn0"], core, None, nl * dn_rows, 0)
    g_dn = _swap_halves("swap_dn1", red["dn1"], core, g_dn, nl * dn_rows, dn_rows).reshape(ffn_down.shape)
    g_glu_f = _swap_halves("swap_glu", red["glu"], core, None, d, 0)
    g_pool_f = _swap_halves("swap_pool", red["pool"], core, None, ng * ps, 0)

    grads = {
        "c_ctx": r_cctx[0], "ada_w": g_ada_w, "ada_b": g_ada_b, "norm_g": g_norm,
        "s5_lam_re": r_lam_re[None], "s5_lam_im": r_lam_im[None], "s5_log_step": r_log_step[None],
        "s5_b_re": r_b_re[None], "s5_b_im": r_b_im[None], "s5_c_re": r_c_re[None], "s5_c_im": r_c_im[None],
        "s5_d": r_dsk, "s5_glu_w": g_glu_f[None], "pool_w": g_pool_f.reshape(pool_w.shape),
        "pool_scale": g_pscale, "ffn_up": g_up, "ffn_conv": g_conv, "ffn_conv_b": r_convb, "ffn_down": g_dn,
    }
    weights = {
        "c_ctx": (c_ctx, m_c_ctx, v_c_ctx), "ada_w": (ada_w, m_ada_w, v_ada_w), "ada_b": (ada_b, m_ada_b, v_ada_b),
        "norm_g": (norm_g, m_norm_g, v_norm_g), "s5_lam_re": (s5_lam_re, m_s5_lam_re, v_s5_lam_re),
        "s5_lam_im": (s5_lam_im, m_s5_lam_im, v_s5_lam_im), "s5_log_step": (s5_log_step, m_s5_log_step, v_s5_log_step),
        "s5_b_re": (s5_b_re, m_s5_b_re, v_s5_b_re), "s5_b_im": (s5_b_im, m_s5_b_im, v_s5_b_im),
        "s5_c_re": (s5_c_re, m_s5_c_re, v_s5_c_re), "s5_c_im": (s5_c_im, m_s5_c_im, v_s5_c_im),
        "s5_d": (s5_d, m_s5_d, v_s5_d), "s5_glu_w": (s5_glu_w, m_s5_glu_w, v_s5_glu_w),
        "pool_w": (pool_w, m_pool_w, v_pool_w), "pool_scale": (pool_scale, m_pool_scale, v_pool_scale),
        "ffn_up": (ffn_up, m_ffn_up, v_ffn_up), "ffn_conv": (ffn_conv, m_ffn_conv, v_ffn_conv),
        "ffn_conv_b": (ffn_conv_b, m_ffn_conv_b, v_ffn_conv_b), "ffn_down": (ffn_down, m_ffn_down, v_ffn_down),
    }
    names = list(weights)
    large = ("ada_w", "s5_glu_w", "pool_w", "ffn_up", "ffn_down")
    delta, new_m, new_v = {}, {}, {}
    for n in large:
        w, m, v = weights[n]
        delta[n], new_m[n], new_v[n] = _adamw(f"adamw_{n}", w, grads[n], m, v)
    rest = [n for n in names if n not in large]
    rest_shapes = [weights[n][0].shape for n in rest]
    packs = [_pack([weights[n][k] for n in rest], 1024) for k in range(3)]
    packs.insert(1, _pack([grads[n] for n in rest], 1024))
    for out, res in zip((delta, new_m, new_v), _adamw("adamw_small", *packs)):
        for n, val in zip(rest, _unpack(res, rest_shapes)):
            out[n] = val

    return (loss, gx[None], *[grads[n] for n in names], *[delta[n] for n in names],
            *[new_m[n] for n in names], *[new_v[n] for n in names])
```

```python
import math

import jax
import jax.numpy as jnp
from jax import lax
from jax.experimental import pallas as pl
from jax.experimental.pallas import tpu as pltpu

F32 = jnp.float32
MXU_DTYPE = jnp.bfloat16
ACT_DTYPE = jnp.bfloat16

LANE = 128
SUB = 8
PACK = 16
VMEM_LIMIT = 56 * 1024 * 1024
ELEMWISE_BLOCK = 1 << 18

N_DEV = 8
N_CHIP = 4
N_SEG = SUB
S5_GROUP = 16
S5_STATE = 64
S5_CB = LANE
S5_H = (S5_CB // S5_GROUP) * S5_STATE
S5_NS = 2 * S5_H
POOL_WINDOWS = (2, 4, 8, 16)
POOL_HALO = 16
GRID_W = 64
POS_BASE = 10000.0
RMS_EPS = 1e-6
N_MOD = 6

ADAM_LR = 0.001
ADAM_B1 = 0.9
ADAM_B2 = 0.999
ADAM_EPS = 1e-08
ADAM_WD = 0.01
ADAM_STEP = 10

NN = (((1,), (0,)), ((), ()))
NT = (((1,), (1,)), ((), ()))
TN = (((0,), (0,)), ((), ()))
MESH = pl.DeviceIdType.MESH


def _tile(n, cap, align=LANE):
    best = None
    for t in range(align, min(n, cap) + 1, align):
        if n % t == 0:
            best = t
    return n if best is None else best


def _params(sem=None):
    return pltpu.CompilerParams(dimension_semantics=sem, vmem_limit_bytes=VMEM_LIMIT)


def _dot(a, b, dims=NN):
    return lax.dot_general(a.astype(MXU_DTYPE), b.astype(MXU_DTYPE), dims, preferred_element_type=F32)


def _sigmoid(x):
    return 1.0 / (1.0 + jnp.exp(-x))


_GELU_C = math.sqrt(2.0 / math.pi)
_GELU_K = 0.044715


def _gelu(x):
    return 0.5 * x * (1.0 + jnp.tanh(_GELU_C * (x + _GELU_K * x * x * x)))


def _gelu_grad(x):
    t = jnp.tanh(_GELU_C * (x + _GELU_K * x * x * x))
    return 0.5 * (1.0 + t) + 0.5 * x * (1.0 - t * t) * _GELU_C * (1.0 + 3.0 * _GELU_K * x * x)


def _rstd(x):
    return lax.rsqrt(jnp.mean(x * x, axis=-1, keepdims=True) + RMS_EPS)


def _norm_bwd(dxh, xh, r):
    return r * (dxh - xh * jnp.mean(dxh * xh, axis=-1, keepdims=True))


def _rowwise(name, fn, rows, tiled, vecs, outs, accs=()):
    seg = rows // N_SEG
    tm = _tile(seg, 256, SUB)
    nt, ntp = rows // tm, seg // tm
    n_t, n_v, n_o, n_a = len(tiled), len(vecs), len(outs), len(accs)

    def spec(width, perm):
        if perm:
            return pl.BlockSpec((tm, width), lambda i: (i % ntp, i // ntp))
        return pl.BlockSpec((tm, width), lambda i: (i, 0))

    args, in_specs = [], []
    for arr, perm in tiled:
        width = arr.shape[-1]
        args.append(arr.reshape(seg, N_SEG * width) if perm else arr)
        in_specs.append(spec(width, perm))
    for v in vecs:
        args.append(v)
        in_specs.append(pl.BlockSpec(v.shape, lambda i: (0, 0)))
    out_shape, out_specs = [], []
    for width, dtype, perm in outs:
        out_shape.append(jax.ShapeDtypeStruct((seg, N_SEG * width) if perm else (rows, width), dtype))
        out_specs.append(spec(width, perm))
    for width in accs:
        out_shape.append(jax.ShapeDtypeStruct((SUB, width), F32))
        out_specs.append(pl.BlockSpec((SUB, width), lambda i: (0, 0)))

    def body(*refs):
        vals = [r[...] for r in refs[:n_t + n_v]]
        o_refs = refs[n_t + n_v:n_t + n_v + n_o]
        a_refs = refs[n_t + n_v + n_o:]
        o_vals, a_vals = fn(*vals)
        for r, v in zip(o_refs, o_vals):
            r[...] = v.astype(r.dtype)
        if n_a:
            @pl.when(pl.program_id(0) == 0)
            def _():
                for r in a_refs:
                    r[...] = jnp.zeros_like(r)
            for r, v in zip(a_refs, a_vals):
                r[...] += v.reshape(tm // SUB, SUB, v.shape[-1]).sum(axis=0)

    res = pl.pallas_call(
        body, name=name, grid=(nt,), in_specs=in_specs, out_specs=out_specs, out_shape=out_shape,
        compiler_params=_params(("arbitrary",)),
    )(*args)
    res = list(res)
    for k, (width, _, perm) in enumerate(outs):
        if perm:
            res[k] = res[k].reshape(rows, width)
    return res[:n_o], [jnp.sum(a, axis=0, keepdims=True) for a in res[n_o:]]


def _prenorm(x, gam, shift, scale1):
    r = _rstd(x)
    return (x * r) * gam * scale1 + shift


def _prenorm_bwd(du, x, gam, scale1):
    r = _rstd(x)
    xh = x * r
    dxn = du * scale1
    dx = _norm_bwd(dxn * gam, xh, r)
    return dx, du, du * (xh * gam), dxn * xh


def _postnorm_bwd(dxo, y, gam, gate):
    r = _rstd(y)
    yh = y * r
    dyn = dxo * gate
    dy = _norm_bwd(dyn * gam, yh, r)
    return dy, dxo * (yh * gam), dyn * yh


def _matmul(name, a, b, dims, grid, a_spec, b_spec, o_spec, out_shape, out_dtype, acc_shape):
    nk = grid[2]

    def body(a_ref, b_ref, o_ref, *scratch):
        part = _dot(a_ref[...], b_ref[...], dims)
        if nk == 1:
            o_ref[...] = part.astype(o_ref.dtype)
        else:
            acc_ref, = scratch
            k = pl.program_id(2)

            @pl.when(k == 0)
            def _():
                acc_ref[...] = part

            @pl.when(k > 0)
            def _():
                acc_ref[...] += part

            @pl.when(k == nk - 1)
            def _():
                o_ref[...] = acc_ref[...].astype(o_ref.dtype)

    return pl.pallas_call(
        body, name=name, grid=grid, in_specs=[a_spec, b_spec], out_specs=o_spec,
        out_shape=jax.ShapeDtypeStruct(out_shape, out_dtype),
        scratch_shapes=[] if nk == 1 else [pltpu.VMEM(acc_shape, F32)],
        compiler_params=_params(("parallel", "parallel", "arbitrary")),
    )(a, b)


def _mm_cols(name, a, w4, layer, out_dtype):
    m, k = a.shape
    ns = w4.shape[-1]
    tm, tn = _tile(m, 1024), _tile(ns, 1536)
    nps = ns // tn
    return _matmul(
        name, a, w4, NN, (m // tm, N_CHIP * nps, 1),
        pl.BlockSpec((tm, k), lambda i, n, kk: (i, 0)),
        pl.BlockSpec((None, None, k, tn), lambda i, n, kk: (n // nps, layer, 0, n % nps)),
        pl.BlockSpec((tm, tn), lambda i, n, kk: (i, n)),
        (m, N_CHIP * ns), out_dtype, None)


def _mm_cols_nt(name, g, w4, layer, out_dtype):
    m = g.shape[0]
    k, ns = w4.shape[-2:]
    tm, tk = _tile(m, 1024), _tile(ns, 1536)
    kps = ns // tk
    return _matmul(
        name, g, w4, NT, (m // tm, 1, N_CHIP * kps),
        pl.BlockSpec((tm, tk), lambda i, n, kk: (i, kk)),
        pl.BlockSpec((None, None, k, tk), lambda i, n, kk: (kk // kps, layer, 0, kk % kps)),
        pl.BlockSpec((tm, k), lambda i, n, kk: (i, 0)),
        (m, k), out_dtype, (tm, k))


def _mm_cols_tn(name, a, g, ns, out_dtype):
    m, k = a.shape
    tkm, tmk, tn = _tile(m, 2048), _tile(k, 1024), _tile(ns, 1536)
    nps = ns // tn
    return _matmul(
        name, a, g, TN, (k // tmk, N_CHIP * nps, m // tkm),
        pl.BlockSpec((tkm, tmk), lambda i, n, kk: (kk, i)),
        pl.BlockSpec((tkm, tn), lambda i, n, kk: (kk, n)),
        pl.BlockSpec((None, tmk, tn), lambda i, n, kk: (n // nps, i, n % nps)),
        (N_CHIP, k, ns), out_dtype, (tmk, tn))


def _mm_rows(name, a, w4, layer, out_dtype):
    m = a.shape[0]
    rs, n = w4.shape[-2:]
    tm, tk = _tile(m, 1024), _tile(rs, 1536)
    kps = rs // tk
    return _matmul(
        name, a, w4, NN, (m // tm, 1, N_CHIP * kps),
        pl.BlockSpec((tm, tk), lambda i, j, kk: (i, kk)),
        pl.BlockSpec((None, None, tk, n), lambda i, j, kk: (kk // kps, layer, kk % kps, 0)),
        pl.BlockSpec((tm, n), lambda i, j, kk: (i, 0)),
        (m, n), out_dtype, (tm, n))


def _mm_rows_nt(name, g, w4, layer, out_dtype):
    m, n = g.shape
    rs = w4.shape[-2]
    tm, tn = _tile(m, 1024), _tile(rs, 1536)
    nps = rs // tn
    return _matmul(
        name, g, w4, NT, (m // tm, N_CHIP * nps, 1),
        pl.BlockSpec((tm, n), lambda i, j, kk: (i, 0)),
        pl.BlockSpec((None, None, tn, n), lambda i, j, kk: (j // nps, layer, j % nps, 0)),
        pl.BlockSpec((tm, tn), lambda i, j, kk: (i, j)),
        (m, N_CHIP * rs), out_dtype, None)


def _mm_rows_tn(name, a, g, rs, out_dtype):
    m = a.shape[0]
    n = g.shape[1]
    tkm, tmr, tn = _tile(m, 2048), _tile(rs, 1536), _tile(n, 1024)
    mps = rs // tmr
    return _matmul(
        name, a, g, TN, (N_CHIP * mps, n // tn, m // tkm),
        pl.BlockSpec((tkm, tmr), lambda i, j, kk: (kk, i)),
        pl.BlockSpec((tkm, tn), lambda i, j, kk: (kk, j)),
        pl.BlockSpec((None, tmr, tn), lambda i, j, kk: (i // mps, i % mps, j)),
        (N_CHIP, rs, n), out_dtype, (tmr, tn))


def _mm_grp(name, a, w, dims, out_dtype):
    m = a.shape[0]
    ng, ch = w.shape[:2]
    tm = _tile(m, 1024)
    return _matmul(
        name, a, w, dims, (m // tm, ng, 1),
        pl.BlockSpec((tm, ch), lambda i, g, kk: (i, g)),
        pl.BlockSpec((None, ch, ch), lambda i, g, kk: (g, 0, 0)),
        pl.BlockSpec((tm, ch), lambda i, g, kk: (i, g)),
        (m, ng * ch), out_dtype, None)


def _mm_grp_tn(name, a, g, ng, out_dtype):
    m = a.shape[0]
    ch = a.shape[1] // ng
    tk = _tile(m, 2048)
    return _matmul(
        name, a, g, TN, (ng, 1, m // tk),
        pl.BlockSpec((tk, ch), lambda i, j, kk: (kk, i)),
        pl.BlockSpec((tk, ch), lambda i, j, kk: (kk, i)),
        pl.BlockSpec((None, ch, ch), lambda i, j, kk: (i, 0, 0)),
        (ng, ch, ch), out_dtype, (ch, ch))


def _view2d(a):
    return a.reshape(-1, a.shape[-1])


def _elementwise(name, fn, ins, out_dtypes):
    r, c = ins[0].shape
    tr = _tile(r, max(PACK, ELEMWISE_BLOCK // c), PACK)
    spec = pl.BlockSpec((tr, c), lambda i: (i, 0))

    def body(*refs):
        outs = fn(*[x[...] for x in refs[:len(ins)]])
        for o_ref, o in zip(refs[len(ins):], outs):
            o_ref[...] = o.astype(o_ref.dtype)

    return pl.pallas_call(
        body, name=name, grid=(r // tr,), in_specs=[spec] * len(ins), out_specs=[spec] * len(out_dtypes),
        out_shape=[jax.ShapeDtypeStruct((r, c), d) for d in out_dtypes],
        compiler_params=_params(("parallel",)),
    )(*ins)


def _adamw_math(w, g, m, v):
    m = ADAM_B1 * m + (1.0 - ADAM_B1) * g
    v = ADAM_B2 * v + (1.0 - ADAM_B2) * (g * g)
    m_hat = m / (1.0 - ADAM_B1 ** ADAM_STEP)
    v_hat = v / (1.0 - ADAM_B2 ** ADAM_STEP)
    delta = -ADAM_LR * (m_hat / (jnp.sqrt(v_hat) + ADAM_EPS) + ADAM_WD * w)
    return delta, m, v


def _adamw(name, w, g, m, v):
    shape = w.shape
    outs = _elementwise(name, _adamw_math, [_view2d(w), _view2d(g), _view2d(m), _view2d(v)], [F32, F32, F32])
    return [o.reshape(shape) for o in outs]


def _coords():
    return lax.axis_index("x"), lax.axis_index("y"), lax.axis_index("c")


def _peer(x, y, c, k):
    return (x ^ (k >> 2), y ^ ((k >> 1) & 1), c ^ (k & 1))


def _all_gather8(name, block):
    r = block.shape[0]

    def body(x_ref, out_ref, send_sems, recv_sems):
        x, y, c = _coords()
        me = 4 * x + 2 * y + c
        out_ref[me] = x_ref[...]
        copies = []
        for k in range(1, N_DEV):
            cp = pltpu.make_async_remote_copy(
                src_ref=x_ref, dst_ref=out_ref.at[me], send_sem=send_sems.at[k], recv_sem=recv_sems.at[k],
                device_id=_peer(x, y, c, k), device_id_type=MESH)
            cp.start()
            copies.append(cp)
        for cp in copies:
            cp.wait()

    return pl.pallas_call(
        body, name=name,
        in_specs=[pl.BlockSpec(memory_space=pltpu.VMEM)], out_specs=pl.BlockSpec(memory_space=pltpu.VMEM),
        out_shape=jax.ShapeDtypeStruct((N_DEV, r, LANE), F32),
        scratch_shapes=[pltpu.SemaphoreType.DMA((N_DEV,)), pltpu.SemaphoreType.DMA((N_DEV,))],
        compiler_params=pltpu.CompilerParams(vmem_limit_bytes=VMEM_LIMIT),
    )(block)


def _all_reduce8(name, parts):
    r = parts.shape[1]

    def body(p_ref, out_ref, rbuf, send1, recv1, send2, recv2):
        x, y, c = _coords()
        me = 4 * x + 2 * y + c
        first = []
        for k in range(1, N_DEV):
            px, py, pc = _peer(x, y, c, k)
            cp = pltpu.make_async_remote_copy(
                src_ref=p_ref.at[4 * px + 2 * py + pc], dst_ref=rbuf.at[me], send_sem=send1.at[k],
                recv_sem=recv1.at[k], device_id=(px, py, pc), device_id_type=MESH)
            cp.start()
            first.append(cp)
        rbuf[me] = p_ref[me]
        for cp in first:
            cp.wait()
        acc = rbuf[0]
        for d in range(1, N_DEV):
            acc = acc + rbuf[d]
        out_ref[me] = acc
        second = []
        for k in range(1, N_DEV):
            cp = pltpu.make_async_remote_copy(
                src_ref=out_ref.at[me], dst_ref=out_ref.at[me], send_sem=send2.at[k], recv_sem=recv2.at[k],
                device_id=_peer(x, y, c, k), device_id_type=MESH)
            cp.start()
            second.append(cp)
        for cp in second:
            cp.wait()

    return pl.pallas_call(
        body, name=name,
        in_specs=[pl.BlockSpec(memory_space=pltpu.VMEM)], out_specs=pl.BlockSpec(memory_space=pltpu.VMEM),
        out_shape=jax.ShapeDtypeStruct((N_DEV, r, LANE), F32),
        scratch_shapes=[pltpu.VMEM((N_DEV, r, LANE), F32)] + [pltpu.SemaphoreType.DMA((N_DEV,))] * 4,
        compiler_params=pltpu.CompilerParams(vmem_limit_bytes=VMEM_LIMIT),
    )(parts)


ANY_SPEC = pl.BlockSpec(memory_space=pl.ANY)
COMM_BLOCK_BYTES = 2 << 20


def _staged_call(name, body, core, grid, in_specs, ins, out_shape, scratch, aliases=None):
    return pl.pallas_call(
        body, name=name,
        grid_spec=pltpu.PrefetchScalarGridSpec(
            num_scalar_prefetch=1, grid=grid, in_specs=in_specs, out_specs=ANY_SPEC, scratch_shapes=scratch),
        out_shape=out_shape, input_output_aliases=aliases or {},
        compiler_params=pltpu.CompilerParams(
            dimension_semantics=("arbitrary",) * len(grid), vmem_limit_bytes=VMEM_LIMIT, has_side_effects=True),
    )(core, *ins)


def _rows_tile(rows, cols, itemsize):
    return _tile(rows, max(PACK, COMM_BLOCK_BYTES // (cols * itemsize)), PACK)


def _chip_peer(x, y, c, k):
    return (x ^ (k >> 1), y ^ (k & 1), c)


def _cast_own_half(name, w, pos):
    r, cols = w.shape
    h = r // 2
    tr = _tile(h, max(PACK, ELEMWISE_BLOCK // cols), PACK)
    nb = h // tr

    def body(p_ref, w_ref, o_ref):
        o_ref[...] = w_ref[...].astype(o_ref.dtype)

    return pl.pallas_call(
        body, name=name,
        grid_spec=pltpu.PrefetchScalarGridSpec(
            num_scalar_prefetch=1, grid=(nb,),
            in_specs=[pl.BlockSpec((tr, cols), lambda j, p: (p[0] * nb + j, 0))],
            out_specs=pl.BlockSpec((tr, cols), lambda j, p: ((2 * p[1] + p[0]) * nb + j, 0))),
        out_shape=jax.ShapeDtypeStruct((N_CHIP * r, cols), MXU_DTYPE),
        compiler_params=_params(("parallel",)),
    )(pos, w)


def _gather_ici_plan(shard_rows):
    def copies(refs, _, send_sems, recv_sems):
        x, y, c = _coords()
        out = []
        for i, (ref, r) in enumerate(zip(refs, shard_rows)):
            h = r // 2
            mine = ref.at[pl.ds(pl.multiple_of((2 * x + y) * r + c * h, PACK), h)]
            for k in (1, 2, 3):
                n = 3 * i + k - 1
                out.append(pltpu.make_async_remote_copy(
                    src_ref=mine, dst_ref=mine, send_sem=send_sems.at[n], recv_sem=recv_sems.at[n],
                    device_id=_chip_peer(x, y, c, k), device_id_type=MESH))
        return out

    return copies


def _swap_gathered(name, part, core, r):
    cols = part.shape[1]
    h = r // 2
    full = jax.ShapeDtypeStruct(part.shape, part.dtype)
    tr2 = _rows_tile(h, cols, 2)
    nb2 = h // tr2

    def swap_body(c_ref, mine_ref, full_ref, send_sem, recv_sem):
        s, j = pl.program_id(0), pl.program_id(1)
        x, y, c = _coords()
        dst = full_ref.at[pl.ds(pl.multiple_of(s * r + c * h + j * tr2, PACK), tr2)]
        cp = pltpu.make_async_remote_copy(
            src_ref=mine_ref, dst_ref=dst, send_sem=send_sem, recv_sem=recv_sem,
            device_id=(x, y, 1 - c), device_id_type=MESH)
        cp.start()
        cp.wait_send()

        @pl.when((s == N_CHIP - 1) & (j == nb2 - 1))
        def _():
            landed = full_ref.at[pl.ds(0, N_CHIP * h)]
            pltpu.make_async_remote_copy(
                src_ref=landed, dst_ref=landed, send_sem=send_sem, recv_sem=recv_sem,
                device_id=(x, y, 1 - c), device_id_type=MESH).wait_recv()

    return _staged_call(
        name + "_d2d", swap_body, core, (N_CHIP, nb2),
        [pl.BlockSpec((tr2, cols), lambda s, j, c_ref: ((2 * s + c_ref[0]) * nb2 + j, 0))], [part], full,
        [pltpu.SemaphoreType.DMA(()), pltpu.SemaphoreType.DMA(())], aliases={1: 0}).reshape(N_CHIP, r, cols)


def _pair_sum(name, g, core):
    n, r, cols = g.shape
    h = r // 2
    tr = _rows_tile(h, cols, g.dtype.itemsize)
    nb = h // tr
    half = jax.ShapeDtypeStruct((n, h, cols), g.dtype)

    def send_body(c_ref, g_ref, got_ref, send_sem, recv_sem):
        s, j = pl.program_id(0), pl.program_id(1)
        x, y, c = _coords()
        dst = got_ref.at[pl.ds(pl.multiple_of(s * h + j * tr, PACK), tr)]
        cp = pltpu.make_async_remote_copy(
            src_ref=g_ref, dst_ref=dst, send_sem=send_sem, recv_sem=recv_sem,
            device_id=(x, y, 1 - c), device_id_type=MESH)
        cp.start()
        cp.wait_send()

        @pl.when((s == n - 1) & (j == nb - 1))
        def _():
            pltpu.make_async_remote_copy(
                src_ref=got_ref, dst_ref=got_ref, send_sem=send_sem, recv_sem=recv_sem,
                device_id=(x, y, 1 - c), device_id_type=MESH).wait_recv()

    got = _staged_call(
        name + "_send", send_body, core, (n, nb),
        [pl.BlockSpec((tr, cols), lambda s, j, c_ref: ((2 * s + 1 - c_ref[0]) * nb + j, 0))],
        [g.reshape(n * r, cols)], jax.ShapeDtypeStruct((n * h, cols), g.dtype),
        [pltpu.SemaphoreType.DMA(()), pltpu.SemaphoreType.DMA(())]).reshape(n, h, cols)

    def add_body(c_ref, own_ref, got_ref, o_ref):
        o_ref[...] = (own_ref[...].astype(F32) + got_ref[...].astype(F32)).astype(o_ref.dtype)

    blk = pl.BlockSpec((None, tr, cols), lambda s, j, c_ref: (s, j, 0))
    return pl.pallas_call(
        add_body, name=name + "_add",
        grid_spec=pltpu.PrefetchScalarGridSpec(
            num_scalar_prefetch=1, grid=(n, nb),
            in_specs=[pl.BlockSpec((None, tr, cols), lambda s, j, c_ref: (s, c_ref[0] * nb + j, 0)), blk],
            out_specs=blk),
        out_shape=half, compiler_params=_params(("parallel", "parallel")),
    )(core, g, got)


def _chip_exchange_plan(pairs):
    shapes = [jax.ShapeDtypeStruct((N_CHIP - 1,) + p.shape[1:], p.dtype) for p in pairs]

    def copies(in_refs, out_refs, send_sems, recv_sems):
        x, y, c = _coords()
        out = []
        for i, (src, dst) in enumerate(zip(in_refs, out_refs)):
            for k in (1, 2, 3):
                px, py, pc = _chip_peer(x, y, c, k)
                n = 3 * i + k - 1
                out.append(pltpu.make_async_remote_copy(
                    src_ref=src.at[2 * px + py], dst_ref=dst.at[k - 1], send_sem=send_sems.at[n],
                    recv_sem=recv_sems.at[n], device_id=(px, py, pc), device_id_type=MESH))
        return out

    return shapes, copies


def _chip_sum(name, pair, got, chip):
    _, h, cols = pair.shape
    tr = _tile(h, max(PACK, ELEMWISE_BLOCK // cols), PACK)

    def body(chip_ref, own_ref, a_ref, b_ref, c_ref, o_ref):
        o_ref[...] = ((own_ref[...].astype(F32) + a_ref[...].astype(F32)) + b_ref[...].astype(F32)) + c_ref[...].astype(F32)

    return pl.pallas_call(
        body, name=name,
        grid_spec=pltpu.PrefetchScalarGridSpec(
            num_scalar_prefetch=1, grid=(h // tr,),
            in_specs=[pl.BlockSpec((None, tr, cols), lambda j, chip_ref: (chip_ref[0], j, 0))]
            + [pl.BlockSpec((None, tr, cols), (lambda j, chip_ref, k=k: (k, j, 0))) for k in range(N_CHIP - 1)],
            out_specs=pl.BlockSpec((tr, cols), lambda j, chip_ref: (j, 0))),
        out_shape=jax.ShapeDtypeStruct((h, cols), F32), compiler_params=_params(("parallel",)),
    )(chip, pair, got, got, got)


def _swap_halves(name, red, core, into, total_rows, base):
    h, cols = red.shape
    tr = _rows_tile(h, cols, 4)
    nb = h // tr

    def body(c_ref, red_ref, *rest):
        full_ref, send_sem, recv_sem, local_sem = rest[-4:]
        j = pl.program_id(0)
        x, y, c = _coords()
        dst = full_ref.at[pl.ds(pl.multiple_of(base + c * h + j * tr, SUB), tr)]
        local = pltpu.make_async_copy(red_ref, dst, local_sem)
        remote = pltpu.make_async_remote_copy(
            src_ref=red_ref, dst_ref=dst, send_sem=send_sem, recv_sem=recv_sem,
            device_id=(x, y, 1 - c), device_id_type=MESH)
        local.start()
        remote.start()
        remote.wait_send()
        local.wait()

        @pl.when(j == nb - 1)
        def _():
            landed = full_ref.at[pl.ds(0, h)]
            pltpu.make_async_remote_copy(
                src_ref=landed, dst_ref=landed, send_sem=send_sem, recv_sem=recv_sem,
                device_id=(x, y, 1 - c), device_id_type=MESH).wait_recv()

    in_specs = [pl.BlockSpec((tr, cols), lambda j, c_ref: (j, 0))]
    ins = [red]
    aliases = None
    if into is not None:
        in_specs.append(ANY_SPEC)
        ins.append(into)
        aliases = {2: 0}
    return _staged_call(
        name, body, core, (nb,), in_specs, ins, jax.ShapeDtypeStruct((total_rows, cols), F32),
        [pltpu.SemaphoreType.DMA(())] * 3, aliases)


def _cmul(ar, ai, br, bi):
    return ar * br - ai * bi, ar * bi + ai * br


def _cpow(ar, ai, n):
    rr, ri = jnp.ones_like(ar), jnp.zeros_like(ai)
    br, bi = ar, ai
    while n:
        if n & 1:
            rr, ri = _cmul(rr, ri, br, bi)
        br, bi = _cmul(br, bi, br, bi)
        n >>= 1
    return rr, ri


def _tile_rows(t):
    if isinstance(t, int):
        return pl.ds(t * SUB, SUB)
    return pl.ds(pl.multiple_of(t * SUB, SUB), SUB)


SCAN_UNROLL = 4


def _unrolled_loop(n, body, carry):
    trips = n // SCAN_UNROLL

    def trip(o, c):
        for k in range(SCAN_UNROLL):
            c = body(o * SCAN_UNROLL + k, c)
        return c

    carry = lax.fori_loop(0, trips, trip, carry)
    for i in range(trips * SCAN_UNROLL, n):
        carry = body(i, carry)
    return carry


def _scan_setup(buf, steps, ar, ai, h0r, h0i, rev):
    def total(i, carry):
        sr, si = carry
        rows = _tile_rows(steps - 1 - i if rev else i)
        pr, pi = _cmul(ar, ai, sr, si)
        return pr + buf[rows, 0:S5_H], pi + buf[rows, S5_H:S5_NS]

    zero = jnp.zeros((SUB, S5_H), F32)
    tot_r, tot_i = _unrolled_loop(steps, total, (zero, zero))
    pw_r, pw_i = _cpow(ar[0:1], ai[0:1], steps)
    row = lax.broadcasted_iota(jnp.int32, (SUB, S5_H), 0)
    cur_r, cur_i = h0r, h0i
    init_r, init_i = zero, zero
    for s in (range(N_SEG - 1, -1, -1) if rev else range(N_SEG)):
        init_r = jnp.where(row == s, cur_r, init_r)
        init_i = jnp.where(row == s, cur_i, init_i)
        nr, ni = _cmul(pw_r, pw_i, cur_r, cur_i)
        cur_r, cur_i = nr + tot_r[s:s + 1], ni + tot_i[s:s + 1]
    return init_r, init_i, cur_r, cur_i


def _scan(buf, steps, ar, ai, h0r, h0i, rev, store):
    init_r, init_i, fin_r, fin_i = _scan_setup(buf, steps, ar, ai, h0r, h0i, rev)
    if store:
        def step(i, carry):
            hr, hi = carry
            rows = _tile_rows(steps - 1 - i if rev else i)
            pr, pi = _cmul(ar, ai, hr, hi)
            hr, hi = pr + buf[rows, 0:S5_H], pi + buf[rows, S5_H:S5_NS]
            buf[rows, 0:S5_H] = hr
            buf[rows, S5_H:S5_NS] = hi
            return hr, hi

        _unrolled_loop(steps, step, (init_r, init_i))
    return fin_r, fin_i


def _adjoint_scan(gbuf, hbuf, steps, ar, ai, l0r, l0i, hin_r, hin_i, rev):
    ci = -ai
    arev = not rev
    init_r, init_i, fin_r, fin_i = _scan_setup(gbuf, steps, ar, ci, l0r, l0i, arev)
    zero = jnp.zeros((SUB, S5_H), F32)

    def update(t, hp_r, hp_i, carry):
        lr, li, dr, di = carry
        rows = _tile_rows(t)
        pr, pi = _cmul(ar, ci, lr, li)
        lr, li = pr + gbuf[rows, 0:S5_H], pi + gbuf[rows, S5_H:S5_NS]
        gbuf[rows, 0:S5_H] = lr
        gbuf[rows, S5_H:S5_NS] = li
        return lr, li, dr + lr * hp_r + li * hp_i, di + li * hp_r - lr * hp_i

    def step(i, carry):
        t = steps - 1 - i if rev is False else i
        prev = _tile_rows(t - 1 if rev is False else t + 1)
        return update(t, hbuf[prev, 0:S5_H], hbuf[prev, S5_H:S5_NS], carry)

    carry = _unrolled_loop(steps - 1, step, (init_r, init_i, zero, zero))
    row = lax.broadcasted_iota(jnp.int32, (SUB, S5_H), 0)
    if rev:
        last, edge, shift, t = _tile_rows(0), N_SEG - 1, SUB - 1, steps - 1
    else:
        last, edge, shift, t = _tile_rows(steps - 1), 0, 1, 0
    hp_r = jnp.where(row == edge, hin_r, pltpu.roll(hbuf[last, 0:S5_H], shift, 0))
    hp_i = jnp.where(row == edge, hin_i, pltpu.roll(hbuf[last, S5_H:S5_NS], shift, 0))
    _, _, dr, di = update(t, hp_r, hp_i, carry)
    return fin_r, fin_i, dr, di


def _s5_chunk(rows):
    return _tile(rows, 512, PACK)


def _s5_forward(u, uc, bblk, cblk, atile, dsk, ride_bufs, ride_copies):
    rows, d = u.shape
    rows_c = uc.shape[0]
    nj = d // S5_CB
    steps, steps_c = rows // N_SEG, rows_c // N_SEG
    rc = _s5_chunk(rows)
    n_ride = len(ride_bufs)
    n_sem = 3 * n_ride

    def body(u_ref, uc_ref, b_ref, c_ref, a_ref, d_ref, *rest):
        y_ref = rest[n_ride]
        ride = rest[n_ride + 1:2 * n_ride + 1]
        buf, bufc, send_sems, recv_sems = rest[2 * n_ride + 1:]

        @pl.when(pl.program_id(0) == 0)
        def _():
            for cp in ride_copies(ride, ride, send_sems, recv_sems):
                cp.start()

        zero = jnp.zeros((1, S5_H), F32)
        for dr in (0, 1):
            rev = dr == 1
            ar, ai = a_ref[dr, :, 0:S5_H], a_ref[dr, :, S5_H:S5_NS]
            bm, cm = b_ref[dr].astype(MXU_DTYPE), c_ref[dr].astype(MXU_DTYPE)
            bufc[...] = _dot(uc_ref[...], bm)
            fin_r, fin_i = _scan(bufc, steps_c, ar, ai, zero, zero, rev, False)

            def project(r, _):
                rs = pl.ds(pl.multiple_of(r * rc, rc), rc)
                buf[rs, :] = _dot(u_ref[rs, :], bm)
                return 0

            lax.fori_loop(0, rows // rc, project, 0)
            _scan(buf, steps, ar, ai, fin_r, fin_i, rev, True)

            def readout(r, _):
                rs = pl.ds(pl.multiple_of(r * rc, rc), rc)
                yv = _dot(buf[rs, :], cm)
                if dr == 0:
                    y_ref[rs, :] = u_ref[rs, :].astype(F32) * d_ref[...] + yv
                else:
                    y_ref[rs, :] += yv
                return 0

            lax.fori_loop(0, rows // rc, readout, 0)

        @pl.when(pl.program_id(0) == nj - 1)
        def _():
            for cp in ride_copies(ride, ride, send_sems, recv_sems):
                cp.wait()

    res = pl.pallas_call(
        body, name="s5_forward", grid=(nj,),
        in_specs=[
            pl.BlockSpec((rows, S5_CB), lambda j: (0, j)),
            pl.BlockSpec((rows_c, S5_CB), lambda j: (0, j)),
            pl.BlockSpec((2, None, S5_CB, S5_NS), lambda j: (0, j, 0, 0)),
            pl.BlockSpec((2, None, S5_NS, S5_CB), lambda j: (0, j, 0, 0)),
            pl.BlockSpec((2, None, SUB, S5_NS), lambda j: (0, j, 0, 0)),
            pl.BlockSpec((1, S5_CB), lambda j: (0, j)),
        ] + [ANY_SPEC] * n_ride,
        out_specs=[pl.BlockSpec((rows, S5_CB), lambda j: (0, j))] + [ANY_SPEC] * n_ride,
        out_shape=[jax.ShapeDtypeStruct((rows, d), F32)]
        + [jax.ShapeDtypeStruct(b.shape, b.dtype) for b in ride_bufs],
        input_output_aliases={6 + i: 1 + i for i in range(n_ride)},
        scratch_shapes=[pltpu.VMEM((rows, S5_NS), F32), pltpu.VMEM((rows_c, S5_NS), F32),
                        pltpu.SemaphoreType.DMA((n_sem,)), pltpu.SemaphoreType.DMA((n_sem,))],
        compiler_params=pltpu.CompilerParams(
            dimension_semantics=("arbitrary",), vmem_limit_bytes=VMEM_LIMIT, has_side_effects=True),
    )(u, uc, bblk, cblk, atile, dsk, *ride_bufs)
    return res[0], res[1:]


def _s5_backward(u, uc, dy, bblk, cblk, atile, dsk, ride_ins, ride_shapes, ride_copies):
    rows, d = u.shape
    rows_c = uc.shape[0]
    nj = d // S5_CB
    steps, steps_c = rows // N_SEG, rows_c // N_SEG
    rc = _s5_chunk(rows)
    nchunk = rows // rc
    n_ride = len(ride_ins)
    n_sem = 3 * n_ride

    def body(u_ref, uc_ref, dy_ref, b_ref, c_ref, a_ref, d_ref, *rest):
        ride_in = rest[:n_ride]
        du_ref, duc_ref, db_ref, dc_ref, da_ref, dd_ref = rest[n_ride:n_ride + 6]
        ride_out = rest[n_ride + 6:2 * n_ride + 6]
        hbuf, gbuf, hcbuf, gcbuf, send_sems, recv_sems = rest[2 * n_ride + 6:]

        @pl.when(pl.program_id(0) == 0)
        def _():
            for cp in ride_copies(ride_in, ride_out, send_sems, recv_sems):
                cp.start()

        zero = jnp.zeros((1, S5_H), F32)
        db_ref[...] = jnp.zeros_like(db_ref)
        dc_ref[...] = jnp.zeros_like(dc_ref)
        dd_ref[...] = jnp.zeros_like(dd_ref)
        for dr in (0, 1):
            rev = dr == 1
            ar, ai = a_ref[dr, :, 0:S5_H], a_ref[dr, :, S5_H:S5_NS]
            bm, cm = b_ref[dr].astype(MXU_DTYPE), c_ref[dr].astype(MXU_DTYPE)
            hcbuf[...] = _dot(uc_ref[...], bm)
            hin_r, hin_i = _scan(hcbuf, steps_c, ar, ai, zero, zero, rev, True)

            def project(r, _):
                rs = pl.ds(pl.multiple_of(r * rc, rc), rc)
                hbuf[rs, :] = _dot(u_ref[rs, :], bm)
                return 0

            lax.fori_loop(0, nchunk, project, 0)
            _scan(hbuf, steps, ar, ai, hin_r, hin_i, rev, True)

            def readout_bwd(r, _):
                rs = pl.ds(pl.multiple_of(r * rc, rc), rc)
                dyv = dy_ref[rs, :]
                gbuf[rs, :] = _dot(dyv, cm, NT)
                dc_ref[dr] += _dot(hbuf[rs, :], dyv, TN)
                return 0

            lax.fori_loop(0, nchunk, readout_bwd, 0)
            lf_r, lf_i, dar, dai = _adjoint_scan(gbuf, hbuf, steps, ar, ai, zero, zero, hin_r, hin_i, rev)
            gcbuf[...] = jnp.zeros_like(gcbuf)
            _, _, dar_c, dai_c = _adjoint_scan(gcbuf, hcbuf, steps_c, ar, ai, lf_r, lf_i, zero, zero, rev)
            da_ref[dr, :, 0:S5_H] = dar + dar_c
            da_ref[dr, :, S5_H:S5_NS] = dai + dai_c

            def project_bwd(r, _):
                rs = pl.ds(pl.multiple_of(r * rc, rc), rc)
                lam = gbuf[rs, :]
                uv = u_ref[rs, :]
                part = _dot(lam, bm, NT)
                db_ref[dr] += _dot(uv, lam, TN)
                if dr == 0:
                    dyv = dy_ref[rs, :].astype(F32)
                    du_ref[rs, :] = part + dyv * d_ref[...]
                    dd_ref[...] += (dyv * uv.astype(F32)).reshape(rc // SUB, SUB, S5_CB).sum(axis=0)
                else:
                    du_ref[rs, :] += part
                return 0

            lax.fori_loop(0, nchunk, project_bwd, 0)
            lam_c = gcbuf[...]
            part_c = _dot(lam_c, bm, NT)
            db_ref[dr] += _dot(uc_ref[...], lam_c, TN)
            if dr == 0:
                duc_ref[...] = part_c
            else:
                duc_ref[...] += part_c

        @pl.when(pl.program_id(0) == nj - 1)
        def _():
            for cp in ride_copies(ride_in, ride_out, send_sems, recv_sems):
                cp.wait()

    blk = lambda r: pl.BlockSpec((r, S5_CB), lambda j: (0, j))
    res = pl.pallas_call(
        body, name="s5_backward", grid=(nj,),
        in_specs=[
            blk(rows), blk(rows_c), blk(rows),
            pl.BlockSpec((2, None, S5_CB, S5_NS), lambda j: (0, j, 0, 0)),
            pl.BlockSpec((2, None, S5_NS, S5_CB), lambda j: (0, j, 0, 0)),
            pl.BlockSpec((2, None, SUB, S5_NS), lambda j: (0, j, 0, 0)),
            pl.BlockSpec((1, S5_CB), lambda j: (0, j)),
        ] + [ANY_SPEC] * n_ride,
        out_specs=[
            blk(rows), blk(rows_c),
            pl.BlockSpec((2, None, S5_CB, S5_NS), lambda j: (0, j, 0, 0)),
            pl.BlockSpec((2, None, S5_NS, S5_CB), lambda j: (0, j, 0, 0)),
            pl.BlockSpec((2, None, SUB, S5_NS), lambda j: (0, j, 0, 0)),
            pl.BlockSpec((SUB, S5_CB), lambda j: (0, j)),
        ] + [ANY_SPEC] * n_ride,
        out_shape=[
            jax.ShapeDtypeStruct((rows, d), F32), jax.ShapeDtypeStruct((rows_c, d), F32),
            jax.ShapeDtypeStruct(bblk.shape, F32), jax.ShapeDtypeStruct(cblk.shape, F32),
            jax.ShapeDtypeStruct(atile.shape, F32), jax.ShapeDtypeStruct((SUB, d), F32),
        ] + list(ride_shapes),
        scratch_shapes=[pltpu.VMEM((rows, S5_NS), F32), pltpu.VMEM((rows, S5_NS), F32),
                        pltpu.VMEM((rows_c, S5_NS), F32), pltpu.VMEM((rows_c, S5_NS), F32),
                        pltpu.SemaphoreType.DMA((n_sem,)), pltpu.SemaphoreType.DMA((n_sem,))],
        compiler_params=pltpu.CompilerParams(
            dimension_semantics=("arbitrary",), vmem_limit_bytes=VMEM_LIMIT, has_side_effects=True),
    )(u, uc, dy, bblk, cblk, atile, dsk, *ride_ins)
    return res[:6], res[6:]


def _s5_prepare(lam_re, lam_im, log_step, b_re, b_im, c_re, c_im):
    nd, g, p = lam_re.shape
    gb = S5_CB // S5_GROUP
    nj = g // gb
    dt = jnp.exp(log_step)[..., None]
    mag = jnp.exp(lam_re * dt)
    abar_re = mag * jnp.cos(lam_im * dt)
    abar_im = mag * jnp.sin(lam_im * dt)
    nr, ni = abar_re - 1.0, abar_im
    den = lam_re * lam_re + lam_im * lam_im
    fr = (nr * lam_re + ni * lam_im) / den
    fi = (ni * lam_re - nr * lam_im) / den
    bbar_re = fr[..., None] * b_re - fi[..., None] * b_im
    bbar_im = fr[..., None] * b_im + fi[..., None] * b_re
    eye = jnp.eye(gb, dtype=bool)

    def diag_in(w):
        w = w.reshape(nd, nj, gb, p, S5_GROUP).transpose(0, 1, 2, 4, 3)
        w = jnp.where(eye[None, None, :, None, :, None], w[:, :, :, :, None, :], 0.0)
        return w.reshape(nd, nj, gb * S5_GROUP, gb * p)

    def diag_out(w):
        w = w.reshape(nd, nj, gb, S5_GROUP, p).transpose(0, 1, 2, 4, 3)
        w = jnp.where(eye[None, None, :, None, :, None], w[:, :, :, :, None, :], 0.0)
        return w.reshape(nd, nj, gb * p, gb * S5_GROUP)

    bblk = jnp.concatenate([diag_in(bbar_re), diag_in(bbar_im)], axis=-1)
    cblk = jnp.concatenate([diag_out(c_re), -diag_out(c_im)], axis=-2)
    a2 = jnp.concatenate([abar_re.reshape(nd, nj, gb * p), abar_im.reshape(nd, nj, gb * p)], axis=-1)
    atile = jnp.broadcast_to(a2[:, :, None, :], (nd, nj, SUB, 2 * gb * p))
    return bblk, cblk, atile


def _shifted(x, prev_row, next_row):
    n = x.shape[0]
    row = lax.broadcasted_iota(jnp.int32, x.shape, 0)
    xp = jnp.where(row == 0, prev_row, pltpu.roll(x, 1, 0))
    xn = jnp.where(row == n - 1, next_row, pltpu.roll(x, n - 1, 0))
    return xp, xn


def _edge_rows(ref, r0, n, total, group):
    lo = pl.multiple_of(jnp.maximum(r0 - group, 0), group)
    hi = pl.multiple_of(jnp.minimum(r0 + n, total - group), group)
    prev_row = ref[pl.ds(lo, group), :].astype(F32)[group - 1:group] * (r0 > 0).astype(F32)
    next_row = ref[pl.ds(hi, group), :].astype(F32)[0:1] * (r0 + n < total).astype(F32)
    return prev_row, next_row


def _conv_rows(ref, r0, n, total, w_ref, b_ref):
    x = ref[pl.ds(r0, n), :].astype(F32)
    xp, xn = _shifted(x, *_edge_rows(ref, r0, n, total, PACK))
    hc = w_ref[0:1, :] * xp + w_ref[1:2, :] * x + w_ref[2:3, :] * xn + b_ref[...]
    return hc, xp, x, xn


def _conv_specs(rows, f, tc):
    nt = f // tc
    val = lambda r: pl.BlockSpec((r, tc), lambda j: (0, j))
    gate = lambda r: pl.BlockSpec((r, tc), lambda j: (0, j + nt))
    return val, gate


def _conv_swiglu_fwd(name, h, cw, cb):
    rows, f2 = h.shape
    f = f2 // 2
    tc = _tile(f, 256)
    rc = _tile(rows, 256, PACK)
    val, gate = _conv_specs(rows, f, tc)

    def body(hv_ref, hg_ref, wv_ref, wg_ref, bv_ref, bg_ref, a_ref):
        def chunk(r, _):
            r0 = pl.multiple_of(r * rc, rc)
            hv = _conv_rows(hv_ref, r0, rc, rows, wv_ref, bv_ref)[0]
            hg = _conv_rows(hg_ref, r0, rc, rows, wg_ref, bg_ref)[0]
            a_ref[pl.ds(r0, rc), :] = (hg * _sigmoid(hg) * hv).astype(a_ref.dtype)
            return 0

        lax.fori_loop(0, rows // rc, chunk, 0)

    return pl.pallas_call(
        body, name=name, grid=(f // tc,),
        in_specs=[val(rows), gate(rows), val(3), gate(3), val(1), gate(1)],
        out_specs=val(rows), out_shape=jax.ShapeDtypeStruct((rows, f), ACT_DTYPE),
        compiler_params=_params(("parallel",)),
    )(h, h, cw, cw, cb, cb)


def _conv_swiglu_bwd(name, da, h, cw, cb):
    rows, f2 = h.shape
    f = f2 // 2
    tc = _tile(f, 256)
    rc = _tile(rows, 256, PACK)
    val, gate = _conv_specs(rows, f, tc)

    def body(da_ref, hv_ref, hg_ref, wv_ref, wg_ref, bv_ref, bg_ref,
             dhv_ref, dhg_ref, dwv_ref, dwg_ref, dbv_ref, dbg_ref, sv, sg):
        def first(r, carry):
            r0 = pl.multiple_of(r * rc, rc)
            rs = pl.ds(r0, rc)
            hv, vp, vx, vn = _conv_rows(hv_ref, r0, rc, rows, wv_ref, bv_ref)
            hg, gp, gx, gn = _conv_rows(hg_ref, r0, rc, rows, wg_ref, bg_ref)
            d = da_ref[rs, :].astype(F32)
            s = _sigmoid(hg)
            dv = d * (hg * s)
            dg = d * hv * (s * (1.0 + hg * (1.0 - s)))
            sv[rs, :] = dv
            sg[rs, :] = dg
            sums = [dv * vp, dv * vx, dv * vn, dv, dg * gp, dg * gx, dg * gn, dg]
            return tuple(c + jnp.sum(x, axis=0, keepdims=True) for c, x in zip(carry, sums))

        zero = jnp.zeros((1, tc), F32)
        acc = lax.fori_loop(0, rows // rc, first, (zero,) * 8)
        for k in range(3):
            dwv_ref[k:k + 1, :] = acc[k]
            dwg_ref[k:k + 1, :] = acc[4 + k]
        dbv_ref[...] = acc[3]
        dbg_ref[...] = acc[7]

        def second(r, _):
            r0 = pl.multiple_of(r * rc, rc)
            rs = pl.ds(r0, rc)
            for s_ref, w_ref, o_ref in ((sv, wv_ref, dhv_ref), (sg, wg_ref, dhg_ref)):
                x = s_ref[rs, :]
                xp, xn = _shifted(x, *_edge_rows(s_ref, r0, rc, rows, SUB))
                o_ref[rs, :] = (w_ref[0:1, :] * xn + w_ref[1:2, :] * x + w_ref[2:3, :] * xp).astype(o_ref.dtype)
            return 0

        lax.fori_loop(0, rows // rc, second, 0)

    res = pl.pallas_call(
        body, name=name, grid=(f // tc,),
        in_specs=[val(rows), val(rows), gate(rows), val(3), gate(3), val(1), gate(1)],
        out_specs=[val(rows), val(rows), val(3), val(3), val(1), val(1)],
        out_shape=[jax.ShapeDtypeStruct((rows, f), ACT_DTYPE)] * 2
        + [jax.ShapeDtypeStruct((3, f), F32)] * 2 + [jax.ShapeDtypeStruct((1, f), F32)] * 2,
        scratch_shapes=[pltpu.VMEM((rows, tc), F32), pltpu.VMEM((rows, tc), F32)],
        compiler_params=_params(("parallel",)),
    )(da, h, h, cw, cw, cb, cb)
    return res


def _pool_band(name, x, transpose, out_dtype):
    rows, d = x.shape
    ng = len(POOL_WINDOWS)
    ch = d // ng
    tm = _tile(rows, 256, PACK)
    win = tm + 2 * POOL_HALO
    assert win <= rows

    def body(x_ref, o_ref):
        half = lax.shift_left(jnp.int32(1), pl.program_id(0))
        t0 = pl.program_id(1) * tm
        ws = pl.multiple_of(jnp.clip(t0 - POOL_HALO, 0, rows - win), PACK)
        i = t0 + lax.broadcasted_iota(jnp.int32, (tm, win), 0)
        j = ws + lax.broadcasted_iota(jnp.int32, (tm, win), 1)

        def inv_count(t):
            hi = jnp.minimum(t + half - 1, rows - 1)
            lo = jnp.maximum(t - half, 0)
            return 1.0 / (hi - lo + 1).astype(F32)

        xw = x_ref[pl.ds(ws, win), :]
        xt = x_ref[pl.ds(pl.multiple_of(t0, PACK), tm), :].astype(F32)
        if transpose:
            band = (j - half <= i) & (i <= j + half - 1)
            tw = ws + lax.broadcasted_iota(jnp.int32, (win, 1), 0)
            o = _dot(band.astype(MXU_DTYPE), xw.astype(F32) * inv_count(tw)) - xt
        else:
            band = (i - half <= j) & (j <= i + half - 1)
            tt = t0 + lax.broadcasted_iota(jnp.int32, (tm, 1), 0)
            o = _dot(band.astype(MXU_DTYPE), xw) * inv_count(tt) - xt
        o_ref[...] = o.astype(o_ref.dtype)

    return pl.pallas_call(
        body, name=name, grid=(ng, rows // tm),
        in_specs=[pl.BlockSpec((rows, ch), lambda g, i: (0, g))],
        out_specs=pl.BlockSpec((tm, ch), lambda g, i: (i, g)),
        out_shape=jax.ShapeDtypeStruct((rows, d), out_dtype),
        compiler_params=_params(("parallel", "arbitrary")),
    )(x)


def _ada_forward(c16, ada_w, ada_b):
    nl, d, cols = ada_w.shape
    tn = _tile(cols, 512)

    def body(c_ref, w_ref, b_ref, o_ref):
        cv = c_ref[...]
        o_ref[...] = _dot(cv * _sigmoid(cv), w_ref[...]) + b_ref[...]

    return pl.pallas_call(
        body, name="ada_forward", grid=(nl, cols // tn),
        in_specs=[pl.BlockSpec((16, d), lambda l, n: (0, 0)),
                  pl.BlockSpec((None, d, tn), lambda l, n: (l, 0, n)),
                  pl.BlockSpec((None, 1, tn), lambda l, n: (l, 0, n))],
        out_specs=pl.BlockSpec((None, 16, tn), lambda l, n: (l, 0, n)),
        out_shape=jax.ShapeDtypeStruct((nl, 16, cols), F32),
        compiler_params=_params(("parallel", "parallel")),
    )(c16, ada_w, ada_b)


def _ada_backward(c16, dmod, ada_w):
    nl, d, cols = ada_w.shape
    tn = _tile(cols, 512)
    nn = cols // tn

    def body(c_ref, g_ref, w_ref, gw_ref, gc_ref):
        cv = c_ref[...]
        s = _sigmoid(cv)
        gv = g_ref[...]
        gw_ref[...] = _dot(cv * s, gv, TN)
        dcond = _dot(gv, w_ref[...], NT)
        row = lax.broadcasted_iota(jnp.int32, dcond.shape, 0)
        dctx = jnp.sum(jnp.where(row >= 8, dcond * (s * (1.0 + cv * (1.0 - s))), 0.0), axis=0, keepdims=True)

        @pl.when((pl.program_id(0) == 0) & (pl.program_id(1) == 0))
        def _():
            gc_ref[...] = jnp.zeros_like(gc_ref)

        gc_ref[...] += dctx

    return pl.pallas_call(
        body, name="ada_backward", grid=(nl, nn),
        in_specs=[pl.BlockSpec((16, d), lambda l, n: (0, 0)),
                  pl.BlockSpec((None, 16, tn), lambda l, n: (l, 0, n)),
                  pl.BlockSpec((None, d, tn), lambda l, n: (l, 0, n))],
        out_specs=[pl.BlockSpec((None, d, tn), lambda l, n: (l, 0, n)),
                   pl.BlockSpec((1, d), lambda l, n: (0, 0))],
        out_shape=[jax.ShapeDtypeStruct((nl, d, cols), F32), jax.ShapeDtypeStruct((1, d), F32)],
        compiler_params=_params(("arbitrary", "arbitrary")),
    )(c16, dmod, ada_w)


def _row_sum16(name, a):
    nl, _, w = a.shape
    tn = _tile(w, 4096)

    def body(a_ref, o_ref):
        o_ref[...] = jnp.sum(a_ref[...], axis=0, keepdims=True)

    return pl.pallas_call(
        body, name=name, grid=(nl, w // tn),
        in_specs=[pl.BlockSpec((None, 16, tn), lambda l, n: (l, 0, n))],
        out_specs=pl.BlockSpec((None, 1, tn), lambda l, n: (l, 0, n)),
        out_shape=jax.ShapeDtypeStruct((nl, 1, w), F32),
        compiler_params=_params(("parallel", "parallel")),
    )(a)


def _pack(arrays, row_align):
    flat = jnp.concatenate([a.reshape(-1).astype(F32) for a in arrays])
    quantum = row_align * LANE
    padded = -(-flat.shape[0] // quantum) * quantum
    return jnp.pad(flat, (0, padded - flat.shape[0])).reshape(-1, LANE)


def _unpack(packed, shapes):
    flat = packed.reshape(-1)
    out, off = [], 0
    for s in shapes:
        n = math.prod(s)
        out.append(flat[off:off + n].reshape(s))
        off += n
    return out


def _grid_pos_emb(n_tokens, dim):
    rows = n_tokens // GRID_W
    r, col = jnp.meshgrid(jnp.arange(rows, dtype=F32), jnp.arange(GRID_W, dtype=F32), indexing="ij")
    quarter = dim // 4
    omega = 1.0 / (POS_BASE ** (jnp.arange(quarter, dtype=F32) / quarter))

    def enc(p):
        ang = p.reshape(-1, 1) * omega[None, :]
        return jnp.concatenate([jnp.sin(ang), jnp.cos(ang)], axis=-1)

    return jnp.concatenate([enc(r), enc(col)], axis=-1)


def _ffn_forward(tag, v, x_in, up4, dn4, layer, cw, cb):
    h = _mm_cols(f"ffn_up_{tag}", v, up4, layer, ACT_DTYPE)
    a = _conv_swiglu_fwd(f"ffn_conv_{tag}", h, cw, cb)
    f = _mm_rows(f"ffn_down_{tag}", a, dn4, layer, F32)
    return h, a, f


def _ffn_backward(tag, dx_out, fb, x_mid, v, h, a, up4, dn4, layer, cw, cb, gate5, gam3, gam2, scale4p1):
    def post(dxo, f, gam, gate):
        dy, dgate, dgam = _postnorm_bwd(dxo, f.astype(F32), gam, gate)
        return (dy,), (dgate, dgam)

    (df,), (dgate5, dgam3) = _rowwise(f"ffn_post_bwd_{tag}", post, dx_out.shape[0],
                                      [(dx_out, False), (fb, False)], [gam3, gate5],
                                      [(dx_out.shape[1], ACT_DTYPE, False)], [dx_out.shape[1]] * 2)
    da = _mm_rows_nt(f"ffn_down_dx_{tag}", df, dn4, layer, ACT_DTYPE)
    g_dn = _mm_rows_tn(f"ffn_down_dw_{tag}", a, df, dn4.shape[-2], ACT_DTYPE)
    dhv, dhg, dwv, dwg, dbv, dbg = _conv_swiglu_bwd(f"ffn_conv_bwd_{tag}", da, h, cw, cb)
    dh = jnp.concatenate([dhv, dhg], axis=1)
    dcw = jnp.concatenate([dwv, dwg], axis=1)
    dcb = jnp.concatenate([dbv, dbg], axis=1)
    dv = _mm_cols_nt(f"ffn_up_dx_{tag}", dh, up4, layer, F32)
    g_up = _mm_cols_tn(f"ffn_up_dw_{tag}", v, dh, up4.shape[-1], ACT_DTYPE)

    def pre(dvv, x, dxo, gam, s1):
        dx, dshift, dscale, dgam = _prenorm_bwd(dvv, x, gam, s1)
        return (dxo + dx,), (dshift, dscale, dgam)

    d = dx_out.shape[1]
    (dx_mid,), (dshift3, dscale4, dgam2) = _rowwise(
        f"ffn_pre_bwd_{tag}", pre, dx_out.shape[0], [(dv, False), (x_mid, False), (dx_out, False)],
        [gam2, scale4p1], [(d, F32, False)], [d] * 3)
    return dx_mid, g_up, g_dn, dcw, dcb, (dshift3, dscale4, dgate5), (dgam2, dgam3)


def kernel(x, c, ctx, c_ctx, ada_w, ada_b, norm_g, s5_lam_re, s5_lam_im, s5_log_step, s5_b_re, s5_b_im, s5_c_re, s5_c_im, s5_d, s5_glu_w, pool_w, pool_scale, ffn_up, ffn_conv, ffn_conv_b, ffn_down, loss_target, m_c_ctx, m_ada_w, m_ada_b, m_norm_g, m_s5_lam_re, m_s5_lam_im, m_s5_log_step, m_s5_b_re, m_s5_b_im, m_s5_c_re, m_s5_c_im, m_s5_d, m_s5_glu_w, m_pool_w, m_pool_scale, m_ffn_up, m_ffn_conv, m_ffn_conv_b, m_ffn_down, v_c_ctx, v_ada_w, v_ada_b, v_norm_g, v_s5_lam_re, v_s5_lam_im, v_s5_log_step, v_s5_b_re, v_s5_b_im, v_s5_c_re, v_s5_c_im, v_s5_d, v_s5_glu_w, v_pool_w, v_pool_scale, v_ffn_up, v_ffn_conv, v_ffn_conv_b, v_ffn_down):
    ix, iy, ic = _coords()
    chip = 2 * ix + iy
    me = 2 * chip + ic
    _, rows, d = x.shape
    rows_c = ctx.shape[1]
    nl = ada_w.shape[0]
    assert nl == 2 and s5_glu_w.shape[0] == 1 and pool_w.shape[0] == 1
    a_cols = ada_w.shape[2]
    f2s = ffn_up.shape[2]
    f2 = N_CHIP * f2s
    ds = d // N_CHIP
    ng = len(POOL_WINDOWS)
    ch = d // ng
    ps = pool_w.shape[2]

    core = jnp.reshape(ic, (1,)).astype(jnp.int32)
    chip_id = jnp.reshape(chip, (1,)).astype(jnp.int32)
    pos = jnp.concatenate([core, chip_id])
    shards = {"up": _view2d(ffn_up), "down": _view2d(ffn_down), "glu": s5_glu_w[0], "pool": _view2d(pool_w[0])}
    gather_bufs = [_cast_own_half(f"cast_{n}", w, pos) for n, w in shards.items()]

    c_all = _all_gather8("gather_cond", _pack([c], SUB))
    c_all = c_all.reshape(N_DEV, -1)[:, :d]
    c16 = jnp.concatenate([c_all, jnp.broadcast_to(c_ctx[None, :], (N_DEV, d))], axis=0)
    ada_b_mine = lax.dynamic_slice_in_dim(ada_b, chip * a_cols, a_cols, axis=1)
    mod_part = _ada_forward(c16, ada_w, ada_b_mine[:, None, :])
    narrow_shapes = [mod_part.shape, norm_g.shape, pool_scale.shape, ffn_conv.shape]
    narrow = _all_gather8("gather_narrow", _pack([mod_part, norm_g, pool_scale, ffn_conv], SUB))
    per_chip = [_unpack(narrow[2 * s], narrow_shapes) for s in range(N_CHIP)]
    mod_all = jnp.concatenate([p[0] for p in per_chip], axis=-1)
    gam = jnp.concatenate([p[1] for p in per_chip], axis=-1)
    pscale = jnp.concatenate([p[2] for p in per_chip], axis=-1)
    conv_w = jnp.concatenate([p[3] for p in per_chip], axis=-1)
    mod_mine = lax.dynamic_index_in_dim(mod_all, me, axis=1, keepdims=False)
    mod_ctx = mod_all[0, N_DEV]

    def mods(vec):
        s0, s1, g2, s3, s4, g5 = [vec[k * d:(k + 1) * d][None, :] for k in range(N_MOD)]
        return s0, 1.0 + s1, g2, s3, 1.0 + s4, g5

    m0, m1, mc = mods(mod_mine[0]), mods(mod_mine[1]), mods(mod_ctx)
    gains = [[gam[l, k][None, :] for k in range(4)] for l in range(nl)]
    conv_b = ffn_conv_b[:, None, :]

    pos = _grid_pos_emb(rows, d)

    def init(xv, pv, g0, shift, s1):
        x0 = xv + pv
        return (x0, _prenorm(x0, g0, shift, s1)), ()

    (x0, u), _ = _rowwise("init", init, rows, [(x[0], False), (pos, False)], [gains[0][0], m0[0], m0[1]],
                          [(d, F32, False), (d, ACT_DTYPE, True)])
    (uc,), _ = _rowwise("ctx_prenorm", lambda cv, g0, shift, s1: ((_prenorm(cv, g0, shift, s1),), ()),
                        rows_c, [(ctx[0], False)], [gains[0][0], mc[0], mc[1]], [(d, ACT_DTYPE, True)])
    s5_params = (s5_lam_re[0], s5_lam_im[0], s5_log_step[0], s5_b_re[0], s5_b_im[0], s5_c_re[0], s5_c_im[0])
    (bblk, cblk, atile), s5_vjp = jax.vjp(_s5_prepare, *s5_params)
    y, gather_bufs = _s5_forward(u, uc, bblk, cblk, atile, s5_d, gather_bufs,
                                 _gather_ici_plan([w.shape[0] for w in shards.values()]))
    full_up, full_dn, glu4, full_pool = [
        _swap_gathered(f"gather_{n}", part, core, w.shape[0])
        for (n, w), part in zip(shards.items(), gather_bufs)]
    up4 = full_up.reshape((N_CHIP,) + ffn_up.shape)
    dn4 = full_dn.reshape((N_CHIP,) + ffn_down.shape)
    pool_full = full_pool.reshape(N_CHIP, ng, ps, ch).transpose(1, 0, 2, 3).reshape(ng, ch, ch)
    (z,), _ = _rowwise("gelu", lambda yv: ((_gelu(yv),), ()), rows, [(y, True)], [], [(d, ACT_DTYPE, False)])
    zz = _mm_cols("glu_proj", z, glu4[:, None], 0, ACT_DTYPE)

    def glu_out(zzv, xv, gate2, g1, g2, shift3, s4):
        zf = zzv.astype(F32)
        o = zf[:, :d] * _sigmoid(zf[:, d:])
        x1 = xv + gate2 * (o * _rstd(o) * g1)
        return (x1, _prenorm(x1, g2, shift3, s4)), ()

    (x1, v0), _ = _rowwise("glu_resid", glu_out, rows, [(zz, False), (x0, False)],
                           [m0[2], gains[0][1], gains[0][2], m0[3], m0[4]], [(d, F32, False), (d, ACT_DTYPE, False)])
    h0, a0, f0 = _ffn_forward("l0", v0, x1, up4, dn4, 0, conv_w[0], conv_b[0])

    def ffn_out(fv, xv, gate5, g3, g0n, shift0, s1):
        x2 = xv + gate5 * (fv * _rstd(fv) * g3)
        return (x2, _prenorm(x2, g0n, shift0, s1), fv), ()

    (x2, u1, fb0), _ = _rowwise("ffn_resid_l0", ffn_out, rows, [(f0, False), (x1, False)],
                                [m0[5], gains[0][3], gains[1][0], m1[0], m1[1]],
                                [(d, F32, False), (d, ACT_DTYPE, False), (d, ACT_DTYPE, False)])

    p1 = _pool_band("pool_band", u1, False, ACT_DTYPE)
    yr = _mm_grp("pool_proj", p1, pool_full, NN, F32)

    def pool_out(yv, xv, ps_, gate2, g1, g2, shift3, s4):
        o = yv * ps_
        x1n = xv + gate2 * (o * _rstd(o) * g1)
        return (x1n, _prenorm(x1n, g2, shift3, s4), yv), ()

    (x3, v1, yb), _ = _rowwise("pool_resid", pool_out, rows, [(yr, False), (x2, False)],
                               [pscale, m1[2], gains[1][1], gains[1][2], m1[3], m1[4]],
                               [(d, F32, False), (d, ACT_DTYPE, False), (d, ACT_DTYPE, False)])
    h1, a1, f1 = _ffn_forward("l1", v1, x3, up4, dn4, 1, conv_w[1], conv_b[1])

    def loss_head(fv, xv, tv, gate5, g3):
        err = xv + gate5 * (fv * _rstd(fv) * g3) - tv
        return (err * (1.0 / d), fv), (err * err,)

    (dx4, fb1), (sq,) = _rowwise("loss_head", loss_head, rows, [(f1, False), (x3, False), (loss_target[0], False)],
                                 [m1[5], gains[1][3]], [(d, F32, False), (d, ACT_DTYPE, False)], [d])
    loss = lax.psum(0.5 * jnp.sum(sq) / d, ("x", "y", "c"))

    dx3, g_up1, g_dn1, dcw1, dcb1, dmod_ffn1, (dgam12, dgam13) = _ffn_backward(
        "l1", dx4, fb1, x3, v1, h1, a1, up4, dn4, 1, conv_w[1], conv_b[1], m1[5], gains[1][3], gains[1][2], m1[4])

    def pool_post(dxo, yv, ps_, g1, gate2):
        yraw = yv.astype(F32)
        dy, dgate, dgam = _postnorm_bwd(dxo, yraw * ps_, g1, gate2)
        return (dy * ps_,), (dgate, dgam, dy * yraw)

    (dyr,), (dgate2_1, dgam11, dpscale) = _rowwise("pool_post_bwd", pool_post, rows, [(dx3, False), (yb, False)],
                                                   [pscale, gains[1][1], m1[2]], [(d, ACT_DTYPE, False)], [d] * 3)
    dp1 = _mm_grp("pool_proj_dx", dyr, pool_full, NT, ACT_DTYPE)
    g_pool = _mm_grp_tn("pool_proj_dw", p1, dyr, ng, ACT_DTYPE)
    du1 = _pool_band("pool_band_bwd", dp1, True, F32)

    def pre_bwd(duv, xv, dxo, g0, s1):
        dx, dshift, dscale, dgam = _prenorm_bwd(duv, xv, g0, s1)
        return (dxo + dx,), (dshift, dscale, dgam)

    (dx2,), (dshift0_1, dscale1_1, dgam10) = _rowwise(
        "pool_pre_bwd", pre_bwd, rows, [(du1, False), (x2, False), (dx3, False)],
        [gains[1][0], m1[1]], [(d, F32, False)], [d] * 3)

    dx1, g_up0, g_dn0, dcw0, dcb0, dmod_ffn0, (dgam02, dgam03) = _ffn_backward(
        "l0", dx2, fb0, x1, v0, h0, a0, up4, dn4, 0, conv_w[0], conv_b[0], m0[5], gains[0][3], gains[0][2], m0[4])

    def glu_post(dxo, zzv, g1, gate2):
        zf = zzv.astype(F32)
        val, s = zf[:, :d], _sigmoid(zf[:, d:])
        do, dgate, dgam = _postnorm_bwd(dxo, val * s, g1, gate2)
        return (jnp.concatenate([do * s, do * val * (s * (1.0 - s))], axis=1),), (dgate, dgam)

    (dzz,), (dgate2_0, dgam01) = _rowwise("glu_post_bwd", glu_post, rows, [(dx1, False), (zz, False)],
                                          [gains[0][1], m0[2]], [(2 * d, ACT_DTYPE, False)], [d] * 2)
    dz = _mm_cols_nt("glu_proj_dx", dzz, glu4[:, None], 0, F32)
    g_glu = _mm_cols_tn("glu_proj_dw", z, dzz, glu4.shape[-1], ACT_DTYPE)
    (dy,), _ = _rowwise("gelu_bwd", lambda dzv, yv: ((dzv * _gelu_grad(yv),), ()), rows,
                        [(dz, False), (y, True)], [], [(d, ACT_DTYPE, True)])
    g_pool4 = g_pool.reshape(ng, N_CHIP, ps, ch).transpose(1, 0, 2, 3).reshape(N_CHIP, ng * ps, ch)
    big = {"up0": g_up0, "up1": g_up1, "dn0": g_dn0, "dn1": g_dn1, "glu": g_glu, "pool": g_pool4}
    pairs = [_pair_sum(f"pair_{n}", g, core) for n, g in big.items()]
    ride_shapes, ride_copies = _chip_exchange_plan(pairs)
    (du0, duc, d_bblk, d_cblk, d_atile, d_dsk), others = _s5_backward(
        u, uc, dy, bblk, cblk, atile, s5_d, pairs, ride_shapes, ride_copies)
    (gx,), (dshift0_0, dscale1_0, dgam00) = _rowwise(
        "s5_pre_bwd", pre_bwd, rows, [(du0, True), (x0, False), (dx1, False)],
        [gains[0][0], m0[1]], [(d, F32, False)], [d] * 3)

    def ctx_bwd(duv, cv, g0, s1):
        _, dshift, dscale, dgam = _prenorm_bwd(duv, cv, g0, s1)
        return (), (dshift, dscale, dgam)

    _, (dshift_c, dscale_c, dgam00c) = _rowwise("ctx_pre_bwd", ctx_bwd, rows_c, [(duc, True), (ctx[0], False)],
                                                [gains[0][0], mc[1]], [], [d] * 3)
    g_s5 = s5_vjp((d_bblk, d_cblk, d_atile))

    zero_d = jnp.zeros((1, d), F32)
    dmod_lat = jnp.stack([
        jnp.concatenate([dshift0_0, dscale1_0, dgate2_0, *dmod_ffn0], axis=1),
        jnp.concatenate([dshift0_1, dscale1_1, dgate2_1, *dmod_ffn1], axis=1)])
    dmod_ctx = jnp.stack([jnp.concatenate([dshift_c, dscale_c] + [zero_d] * 4, axis=1),
                          jnp.zeros((1, N_MOD * d), F32)])
    dmod_shape = (nl, 2, N_MOD * d)
    dmod_all = _all_gather8("gather_dmod", _pack([jnp.concatenate([dmod_lat, dmod_ctx], axis=1)], SUB))
    dmod_all = jnp.stack([_unpack(dmod_all[k], [dmod_shape])[0] for k in range(N_DEV)])
    dmod16 = jnp.concatenate([dmod_all[:, :, 0], dmod_all[:, :, 1]], axis=0).transpose(1, 0, 2)
    dmod16_mine = lax.dynamic_slice_in_dim(dmod16, chip * a_cols, a_cols, axis=2)
    g_ada_w, g_cctx_part = _ada_backward(c16, dmod16_mine, ada_w)
    g_ada_b = _row_sum16("ada_bias_grad", dmod16)[:, 0]

    d_gam = jnp.stack([jnp.concatenate([dgam00 + dgam00c, dgam01, dgam02, dgam03], axis=0),
                       jnp.concatenate([dgam10, dgam11, dgam12, dgam13], axis=0)])
    small = [d_gam, 0.5 * g_cctx_part, *g_s5, jnp.sum(d_dsk, axis=0, keepdims=True), dpscale,
             jnp.stack([dcw0, dcw1]), jnp.stack([dcb0[0], dcb1[0]])]
    small_shapes = [s.shape for s in small]
    packed = _pack(small, N_DEV * SUB)
    summed = _all_reduce8("reduce_small", packed.reshape(N_DEV, -1, LANE)).reshape(-1, LANE)
    (r_gam, r_cctx, r_lam_re, r_lam_im, r_log_step, r_b_re, r_b_im, r_c_re, r_c_im,
     r_dsk, r_pscale, r_conv, r_convb) = _unpack(summed, small_shapes)
    g_norm = lax.dynamic_slice_in_dim(r_gam, chip * ds, ds, axis=2)
    g_pscale = lax.dynamic_slice_in_dim(r_pscale, chip * ds, ds, axis=1)
    g_conv = lax.dynamic_slice_in_dim(r_conv, chip * f2s, f2s, axis=2)

    red = {n: _chip_sum(f"chip_sum_{n}", p, o, chip_id) for n, p, o in zip(big, pairs, others)}
    d_rows, dn_rows = ffn_up.shape[1], ffn_down.shape[1]
    g_up = _swap_halves("swap_up0", red["up0"], core, None, nl * d_rows, 0)
    g_up = _swap_halves("swap_up1", red["up1"], core, g_up, nl * d_rows, d_rows).reshape(ffn_up.shape)
    g_dn = _swap_halves("swap_dn0", red["dn0"], core, None, nl * dn_rows, 0)
    g_dn = _swap_halves("swap_dn1", red["dn1"], core, g_dn, nl * dn_rows, dn_rows).reshape(ffn_down.shape)
    g_glu_f = _swap_halves("swap_glu", red["glu"], core, None, d, 0)
    g_pool_f = _swap_halves("swap_pool", red["pool"], core, None, ng * ps, 0)

    grads = {
        "c_ctx": r_cctx[0], "ada_w": g_ada_w, "ada_b": g_ada_b, "norm_g": g_norm,
        "s5_lam_re": r_lam_re[None], "s5_lam_im": r_lam_im[None], "s5_log_step": r_log_step[None],
        "s5_b_re": r_b_re[None], "s5_b_im": r_b_im[None], "s5_c_re": r_c_re[None], "s5_c_im": r_c_im[None],
        "s5_d": r_dsk, "s5_glu_w": g_glu_f[None], "pool_w": g_pool_f.reshape(pool_w.shape),
        "pool_scale": g_pscale, "ffn_up": g_up, "ffn_conv": g_conv, "ffn_conv_b": r_convb, "ffn_down": g_dn,
    }
    weights = {
        "c_ctx": (c_ctx, m_c_ctx, v_c_ctx), "ada_w": (ada_w, m_ada_w, v_ada_w), "ada_b": (ada_b, m_ada_b, v_ada_b),
        "norm_g": (norm_g, m_norm_g, v_norm_g), "s5_lam_re": (s5_lam_re, m_s5_lam_re, v_s5_lam_re),
        "s5_lam_im": (s5_lam_im, m_s5_lam_im, v_s5_lam_im), "s5_log_step": (s5_log_step, m_s5_log_step, v_s5_log_step),
        "s5_b_re": (s5_b_re, m_s5_b_re, v_s5_b_re), "s5_b_im": (s5_b_im, m_s5_b_im, v_s5_b_im),
        "s5_c_re": (s5_c_re, m_s5_c_re, v_s5_c_re), "s5_c_im": (s5_c_im, m_s5_c_im, v_s5_c_im),
        "s5_d": (s5_d, m_s5_d, v_s5_d), "s5_glu_w": (s5_glu_w, m_s5_glu_w, v_s5_glu_w),
        "pool_w": (pool_w, m_pool_w, v_pool_w), "pool_scale": (pool_scale, m_pool_scale, v_pool_scale),
        "ffn_up": (ffn_up, m_ffn_up, v_ffn_up), "ffn_conv": (ffn_conv, m_ffn_conv, v_ffn_conv),
        "ffn_conv_b": (ffn_conv_b, m_ffn_conv_b, v_ffn_conv_b), "ffn_down": (ffn_down, m_ffn_down, v_ffn_down),
    }
    names = list(weights)
    large = ("ada_w", "s5_glu_w", "pool_w", "ffn_up", "ffn_down")
    delta, new_m, new_v = {}, {}, {}
    for n in names:
        w, m, v = weights[n]
        if n in large:
            delta[n], new_m[n], new_v[n] = _adamw(f"adamw_{n}", w, grads[n], m, v)
        else:
            shape = w.shape
            view = (-1, LANE) if w.size % LANE == 0 else (1, -1)
            res = _adamw(f"adamw_{n}", *[t.reshape(view) for t in (w, grads[n], m, v)])
            delta[n], new_m[n], new_v[n] = [t.reshape(shape) for t in res]

    return (loss, gx[None], *[grads[n] for n in names], *[delta[n] for n in names],
            *[new_m[n] for n in names], *[new_v[n] for n in names])
```

```python
import math

import jax
import jax.numpy as jnp
from jax import lax
from jax.experimental import pallas as pl
from jax.experimental.pallas import tpu as pltpu

F32 = jnp.float32
MXU_DTYPE = jnp.bfloat16
ACT_DTYPE = jnp.bfloat16

LANE = 128
SUB = 8
PACK = 16
VMEM_LIMIT = 56 * 1024 * 1024
ELEMWISE_BLOCK = 1 << 18

N_DEV = 8
N_CHIP = 4
N_SEG = SUB
S5_GROUP = 16
S5_STATE = 64
S5_CB = LANE
S5_H = (S5_CB // S5_GROUP) * S5_STATE
S5_NS = 2 * S5_H
POOL_WINDOWS = (2, 4, 8, 16)
POOL_HALO = 16
GRID_W = 64
POS_BASE = 10000.0
RMS_EPS = 1e-6
N_MOD = 6

ADAM_LR = 0.001
ADAM_B1 = 0.9
ADAM_B2 = 0.999
ADAM_EPS = 1e-08
ADAM_WD = 0.01
ADAM_STEP = 10

NN = (((1,), (0,)), ((), ()))
NT = (((1,), (1,)), ((), ()))
TN = (((0,), (0,)), ((), ()))
MESH = pl.DeviceIdType.MESH


def _tile(n, cap, align=LANE):
    best = None
    for t in range(align, min(n, cap) + 1, align):
        if n % t == 0:
            best = t
    return n if best is None else best


def _params(sem=None):
    return pltpu.CompilerParams(dimension_semantics=sem, vmem_limit_bytes=VMEM_LIMIT)


def _dot(a, b, dims=NN):
    return lax.dot_general(a.astype(MXU_DTYPE), b.astype(MXU_DTYPE), dims, preferred_element_type=F32)


def _sigmoid(x):
    return 0.5 * jnp.tanh(0.5 * x) + 0.5


_GELU_C = math.sqrt(2.0 / math.pi)
_GELU_K = 0.044715


def _gelu(x):
    return 0.5 * x * (1.0 + jnp.tanh(_GELU_C * (x + _GELU_K * x * x * x)))


def _gelu_grad(x):
    t = jnp.tanh(_GELU_C * (x + _GELU_K * x * x * x))
    return 0.5 * (1.0 + t) + 0.5 * x * (1.0 - t * t) * _GELU_C * (1.0 + 3.0 * _GELU_K * x * x)


def _rstd(x):
    return lax.rsqrt(jnp.mean(x * x, axis=-1, keepdims=True) + RMS_EPS)


def _norm_bwd(dxh, xh, r):
    return r * (dxh - xh * jnp.mean(dxh * xh, axis=-1, keepdims=True))


def _rowwise(name, fn, rows, tiled, vecs, outs, accs=()):
    seg = rows // N_SEG
    tm = _tile(seg, 256, SUB)
    nt, ntp = rows // tm, seg // tm
    n_t, n_v, n_o, n_a = len(tiled), len(vecs), len(outs), len(accs)

    def spec(width, perm):
        if perm:
            return pl.BlockSpec((tm, width), lambda i: (i % ntp, i // ntp))
        return pl.BlockSpec((tm, width), lambda i: (i, 0))

    args, in_specs = [], []
    for arr, perm in tiled:
        width = arr.shape[-1]
        args.append(arr.reshape(seg, N_SEG * width) if perm else arr)
        in_specs.append(spec(width, perm))
    for v in vecs:
        args.append(v)
        in_specs.append(pl.BlockSpec(v.shape, lambda i: (0, 0)))
    out_shape, out_specs = [], []
    for width, dtype, perm in outs:
        out_shape.append(jax.ShapeDtypeStruct((seg, N_SEG * width) if perm else (rows, width), dtype))
        out_specs.append(spec(width, perm))
    for width in accs:
        out_shape.append(jax.ShapeDtypeStruct((SUB, width), F32))
        out_specs.append(pl.BlockSpec((SUB, width), lambda i: (0, 0)))

    def body(*refs):
        vals = [r[...] for r in refs[:n_t + n_v]]
        o_refs = refs[n_t + n_v:n_t + n_v + n_o]
        a_refs = refs[n_t + n_v + n_o:]
        o_vals, a_vals = fn(*vals)
        for r, v in zip(o_refs, o_vals):
            r[...] = v.astype(r.dtype)
        if n_a:
            @pl.when(pl.program_id(0) == 0)
            def _():
                for r in a_refs:
                    r[...] = jnp.zeros_like(r)
            for r, v in zip(a_refs, a_vals):
                r[...] += v.reshape(tm // SUB, SUB, v.shape[-1]).sum(axis=0)

    res = pl.pallas_call(
        body, name=name, grid=(nt,), in_specs=in_specs, out_specs=out_specs, out_shape=out_shape,
        compiler_params=_params(("arbitrary",)),
    )(*args)
    res = list(res)
    for k, (width, _, perm) in enumerate(outs):
        if perm:
            res[k] = res[k].reshape(rows, width)
    return res[:n_o], [jnp.sum(a, axis=0, keepdims=True) for a in res[n_o:]]


def _prenorm(x, gam, shift, scale1):
    r = _rstd(x)
    return (x * r) * gam * scale1 + shift


def _prenorm_bwd(du, x, gam, scale1):
    r = _rstd(x)
    xh = x * r
    dxn = du * scale1
    dx = _norm_bwd(dxn * gam, xh, r)
    return dx, du, du * (xh * gam), dxn * xh


def _postnorm_bwd(dxo, y, gam, gate):
    r = _rstd(y)
    yh = y * r
    dyn = dxo * gate
    dy = _norm_bwd(dyn * gam, yh, r)
    return dy, dxo * (yh * gam), dyn * yh


def _matmul(name, a, b, dims, grid, a_spec, b_spec, o_spec, out_shape, out_dtype, acc_shape,
            pair=None, ride=None):
    nk = grid[2]
    ins = list(a if pair and pair[0] == "a" else [a]) + list(b if pair and pair[0] == "b" else [b])
    specs = list(a_spec if pair and pair[0] == "a" else [a_spec]) + list(b_spec if pair and pair[0] == "b" else [b_spec])
    n_in = len(ins)
    bufs, copies = ride if ride else ((), None)
    n_ride = len(bufs)

    def body(*refs):
        in_refs = refs[:n_in]
        o_ref = refs[n_in + n_ride]
        ride_refs = refs[n_in + n_ride + 1:n_in + 2 * n_ride + 1]
        scratch = refs[n_in + 2 * n_ride + 1:]
        pid = [pl.program_id(ax) for ax in range(3)]
        if n_ride:
            send_sems, recv_sems = scratch[-2:]

            @pl.when((pid[0] == 0) & (pid[1] == 0) & (pid[2] == 0))
            def _():
                for cp in copies(ride_refs, ride_refs, send_sems, recv_sems):
                    cp.start()

        vals = [r[...] for r in in_refs]
        if pair:
            first = pid[pair[1]] < pair[2]
            picked = jnp.where(first, vals[0], vals[1]) if pair[0] == "a" else jnp.where(first, vals[1], vals[2])
            vals = [picked, vals[2]] if pair[0] == "a" else [vals[0], picked]
        part = _dot(vals[0], vals[1], dims)
        if nk == 1:
            o_ref[...] = part.astype(o_ref.dtype)
        else:
            acc_ref = scratch[0]

            @pl.when(pid[2] == 0)
            def _():
                acc_ref[...] = part

            @pl.when(pid[2] > 0)
            def _():
                acc_ref[...] += part

            @pl.when(pid[2] == nk - 1)
            def _():
                o_ref[...] = acc_ref[...].astype(o_ref.dtype)

        if n_ride:
            @pl.when((pid[0] == grid[0] - 1) & (pid[1] == grid[1] - 1) & (pid[2] == nk - 1))
            def _():
                for cp in copies(ride_refs, ride_refs, send_sems, recv_sems):
                    cp.wait()

    scratch_shapes = [] if nk == 1 else [pltpu.VMEM(acc_shape, F32)]
    out_shapes = [jax.ShapeDtypeStruct(out_shape, out_dtype)]
    if not n_ride:
        return pl.pallas_call(
            body, name=name, grid=grid, in_specs=specs, out_specs=[o_spec], out_shape=out_shapes,
            scratch_shapes=scratch_shapes, compiler_params=_params(("parallel", "parallel", "arbitrary")),
        )(*ins)[0]
    n_sem = 3 * n_ride
    res = pl.pallas_call(
        body, name=name, grid=grid, in_specs=specs + [ANY_SPEC] * n_ride,
        out_specs=[o_spec] + [ANY_SPEC] * n_ride,
        out_shape=out_shapes + [jax.ShapeDtypeStruct(x.shape, x.dtype) for x in bufs],
        input_output_aliases={n_in + i: 1 + i for i in range(n_ride)},
        scratch_shapes=scratch_shapes + [pltpu.SemaphoreType.DMA((n_sem,)), pltpu.SemaphoreType.DMA((n_sem,))],
        compiler_params=pltpu.CompilerParams(
            dimension_semantics=("arbitrary",) * 3, vmem_limit_bytes=VMEM_LIMIT, has_side_effects=True),
    )(*ins, *bufs)
    return res[0], res[1:]


def _mm_cols(name, a, w4, layer, out_dtype, ride=None):
    m, k = a.shape
    ns = w4.shape[-1]
    tm, tn = _tile(m, 1024), _tile(ns, 1536)
    nps = ns // tn
    return _matmul(
        name, a, w4, NN, (m // tm, N_CHIP * nps, 1),
        pl.BlockSpec((tm, k), lambda i, n, kk: (i, 0)),
        pl.BlockSpec((None, None, k, tn), lambda i, n, kk: (n // nps, layer, 0, n % nps)),
        pl.BlockSpec((tm, tn), lambda i, n, kk: (i, n)),
        (m, N_CHIP * ns), out_dtype, None, ride=ride)


def _halves_specs(g, rows_blk, cols_blk, tiles_half, row_of, col_of):
    if not isinstance(g, tuple):
        return g, pl.BlockSpec((rows_blk, cols_blk), lambda *p: (row_of(p), col_of(p)))
    left = pl.BlockSpec((rows_blk, cols_blk), lambda *p: (row_of(p), jnp.minimum(col_of(p), tiles_half - 1)))
    right = pl.BlockSpec((rows_blk, cols_blk), lambda *p: (row_of(p), jnp.maximum(col_of(p) - tiles_half, 0)))
    return g, (left, right)


def _mm_cols_nt(name, g, w4, layer, out_dtype):
    m = (g[0] if isinstance(g, tuple) else g).shape[0]
    k, ns = w4.shape[-2:]
    tm, tk = _tile(m, 1024), _tile(ns, 1536)
    kps = ns // tk
    half = N_CHIP * kps // 2
    g, g_spec = _halves_specs(g, tm, tk, half, lambda p: p[0], lambda p: p[2])
    return _matmul(
        name, g, w4, NT, (m // tm, 1, N_CHIP * kps), g_spec,
        pl.BlockSpec((None, None, k, tk), lambda i, n, kk: (kk // kps, layer, 0, kk % kps)),
        pl.BlockSpec((tm, k), lambda i, n, kk: (i, 0)),
        (m, k), out_dtype, (tm, k), pair=("a", 2, half) if isinstance(g, tuple) else None)


def _mm_cols_tn(name, a, g, ns, out_dtype):
    m, k = a.shape
    tkm, tmk, tn = _tile(m, 2048), _tile(k, 1024), _tile(ns, 1536)
    nps = ns // tn
    half = N_CHIP * nps // 2
    g, g_spec = _halves_specs(g, tkm, tn, half, lambda p: p[2], lambda p: p[1])
    return _matmul(
        name, a, g, TN, (k // tmk, N_CHIP * nps, m // tkm),
        pl.BlockSpec((tkm, tmk), lambda i, n, kk: (kk, i)), g_spec,
        pl.BlockSpec((None, tmk, tn), lambda i, n, kk: (n // nps, i, n % nps)),
        (N_CHIP, k, ns), out_dtype, (tmk, tn), pair=("b", 1, half) if isinstance(g, tuple) else None)


def _mm_rows(name, a, w4, layer, out_dtype):
    m = a.shape[0]
    rs, n = w4.shape[-2:]
    tm, tk = _tile(m, 1024), _tile(rs, 1536)
    kps = rs // tk
    return _matmul(
        name, a, w4, NN, (m // tm, 1, N_CHIP * kps),
        pl.BlockSpec((tm, tk), lambda i, j, kk: (i, kk)),
        pl.BlockSpec((None, None, tk, n), lambda i, j, kk: (kk // kps, layer, kk % kps, 0)),
        pl.BlockSpec((tm, n), lambda i, j, kk: (i, 0)),
        (m, n), out_dtype, (tm, n))


def _mm_rows_nt(name, g, w4, layer, out_dtype):
    m, n = g.shape
    rs = w4.shape[-2]
    tm, tn = _tile(m, 1024), _tile(rs, 1536)
    nps = rs // tn
    return _matmul(
        name, g, w4, NT, (m // tm, N_CHIP * nps, 1),
        pl.BlockSpec((tm, n), lambda i, j, kk: (i, 0)),
        pl.BlockSpec((None, None, tn, n), lambda i, j, kk: (j // nps, layer, j % nps, 0)),
        pl.BlockSpec((tm, tn), lambda i, j, kk: (i, j)),
        (m, N_CHIP * rs), out_dtype, None)


def _mm_rows_tn(name, a, g, rs, out_dtype):
    m = a.shape[0]
    n = g.shape[1]
    tkm, tmr, tn = _tile(m, 2048), _tile(rs, 1536), _tile(n, 1024)
    mps = rs // tmr
    return _matmul(
        name, a, g, TN, (N_CHIP * mps, n // tn, m // tkm),
        pl.BlockSpec((tkm, tmr), lambda i, j, kk: (kk, i)),
        pl.BlockSpec((tkm, tn), lambda i, j, kk: (kk, j)),
        pl.BlockSpec((None, tmr, tn), lambda i, j, kk: (i // mps, i % mps, j)),
        (N_CHIP, rs, n), out_dtype, (tmr, tn))


def _mm_grp(name, a, w, dims, out_dtype):
    m = a.shape[0]
    ng, ch = w.shape[:2]
    tm = _tile(m, 1024)
    return _matmul(
        name, a, w, dims, (m // tm, ng, 1),
        pl.BlockSpec((tm, ch), lambda i, g, kk: (i, g)),
        pl.BlockSpec((None, ch, ch), lambda i, g, kk: (g, 0, 0)),
        pl.BlockSpec((tm, ch), lambda i, g, kk: (i, g)),
        (m, ng * ch), out_dtype, None)


def _mm_grp_tn(name, a, g, ng, out_dtype):
    m = a.shape[0]
    ch = a.shape[1] // ng
    tk = _tile(m, 2048)
    return _matmul(
        name, a, g, TN, (ng, 1, m // tk),
        pl.BlockSpec((tk, ch), lambda i, j, kk: (kk, i)),
        pl.BlockSpec((tk, ch), lambda i, j, kk: (kk, i)),
        pl.BlockSpec((None, ch, ch), lambda i, j, kk: (i, 0, 0)),
        (ng, ch, ch), out_dtype, (ch, ch))


def _view2d(a):
    return a.reshape(-1, a.shape[-1])


def _elementwise(name, fn, ins, out_dtypes):
    r, c = ins[0].shape
    lanes = -(-c // LANE) * LANE
    tr = _tile(r, max(PACK, ELEMWISE_BLOCK // lanes), PACK)
    spec = pl.BlockSpec((tr, c), lambda i: (i, 0))

    def body(*refs):
        outs = fn(*[x[...] for x in refs[:len(ins)]])
        for o_ref, o in zip(refs[len(ins):], outs):
            o_ref[...] = o.astype(o_ref.dtype)

    return pl.pallas_call(
        body, name=name, grid=(r // tr,), in_specs=[spec] * len(ins), out_specs=[spec] * len(out_dtypes),
        out_shape=[jax.ShapeDtypeStruct((r, c), d) for d in out_dtypes],
        compiler_params=_params(("parallel",)),
    )(*ins)


def _adamw_math(w, g, m, v):
    m = ADAM_B1 * m + (1.0 - ADAM_B1) * g
    v = ADAM_B2 * v + (1.0 - ADAM_B2) * (g * g)
    m_hat = m / (1.0 - ADAM_B1 ** ADAM_STEP)
    v_hat = v / (1.0 - ADAM_B2 ** ADAM_STEP)
    delta = -ADAM_LR * (m_hat / (jnp.sqrt(v_hat) + ADAM_EPS) + ADAM_WD * w)
    return delta, m, v


def _adamw(name, w, g, m, v):
    shape = w.shape
    outs = _elementwise(name, _adamw_math, [_view2d(w), _view2d(g), _view2d(m), _view2d(v)], [F32, F32, F32])
    return [o.reshape(shape) for o in outs]


def _coords():
    return lax.axis_index("x"), lax.axis_index("y"), lax.axis_index("c")


def _peer(x, y, c, k):
    return (x ^ (k >> 2), y ^ ((k >> 1) & 1), c ^ (k & 1))


def _all_gather8(name, block):
    r = block.shape[0]

    def body(x_ref, out_ref, send_sems, recv_sems):
        x, y, c = _coords()
        me = 4 * x + 2 * y + c
        out_ref[me] = x_ref[...]
        copies = []
        for k in range(1, N_DEV):
            cp = pltpu.make_async_remote_copy(
                src_ref=x_ref, dst_ref=out_ref.at[me], send_sem=send_sems.at[k], recv_sem=recv_sems.at[k],
                device_id=_peer(x, y, c, k), device_id_type=MESH)
            cp.start()
            copies.append(cp)
        for cp in copies:
            cp.wait()

    return pl.pallas_call(
        body, name=name,
        in_specs=[pl.BlockSpec(memory_space=pltpu.VMEM)], out_specs=pl.BlockSpec(memory_space=pltpu.VMEM),
        out_shape=jax.ShapeDtypeStruct((N_DEV, r, LANE), F32),
        scratch_shapes=[pltpu.SemaphoreType.DMA((N_DEV,)), pltpu.SemaphoreType.DMA((N_DEV,))],
        compiler_params=pltpu.CompilerParams(vmem_limit_bytes=VMEM_LIMIT),
    )(block)


def _all_reduce8(name, parts):
    r = parts.shape[1]

    def body(p_ref, out_ref, rbuf, send1, recv1, send2, recv2):
        x, y, c = _coords()
        me = 4 * x + 2 * y + c
        first = []
        for k in range(1, N_DEV):
            px, py, pc = _peer(x, y, c, k)
            cp = pltpu.make_async_remote_copy(
                src_ref=p_ref.at[4 * px + 2 * py + pc], dst_ref=rbuf.at[me], send_sem=send1.at[k],
                recv_sem=recv1.at[k], device_id=(px, py, pc), device_id_type=MESH)
            cp.start()
            first.append(cp)
        rbuf[me] = p_ref[me]
        for cp in first:
            cp.wait()
        acc = rbuf[0]
        for d in range(1, N_DEV):
            acc = acc + rbuf[d]
        out_ref[me] = acc
        second = []
        for k in range(1, N_DEV):
            cp = pltpu.make_async_remote_copy(
                src_ref=out_ref.at[me], dst_ref=out_ref.at[me], send_sem=send2.at[k], recv_sem=recv2.at[k],
                device_id=_peer(x, y, c, k), device_id_type=MESH)
            cp.start()
            second.append(cp)
        for cp in second:
            cp.wait()

    return pl.pallas_call(
        body, name=name,
        in_specs=[pl.BlockSpec(memory_space=pltpu.VMEM)], out_specs=pl.BlockSpec(memory_space=pltpu.VMEM),
        out_shape=jax.ShapeDtypeStruct((N_DEV, r, LANE), F32),
        scratch_shapes=[pltpu.VMEM((N_DEV, r, LANE), F32)] + [pltpu.SemaphoreType.DMA((N_DEV,))] * 4,
        compiler_params=pltpu.CompilerParams(vmem_limit_bytes=VMEM_LIMIT),
    )(parts)


ANY_SPEC = pl.BlockSpec(memory_space=pl.ANY)
COMM_BLOCK_BYTES = 2 << 20


def _staged_call(name, body, core, grid, in_specs, ins, out_shape, scratch, aliases=None):
    return pl.pallas_call(
        body, name=name,
        grid_spec=pltpu.PrefetchScalarGridSpec(
            num_scalar_prefetch=1, grid=grid, in_specs=in_specs, out_specs=ANY_SPEC, scratch_shapes=scratch),
        out_shape=out_shape, input_output_aliases=aliases or {},
        compiler_params=pltpu.CompilerParams(
            dimension_semantics=("arbitrary",) * len(grid), vmem_limit_bytes=VMEM_LIMIT, has_side_effects=True),
    )(core, *ins)


def _rows_tile(rows, cols, itemsize):
    return _tile(rows, max(PACK, COMM_BLOCK_BYTES // (cols * itemsize)), PACK)


def _chip_peer(x, y, c, k):
    return (x ^ (k >> 1), y ^ (k & 1), c)


def _cast_own_half(name, w, pos):
    r, cols = w.shape
    h = r // 2
    tr = _tile(h, max(PACK, ELEMWISE_BLOCK // cols), PACK)
    nb = h // tr

    def body(p_ref, w_ref, o_ref):
        o_ref[...] = w_ref[...].astype(o_ref.dtype)

    return pl.pallas_call(
        body, name=name,
        grid_spec=pltpu.PrefetchScalarGridSpec(
            num_scalar_prefetch=1, grid=(nb,),
            in_specs=[pl.BlockSpec((tr, cols), lambda j, p: (p[0] * nb + j, 0))],
            out_specs=pl.BlockSpec((tr, cols), lambda j, p: ((2 * p[1] + p[0]) * nb + j, 0))),
        out_shape=jax.ShapeDtypeStruct((N_CHIP * r, cols), MXU_DTYPE),
        compiler_params=_params(("parallel",)),
    )(pos, w)


def _gather_ici_plan(shard_rows):
    def copies(refs, _, send_sems, recv_sems):
        x, y, c = _coords()
        out = []
        for i, (ref, r) in enumerate(zip(refs, shard_rows)):
            h = r // 2
            mine = ref.at[pl.ds(pl.multiple_of((2 * x + y) * r + c * h, PACK), h)]
            for k in (1, 2, 3):
                n = 3 * i + k - 1
                out.append(pltpu.make_async_remote_copy(
                    src_ref=mine, dst_ref=mine, send_sem=send_sems.at[n], recv_sem=recv_sems.at[n],
                    device_id=_chip_peer(x, y, c, k), device_id_type=MESH))
        return out

    return copies


def _swap_gathered(name, part, core, r):
    cols = part.shape[1]
    h = r // 2
    full = jax.ShapeDtypeStruct(part.shape, part.dtype)
    tr2 = _rows_tile(h, cols, 2)
    nb2 = h // tr2

    def swap_body(c_ref, mine_ref, full_ref, send_sem, recv_sem):
        s, j = pl.program_id(0), pl.program_id(1)
        x, y, c = _coords()
        dst = full_ref.at[pl.ds(pl.multiple_of(s * r + c * h + j * tr2, PACK), tr2)]
        cp = pltpu.make_async_remote_copy(
            src_ref=mine_ref, dst_ref=dst, send_sem=send_sem, recv_sem=recv_sem,
            device_id=(x, y, 1 - c), device_id_type=MESH)
        cp.start()
        cp.wait_send()

        @pl.when((s == N_CHIP - 1) & (j == nb2 - 1))
        def _():
            landed = full_ref.at[pl.ds(0, N_CHIP * h)]
            pltpu.make_async_remote_copy(
                src_ref=landed, dst_ref=landed, send_sem=send_sem, recv_sem=recv_sem,
                device_id=(x, y, 1 - c), device_id_type=MESH).wait_recv()

    return _staged_call(
        name + "_d2d", swap_body, core, (N_CHIP, nb2),
        [pl.BlockSpec((tr2, cols), lambda s, j, c_ref: ((2 * s + c_ref[0]) * nb2 + j, 0))], [part], full,
        [pltpu.SemaphoreType.DMA(()), pltpu.SemaphoreType.DMA(())], aliases={1: 0}).reshape(N_CHIP, r, cols)


def _pair_sum(name, g, core):
    n, r, cols = g.shape
    h = r // 2
    tr = _rows_tile(h, cols, g.dtype.itemsize)
    nb = h // tr
    half = jax.ShapeDtypeStruct((n, h, cols), g.dtype)

    def send_body(c_ref, g_ref, got_ref, send_sem, recv_sem):
        s, j = pl.program_id(0), pl.program_id(1)
        x, y, c = _coords()
        dst = got_ref.at[pl.ds(pl.multiple_of(s * h + j * tr, PACK), tr)]
        cp = pltpu.make_async_remote_copy(
            src_ref=g_ref, dst_ref=dst, send_sem=send_sem, recv_sem=recv_sem,
            device_id=(x, y, 1 - c), device_id_type=MESH)
        cp.start()
        cp.wait_send()

        @pl.when((s == n - 1) & (j == nb - 1))
        def _():
            pltpu.make_async_remote_copy(
                src_ref=got_ref, dst_ref=got_ref, send_sem=send_sem, recv_sem=recv_sem,
                device_id=(x, y, 1 - c), device_id_type=MESH).wait_recv()

    got = _staged_call(
        name + "_send", send_body, core, (n, nb),
        [pl.BlockSpec((tr, cols), lambda s, j, c_ref: ((2 * s + 1 - c_ref[0]) * nb + j, 0))],
        [g.reshape(n * r, cols)], jax.ShapeDtypeStruct((n * h, cols), g.dtype),
        [pltpu.SemaphoreType.DMA(()), pltpu.SemaphoreType.DMA(())]).reshape(n, h, cols)

    def add_body(c_ref, own_ref, got_ref, o_ref):
        o_ref[...] = (own_ref[...].astype(F32) + got_ref[...].astype(F32)).astype(o_ref.dtype)

    blk = pl.BlockSpec((None, tr, cols), lambda s, j, c_ref: (s, j, 0))
    return pl.pallas_call(
        add_body, name=name + "_add",
        grid_spec=pltpu.PrefetchScalarGridSpec(
            num_scalar_prefetch=1, grid=(n, nb),
            in_specs=[pl.BlockSpec((None, tr, cols), lambda s, j, c_ref: (s, c_ref[0] * nb + j, 0)), blk],
            out_specs=blk),
        out_shape=half, compiler_params=_params(("parallel", "parallel")),
    )(core, g, got)


def _chip_exchange_plan(pairs):
    shapes = [jax.ShapeDtypeStruct((N_CHIP - 1,) + p.shape[1:], p.dtype) for p in pairs]

    def copies(in_refs, out_refs, send_sems, recv_sems):
        x, y, c = _coords()
        out = []
        for i, (src, dst) in enumerate(zip(in_refs, out_refs)):
            for k in (1, 2, 3):
                px, py, pc = _chip_peer(x, y, c, k)
                n = 3 * i + k - 1
                out.append(pltpu.make_async_remote_copy(
                    src_ref=src.at[2 * px + py], dst_ref=dst.at[k - 1], send_sem=send_sems.at[n],
                    recv_sem=recv_sems.at[n], device_id=(px, py, pc), device_id_type=MESH))
        return out

    return shapes, copies


def _chip_sum(name, pair, got, chip):
    _, h, cols = pair.shape
    tr = _tile(h, max(PACK, ELEMWISE_BLOCK // cols), PACK)

    def body(chip_ref, own_ref, a_ref, b_ref, c_ref, o_ref):
        o_ref[...] = ((own_ref[...].astype(F32) + a_ref[...].astype(F32)) + b_ref[...].astype(F32)) + c_ref[...].astype(F32)

    return pl.pallas_call(
        body, name=name,
        grid_spec=pltpu.PrefetchScalarGridSpec(
            num_scalar_prefetch=1, grid=(h // tr,),
            in_specs=[pl.BlockSpec((None, tr, cols), lambda j, chip_ref: (chip_ref[0], j, 0))]
            + [pl.BlockSpec((None, tr, cols), (lambda j, chip_ref, k=k: (k, j, 0))) for k in range(N_CHIP - 1)],
            out_specs=pl.BlockSpec((tr, cols), lambda j, chip_ref: (j, 0))),
        out_shape=jax.ShapeDtypeStruct((h, cols), F32), compiler_params=_params(("parallel",)),
    )(chip, pair, got, got, got)


def _swap_halves(name, red, core, into, total_rows, base):
    h, cols = red.shape
    tr = _rows_tile(h, cols, 4)
    nb = h // tr

    def body(c_ref, red_ref, *rest):
        full_ref, send_sem, recv_sem, local_sem = rest[-4:]
        j = pl.program_id(0)
        x, y, c = _coords()
        dst = full_ref.at[pl.ds(pl.multiple_of(base + c * h + j * tr, SUB), tr)]
        local = pltpu.make_async_copy(red_ref, dst, local_sem)
        remote = pltpu.make_async_remote_copy(
            src_ref=red_ref, dst_ref=dst, send_sem=send_sem, recv_sem=recv_sem,
            device_id=(x, y, 1 - c), device_id_type=MESH)
        local.start()
        remote.start()
        remote.wait_send()
        local.wait()

        @pl.when(j == nb - 1)
        def _():
            landed = full_ref.at[pl.ds(0, h)]
            pltpu.make_async_remote_copy(
                src_ref=landed, dst_ref=landed, send_sem=send_sem, recv_sem=recv_sem,
                device_id=(x, y, 1 - c), device_id_type=MESH).wait_recv()

    in_specs = [pl.BlockSpec((tr, cols), lambda j, c_ref: (j, 0))]
    ins = [red]
    aliases = None
    if into is not None:
        in_specs.append(ANY_SPEC)
        ins.append(into)
        aliases = {2: 0}
    return _staged_call(
        name, body, core, (nb,), in_specs, ins, jax.ShapeDtypeStruct((total_rows, cols), F32),
        [pltpu.SemaphoreType.DMA(())] * 3, aliases)


def _cmul(ar, ai, br, bi):
    return ar * br - ai * bi, ar * bi + ai * br


def _cpow(ar, ai, n):
    rr, ri = jnp.ones_like(ar), jnp.zeros_like(ai)
    br, bi = ar, ai
    while n:
        if n & 1:
            rr, ri = _cmul(rr, ri, br, bi)
        br, bi = _cmul(br, bi, br, bi)
        n >>= 1
    return rr, ri


def _tile_rows(t):
    if isinstance(t, int):
        return pl.ds(t * SUB, SUB)
    return pl.ds(pl.multiple_of(t * SUB, SUB), SUB)


SCAN_UNROLL = 4


def _unrolled_loop(n, body, carry):
    trips = n // SCAN_UNROLL

    def trip(o, c):
        for k in range(SCAN_UNROLL):
            c = body(o * SCAN_UNROLL + k, c)
        return c

    carry = lax.fori_loop(0, trips, trip, carry)
    for i in range(trips * SCAN_UNROLL, n):
        carry = body(i, carry)
    return carry


def _scan_setup(buf, steps, ar, ai, h0r, h0i, rev):
    def total(i, carry):
        sr, si = carry
        rows = _tile_rows(steps - 1 - i if rev else i)
        pr, pi = _cmul(ar, ai, sr, si)
        return pr + buf[rows, 0:S5_H], pi + buf[rows, S5_H:S5_NS]

    zero = jnp.zeros((SUB, S5_H), F32)
    tot_r, tot_i = _unrolled_loop(steps, total, (zero, zero))
    pw_r, pw_i = _cpow(ar[0:1], ai[0:1], steps)
    row = lax.broadcasted_iota(jnp.int32, (SUB, S5_H), 0)
    cur_r, cur_i = h0r, h0i
    init_r, init_i = zero, zero
    for s in (range(N_SEG - 1, -1, -1) if rev else range(N_SEG)):
        init_r = jnp.where(row == s, cur_r, init_r)
        init_i = jnp.where(row == s, cur_i, init_i)
        nr, ni = _cmul(pw_r, pw_i, cur_r, cur_i)
        cur_r, cur_i = nr + tot_r[s:s + 1], ni + tot_i[s:s + 1]
    return init_r, init_i, cur_r, cur_i


def _scan(buf, steps, ar, ai, h0r, h0i, rev, store):
    init_r, init_i, fin_r, fin_i = _scan_setup(buf, steps, ar, ai, h0r, h0i, rev)
    if store:
        def step(i, carry):
            hr, hi = carry
            rows = _tile_rows(steps - 1 - i if rev else i)
            pr, pi = _cmul(ar, ai, hr, hi)
            hr, hi = pr + buf[rows, 0:S5_H], pi + buf[rows, S5_H:S5_NS]
            buf[rows, 0:S5_H] = hr
            buf[rows, S5_H:S5_NS] = hi
            return hr, hi

        _unrolled_loop(steps, step, (init_r, init_i))
    return fin_r, fin_i


def _adjoint_scan(gbuf, hbuf, steps, ar, ai, l0r, l0i, hin_r, hin_i, rev):
    ci = -ai
    arev = not rev
    init_r, init_i, fin_r, fin_i = _scan_setup(gbuf, steps, ar, ci, l0r, l0i, arev)
    zero = jnp.zeros((SUB, S5_H), F32)

    def update(t, hp_r, hp_i, carry):
        lr, li, dr, di = carry
        rows = _tile_rows(t)
        pr, pi = _cmul(ar, ci, lr, li)
        lr, li = pr + gbuf[rows, 0:S5_H], pi + gbuf[rows, S5_H:S5_NS]
        gbuf[rows, 0:S5_H] = lr
        gbuf[rows, S5_H:S5_NS] = li
        return lr, li, dr + lr * hp_r + li * hp_i, di + li * hp_r - lr * hp_i

    def step(i, carry):
        t = steps - 1 - i if rev is False else i
        prev = _tile_rows(t - 1 if rev is False else t + 1)
        return update(t, hbuf[prev, 0:S5_H], hbuf[prev, S5_H:S5_NS], carry)

    carry = _unrolled_loop(steps - 1, step, (init_r, init_i, zero, zero))
    row = lax.broadcasted_iota(jnp.int32, (SUB, S5_H), 0)
    if rev:
        last, edge, shift, t = _tile_rows(0), N_SEG - 1, SUB - 1, steps - 1
    else:
        last, edge, shift, t = _tile_rows(steps - 1), 0, 1, 0
    hp_r = jnp.where(row == edge, hin_r, pltpu.roll(hbuf[last, 0:S5_H], shift, 0))
    hp_i = jnp.where(row == edge, hin_i, pltpu.roll(hbuf[last, S5_H:S5_NS], shift, 0))
    _, _, dr, di = update(t, hp_r, hp_i, carry)
    return fin_r, fin_i, dr, di


def _s5_chunk(rows):
    return _tile(rows, 512, PACK)


def _s5_forward(u, uc, bblk, cblk, atile, dsk, ride_bufs, ride_copies):
    rows, d = u.shape
    rows_c = uc.shape[0]
    nj = d // S5_CB
    steps, steps_c = rows // N_SEG, rows_c // N_SEG
    rc = _s5_chunk(rows)
    n_ride = len(ride_bufs)
    n_sem = 3 * n_ride

    def body(u_ref, uc_ref, b_ref, c_ref, a_ref, d_ref, *rest):
        y_ref = rest[n_ride]
        ride = rest[n_ride + 1:2 * n_ride + 1]
        buf, bufc, send_sems, recv_sems = rest[2 * n_ride + 1:]

        @pl.when(pl.program_id(0) == 0)
        def _():
            for cp in ride_copies(ride, ride, send_sems, recv_sems):
                cp.start()

        zero = jnp.zeros((1, S5_H), F32)
        for dr in (0, 1):
            rev = dr == 1
            ar, ai = a_ref[dr, :, 0:S5_H], a_ref[dr, :, S5_H:S5_NS]
            bm, cm = b_ref[dr].astype(MXU_DTYPE), c_ref[dr].astype(MXU_DTYPE)
            bufc[...] = _dot(uc_ref[...], bm)
            fin_r, fin_i = _scan(bufc, steps_c, ar, ai, zero, zero, rev, False)

            def project(r, _):
                rs = pl.ds(pl.multiple_of(r * rc, rc), rc)
                buf[rs, :] = _dot(u_ref[rs, :], bm)
                return 0

            lax.fori_loop(0, rows // rc, project, 0)
            _scan(buf, steps, ar, ai, fin_r, fin_i, rev, True)

            def readout(r, _):
                rs = pl.ds(pl.multiple_of(r * rc, rc), rc)
                yv = _dot(buf[rs, :], cm)
                if dr == 0:
                    y_ref[rs, :] = u_ref[rs, :].astype(F32) * d_ref[...] + yv
                else:
                    y_ref[rs, :] += yv
                return 0

            lax.fori_loop(0, rows // rc, readout, 0)

        @pl.when(pl.program_id(0) == nj - 1)
        def _():
            for cp in ride_copies(ride, ride, send_sems, recv_sems):
                cp.wait()

    res = pl.pallas_call(
        body, name="s5_forward", grid=(nj,),
        in_specs=[
            pl.BlockSpec((rows, S5_CB), lambda j: (0, j)),
            pl.BlockSpec((rows_c, S5_CB), lambda j: (0, j)),
            pl.BlockSpec((2, None, S5_CB, S5_NS), lambda j: (0, j, 0, 0)),
            pl.BlockSpec((2, None, S5_NS, S5_CB), lambda j: (0, j, 0, 0)),
            pl.BlockSpec((2, None, SUB, S5_NS), lambda j: (0, j, 0, 0)),
            pl.BlockSpec((1, S5_CB), lambda j: (0, j)),
        ] + [ANY_SPEC] * n_ride,
        out_specs=[pl.BlockSpec((rows, S5_CB), lambda j: (0, j))] + [ANY_SPEC] * n_ride,
        out_shape=[jax.ShapeDtypeStruct((rows, d), F32)]
        + [jax.ShapeDtypeStruct(b.shape, b.dtype) for b in ride_bufs],
        input_output_aliases={6 + i: 1 + i for i in range(n_ride)},
        scratch_shapes=[pltpu.VMEM((rows, S5_NS), F32), pltpu.VMEM((rows_c, S5_NS), F32),
                        pltpu.SemaphoreType.DMA((n_sem,)), pltpu.SemaphoreType.DMA((n_sem,))],
        compiler_params=pltpu.CompilerParams(
            dimension_semantics=("arbitrary",), vmem_limit_bytes=VMEM_LIMIT, has_side_effects=True),
    )(u, uc, bblk, cblk, atile, dsk, *ride_bufs)
    return res[0], res[1:]


def _s5_backward(u, uc, dy, bblk, cblk, atile, dsk, ride_ins, ride_shapes, ride_copies):
    rows, d = u.shape
    rows_c = uc.shape[0]
    nj = d // S5_CB
    steps, steps_c = rows // N_SEG, rows_c // N_SEG
    rc = _s5_chunk(rows)
    nchunk = rows // rc
    n_ride = len(ride_ins)
    n_sem = 3 * n_ride

    def body(u_ref, uc_ref, dy_ref, b_ref, c_ref, a_ref, d_ref, *rest):
        ride_in = rest[:n_ride]
        du_ref, duc_ref, db_ref, dc_ref, da_ref, dd_ref = rest[n_ride:n_ride + 6]
        ride_out = rest[n_ride + 6:2 * n_ride + 6]
        hbuf, gbuf, hcbuf, gcbuf, send_sems, recv_sems = rest[2 * n_ride + 6:]

        @pl.when(pl.program_id(0) == 0)
        def _():
            for cp in ride_copies(ride_in, ride_out, send_sems, recv_sems):
                cp.start()

        zero = jnp.zeros((1, S5_H), F32)
        db_ref[...] = jnp.zeros_like(db_ref)
        dc_ref[...] = jnp.zeros_like(dc_ref)
        dd_ref[...] = jnp.zeros_like(dd_ref)
        for dr in (0, 1):
            rev = dr == 1
            ar, ai = a_ref[dr, :, 0:S5_H], a_ref[dr, :, S5_H:S5_NS]
            bm, cm = b_ref[dr].astype(MXU_DTYPE), c_ref[dr].astype(MXU_DTYPE)
            hcbuf[...] = _dot(uc_ref[...], bm)
            hin_r, hin_i = _scan(hcbuf, steps_c, ar, ai, zero, zero, rev, True)

            def project(r, _):
                rs = pl.ds(pl.multiple_of(r * rc, rc), rc)
                hbuf[rs, :] = _dot(u_ref[rs, :], bm)
                return 0

            lax.fori_loop(0, nchunk, project, 0)
            _scan(hbuf, steps, ar, ai, hin_r, hin_i, rev, True)

            def readout_bwd(r, _):
                rs = pl.ds(pl.multiple_of(r * rc, rc), rc)
                dyv = dy_ref[rs, :]
                gbuf[rs, :] = _dot(dyv, cm, NT)
                dc_ref[dr] += _dot(hbuf[rs, :], dyv, TN)
                return 0

            lax.fori_loop(0, nchunk, readout_bwd, 0)
            lf_r, lf_i, dar, dai = _adjoint_scan(gbuf, hbuf, steps, ar, ai, zero, zero, hin_r, hin_i, rev)
            gcbuf[...] = jnp.zeros_like(gcbuf)
            _, _, dar_c, dai_c = _adjoint_scan(gcbuf, hcbuf, steps_c, ar, ai, lf_r, lf_i, zero, zero, rev)
            da_ref[dr, :, 0:S5_H] = dar + dar_c
            da_ref[dr, :, S5_H:S5_NS] = dai + dai_c

            def project_bwd(r, _):
                rs = pl.ds(pl.multiple_of(r * rc, rc), rc)
                lam = gbuf[rs, :]
                uv = u_ref[rs, :]
                part = _dot(lam, bm, NT)
                db_ref[dr] += _dot(uv, lam, TN)
                if dr == 0:
                    dyv = dy_ref[rs, :].astype(F32)
                    du_ref[rs, :] = part + dyv * d_ref[...]
                    dd_ref[...] += (dyv * uv.astype(F32)).reshape(rc // SUB, SUB, S5_CB).sum(axis=0)
                else:
                    du_ref[rs, :] += part
                return 0

            lax.fori_loop(0, nchunk, project_bwd, 0)
            lam_c = gcbuf[...]
            part_c = _dot(lam_c, bm, NT)
            db_ref[dr] += _dot(uc_ref[...], lam_c, TN)
            if dr == 0:
                duc_ref[...] = part_c
            else:
                duc_ref[...] += part_c

        @pl.when(pl.program_id(0) == nj - 1)
        def _():
            for cp in ride_copies(ride_in, ride_out, send_sems, recv_sems):
                cp.wait()

    blk = lambda r: pl.BlockSpec((r, S5_CB), lambda j: (0, j))
    res = pl.pallas_call(
        body, name="s5_backward", grid=(nj,),
        in_specs=[
            blk(rows), blk(rows_c), blk(rows),
            pl.BlockSpec((2, None, S5_CB, S5_NS), lambda j: (0, j, 0, 0)),
            pl.BlockSpec((2, None, S5_NS, S5_CB), lambda j: (0, j, 0, 0)),
            pl.BlockSpec((2, None, SUB, S5_NS), lambda j: (0, j, 0, 0)),
            pl.BlockSpec((1, S5_CB), lambda j: (0, j)),
        ] + [ANY_SPEC] * n_ride,
        out_specs=[
            blk(rows), blk(rows_c),
            pl.BlockSpec((2, None, S5_CB, S5_NS), lambda j: (0, j, 0, 0)),
            pl.BlockSpec((2, None, S5_NS, S5_CB), lambda j: (0, j, 0, 0)),
            pl.BlockSpec((2, None, SUB, S5_NS), lambda j: (0, j, 0, 0)),
            pl.BlockSpec((SUB, S5_CB), lambda j: (0, j)),
        ] + [ANY_SPEC] * n_ride,
        out_shape=[
            jax.ShapeDtypeStruct((rows, d), F32), jax.ShapeDtypeStruct((rows_c, d), F32),
            jax.ShapeDtypeStruct(bblk.shape, F32), jax.ShapeDtypeStruct(cblk.shape, F32),
            jax.ShapeDtypeStruct(atile.shape, F32), jax.ShapeDtypeStruct((SUB, d), F32),
        ] + list(ride_shapes),
        scratch_shapes=[pltpu.VMEM((rows, S5_NS), F32), pltpu.VMEM((rows, S5_NS), F32),
                        pltpu.VMEM((rows_c, S5_NS), F32), pltpu.VMEM((rows_c, S5_NS), F32),
                        pltpu.SemaphoreType.DMA((n_sem,)), pltpu.SemaphoreType.DMA((n_sem,))],
        compiler_params=pltpu.CompilerParams(
            dimension_semantics=("arbitrary",), vmem_limit_bytes=VMEM_LIMIT, has_side_effects=True),
    )(u, uc, dy, bblk, cblk, atile, dsk, *ride_ins)
    return res[:6], res[6:]


def _s5_prepare(lam_re, lam_im, log_step, b_re, b_im, c_re, c_im):
    nd, g, p = lam_re.shape
    gb = S5_CB // S5_GROUP
    nj = g // gb
    dt = jnp.exp(log_step)[..., None]
    mag = jnp.exp(lam_re * dt)
    abar_re = mag * jnp.cos(lam_im * dt)
    abar_im = mag * jnp.sin(lam_im * dt)
    nr, ni = abar_re - 1.0, abar_im
    den = lam_re * lam_re + lam_im * lam_im
    fr = (nr * lam_re + ni * lam_im) / den
    fi = (ni * lam_re - nr * lam_im) / den
    bbar_re = fr[..., None] * b_re - fi[..., None] * b_im
    bbar_im = fr[..., None] * b_im + fi[..., None] * b_re
    eye = jnp.eye(gb, dtype=bool)

    def diag_in(w):
        w = w.reshape(nd, nj, gb, p, S5_GROUP).transpose(0, 1, 2, 4, 3)
        w = jnp.where(eye[None, None, :, None, :, None], w[:, :, :, :, None, :], 0.0)
        return w.reshape(nd, nj, gb * S5_GROUP, gb * p)

    def diag_out(w):
        w = w.reshape(nd, nj, gb, S5_GROUP, p).transpose(0, 1, 2, 4, 3)
        w = jnp.where(eye[None, None, :, None, :, None], w[:, :, :, :, None, :], 0.0)
        return w.reshape(nd, nj, gb * p, gb * S5_GROUP)

    bblk = jnp.concatenate([diag_in(bbar_re), diag_in(bbar_im)], axis=-1)
    cblk = jnp.concatenate([diag_out(c_re), -diag_out(c_im)], axis=-2)
    a2 = jnp.concatenate([abar_re.reshape(nd, nj, gb * p), abar_im.reshape(nd, nj, gb * p)], axis=-1)
    atile = jnp.broadcast_to(a2[:, :, None, :], (nd, nj, SUB, 2 * gb * p))
    return bblk, cblk, atile


def _shifted(x, prev_row, next_row):
    n = x.shape[0]
    row = lax.broadcasted_iota(jnp.int32, (SUB, x.shape[1]), 0)
    xp = pltpu.roll(x, 1, 0)
    xn = pltpu.roll(x, n - 1, 0)
    xp = jnp.concatenate([jnp.where(row == 0, prev_row, xp[:SUB]), xp[SUB:]], axis=0)
    xn = jnp.concatenate([xn[:n - SUB], jnp.where(row == SUB - 1, next_row, xn[n - SUB:])], axis=0)
    return xp, xn


def _edge_rows(ref, r0, n, total, group):
    lo = pl.multiple_of(jnp.maximum(r0 - group, 0), group)
    hi = pl.multiple_of(jnp.minimum(r0 + n, total - group), group)
    prev_row = ref[pl.ds(lo, group), :].astype(F32)[group - 1:group] * (r0 > 0).astype(F32)
    next_row = ref[pl.ds(hi, group), :].astype(F32)[0:1] * (r0 + n < total).astype(F32)
    return prev_row, next_row


def _conv_rows(ref, r0, n, total, w_ref, b_ref):
    x = ref[pl.ds(r0, n), :].astype(F32)
    xp, xn = _shifted(x, *_edge_rows(ref, r0, n, total, PACK))
    hc = w_ref[0:1, :] * xp + w_ref[1:2, :] * x + w_ref[2:3, :] * xn + b_ref[...]
    return hc, xp, x, xn


def _conv_specs(rows, f, tc):
    nt = f // tc
    val = lambda r: pl.BlockSpec((r, tc), lambda j: (0, j))
    gate = lambda r: pl.BlockSpec((r, tc), lambda j: (0, j + nt))
    return val, gate


def _conv_swiglu_fwd(name, h, cw, cb):
    rows, f2 = h.shape
    f = f2 // 2
    tc = _tile(f, 256)
    rc = _tile(rows, 256, PACK)
    val, gate = _conv_specs(rows, f, tc)

    def body(hv_ref, hg_ref, wv_ref, wg_ref, bv_ref, bg_ref, a_ref):
        def chunk(r, _):
            r0 = pl.multiple_of(r * rc, rc)
            hv = _conv_rows(hv_ref, r0, rc, rows, wv_ref, bv_ref)[0]
            hg = _conv_rows(hg_ref, r0, rc, rows, wg_ref, bg_ref)[0]
            a_ref[pl.ds(r0, rc), :] = (hg * _sigmoid(hg) * hv).astype(a_ref.dtype)
            return 0

        lax.fori_loop(0, rows // rc, chunk, 0)

    return pl.pallas_call(
        body, name=name, grid=(f // tc,),
        in_specs=[val(rows), gate(rows), val(3), gate(3), val(1), gate(1)],
        out_specs=val(rows), out_shape=jax.ShapeDtypeStruct((rows, f), ACT_DTYPE),
        compiler_params=_params(("parallel",)),
    )(h, h, cw, cw, cb, cb)


def _conv_swiglu_bwd(name, da, h, cw, cb):
    rows, f2 = h.shape
    f = f2 // 2
    tc = _tile(f, 256)
    rc = _tile(rows, 256, PACK)
    val, gate = _conv_specs(rows, f, tc)

    def body(da_ref, hv_ref, hg_ref, wv_ref, wg_ref, bv_ref, bg_ref,
             dhv_ref, dhg_ref, dwv_ref, dwg_ref, dbv_ref, dbg_ref, sv, sg):
        def first(r, carry):
            r0 = pl.multiple_of(r * rc, rc)
            rs = pl.ds(r0, rc)
            hv, vp, vx, vn = _conv_rows(hv_ref, r0, rc, rows, wv_ref, bv_ref)
            hg, gp, gx, gn = _conv_rows(hg_ref, r0, rc, rows, wg_ref, bg_ref)
            d = da_ref[rs, :].astype(F32)
            s = _sigmoid(hg)
            dv = d * (hg * s)
            dg = d * hv * (s * (1.0 + hg * (1.0 - s)))
            sv[rs, :] = dv
            sg[rs, :] = dg
            sums = [dv * vp, dv * vx, dv * vn, dv, dg * gp, dg * gx, dg * gn, dg]
            return tuple(c + jnp.sum(x, axis=0, keepdims=True) for c, x in zip(carry, sums))

        zero = jnp.zeros((1, tc), F32)
        acc = lax.fori_loop(0, rows // rc, first, (zero,) * 8)
        for k in range(3):
            dwv_ref[k:k + 1, :] = acc[k]
            dwg_ref[k:k + 1, :] = acc[4 + k]
        dbv_ref[...] = acc[3]
        dbg_ref[...] = acc[7]

        def second(r, _):
            r0 = pl.multiple_of(r * rc, rc)
            rs = pl.ds(r0, rc)
            for s_ref, w_ref, o_ref in ((sv, wv_ref, dhv_ref), (sg, wg_ref, dhg_ref)):
                x = s_ref[rs, :]
                xp, xn = _shifted(x, *_edge_rows(s_ref, r0, rc, rows, SUB))
                o_ref[rs, :] = (w_ref[0:1, :] * xn + w_ref[1:2, :] * x + w_ref[2:3, :] * xp).astype(o_ref.dtype)
            return 0

        lax.fori_loop(0, rows // rc, second, 0)

    res = pl.pallas_call(
        body, name=name, grid=(f // tc,),
        in_specs=[val(rows), val(rows), gate(rows), val(3), gate(3), val(1), gate(1)],
        out_specs=[val(rows), val(rows), val(3), val(3), val(1), val(1)],
        out_shape=[jax.ShapeDtypeStruct((rows, f), ACT_DTYPE)] * 2
        + [jax.ShapeDtypeStruct((3, f), F32)] * 2 + [jax.ShapeDtypeStruct((1, f), F32)] * 2,
        scratch_shapes=[pltpu.VMEM((rows, tc), F32), pltpu.VMEM((rows, tc), F32)],
        compiler_params=_params(("parallel",)),
    )(da, h, h, cw, cw, cb, cb)
    return res


def _pool_band(name, x, transpose, out_dtype):
    rows, d = x.shape
    ng = len(POOL_WINDOWS)
    ch = d // ng
    tm = _tile(rows, 256, PACK)
    win = tm + 2 * POOL_HALO
    assert win <= rows

    def body(x_ref, o_ref):
        half = lax.shift_left(jnp.int32(1), pl.program_id(0))
        t0 = pl.program_id(1) * tm
        ws = pl.multiple_of(jnp.clip(t0 - POOL_HALO, 0, rows - win), PACK)
        i = t0 + lax.broadcasted_iota(jnp.int32, (tm, win), 0)
        j = ws + lax.broadcasted_iota(jnp.int32, (tm, win), 1)

        def inv_count(t):
            hi = jnp.minimum(t + half - 1, rows - 1)
            lo = jnp.maximum(t - half, 0)
            return 1.0 / (hi - lo + 1).astype(F32)

        xw = x_ref[pl.ds(ws, win), :]
        xt = x_ref[pl.ds(pl.multiple_of(t0, PACK), tm), :].astype(F32)
        if transpose:
            band = (j - half <= i) & (i <= j + half - 1)
            tw = ws + lax.broadcasted_iota(jnp.int32, (win, 1), 0)
            o = _dot(band.astype(MXU_DTYPE), xw.astype(F32) * inv_count(tw)) - xt
        else:
            band = (i - half <= j) & (j <= i + half - 1)
            tt = t0 + lax.broadcasted_iota(jnp.int32, (tm, 1), 0)
            o = _dot(band.astype(MXU_DTYPE), xw) * inv_count(tt) - xt
        o_ref[...] = o.astype(o_ref.dtype)

    return pl.pallas_call(
        body, name=name, grid=(ng, rows // tm),
        in_specs=[pl.BlockSpec((rows, ch), lambda g, i: (0, g))],
        out_specs=pl.BlockSpec((tm, ch), lambda g, i: (i, g)),
        out_shape=jax.ShapeDtypeStruct((rows, d), out_dtype),
        compiler_params=_params(("parallel", "arbitrary")),
    )(x)


def _ada_forward(c16, ada_w, ada_b):
    nl, d, cols = ada_w.shape
    tn = _tile(cols, 512)

    def body(c_ref, w_ref, b_ref, o_ref):
        cv = c_ref[...]
        o_ref[...] = _dot(cv * _sigmoid(cv), w_ref[...]) + b_ref[...]

    return pl.pallas_call(
        body, name="ada_forward", grid=(nl, cols // tn),
        in_specs=[pl.BlockSpec((16, d), lambda l, n: (0, 0)),
                  pl.BlockSpec((None, d, tn), lambda l, n: (l, 0, n)),
                  pl.BlockSpec((None, 1, tn), lambda l, n: (l, 0, n))],
        out_specs=pl.BlockSpec((None, 16, tn), lambda l, n: (l, 0, n)),
        out_shape=jax.ShapeDtypeStruct((nl, 16, cols), F32),
        compiler_params=_params(("parallel", "parallel")),
    )(c16, ada_w, ada_b)


def _ada_backward(c16, dmod, ada_w):
    nl, d, cols = ada_w.shape
    tn = _tile(cols, 512)
    nn = cols // tn

    def body(c_ref, g_ref, w_ref, gw_ref, gc_ref):
        cv = c_ref[...]
        s = _sigmoid(cv)
        gv = g_ref[...]
        gw_ref[...] = _dot(cv * s, gv, TN)
        dcond = _dot(gv, w_ref[...], NT)
        row = lax.broadcasted_iota(jnp.int32, dcond.shape, 0)
        dctx = jnp.sum(jnp.where(row >= 8, dcond * (s * (1.0 + cv * (1.0 - s))), 0.0), axis=0, keepdims=True)

        @pl.when((pl.program_id(0) == 0) & (pl.program_id(1) == 0))
        def _():
            gc_ref[...] = jnp.zeros_like(gc_ref)

        gc_ref[...] += dctx

    return pl.pallas_call(
        body, name="ada_backward", grid=(nl, nn),
        in_specs=[pl.BlockSpec((16, d), lambda l, n: (0, 0)),
                  pl.BlockSpec((None, 16, tn), lambda l, n: (l, 0, n)),
                  pl.BlockSpec((None, d, tn), lambda l, n: (l, 0, n))],
        out_specs=[pl.BlockSpec((None, d, tn), lambda l, n: (l, 0, n)),
                   pl.BlockSpec((1, d), lambda l, n: (0, 0))],
        out_shape=[jax.ShapeDtypeStruct((nl, d, cols), F32), jax.ShapeDtypeStruct((1, d), F32)],
        compiler_params=_params(("arbitrary", "arbitrary")),
    )(c16, dmod, ada_w)


def _row_sum16(name, a):
    nl, _, w = a.shape
    tn = _tile(w, 4096)

    def body(a_ref, o_ref):
        o_ref[...] = jnp.sum(a_ref[...], axis=0, keepdims=True)

    return pl.pallas_call(
        body, name=name, grid=(nl, w // tn),
        in_specs=[pl.BlockSpec((None, 16, tn), lambda l, n: (l, 0, n))],
        out_specs=pl.BlockSpec((None, 1, tn), lambda l, n: (l, 0, n)),
        out_shape=jax.ShapeDtypeStruct((nl, 1, w), F32),
        compiler_params=_params(("parallel", "parallel")),
    )(a)


def _pack(arrays, row_align):
    flat = jnp.concatenate([a.reshape(-1).astype(F32) for a in arrays])
    quantum = row_align * LANE
    padded = -(-flat.shape[0] // quantum) * quantum
    return jnp.pad(flat, (0, padded - flat.shape[0])).reshape(-1, LANE)


def _unpack(packed, shapes):
    flat = packed.reshape(-1)
    out, off = [], 0
    for s in shapes:
        n = math.prod(s)
        out.append(flat[off:off + n].reshape(s))
        off += n
    return out


def _grid_pos_emb(n_tokens, dim):
    rows = n_tokens // GRID_W
    r, col = jnp.meshgrid(jnp.arange(rows, dtype=F32), jnp.arange(GRID_W, dtype=F32), indexing="ij")
    quarter = dim // 4
    omega = 1.0 / (POS_BASE ** (jnp.arange(quarter, dtype=F32) / quarter))

    def enc(p):
        ang = p.reshape(-1, 1) * omega[None, :]
        return jnp.concatenate([jnp.sin(ang), jnp.cos(ang)], axis=-1)

    return jnp.concatenate([enc(r), enc(col)], axis=-1)


def _ffn_forward(tag, v, x_in, up4, dn4, layer, cw, cb):
    h = _mm_cols(f"ffn_up_{tag}", v, up4, layer, ACT_DTYPE)
    a = _conv_swiglu_fwd(f"ffn_conv_{tag}", h, cw, cb)
    f = _mm_rows(f"ffn_down_{tag}", a, dn4, layer, F32)
    return h, a, f


def _ffn_backward(tag, dx_out, fb, x_mid, v, h, a, up4, dn4, layer, cw, cb, gate5, gam3, gam2, scale4p1):
    def post(dxo, f, gam, gate):
        dy, dgate, dgam = _postnorm_bwd(dxo, f.astype(F32), gam, gate)
        return (dy,), (dgate, dgam)

    (df,), (dgate5, dgam3) = _rowwise(f"ffn_post_bwd_{tag}", post, dx_out.shape[0],
                                      [(dx_out, False), (fb, False)], [gam3, gate5],
                                      [(dx_out.shape[1], ACT_DTYPE, False)], [dx_out.shape[1]] * 2)
    da = _mm_rows_nt(f"ffn_down_dx_{tag}", df, dn4, layer, ACT_DTYPE)
    g_dn = _mm_rows_tn(f"ffn_down_dw_{tag}", a, df, dn4.shape[-2], ACT_DTYPE)
    dhv, dhg, dwv, dwg, dbv, dbg = _conv_swiglu_bwd(f"ffn_conv_bwd_{tag}", da, h, cw, cb)
    dh = (dhv, dhg)
    dcw = jnp.concatenate([dwv, dwg], axis=1)
    dcb = jnp.concatenate([dbv, dbg], axis=1)
    dv = _mm_cols_nt(f"ffn_up_dx_{tag}", dh, up4, layer, F32)
    g_up = _mm_cols_tn(f"ffn_up_dw_{tag}", v, dh, up4.shape[-1], ACT_DTYPE)

    def pre(dvv, x, dxo, gam, s1):
        dx, dshift, dscale, dgam = _prenorm_bwd(dvv, x, gam, s1)
        return (dxo + dx,), (dshift, dscale, dgam)

    d = dx_out.shape[1]
    (dx_mid,), (dshift3, dscale4, dgam2) = _rowwise(
        f"ffn_pre_bwd_{tag}", pre, dx_out.shape[0], [(dv, False), (x_mid, False), (dx_out, False)],
        [gam2, scale4p1], [(d, F32, False)], [d] * 3)
    return dx_mid, g_up, g_dn, dcw, dcb, (dshift3, dscale4, dgate5), (dgam2, dgam3)


def kernel(x, c, ctx, c_ctx, ada_w, ada_b, norm_g, s5_lam_re, s5_lam_im, s5_log_step, s5_b_re, s5_b_im, s5_c_re, s5_c_im, s5_d, s5_glu_w, pool_w, pool_scale, ffn_up, ffn_conv, ffn_conv_b, ffn_down, loss_target, m_c_ctx, m_ada_w, m_ada_b, m_norm_g, m_s5_lam_re, m_s5_lam_im, m_s5_log_step, m_s5_b_re, m_s5_b_im, m_s5_c_re, m_s5_c_im, m_s5_d, m_s5_glu_w, m_pool_w, m_pool_scale, m_ffn_up, m_ffn_conv, m_ffn_conv_b, m_ffn_down, v_c_ctx, v_ada_w, v_ada_b, v_norm_g, v_s5_lam_re, v_s5_lam_im, v_s5_log_step, v_s5_b_re, v_s5_b_im, v_s5_c_re, v_s5_c_im, v_s5_d, v_s5_glu_w, v_pool_w, v_pool_scale, v_ffn_up, v_ffn_conv, v_ffn_conv_b, v_ffn_down):
    ix, iy, ic = _coords()
    chip = 2 * ix + iy
    me = 2 * chip + ic
    _, rows, d = x.shape
    rows_c = ctx.shape[1]
    nl = ada_w.shape[0]
    assert nl == 2 and s5_glu_w.shape[0] == 1 and pool_w.shape[0] == 1
    a_cols = ada_w.shape[2]
    f2s = ffn_up.shape[2]
    f2 = N_CHIP * f2s
    ds = d // N_CHIP
    ng = len(POOL_WINDOWS)
    ch = d // ng
    ps = pool_w.shape[2]

    core = jnp.reshape(ic, (1,)).astype(jnp.int32)
    chip_id = jnp.reshape(chip, (1,)).astype(jnp.int32)
    pos = jnp.concatenate([core, chip_id])
    shards = {"up": _view2d(ffn_up), "down": _view2d(ffn_down), "glu": s5_glu_w[0], "pool": _view2d(pool_w[0])}
    gather_bufs = [_cast_own_half(f"cast_{n}", w, pos) for n, w in shards.items()]

    c_all = _all_gather8("gather_cond", _pack([c], SUB))
    c_all = c_all.reshape(N_DEV, -1)[:, :d]
    c16 = jnp.concatenate([c_all, jnp.broadcast_to(c_ctx[None, :], (N_DEV, d))], axis=0)
    ada_b_mine = lax.dynamic_slice_in_dim(ada_b, chip * a_cols, a_cols, axis=1)
    mod_part = _ada_forward(c16, ada_w, ada_b_mine[:, None, :])
    narrow_shapes = [mod_part.shape, norm_g.shape, pool_scale.shape, ffn_conv.shape]
    narrow = _all_gather8("gather_narrow", _pack([mod_part, norm_g, pool_scale, ffn_conv], SUB))
    per_chip = [_unpack(narrow[2 * s], narrow_shapes) for s in range(N_CHIP)]
    mod_all = jnp.concatenate([p[0] for p in per_chip], axis=-1)
    gam = jnp.concatenate([p[1] for p in per_chip], axis=-1)
    pscale = jnp.concatenate([p[2] for p in per_chip], axis=-1)
    conv_w = jnp.concatenate([p[3] for p in per_chip], axis=-1)
    mod_mine = lax.dynamic_index_in_dim(mod_all, me, axis=1, keepdims=False)
    mod_ctx = mod_all[0, N_DEV]

    def mods(vec):
        s0, s1, g2, s3, s4, g5 = [vec[k * d:(k + 1) * d][None, :] for k in range(N_MOD)]
        return s0, 1.0 + s1, g2, s3, 1.0 + s4, g5

    m0, m1, mc = mods(mod_mine[0]), mods(mod_mine[1]), mods(mod_ctx)
    gains = [[gam[l, k][None, :] for k in range(4)] for l in range(nl)]
    conv_b = ffn_conv_b[:, None, :]

    pos = _grid_pos_emb(rows, d)

    def init(xv, pv, g0, shift, s1):
        x0 = xv + pv
        return (x0, _prenorm(x0, g0, shift, s1)), ()

    (x0, u), _ = _rowwise("init", init, rows, [(x[0], False), (pos, False)], [gains[0][0], m0[0], m0[1]],
                          [(d, F32, False), (d, ACT_DTYPE, True)])
    (uc,), _ = _rowwise("ctx_prenorm", lambda cv, g0, shift, s1: ((_prenorm(cv, g0, shift, s1),), ()),
                        rows_c, [(ctx[0], False)], [gains[0][0], mc[0], mc[1]], [(d, ACT_DTYPE, True)])
    s5_params = (s5_lam_re[0], s5_lam_im[0], s5_log_step[0], s5_b_re[0], s5_b_im[0], s5_c_re[0], s5_c_im[0])
    (bblk, cblk, atile), s5_vjp = jax.vjp(_s5_prepare, *s5_params)
    buf_up, buf_dn, buf_glu, buf_pool = gather_bufs
    rows_of = {n: w.shape[0] for n, w in shards.items()}
    y, (buf_up, buf_glu) = _s5_forward(u, uc, bblk, cblk, atile, s5_d, [buf_up, buf_glu],
                                       _gather_ici_plan([rows_of["up"], rows_of["glu"]]))
    up4 = _swap_gathered("gather_up", buf_up, core, rows_of["up"]).reshape((N_CHIP,) + ffn_up.shape)
    glu4 = _swap_gathered("gather_glu", buf_glu, core, rows_of["glu"])
    (z,), _ = _rowwise("gelu", lambda yv: ((_gelu(yv),), ()), rows, [(y, True)], [], [(d, ACT_DTYPE, False)])
    zz = _mm_cols("glu_proj", z, glu4[:, None], 0, ACT_DTYPE)

    def glu_out(zzv, xv, gate2, g1, g2, shift3, s4):
        zf = zzv.astype(F32)
        o = zf[:, :d] * _sigmoid(zf[:, d:])
        x1 = xv + gate2 * (o * _rstd(o) * g1)
        return (x1, _prenorm(x1, g2, shift3, s4)), ()

    (x1, v0), _ = _rowwise("glu_resid", glu_out, rows, [(zz, False), (x0, False)],
                           [m0[2], gains[0][1], gains[0][2], m0[3], m0[4]], [(d, F32, False), (d, ACT_DTYPE, False)])
    h0, (buf_dn, buf_pool) = _mm_cols(
        "ffn_up_l0", v0, up4, 0, ACT_DTYPE,
        ride=([buf_dn, buf_pool], _gather_ici_plan([rows_of["down"], rows_of["pool"]])))
    dn4 = _swap_gathered("gather_down", buf_dn, core, rows_of["down"]).reshape((N_CHIP,) + ffn_down.shape)
    pool_full = _swap_gathered("gather_pool", buf_pool, core, rows_of["pool"])
    pool_full = pool_full.reshape(N_CHIP, ng, ps, ch).transpose(1, 0, 2, 3).reshape(ng, ch, ch)
    a0 = _conv_swiglu_fwd("ffn_conv_l0", h0, conv_w[0], conv_b[0])
    f0 = _mm_rows("ffn_down_l0", a0, dn4, 0, F32)

    def ffn_out(fv, xv, gate5, g3, g0n, shift0, s1):
        x2 = xv + gate5 * (fv * _rstd(fv) * g3)
        return (x2, _prenorm(x2, g0n, shift0, s1), fv), ()

    (x2, u1, fb0), _ = _rowwise("ffn_resid_l0", ffn_out, rows, [(f0, False), (x1, False)],
                                [m0[5], gains[0][3], gains[1][0], m1[0], m1[1]],
                                [(d, F32, False), (d, ACT_DTYPE, False), (d, ACT_DTYPE, False)])

    p1 = _pool_band("pool_band", u1, False, ACT_DTYPE)
    yr = _mm_grp("pool_proj", p1, pool_full, NN, F32)

    def pool_out(yv, xv, ps_, gate2, g1, g2, shift3, s4):
        o = yv * ps_
        x1n = xv + gate2 * (o * _rstd(o) * g1)
        return (x1n, _prenorm(x1n, g2, shift3, s4), yv), ()

    (x3, v1, yb), _ = _rowwise("pool_resid", pool_out, rows, [(yr, False), (x2, False)],
                               [pscale, m1[2], gains[1][1], gains[1][2], m1[3], m1[4]],
                               [(d, F32, False), (d, ACT_DTYPE, False), (d, ACT_DTYPE, False)])
    h1, a1, f1 = _ffn_forward("l1", v1, x3, up4, dn4, 1, conv_w[1], conv_b[1])

    def loss_head(fv, xv, tv, gate5, g3):
        err = xv + gate5 * (fv * _rstd(fv) * g3) - tv
        return (err * (1.0 / d), fv), (err * err,)

    (dx4, fb1), (sq,) = _rowwise("loss_head", loss_head, rows, [(f1, False), (x3, False), (loss_target[0], False)],
                                 [m1[5], gains[1][3]], [(d, F32, False), (d, ACT_DTYPE, False)], [d])
    loss = lax.psum(0.5 * jnp.sum(sq) / d, ("x", "y", "c"))

    dx3, g_up1, g_dn1, dcw1, dcb1, dmod_ffn1, (dgam12, dgam13) = _ffn_backward(
        "l1", dx4, fb1, x3, v1, h1, a1, up4, dn4, 1, conv_w[1], conv_b[1], m1[5], gains[1][3], gains[1][2], m1[4])

    def pool_post(dxo, yv, ps_, g1, gate2):
        yraw = yv.astype(F32)
        dy, dgate, dgam = _postnorm_bwd(dxo, yraw * ps_, g1, gate2)
        return (dy * ps_,), (dgate, dgam, dy * yraw)

    (dyr,), (dgate2_1, dgam11, dpscale) = _rowwise("pool_post_bwd", pool_post, rows, [(dx3, False), (yb, False)],
                                                   [pscale, gains[1][1], m1[2]], [(d, ACT_DTYPE, False)], [d] * 3)
    dp1 = _mm_grp("pool_proj_dx", dyr, pool_full, NT, ACT_DTYPE)
    g_pool = _mm_grp_tn("pool_proj_dw", p1, dyr, ng, ACT_DTYPE)
    du1 = _pool_band("pool_band_bwd", dp1, True, F32)

    def pre_bwd(duv, xv, dxo, g0, s1):
        dx, dshift, dscale, dgam = _prenorm_bwd(duv, xv, g0, s1)
        return (dxo + dx,), (dshift, dscale, dgam)

    (dx2,), (dshift0_1, dscale1_1, dgam10) = _rowwise(
        "pool_pre_bwd", pre_bwd, rows, [(du1, False), (x2, False), (dx3, False)],
        [gains[1][0], m1[1]], [(d, F32, False)], [d] * 3)

    dx1, g_up0, g_dn0, dcw0, dcb0, dmod_ffn0, (dgam02, dgam03) = _ffn_backward(
        "l0", dx2, fb0, x1, v0, h0, a0, up4, dn4, 0, conv_w[0], conv_b[0], m0[5], gains[0][3], gains[0][2], m0[4])

    def glu_post(dxo, zzv, g1, gate2):
        zf = zzv.astype(F32)
        val, s = zf[:, :d], _sigmoid(zf[:, d:])
        do, dgate, dgam = _postnorm_bwd(dxo, val * s, g1, gate2)
        return (jnp.concatenate([do * s, do * val * (s * (1.0 - s))], axis=1),), (dgate, dgam)

    (dzz,), (dgate2_0, dgam01) = _rowwise("glu_post_bwd", glu_post, rows, [(dx1, False), (zz, False)],
                                          [gains[0][1], m0[2]], [(2 * d, ACT_DTYPE, False)], [d] * 2)
    dz = _mm_cols_nt("glu_proj_dx", dzz, glu4[:, None], 0, F32)
    g_glu = _mm_cols_tn("glu_proj_dw", z, dzz, glu4.shape[-1], ACT_DTYPE)
    (dy,), _ = _rowwise("gelu_bwd", lambda dzv, yv: ((dzv * _gelu_grad(yv),), ()), rows,
                        [(dz, False), (y, True)], [], [(d, ACT_DTYPE, True)])
    g_pool4 = g_pool.reshape(ng, N_CHIP, ps, ch).transpose(1, 0, 2, 3).reshape(N_CHIP, ng * ps, ch)
    big = {"up0": g_up0, "up1": g_up1, "dn0": g_dn0, "dn1": g_dn1, "glu": g_glu, "pool": g_pool4}
    pairs = [_pair_sum(f"pair_{n}", g, core) for n, g in big.items()]
    ride_shapes, ride_copies = _chip_exchange_plan(pairs)
    (du0, duc, d_bblk, d_cblk, d_atile, d_dsk), others = _s5_backward(
        u, uc, dy, bblk, cblk, atile, s5_d, pairs, ride_shapes, ride_copies)
    (gx,), (dshift0_0, dscale1_0, dgam00) = _rowwise(
        "s5_pre_bwd", pre_bwd, rows, [(du0, True), (x0, False), (dx1, False)],
        [gains[0][0], m0[1]], [(d, F32, False)], [d] * 3)

    def ctx_bwd(duv, cv, g0, s1):
        _, dshift, dscale, dgam = _prenorm_bwd(duv, cv, g0, s1)
        return (), (dshift, dscale, dgam)

    _, (dshift_c, dscale_c, dgam00c) = _rowwise("ctx_pre_bwd", ctx_bwd, rows_c, [(duc, True), (ctx[0], False)],
                                                [gains[0][0], mc[1]], [], [d] * 3)
    g_s5 = s5_vjp((d_bblk, d_cblk, d_atile))

    zero_d = jnp.zeros((1, d), F32)
    dmod_lat = jnp.stack([
        jnp.concatenate([dshift0_0, dscale1_0, dgate2_0, *dmod_ffn0], axis=1),
        jnp.concatenate([dshift0_1, dscale1_1, dgate2_1, *dmod_ffn1], axis=1)])
    dmod_ctx = jnp.stack([jnp.concatenate([dshift_c, dscale_c] + [zero_d] * 4, axis=1),
                          jnp.zeros((1, N_MOD * d), F32)])
    dmod_shape = (nl, 2, N_MOD * d)
    dmod_all = _all_gather8("gather_dmod", _pack([jnp.concatenate([dmod_lat, dmod_ctx], axis=1)], SUB))
    dmod_all = jnp.stack([_unpack(dmod_all[k], [dmod_shape])[0] for k in range(N_DEV)])
    dmod16 = jnp.concatenate([dmod_all[:, :, 0], dmod_all[:, :, 1]], axis=0).transpose(1, 0, 2)
    dmod16_mine = lax.dynamic_slice_in_dim(dmod16, chip * a_cols, a_cols, axis=2)
    g_ada_w, g_cctx_part = _ada_backward(c16, dmod16_mine, ada_w)
    g_ada_b = _row_sum16("ada_bias_grad", dmod16)[:, 0]

    d_gam = jnp.stack([jnp.concatenate([dgam00 + dgam00c, dgam01, dgam02, dgam03], axis=0),
                       jnp.concatenate([dgam10, dgam11, dgam12, dgam13], axis=0)])
    small = [d_gam, 0.5 * g_cctx_part, *g_s5, jnp.sum(d_dsk, axis=0, keepdims=True), dpscale,
             jnp.stack([dcw0, dcw1]), jnp.stack([dcb0[0], dcb1[0]])]
    small_shapes = [s.shape for s in small]
    packed = _pack(small, N_DEV * SUB)
    summed = _all_reduce8("reduce_small", packed.reshape(N_DEV, -1, LANE)).reshape(-1, LANE)
    (r_gam, r_cctx, r_lam_re, r_lam_im, r_log_step, r_b_re, r_b_im, r_c_re, r_c_im,
     r_dsk, r_pscale, r_conv, r_convb) = _unpack(summed, small_shapes)
    g_norm = lax.dynamic_slice_in_dim(r_gam, chip * ds, ds, axis=2)
    g_pscale = lax.dynamic_slice_in_dim(r_pscale, chip * ds, ds, axis=1)
    g_conv = lax.dynamic_slice_in_dim(r_conv, chip * f2s, f2s, axis=2)

    red = {n: _chip_sum(f"chip_sum_{n}", p, o, chip_id) for n, p, o in zip(big, pairs, others)}
    d_rows, dn_rows = ffn_up.shape[1], ffn_down.shape[1]
    g_up = _swap_halves("swap_up0", red["up0"], core, None, nl * d_rows, 0)
    g_up = _swap_halves("swap_up1", red["up1"], core, g_up, nl * d_rows, d_rows).reshape(ffn_up.shape)
    g_dn = _swap_halves("swap_dn0", red["dn0"], core, None, nl * dn_rows, 0)
    g_dn = _swap_halves("swap_dn1", red["dn1"], core, g_dn, nl * dn_rows, dn_rows).reshape(ffn_down.shape)
    g_glu_f = _swap_halves("swap_glu", red["glu"], core, None, d, 0)
    g_pool_f = _swap_halves("swap_pool", red["pool"], core, None, ng * ps, 0)

    grads = {
        "c_ctx": r_cctx[0], "ada_w": g_ada_w, "ada_b": g_ada_b, "norm_g": g_norm,
        "s5_lam_re": r_lam_re[None], "s5_lam_im": r_lam_im[None], "s5_log_step": r_log_step[None],
        "s5_b_re": r_b_re[None], "s5_b_im": r_b_im[None], "s5_c_re": r_c_re[None], "s5_c_im": r_c_im[None],
        "s5_d": r_dsk, "s5_glu_w": g_glu_f[None], "pool_w": g_pool_f.reshape(pool_w.shape),
        "pool_scale": g_pscale, "ffn_up": g_up, "ffn_conv": g_conv, "ffn_conv_b": r_convb, "ffn_down": g_dn,
    }
    weights = {
        "c_ctx": (c_ctx, m_c_ctx, v_c_ctx), "ada_w": (ada_w, m_ada_w, v_ada_w), "ada_b": (ada_b, m_ada_b, v_ada_b),
        "norm_g": (norm_g, m_norm_g, v_norm_g), "s5_lam_re": (s5_lam_re, m_s5_lam_re, v_s5_lam_re),
        "s5_lam_im": (s5_lam_im, m_s5_lam_im, v_s5_lam_im), "s5_log_step": (s5_log_step, m_s5_log_step, v_s5_log_step),
        "s5_b_re": (s5_b_re, m_s5_b_re, v_s5_b_re), "s5_b_im": (s5_b_im, m_s5_b_im, v_s5_b_im),
        "s5_c_re": (s5_c_re, m_s5_c_re, v_s5_c_re), "s5_c_im": (s5_c_im, m_s5_c_im, v_s5_c_im),
        "s5_d": (s5_d, m_s5_d, v_s5_d), "s5_glu_w": (s5_glu_w, m_s5_glu_w, v_s5_glu_w),
        "pool_w": (pool_w, m_pool_w, v_pool_w), "pool_scale": (pool_scale, m_pool_scale, v_pool_scale),
        "ffn_up": (ffn_up, m_ffn_up, v_ffn_up), "ffn_conv": (ffn_conv, m_ffn_conv, v_ffn_conv),
        "ffn_conv_b": (ffn_conv_b, m_ffn_conv_b, v_ffn_conv_b), "ffn_down": (ffn_down, m_ffn_down, v_ffn_down),
    }
    names = list(weights)
    large = ("ada_w", "s5_glu_w", "pool_w", "ffn_up", "ffn_down")
    delta, new_m, new_v = {}, {}, {}
    for n in names:
        w, m, v = weights[n]
        if n in large:
            delta[n], new_m[n], new_v[n] = _adamw(f"adamw_{n}", w, grads[n], m, v)
        else:
            shape = w.shape
            view = (1, shape[0]) if w.ndim == 1 else shape
            res = _adamw(f"adamw_{n}", *[t.reshape(view) for t in (w, grads[n], m, v)])
            delta[n], new_m[n], new_v[n] = [t.reshape(shape) for t in res]

    return (loss, gx[None], *[grads[n] for n in names], *[delta[n] for n in names],
            *[new_m[n] for n in names], *[new_v[n] for n in names])
```

```python
import math

import jax
import jax.numpy as jnp
from jax import lax
from jax.experimental import pallas as pl
from jax.experimental.pallas import tpu as pltpu

F32 = jnp.float32
MXU_DTYPE = jnp.bfloat16
ACT_DTYPE = jnp.bfloat16

LANE = 128
SUB = 8
PACK = 16
VMEM_LIMIT = 56 * 1024 * 1024
ELEMWISE_BLOCK = 1 << 18

N_DEV = 8
N_CHIP = 4
N_SEG = SUB
S5_GROUP = 16
S5_STATE = 64
S5_CB = LANE
S5_H = (S5_CB // S5_GROUP) * S5_STATE
S5_NS = 2 * S5_H
POOL_WINDOWS = (2, 4, 8, 16)
POOL_HALO = 16
GRID_W = 64
POS_BASE = 10000.0
RMS_EPS = 1e-6
N_MOD = 6

ADAM_LR = 0.001
ADAM_B1 = 0.9
ADAM_B2 = 0.999
ADAM_EPS = 1e-08
ADAM_WD = 0.01
ADAM_STEP = 10

NN = (((1,), (0,)), ((), ()))
NT = (((1,), (1,)), ((), ()))
TN = (((0,), (0,)), ((), ()))
MESH = pl.DeviceIdType.MESH


def _tile(n, cap, align=LANE):
    best = None
    for t in range(align, min(n, cap) + 1, align):
        if n % t == 0:
            best = t
    return n if best is None else best


def _params(sem=None):
    return pltpu.CompilerParams(dimension_semantics=sem, vmem_limit_bytes=VMEM_LIMIT)


def _dot(a, b, dims=NN):
    return lax.dot_general(a.astype(MXU_DTYPE), b.astype(MXU_DTYPE), dims, preferred_element_type=F32)


def _sigmoid(x):
    return 0.5 * jnp.tanh(0.5 * x) + 0.5


_GELU_C = math.sqrt(2.0 / math.pi)
_GELU_K = 0.044715


def _gelu(x):
    return 0.5 * x * (1.0 + jnp.tanh(_GELU_C * (x + _GELU_K * x * x * x)))


def _gelu_grad(x):
    t = jnp.tanh(_GELU_C * (x + _GELU_K * x * x * x))
    return 0.5 * (1.0 + t) + 0.5 * x * (1.0 - t * t) * _GELU_C * (1.0 + 3.0 * _GELU_K * x * x)


def _rstd(x):
    return lax.rsqrt(jnp.mean(x * x, axis=-1, keepdims=True) + RMS_EPS)


def _norm_bwd(dxh, xh, r):
    return r * (dxh - xh * jnp.mean(dxh * xh, axis=-1, keepdims=True))


def _rowwise(name, fn, rows, tiled, vecs, outs, accs=()):
    seg = rows // N_SEG
    tm = _tile(seg, 256, SUB)
    nt, ntp = rows // tm, seg // tm
    n_t, n_v, n_o, n_a = len(tiled), len(vecs), len(outs), len(accs)

    def spec(width, perm):
        if perm:
            return pl.BlockSpec((tm, width), lambda i: (i % ntp, i // ntp))
        return pl.BlockSpec((tm, width), lambda i: (i, 0))

    args, in_specs = [], []
    for arr, perm in tiled:
        width = arr.shape[-1]
        args.append(arr.reshape(seg, N_SEG * width) if perm else arr)
        in_specs.append(spec(width, perm))
    for v in vecs:
        if isinstance(v, tuple):
            args.append(v[0])
            in_specs.append(pl.BlockSpec((v[1], v[0].shape[1]), lambda i: (i, 0)))
        else:
            args.append(v)
            in_specs.append(pl.BlockSpec(v.shape, lambda i: (0, 0)))
    out_shape, out_specs = [], []
    for width, dtype, perm in outs:
        out_shape.append(jax.ShapeDtypeStruct((seg, N_SEG * width) if perm else (rows, width), dtype))
        out_specs.append(spec(width, perm))
    for width in accs:
        out_shape.append(jax.ShapeDtypeStruct((SUB, width), F32))
        out_specs.append(pl.BlockSpec((SUB, width), lambda i: (0, 0)))

    def body(*refs):
        vals = [r[...] for r in refs[:n_t + n_v]]
        o_refs = refs[n_t + n_v:n_t + n_v + n_o]
        a_refs = refs[n_t + n_v + n_o:]
        o_vals, a_vals = fn(*vals)
        for r, v in zip(o_refs, o_vals):
            r[...] = v.astype(r.dtype)
        if n_a:
            @pl.when(pl.program_id(0) == 0)
            def _():
                for r in a_refs:
                    r[...] = jnp.zeros_like(r)
            for r, v in zip(a_refs, a_vals):
                r[...] += v.reshape(tm // SUB, SUB, v.shape[-1]).sum(axis=0)

    res = pl.pallas_call(
        body, name=name, grid=(nt,), in_specs=in_specs, out_specs=out_specs, out_shape=out_shape,
        compiler_params=_params(("arbitrary",)),
    )(*args)
    res = list(res)
    for k, (width, _, perm) in enumerate(outs):
        if perm:
            res[k] = res[k].reshape(rows, width)
    return res[:n_o], [jnp.sum(a, axis=0, keepdims=True) for a in res[n_o:]]


def _prenorm(x, gam, shift, scale1):
    r = _rstd(x)
    return (x * r) * gam * scale1 + shift


def _prenorm_bwd(du, x, gam, scale1):
    r = _rstd(x)
    xh = x * r
    dxn = du * scale1
    dx = _norm_bwd(dxn * gam, xh, r)
    return dx, du, du * (xh * gam), dxn * xh


def _postnorm_bwd(dxo, y, gam, gate):
    r = _rstd(y)
    yh = y * r
    dyn = dxo * gate
    dy = _norm_bwd(dyn * gam, yh, r)
    return dy, dxo * (yh * gam), dyn * yh


def _matmul(name, a, b, dims, grid, a_spec, b_spec, o_spec, out_shape, out_dtype, acc_shape,
            pair=None, ride=None):
    nk = grid[2]
    ins = list(a if pair and pair[0] == "a" else [a]) + list(b if pair and pair[0] == "b" else [b])
    specs = list(a_spec if pair and pair[0] == "a" else [a_spec]) + list(b_spec if pair and pair[0] == "b" else [b_spec])
    n_in = len(ins)
    bufs, copies = ride if ride else ((), None)
    n_ride = len(bufs)

    def body(*refs):
        in_refs = refs[:n_in]
        o_ref = refs[n_in + n_ride]
        ride_refs = refs[n_in + n_ride + 1:n_in + 2 * n_ride + 1]
        scratch = refs[n_in + 2 * n_ride + 1:]
        pid = [pl.program_id(ax) for ax in range(3)]
        if n_ride:
            send_sems, recv_sems = scratch[-2:]

            @pl.when((pid[0] == 0) & (pid[1] == 0) & (pid[2] == 0))
            def _():
                for cp in copies(ride_refs, ride_refs, send_sems, recv_sems):
                    cp.start()

        vals = [r[...] for r in in_refs]
        if pair:
            first = pid[pair[1]] < pair[2]
            picked = jnp.where(first, vals[0], vals[1]) if pair[0] == "a" else jnp.where(first, vals[1], vals[2])
            vals = [picked, vals[2]] if pair[0] == "a" else [vals[0], picked]
        part = _dot(vals[0], vals[1], dims)
        if nk == 1:
            o_ref[...] = part.astype(o_ref.dtype)
        else:
            acc_ref = scratch[0]

            @pl.when(pid[2] == 0)
            def _():
                acc_ref[...] = part

            @pl.when(pid[2] > 0)
            def _():
                acc_ref[...] += part

            @pl.when(pid[2] == nk - 1)
            def _():
                o_ref[...] = acc_ref[...].astype(o_ref.dtype)

        if n_ride:
            @pl.when((pid[0] == grid[0] - 1) & (pid[1] == grid[1] - 1) & (pid[2] == nk - 1))
            def _():
                for cp in copies(ride_refs, ride_refs, send_sems, recv_sems):
                    cp.wait()

    scratch_shapes = [] if nk == 1 else [pltpu.VMEM(acc_shape, F32)]
    out_shapes = [jax.ShapeDtypeStruct(out_shape, out_dtype)]
    if not n_ride:
        return pl.pallas_call(
            body, name=name, grid=grid, in_specs=specs, out_specs=[o_spec], out_shape=out_shapes,
            scratch_shapes=scratch_shapes, compiler_params=_params(("parallel", "parallel", "arbitrary")),
        )(*ins)[0]
    n_sem = 3 * n_ride
    res = pl.pallas_call(
        body, name=name, grid=grid, in_specs=specs + [ANY_SPEC] * n_ride,
        out_specs=[o_spec] + [ANY_SPEC] * n_ride,
        out_shape=out_shapes + [jax.ShapeDtypeStruct(x.shape, x.dtype) for x in bufs],
        input_output_aliases={n_in + i: 1 + i for i in range(n_ride)},
        scratch_shapes=scratch_shapes + [pltpu.SemaphoreType.DMA((n_sem,)), pltpu.SemaphoreType.DMA((n_sem,))],
        compiler_params=pltpu.CompilerParams(
            dimension_semantics=("arbitrary",) * 3, vmem_limit_bytes=VMEM_LIMIT, has_side_effects=True),
    )(*ins, *bufs)
    return res[0], res[1:]


def _mm_cols(name, a, w4, layer, out_dtype, ride=None):
    m, k = a.shape
    ns = w4.shape[-1]
    tm, tn = _tile(m, 1024), _tile(ns, 1536)
    nps = ns // tn
    return _matmul(
        name, a, w4, NN, (m // tm, N_CHIP * nps, 1),
        pl.BlockSpec((tm, k), lambda i, n, kk: (i, 0)),
        pl.BlockSpec((None, None, k, tn), lambda i, n, kk: (n // nps, layer, 0, n % nps)),
        pl.BlockSpec((tm, tn), lambda i, n, kk: (i, n)),
        (m, N_CHIP * ns), out_dtype, None, ride=ride)


def _halves_specs(g, rows_blk, cols_blk, tiles_half, row_of, col_of):
    if not isinstance(g, tuple):
        return g, pl.BlockSpec((rows_blk, cols_blk), lambda *p: (row_of(p), col_of(p)))
    left = pl.BlockSpec((rows_blk, cols_blk), lambda *p: (row_of(p), jnp.minimum(col_of(p), tiles_half - 1)))
    right = pl.BlockSpec((rows_blk, cols_blk), lambda *p: (row_of(p), jnp.maximum(col_of(p) - tiles_half, 0)))
    return g, (left, right)


def _mm_cols_nt(name, g, w4, layer, out_dtype):
    m = (g[0] if isinstance(g, tuple) else g).shape[0]
    k, ns = w4.shape[-2:]
    tm, tk = _tile(m, 1024), _tile(ns, 1536)
    kps = ns // tk
    half = N_CHIP * kps // 2
    g, g_spec = _halves_specs(g, tm, tk, half, lambda p: p[0], lambda p: p[2])
    return _matmul(
        name, g, w4, NT, (m // tm, 1, N_CHIP * kps), g_spec,
        pl.BlockSpec((None, None, k, tk), lambda i, n, kk: (kk // kps, layer, 0, kk % kps)),
        pl.BlockSpec((tm, k), lambda i, n, kk: (i, 0)),
        (m, k), out_dtype, (tm, k), pair=("a", 2, half) if isinstance(g, tuple) else None)


def _mm_cols_tn(name, a, g, ns, out_dtype):
    m, k = a.shape
    tkm, tmk, tn = _tile(m, 2048), _tile(k, 1024), _tile(ns, 1536)
    nps = ns // tn
    half = N_CHIP * nps // 2
    g, g_spec = _halves_specs(g, tkm, tn, half, lambda p: p[2], lambda p: p[1])
    return _matmul(
        name, a, g, TN, (k // tmk, N_CHIP * nps, m // tkm),
        pl.BlockSpec((tkm, tmk), lambda i, n, kk: (kk, i)), g_spec,
        pl.BlockSpec((None, tmk, tn), lambda i, n, kk: (n // nps, i, n % nps)),
        (N_CHIP, k, ns), out_dtype, (tmk, tn), pair=("b", 1, half) if isinstance(g, tuple) else None)


def _mm_rows(name, a, w4, layer, out_dtype):
    m = a.shape[0]
    rs, n = w4.shape[-2:]
    tm, tk = _tile(m, 1024), _tile(rs, 1536)
    kps = rs // tk
    return _matmul(
        name, a, w4, NN, (m // tm, 1, N_CHIP * kps),
        pl.BlockSpec((tm, tk), lambda i, j, kk: (i, kk)),
        pl.BlockSpec((None, None, tk, n), lambda i, j, kk: (kk // kps, layer, kk % kps, 0)),
        pl.BlockSpec((tm, n), lambda i, j, kk: (i, 0)),
        (m, n), out_dtype, (tm, n))


def _mm_rows_nt(name, g, w4, layer, out_dtype):
    m, n = g.shape
    rs = w4.shape[-2]
    tm, tn = _tile(m, 1024), _tile(rs, 1536)
    nps = rs // tn
    return _matmul(
        name, g, w4, NT, (m // tm, N_CHIP * nps, 1),
        pl.BlockSpec((tm, n), lambda i, j, kk: (i, 0)),
        pl.BlockSpec((None, None, tn, n), lambda i, j, kk: (j // nps, layer, j % nps, 0)),
        pl.BlockSpec((tm, tn), lambda i, j, kk: (i, j)),
        (m, N_CHIP * rs), out_dtype, None)


def _mm_rows_tn(name, a, g, rs, out_dtype):
    m = a.shape[0]
    n = g.shape[1]
    tkm, tmr, tn = _tile(m, 2048), _tile(rs, 1536), _tile(n, 1024)
    mps = rs // tmr
    return _matmul(
        name, a, g, TN, (N_CHIP * mps, n // tn, m // tkm),
        pl.BlockSpec((tkm, tmr), lambda i, j, kk: (kk, i)),
        pl.BlockSpec((tkm, tn), lambda i, j, kk: (kk, j)),
        pl.BlockSpec((None, tmr, tn), lambda i, j, kk: (i // mps, i % mps, j)),
        (N_CHIP, rs, n), out_dtype, (tmr, tn))


def _mm_grp(name, a, w, dims, out_dtype):
    m = a.shape[0]
    ng, ch = w.shape[:2]
    tm = _tile(m, 1024)
    return _matmul(
        name, a, w, dims, (m // tm, ng, 1),
        pl.BlockSpec((tm, ch), lambda i, g, kk: (i, g)),
        pl.BlockSpec((None, ch, ch), lambda i, g, kk: (g, 0, 0)),
        pl.BlockSpec((tm, ch), lambda i, g, kk: (i, g)),
        (m, ng * ch), out_dtype, None)


def _mm_grp_tn(name, a, g, ng, out_dtype):
    m = a.shape[0]
    ch = a.shape[1] // ng
    tk = _tile(m, 2048)
    return _matmul(
        name, a, g, TN, (ng, 1, m // tk),
        pl.BlockSpec((tk, ch), lambda i, j, kk: (kk, i)),
        pl.BlockSpec((tk, ch), lambda i, j, kk: (kk, i)),
        pl.BlockSpec((None, ch, ch), lambda i, j, kk: (i, 0, 0)),
        (ng, ch, ch), out_dtype, (ch, ch))


def _view2d(a):
    return a.reshape(-1, a.shape[-1])


def _elementwise(name, fn, ins, out_dtypes):
    r, c = ins[0].shape
    lanes = -(-c // LANE) * LANE
    tr = _tile(r, max(PACK, ELEMWISE_BLOCK // lanes), PACK)
    spec = pl.BlockSpec((tr, c), lambda i: (i, 0))

    def body(*refs):
        outs = fn(*[x[...] for x in refs[:len(ins)]])
        for o_ref, o in zip(refs[len(ins):], outs):
            o_ref[...] = o.astype(o_ref.dtype)

    return pl.pallas_call(
        body, name=name, grid=(r // tr,), in_specs=[spec] * len(ins), out_specs=[spec] * len(out_dtypes),
        out_shape=[jax.ShapeDtypeStruct((r, c), d) for d in out_dtypes],
        compiler_params=_params(("parallel",)),
    )(*ins)


def _adamw_math(w, g, m, v):
    m = ADAM_B1 * m + (1.0 - ADAM_B1) * g
    v = ADAM_B2 * v + (1.0 - ADAM_B2) * (g * g)
    m_hat = m / (1.0 - ADAM_B1 ** ADAM_STEP)
    v_hat = v / (1.0 - ADAM_B2 ** ADAM_STEP)
    delta = -ADAM_LR * (m_hat / (jnp.sqrt(v_hat) + ADAM_EPS) + ADAM_WD * w)
    return delta, m, v


def _adamw(name, w, g, m, v):
    shape = w.shape
    outs = _elementwise(name, _adamw_math, [_view2d(w), _view2d(g), _view2d(m), _view2d(v)], [F32, F32, F32])
    return [o.reshape(shape) for o in outs]


def _coords():
    return lax.axis_index("x"), lax.axis_index("y"), lax.axis_index("c")


def _peer(x, y, c, k):
    return (x ^ (k >> 2), y ^ ((k >> 1) & 1), c ^ (k & 1))


def _all_gather8(name, block):
    r = block.shape[0]

    def body(x_ref, out_ref, send_sems, recv_sems):
        x, y, c = _coords()
        me = 4 * x + 2 * y + c
        out_ref[me] = x_ref[...]
        copies = []
        for k in range(1, N_DEV):
            cp = pltpu.make_async_remote_copy(
                src_ref=x_ref, dst_ref=out_ref.at[me], send_sem=send_sems.at[k], recv_sem=recv_sems.at[k],
                device_id=_peer(x, y, c, k), device_id_type=MESH)
            cp.start()
            copies.append(cp)
        for cp in copies:
            cp.wait()

    return pl.pallas_call(
        body, name=name,
        in_specs=[pl.BlockSpec(memory_space=pltpu.VMEM)], out_specs=pl.BlockSpec(memory_space=pltpu.VMEM),
        out_shape=jax.ShapeDtypeStruct((N_DEV, r, LANE), F32),
        scratch_shapes=[pltpu.SemaphoreType.DMA((N_DEV,)), pltpu.SemaphoreType.DMA((N_DEV,))],
        compiler_params=pltpu.CompilerParams(vmem_limit_bytes=VMEM_LIMIT),
    )(block)


def _all_reduce8(name, parts):
    r = parts.shape[1]

    def body(p_ref, out_ref, rbuf, send1, recv1, send2, recv2):
        x, y, c = _coords()
        me = 4 * x + 2 * y + c
        first = []
        for k in range(1, N_DEV):
            px, py, pc = _peer(x, y, c, k)
            cp = pltpu.make_async_remote_copy(
                src_ref=p_ref.at[4 * px + 2 * py + pc], dst_ref=rbuf.at[me], send_sem=send1.at[k],
                recv_sem=recv1.at[k], device_id=(px, py, pc), device_id_type=MESH)
            cp.start()
            first.append(cp)
        rbuf[me] = p_ref[me]
        for cp in first:
            cp.wait()
        acc = rbuf[0]
        for d in range(1, N_DEV):
            acc = acc + rbuf[d]
        out_ref[me] = acc
        second = []
        for k in range(1, N_DEV):
            cp = pltpu.make_async_remote_copy(
                src_ref=out_ref.at[me], dst_ref=out_ref.at[me], send_sem=send2.at[k], recv_sem=recv2.at[k],
                device_id=_peer(x, y, c, k), device_id_type=MESH)
            cp.start()
            second.append(cp)
        for cp in second:
            cp.wait()

    return pl.pallas_call(
        body, name=name,
        in_specs=[pl.BlockSpec(memory_space=pltpu.VMEM)], out_specs=pl.BlockSpec(memory_space=pltpu.VMEM),
        out_shape=jax.ShapeDtypeStruct((N_DEV, r, LANE), F32),
        scratch_shapes=[pltpu.VMEM((N_DEV, r, LANE), F32)] + [pltpu.SemaphoreType.DMA((N_DEV,))] * 4,
        compiler_params=pltpu.CompilerParams(vmem_limit_bytes=VMEM_LIMIT),
    )(parts)


ANY_SPEC = pl.BlockSpec(memory_space=pl.ANY)
COMM_BLOCK_BYTES = 2 << 20


def _staged_call(name, body, core, grid, in_specs, ins, out_shape, scratch, aliases=None):
    return pl.pallas_call(
        body, name=name,
        grid_spec=pltpu.PrefetchScalarGridSpec(
            num_scalar_prefetch=1, grid=grid, in_specs=in_specs, out_specs=ANY_SPEC, scratch_shapes=scratch),
        out_shape=out_shape, input_output_aliases=aliases or {},
        compiler_params=pltpu.CompilerParams(
            dimension_semantics=("arbitrary",) * len(grid), vmem_limit_bytes=VMEM_LIMIT, has_side_effects=True),
    )(core, *ins)


def _rows_tile(rows, cols, itemsize):
    return _tile(rows, max(PACK, COMM_BLOCK_BYTES // (cols * itemsize)), PACK)


def _chip_peer(x, y, c, k):
    return (x ^ (k >> 1), y ^ (k & 1), c)


def _cast_own_half(name, w, pos):
    r, cols = w.shape
    h = r // 2
    tr = _tile(h, max(PACK, ELEMWISE_BLOCK // cols), PACK)
    nb = h // tr

    def body(p_ref, w_ref, o_ref):
        o_ref[...] = w_ref[...].astype(o_ref.dtype)

    return pl.pallas_call(
        body, name=name,
        grid_spec=pltpu.PrefetchScalarGridSpec(
            num_scalar_prefetch=1, grid=(nb,),
            in_specs=[pl.BlockSpec((tr, cols), lambda j, p: (p[0] * nb + j, 0))],
            out_specs=pl.BlockSpec((tr, cols), lambda j, p: ((2 * p[1] + p[0]) * nb + j, 0))),
        out_shape=jax.ShapeDtypeStruct((N_CHIP * r, cols), MXU_DTYPE),
        compiler_params=_params(("parallel",)),
    )(pos, w)


def _gather_ici_plan(shard_rows):
    def copies(refs, _, send_sems, recv_sems):
        x, y, c = _coords()
        out = []
        for i, (ref, r) in enumerate(zip(refs, shard_rows)):
            h = r // 2
            mine = ref.at[pl.ds(pl.multiple_of((2 * x + y) * r + c * h, PACK), h)]
            for k in (1, 2, 3):
                n = 3 * i + k - 1
                out.append(pltpu.make_async_remote_copy(
                    src_ref=mine, dst_ref=mine, send_sem=send_sems.at[n], recv_sem=recv_sems.at[n],
                    device_id=_chip_peer(x, y, c, k), device_id_type=MESH))
        return out

    return copies


def _swap_gathered(name, part, core, r):
    cols = part.shape[1]
    h = r // 2
    full = jax.ShapeDtypeStruct(part.shape, part.dtype)
    tr2 = _rows_tile(h, cols, 2)
    nb2 = h // tr2

    def swap_body(c_ref, mine_ref, full_ref, send_sem, recv_sem):
        s, j = pl.program_id(0), pl.program_id(1)
        x, y, c = _coords()
        dst = full_ref.at[pl.ds(pl.multiple_of(s * r + c * h + j * tr2, PACK), tr2)]
        cp = pltpu.make_async_remote_copy(
            src_ref=mine_ref, dst_ref=dst, send_sem=send_sem, recv_sem=recv_sem,
            device_id=(x, y, 1 - c), device_id_type=MESH)
        cp.start()
        cp.wait_send()

        @pl.when((s == N_CHIP - 1) & (j == nb2 - 1))
        def _():
            landed = full_ref.at[pl.ds(0, N_CHIP * h)]
            pltpu.make_async_remote_copy(
                src_ref=landed, dst_ref=landed, send_sem=send_sem, recv_sem=recv_sem,
                device_id=(x, y, 1 - c), device_id_type=MESH).wait_recv()

    return _staged_call(
        name + "_d2d", swap_body, core, (N_CHIP, nb2),
        [pl.BlockSpec((tr2, cols), lambda s, j, c_ref: ((2 * s + c_ref[0]) * nb2 + j, 0))], [part], full,
        [pltpu.SemaphoreType.DMA(()), pltpu.SemaphoreType.DMA(())], aliases={1: 0}).reshape(N_CHIP, r, cols)


def _pair_sum(name, g, core):
    n, r, cols = g.shape
    h = r // 2
    tr = _rows_tile(h, cols, g.dtype.itemsize)
    nb = h // tr
    half = jax.ShapeDtypeStruct((n, h, cols), g.dtype)

    def send_body(c_ref, g_ref, got_ref, send_sem, recv_sem):
        s, j = pl.program_id(0), pl.program_id(1)
        x, y, c = _coords()
        dst = got_ref.at[pl.ds(pl.multiple_of(s * h + j * tr, PACK), tr)]
        cp = pltpu.make_async_remote_copy(
            src_ref=g_ref, dst_ref=dst, send_sem=send_sem, recv_sem=recv_sem,
            device_id=(x, y, 1 - c), device_id_type=MESH)
        cp.start()
        cp.wait_send()

        @pl.when((s == n - 1) & (j == nb - 1))
        def _():
            pltpu.make_async_remote_copy(
                src_ref=got_ref, dst_ref=got_ref, send_sem=send_sem, recv_sem=recv_sem,
                device_id=(x, y, 1 - c), device_id_type=MESH).wait_recv()

    got = _staged_call(
        name + "_send", send_body, core, (n, nb),
        [pl.BlockSpec((tr, cols), lambda s, j, c_ref: ((2 * s + 1 - c_ref[0]) * nb + j, 0))],
        [g.reshape(n * r, cols)], jax.ShapeDtypeStruct((n * h, cols), g.dtype),
        [pltpu.SemaphoreType.DMA(()), pltpu.SemaphoreType.DMA(())]).reshape(n, h, cols)

    def add_body(c_ref, own_ref, got_ref, o_ref):
        o_ref[...] = (own_ref[...].astype(F32) + got_ref[...].astype(F32)).astype(o_ref.dtype)

    blk = pl.BlockSpec((None, tr, cols), lambda s, j, c_ref: (s, j, 0))
    return pl.pallas_call(
        add_body, name=name + "_add",
        grid_spec=pltpu.PrefetchScalarGridSpec(
            num_scalar_prefetch=1, grid=(n, nb),
            in_specs=[pl.BlockSpec((None, tr, cols), lambda s, j, c_ref: (s, c_ref[0] * nb + j, 0)), blk],
            out_specs=blk),
        out_shape=half, compiler_params=_params(("parallel", "parallel")),
    )(core, g, got)


def _chip_exchange_plan(pairs):
    shapes = [jax.ShapeDtypeStruct((N_CHIP - 1,) + p.shape[1:], p.dtype) for p in pairs]

    def copies(in_refs, out_refs, send_sems, recv_sems):
        x, y, c = _coords()
        out = []
        for i, (src, dst) in enumerate(zip(in_refs, out_refs)):
            for k in (1, 2, 3):
                px, py, pc = _chip_peer(x, y, c, k)
                n = 3 * i + k - 1
                out.append(pltpu.make_async_remote_copy(
                    src_ref=src.at[2 * px + py], dst_ref=dst.at[k - 1], send_sem=send_sems.at[n],
                    recv_sem=recv_sems.at[n], device_id=(px, py, pc), device_id_type=MESH))
        return out

    return shapes, copies


def _chip_sum(name, pair, got, chip):
    _, h, cols = pair.shape
    tr = _tile(h, max(PACK, ELEMWISE_BLOCK // cols), PACK)

    def body(chip_ref, own_ref, a_ref, b_ref, c_ref, o_ref):
        o_ref[...] = ((own_ref[...].astype(F32) + a_ref[...].astype(F32)) + b_ref[...].astype(F32)) + c_ref[...].astype(F32)

    return pl.pallas_call(
        body, name=name,
        grid_spec=pltpu.PrefetchScalarGridSpec(
            num_scalar_prefetch=1, grid=(h // tr,),
            in_specs=[pl.BlockSpec((None, tr, cols), lambda j, chip_ref: (chip_ref[0], j, 0))]
            + [pl.BlockSpec((None, tr, cols), (lambda j, chip_ref, k=k: (k, j, 0))) for k in range(N_CHIP - 1)],
            out_specs=pl.BlockSpec((tr, cols), lambda j, chip_ref: (j, 0))),
        out_shape=jax.ShapeDtypeStruct((h, cols), F32), compiler_params=_params(("parallel",)),
    )(chip, pair, got, got, got)


def _swap_halves(name, red, core, into, total_rows, base):
    h, cols = red.shape
    tr = _rows_tile(h, cols, 4)
    nb = h // tr

    def body(c_ref, red_ref, *rest):
        full_ref, send_sem, recv_sem, local_sem = rest[-4:]
        j = pl.program_id(0)
        x, y, c = _coords()
        dst = full_ref.at[pl.ds(pl.multiple_of(base + c * h + j * tr, SUB), tr)]
        local = pltpu.make_async_copy(red_ref, dst, local_sem)
        remote = pltpu.make_async_remote_copy(
            src_ref=red_ref, dst_ref=dst, send_sem=send_sem, recv_sem=recv_sem,
            device_id=(x, y, 1 - c), device_id_type=MESH)
        local.start()
        remote.start()
        remote.wait_send()
        local.wait()

        @pl.when(j == nb - 1)
        def _():
            landed = full_ref.at[pl.ds(0, h)]
            pltpu.make_async_remote_copy(
                src_ref=landed, dst_ref=landed, send_sem=send_sem, recv_sem=recv_sem,
                device_id=(x, y, 1 - c), device_id_type=MESH).wait_recv()

    in_specs = [pl.BlockSpec((tr, cols), lambda j, c_ref: (j, 0))]
    ins = [red]
    aliases = None
    if into is not None:
        in_specs.append(ANY_SPEC)
        ins.append(into)
        aliases = {2: 0}
    return _staged_call(
        name, body, core, (nb,), in_specs, ins, jax.ShapeDtypeStruct((total_rows, cols), F32),
        [pltpu.SemaphoreType.DMA(())] * 3, aliases)


def _cmul(ar, ai, br, bi):
    return ar * br - ai * bi, ar * bi + ai * br


def _cpow(ar, ai, n):
    rr, ri = jnp.ones_like(ar), jnp.zeros_like(ai)
    br, bi = ar, ai
    while n:
        if n & 1:
            rr, ri = _cmul(rr, ri, br, bi)
        br, bi = _cmul(br, bi, br, bi)
        n >>= 1
    return rr, ri


def _tile_rows(t):
    if isinstance(t, int):
        return pl.ds(t * SUB, SUB)
    return pl.ds(pl.multiple_of(t * SUB, SUB), SUB)


SCAN_UNROLL = 4


def _unrolled_loop(n, body, carry):
    trips = n // SCAN_UNROLL

    def trip(o, c):
        for k in range(SCAN_UNROLL):
            c = body(o * SCAN_UNROLL + k, c)
        return c

    carry = lax.fori_loop(0, trips, trip, carry)
    for i in range(trips * SCAN_UNROLL, n):
        carry = body(i, carry)
    return carry


def _scan_setup(buf, steps, ar, ai, h0r, h0i, rev):
    def total(i, carry):
        sr, si = carry
        rows = _tile_rows(steps - 1 - i if rev else i)
        pr, pi = _cmul(ar, ai, sr, si)
        return pr + buf[rows, 0:S5_H], pi + buf[rows, S5_H:S5_NS]

    zero = jnp.zeros((SUB, S5_H), F32)
    tot_r, tot_i = _unrolled_loop(steps, total, (zero, zero))
    pw_r, pw_i = _cpow(ar[0:1], ai[0:1], steps)
    row = lax.broadcasted_iota(jnp.int32, (SUB, S5_H), 0)
    cur_r, cur_i = h0r, h0i
    init_r, init_i = zero, zero
    for s in (range(N_SEG - 1, -1, -1) if rev else range(N_SEG)):
        init_r = jnp.where(row == s, cur_r, init_r)
        init_i = jnp.where(row == s, cur_i, init_i)
        nr, ni = _cmul(pw_r, pw_i, cur_r, cur_i)
        cur_r, cur_i = nr + tot_r[s:s + 1], ni + tot_i[s:s + 1]
    return init_r, init_i, cur_r, cur_i


def _scan(buf, steps, ar, ai, h0r, h0i, rev, store):
    init_r, init_i, fin_r, fin_i = _scan_setup(buf, steps, ar, ai, h0r, h0i, rev)
    if store:
        def step(i, carry):
            hr, hi = carry
            rows = _tile_rows(steps - 1 - i if rev else i)
            pr, pi = _cmul(ar, ai, hr, hi)
            hr, hi = pr + buf[rows, 0:S5_H], pi + buf[rows, S5_H:S5_NS]
            buf[rows, 0:S5_H] = hr
            buf[rows, S5_H:S5_NS] = hi
            return hr, hi

        _unrolled_loop(steps, step, (init_r, init_i))
    return fin_r, fin_i


def _adjoint_scan(gbuf, hbuf, steps, ar, ai, l0r, l0i, hin_r, hin_i, rev):
    ci = -ai
    arev = not rev
    init_r, init_i, fin_r, fin_i = _scan_setup(gbuf, steps, ar, ci, l0r, l0i, arev)
    zero = jnp.zeros((SUB, S5_H), F32)

    def update(t, hp_r, hp_i, carry):
        lr, li, dr, di = carry
        rows = _tile_rows(t)
        pr, pi = _cmul(ar, ci, lr, li)
        lr, li = pr + gbuf[rows, 0:S5_H], pi + gbuf[rows, S5_H:S5_NS]
        gbuf[rows, 0:S5_H] = lr
        gbuf[rows, S5_H:S5_NS] = li
        return lr, li, dr + lr * hp_r + li * hp_i, di + li * hp_r - lr * hp_i

    def step(i, carry):
        t = steps - 1 - i if rev is False else i
        prev = _tile_rows(t - 1 if rev is False else t + 1)
        return update(t, hbuf[prev, 0:S5_H], hbuf[prev, S5_H:S5_NS], carry)

    carry = _unrolled_loop(steps - 1, step, (init_r, init_i, zero, zero))
    row = lax.broadcasted_iota(jnp.int32, (SUB, S5_H), 0)
    if rev:
        last, edge, shift, t = _tile_rows(0), N_SEG - 1, SUB - 1, steps - 1
    else:
        last, edge, shift, t = _tile_rows(steps - 1), 0, 1, 0
    hp_r = jnp.where(row == edge, hin_r, pltpu.roll(hbuf[last, 0:S5_H], shift, 0))
    hp_i = jnp.where(row == edge, hin_i, pltpu.roll(hbuf[last, S5_H:S5_NS], shift, 0))
    _, _, dr, di = update(t, hp_r, hp_i, carry)
    return fin_r, fin_i, dr, di


def _s5_chunk(rows):
    return _tile(rows, 512, PACK)


def _s5_forward(u, uc, bblk, cblk, atile, dsk, ride_bufs, ride_copies):
    rows, d = u.shape
    rows_c = uc.shape[0]
    nj = d // S5_CB
    steps, steps_c = rows // N_SEG, rows_c // N_SEG
    rc = _s5_chunk(rows)
    n_ride = len(ride_bufs)
    n_sem = 3 * n_ride

    def body(u_ref, uc_ref, b_ref, c_ref, a_ref, d_ref, *rest):
        y_ref = rest[n_ride]
        ride = rest[n_ride + 1:2 * n_ride + 1]
        buf, bufc, send_sems, recv_sems = rest[2 * n_ride + 1:]

        @pl.when(pl.program_id(0) == 0)
        def _():
            for cp in ride_copies(ride, ride, send_sems, recv_sems):
                cp.start()

        zero = jnp.zeros((1, S5_H), F32)
        for dr in (0, 1):
            rev = dr == 1
            ar, ai = a_ref[dr, :, 0:S5_H], a_ref[dr, :, S5_H:S5_NS]
            bm, cm = b_ref[dr].astype(MXU_DTYPE), c_ref[dr].astype(MXU_DTYPE)
            bufc[...] = _dot(uc_ref[...], bm)
            fin_r, fin_i = _scan(bufc, steps_c, ar, ai, zero, zero, rev, False)

            def project(r, _):
                rs = pl.ds(pl.multiple_of(r * rc, rc), rc)
                buf[rs, :] = _dot(u_ref[rs, :], bm)
                return 0

            lax.fori_loop(0, rows // rc, project, 0)
            _scan(buf, steps, ar, ai, fin_r, fin_i, rev, True)

            def readout(r, _):
                rs = pl.ds(pl.multiple_of(r * rc, rc), rc)
                yv = _dot(buf[rs, :], cm)
                if dr == 0:
                    y_ref[rs, :] = u_ref[rs, :].astype(F32) * d_ref[...] + yv
                else:
                    y_ref[rs, :] += yv
                return 0

            lax.fori_loop(0, rows // rc, readout, 0)

        @pl.when(pl.program_id(0) == nj - 1)
        def _():
            for cp in ride_copies(ride, ride, send_sems, recv_sems):
                cp.wait()

    res = pl.pallas_call(
        body, name="s5_forward", grid=(nj,),
        in_specs=[
            pl.BlockSpec((rows, S5_CB), lambda j: (0, j)),
            pl.BlockSpec((rows_c, S5_CB), lambda j: (0, j)),
            pl.BlockSpec((2, None, S5_CB, S5_NS), lambda j: (0, j, 0, 0)),
            pl.BlockSpec((2, None, S5_NS, S5_CB), lambda j: (0, j, 0, 0)),
            pl.BlockSpec((2, None, SUB, S5_NS), lambda j: (0, j, 0, 0)),
            pl.BlockSpec((1, S5_CB), lambda j: (0, j)),
        ] + [ANY_SPEC] * n_ride,
        out_specs=[pl.BlockSpec((rows, S5_CB), lambda j: (0, j))] + [ANY_SPEC] * n_ride,
        out_shape=[jax.ShapeDtypeStruct((rows, d), F32)]
        + [jax.ShapeDtypeStruct(b.shape, b.dtype) for b in ride_bufs],
        input_output_aliases={6 + i: 1 + i for i in range(n_ride)},
        scratch_shapes=[pltpu.VMEM((rows, S5_NS), F32), pltpu.VMEM((rows_c, S5_NS), F32),
                        pltpu.SemaphoreType.DMA((n_sem,)), pltpu.SemaphoreType.DMA((n_sem,))],
        compiler_params=pltpu.CompilerParams(
            dimension_semantics=("arbitrary",), vmem_limit_bytes=VMEM_LIMIT, has_side_effects=True),
    )(u, uc, bblk, cblk, atile, dsk, *ride_bufs)
    return res[0], res[1:]


def _s5_backward(u, uc, dy, bblk, cblk, atile, dsk, ride_ins, ride_shapes, ride_copies):
    rows, d = u.shape
    rows_c = uc.shape[0]
    nj = d // S5_CB
    steps, steps_c = rows // N_SEG, rows_c // N_SEG
    rc = _s5_chunk(rows)
    nchunk = rows // rc
    n_ride = len(ride_ins)
    n_sem = 3 * n_ride

    def body(u_ref, uc_ref, dy_ref, b_ref, c_ref, a_ref, d_ref, *rest):
        ride_in = rest[:n_ride]
        du_ref, duc_ref, db_ref, dc_ref, da_ref, dd_ref = rest[n_ride:n_ride + 6]
        ride_out = rest[n_ride + 6:2 * n_ride + 6]
        hbuf, gbuf, hcbuf, gcbuf, send_sems, recv_sems = rest[2 * n_ride + 6:]

        @pl.when(pl.program_id(0) == 0)
        def _():
            for cp in ride_copies(ride_in, ride_out, send_sems, recv_sems):
                cp.start()

        zero = jnp.zeros((1, S5_H), F32)
        db_ref[...] = jnp.zeros_like(db_ref)
        dc_ref[...] = jnp.zeros_like(dc_ref)
        dd_ref[...] = jnp.zeros_like(dd_ref)
        for dr in (0, 1):
            rev = dr == 1
            ar, ai = a_ref[dr, :, 0:S5_H], a_ref[dr, :, S5_H:S5_NS]
            bm, cm = b_ref[dr].astype(MXU_DTYPE), c_ref[dr].astype(MXU_DTYPE)
            hcbuf[...] = _dot(uc_ref[...], bm)
            hin_r, hin_i = _scan(hcbuf, steps_c, ar, ai, zero, zero, rev, True)

            def project(r, _):
                rs = pl.ds(pl.multiple_of(r * rc, rc), rc)
                hbuf[rs, :] = _dot(u_ref[rs, :], bm)
                return 0

            lax.fori_loop(0, nchunk, project, 0)
            _scan(hbuf, steps, ar, ai, hin_r, hin_i, rev, True)

            def readout_bwd(r, _):
                rs = pl.ds(pl.multiple_of(r * rc, rc), rc)
                dyv = dy_ref[rs, :]
                gbuf[rs, :] = _dot(dyv, cm, NT)
                dc_ref[dr] += _dot(hbuf[rs, :], dyv, TN)
                return 0

            lax.fori_loop(0, nchunk, readout_bwd, 0)
            lf_r, lf_i, dar, dai = _adjoint_scan(gbuf, hbuf, steps, ar, ai, zero, zero, hin_r, hin_i, rev)
            gcbuf[...] = jnp.zeros_like(gcbuf)
            _, _, dar_c, dai_c = _adjoint_scan(gcbuf, hcbuf, steps_c, ar, ai, lf_r, lf_i, zero, zero, rev)
            da_ref[dr, :, 0:S5_H] = dar + dar_c
            da_ref[dr, :, S5_H:S5_NS] = dai + dai_c

            def project_bwd(r, _):
                rs = pl.ds(pl.multiple_of(r * rc, rc), rc)
                lam = gbuf[rs, :]
                uv = u_ref[rs, :]
                part = _dot(lam, bm, NT)
                db_ref[dr] += _dot(uv, lam, TN)
                if dr == 0:
                    dyv = dy_ref[rs, :].astype(F32)
                    du_ref[rs, :] = part + dyv * d_ref[...]
                    dd_ref[...] += (dyv * uv.astype(F32)).reshape(rc // SUB, SUB, S5_CB).sum(axis=0)
                else:
                    du_ref[rs, :] += part
                return 0

            lax.fori_loop(0, nchunk, project_bwd, 0)
            lam_c = gcbuf[...]
            part_c = _dot(lam_c, bm, NT)
            db_ref[dr] += _dot(uc_ref[...], lam_c, TN)
            if dr == 0:
                duc_ref[...] = part_c
            else:
                duc_ref[...] += part_c

        @pl.when(pl.program_id(0) == nj - 1)
        def _():
            for cp in ride_copies(ride_in, ride_out, send_sems, recv_sems):
                cp.wait()

    blk = lambda r: pl.BlockSpec((r, S5_CB), lambda j: (0, j))
    res = pl.pallas_call(
        body, name="s5_backward", grid=(nj,),
        in_specs=[
            blk(rows), blk(rows_c), blk(rows),
            pl.BlockSpec((2, None, S5_CB, S5_NS), lambda j: (0, j, 0, 0)),
            pl.BlockSpec((2, None, S5_NS, S5_CB), lambda j: (0, j, 0, 0)),
            pl.BlockSpec((2, None, SUB, S5_NS), lambda j: (0, j, 0, 0)),
            pl.BlockSpec((1, S5_CB), lambda j: (0, j)),
        ] + [ANY_SPEC] * n_ride,
        out_specs=[
            blk(rows), blk(rows_c),
            pl.BlockSpec((2, None, S5_CB, S5_NS), lambda j: (0, j, 0, 0)),
            pl.BlockSpec((2, None, S5_NS, S5_CB), lambda j: (0, j, 0, 0)),
            pl.BlockSpec((2, None, SUB, S5_NS), lambda j: (0, j, 0, 0)),
            pl.BlockSpec((SUB, S5_CB), lambda j: (0, j)),
        ] + [ANY_SPEC] * n_ride,
        out_shape=[
            jax.ShapeDtypeStruct((rows, d), F32), jax.ShapeDtypeStruct((rows_c, d), F32),
            jax.ShapeDtypeStruct(bblk.shape, F32), jax.ShapeDtypeStruct(cblk.shape, F32),
            jax.ShapeDtypeStruct(atile.shape, F32), jax.ShapeDtypeStruct((SUB, d), F32),
        ] + list(ride_shapes),
        scratch_shapes=[pltpu.VMEM((rows, S5_NS), F32), pltpu.VMEM((rows, S5_NS), F32),
                        pltpu.VMEM((rows_c, S5_NS), F32), pltpu.VMEM((rows_c, S5_NS), F32),
                        pltpu.SemaphoreType.DMA((n_sem,)), pltpu.SemaphoreType.DMA((n_sem,))],
        compiler_params=pltpu.CompilerParams(
            dimension_semantics=("arbitrary",), vmem_limit_bytes=VMEM_LIMIT, has_side_effects=True),
    )(u, uc, dy, bblk, cblk, atile, dsk, *ride_ins)
    return res[:6], res[6:]


def _s5_prepare(lam_re, lam_im, log_step, b_re, b_im, c_re, c_im):
    nd, g, p = lam_re.shape
    gb = S5_CB // S5_GROUP
    nj = g // gb
    dt = jnp.exp(log_step)[..., None]
    mag = jnp.exp(lam_re * dt)
    abar_re = mag * jnp.cos(lam_im * dt)
    abar_im = mag * jnp.sin(lam_im * dt)
    nr, ni = abar_re - 1.0, abar_im
    den = lam_re * lam_re + lam_im * lam_im
    fr = (nr * lam_re + ni * lam_im) / den
    fi = (ni * lam_re - nr * lam_im) / den
    bbar_re = fr[..., None] * b_re - fi[..., None] * b_im
    bbar_im = fr[..., None] * b_im + fi[..., None] * b_re
    eye = jnp.eye(gb, dtype=bool)

    def diag_in(w):
        w = w.reshape(nd, nj, gb, p, S5_GROUP).transpose(0, 1, 2, 4, 3)
        w = jnp.where(eye[None, None, :, None, :, None], w[:, :, :, :, None, :], 0.0)
        return w.reshape(nd, nj, gb * S5_GROUP, gb * p)

    def diag_out(w):
        w = w.reshape(nd, nj, gb, S5_GROUP, p).transpose(0, 1, 2, 4, 3)
        w = jnp.where(eye[None, None, :, None, :, None], w[:, :, :, :, None, :], 0.0)
        return w.reshape(nd, nj, gb * p, gb * S5_GROUP)

    bblk = jnp.concatenate([diag_in(bbar_re), diag_in(bbar_im)], axis=-1)
    cblk = jnp.concatenate([diag_out(c_re), -diag_out(c_im)], axis=-2)
    a2 = jnp.concatenate([abar_re.reshape(nd, nj, gb * p), abar_im.reshape(nd, nj, gb * p)], axis=-1)
    atile = jnp.broadcast_to(a2[:, :, None, :], (nd, nj, SUB, 2 * gb * p))
    return bblk, cblk, atile


def _shifted(x, prev_row, next_row):
    n = x.shape[0]
    row = lax.broadcasted_iota(jnp.int32, (SUB, x.shape[1]), 0)
    xp = pltpu.roll(x, 1, 0)
    xn = pltpu.roll(x, n - 1, 0)
    xp = jnp.concatenate([jnp.where(row == 0, prev_row, xp[:SUB]), xp[SUB:]], axis=0)
    xn = jnp.concatenate([xn[:n - SUB], jnp.where(row == SUB - 1, next_row, xn[n - SUB:])], axis=0)
    return xp, xn


def _edge_rows(ref, r0, n, total, group):
    lo = pl.multiple_of(jnp.maximum(r0 - group, 0), group)
    hi = pl.multiple_of(jnp.minimum(r0 + n, total - group), group)
    prev_row = ref[pl.ds(lo, group), :].astype(F32)[group - 1:group] * (r0 > 0).astype(F32)
    next_row = ref[pl.ds(hi, group), :].astype(F32)[0:1] * (r0 + n < total).astype(F32)
    return prev_row, next_row


def _conv_rows(ref, r0, n, total, w_ref, b_ref):
    win = n + 2 * PACK
    ws = pl.multiple_of(jnp.clip(r0 - PACK, 0, total - win), PACK)
    xw = ref[pl.ds(ws, win), :]
    i = r0 + lax.broadcasted_iota(jnp.int32, (n, win), 0)
    j = ws + lax.broadcasted_iota(jnp.int32, (n, win), 1)
    xp = _dot((j == i - 1).astype(MXU_DTYPE), xw)
    xn = _dot((j == i + 1).astype(MXU_DTYPE), xw)
    x = ref[pl.ds(r0, n), :].astype(F32)
    hc = w_ref[0:1, :] * xp + w_ref[1:2, :] * x + w_ref[2:3, :] * xn + b_ref[...]
    return hc, xp, x, xn


def _conv_specs(rows, f, tc):
    nt = f // tc
    val = lambda r: pl.BlockSpec((r, tc), lambda j: (0, j))
    gate = lambda r: pl.BlockSpec((r, tc), lambda j: (0, j + nt))
    return val, gate


def _conv_swiglu_fwd(name, h, cw, cb):
    rows, f2 = h.shape
    f = f2 // 2
    tc = _tile(f, 256)
    rc = _tile(rows, 256, PACK)
    val, gate = _conv_specs(rows, f, tc)

    def body(hv_ref, hg_ref, wv_ref, wg_ref, bv_ref, bg_ref, a_ref):
        def chunk(r, _):
            r0 = pl.multiple_of(r * rc, rc)
            hv = _conv_rows(hv_ref, r0, rc, rows, wv_ref, bv_ref)[0]
            hg = _conv_rows(hg_ref, r0, rc, rows, wg_ref, bg_ref)[0]
            a_ref[pl.ds(r0, rc), :] = (hg * _sigmoid(hg) * hv).astype(a_ref.dtype)
            return 0

        lax.fori_loop(0, rows // rc, chunk, 0)

    return pl.pallas_call(
        body, name=name, grid=(f // tc,),
        in_specs=[val(rows), gate(rows), val(3), gate(3), val(1), gate(1)],
        out_specs=val(rows), out_shape=jax.ShapeDtypeStruct((rows, f), ACT_DTYPE),
        compiler_params=_params(("parallel",)),
    )(h, h, cw, cw, cb, cb)


def _conv_swiglu_bwd(name, da, h, cw, cb):
    rows, f2 = h.shape
    f = f2 // 2
    tc = _tile(f, 256)
    rc = _tile(rows, 256, PACK)
    val, gate = _conv_specs(rows, f, tc)

    def body(da_ref, hv_ref, hg_ref, wv_ref, wg_ref, bv_ref, bg_ref,
             dhv_ref, dhg_ref, dwv_ref, dwg_ref, dbv_ref, dbg_ref, sv, sg):
        def first(r, carry):
            r0 = pl.multiple_of(r * rc, rc)
            rs = pl.ds(r0, rc)
            hv, vp, vx, vn = _conv_rows(hv_ref, r0, rc, rows, wv_ref, bv_ref)
            hg, gp, gx, gn = _conv_rows(hg_ref, r0, rc, rows, wg_ref, bg_ref)
            d = da_ref[rs, :].astype(F32)
            s = _sigmoid(hg)
            dv = d * (hg * s)
            dg = d * hv * (s * (1.0 + hg * (1.0 - s)))
            sv[rs, :] = dv
            sg[rs, :] = dg
            sums = [dv * vp, dv * vx, dv * vn, dv, dg * gp, dg * gx, dg * gn, dg]
            return tuple(c + jnp.sum(x, axis=0, keepdims=True) for c, x in zip(carry, sums))

        zero = jnp.zeros((1, tc), F32)
        acc = lax.fori_loop(0, rows // rc, first, (zero,) * 8)
        for k in range(3):
            dwv_ref[k:k + 1, :] = acc[k]
            dwg_ref[k:k + 1, :] = acc[4 + k]
        dbv_ref[...] = acc[3]
        dbg_ref[...] = acc[7]

        def second(r, _):
            r0 = pl.multiple_of(r * rc, rc)
            rs = pl.ds(r0, rc)
            for s_ref, w_ref, o_ref in ((sv, wv_ref, dhv_ref), (sg, wg_ref, dhg_ref)):
                x = s_ref[rs, :]
                xp, xn = _shifted(x, *_edge_rows(s_ref, r0, rc, rows, SUB))
                o_ref[rs, :] = (w_ref[0:1, :] * xn + w_ref[1:2, :] * x + w_ref[2:3, :] * xp).astype(o_ref.dtype)
            return 0

        lax.fori_loop(0, rows // rc, second, 0)

    res = pl.pallas_call(
        body, name=name, grid=(f // tc,),
        in_specs=[val(rows), val(rows), gate(rows), val(3), gate(3), val(1), gate(1)],
        out_specs=[val(rows), val(rows), val(3), val(3), val(1), val(1)],
        out_shape=[jax.ShapeDtypeStruct((rows, f), ACT_DTYPE)] * 2
        + [jax.ShapeDtypeStruct((3, f), F32)] * 2 + [jax.ShapeDtypeStruct((1, f), F32)] * 2,
        scratch_shapes=[pltpu.VMEM((rows, tc), F32), pltpu.VMEM((rows, tc), F32)],
        compiler_params=_params(("parallel",)),
    )(da, h, h, cw, cw, cb, cb)
    return res


def _pool_band(name, x, transpose, out_dtype):
    rows, d = x.shape
    ng = len(POOL_WINDOWS)
    ch = d // ng
    tm = _tile(rows, 256, PACK)
    win = tm + 2 * POOL_HALO
    assert win <= rows

    def body(x_ref, o_ref):
        half = lax.shift_left(jnp.int32(1), pl.program_id(0))
        t0 = pl.program_id(1) * tm
        ws = pl.multiple_of(jnp.clip(t0 - POOL_HALO, 0, rows - win), PACK)
        i = t0 + lax.broadcasted_iota(jnp.int32, (tm, win), 0)
        j = ws + lax.broadcasted_iota(jnp.int32, (tm, win), 1)

        def inv_count(t):
            hi = jnp.minimum(t + half - 1, rows - 1)
            lo = jnp.maximum(t - half, 0)
            return 1.0 / (hi - lo + 1).astype(F32)

        xw = x_ref[pl.ds(ws, win), :]
        xt = x_ref[pl.ds(pl.multiple_of(t0, PACK), tm), :].astype(F32)
        if transpose:
            band = (j - half <= i) & (i <= j + half - 1)
            tw = ws + lax.broadcasted_iota(jnp.int32, (win, 1), 0)
            o = _dot(band.astype(MXU_DTYPE), xw.astype(F32) * inv_count(tw)) - xt
        else:
            band = (i - half <= j) & (j <= i + half - 1)
            tt = t0 + lax.broadcasted_iota(jnp.int32, (tm, 1), 0)
            o = _dot(band.astype(MXU_DTYPE), xw) * inv_count(tt) - xt
        o_ref[...] = o.astype(o_ref.dtype)

    return pl.pallas_call(
        body, name=name, grid=(ng, rows // tm),
        in_specs=[pl.BlockSpec((rows, ch), lambda g, i: (0, g))],
        out_specs=pl.BlockSpec((tm, ch), lambda g, i: (i, g)),
        out_shape=jax.ShapeDtypeStruct((rows, d), out_dtype),
        compiler_params=_params(("parallel", "arbitrary")),
    )(x)


def _ada_forward(c16, ada_w, ada_b):
    nl, d, cols = ada_w.shape
    tn = _tile(cols, 512)

    def body(c_ref, w_ref, b_ref, o_ref):
        cv = c_ref[...]
        o_ref[...] = _dot(cv * _sigmoid(cv), w_ref[...]) + b_ref[...]

    return pl.pallas_call(
        body, name="ada_forward", grid=(nl, cols // tn),
        in_specs=[pl.BlockSpec((16, d), lambda l, n: (0, 0)),
                  pl.BlockSpec((None, d, tn), lambda l, n: (l, 0, n)),
                  pl.BlockSpec((None, 1, tn), lambda l, n: (l, 0, n))],
        out_specs=pl.BlockSpec((None, 16, tn), lambda l, n: (l, 0, n)),
        out_shape=jax.ShapeDtypeStruct((nl, 16, cols), F32),
        compiler_params=_params(("parallel", "parallel")),
    )(c16, ada_w, ada_b)


def _ada_backward(c16, dmod, ada_w):
    nl, d, cols = ada_w.shape
    tn = _tile(cols, 512)
    nn = cols // tn

    def body(c_ref, g_ref, w_ref, gw_ref, gc_ref):
        cv = c_ref[...]
        s = _sigmoid(cv)
        gv = g_ref[...]
        gw_ref[...] = _dot(cv * s, gv, TN)
        dcond = _dot(gv, w_ref[...], NT)
        row = lax.broadcasted_iota(jnp.int32, dcond.shape, 0)
        dctx = jnp.sum(jnp.where(row >= 8, dcond * (s * (1.0 + cv * (1.0 - s))), 0.0), axis=0, keepdims=True)

        @pl.when((pl.program_id(0) == 0) & (pl.program_id(1) == 0))
        def _():
            gc_ref[...] = jnp.zeros_like(gc_ref)

        gc_ref[...] += dctx

    return pl.pallas_call(
        body, name="ada_backward", grid=(nl, nn),
        in_specs=[pl.BlockSpec((16, d), lambda l, n: (0, 0)),
                  pl.BlockSpec((None, 16, tn), lambda l, n: (l, 0, n)),
                  pl.BlockSpec((None, d, tn), lambda l, n: (l, 0, n))],
        out_specs=[pl.BlockSpec((None, d, tn), lambda l, n: (l, 0, n)),
                   pl.BlockSpec((1, d), lambda l, n: (0, 0))],
        out_shape=[jax.ShapeDtypeStruct((nl, d, cols), F32), jax.ShapeDtypeStruct((1, d), F32)],
        compiler_params=_params(("arbitrary", "arbitrary")),
    )(c16, dmod, ada_w)


def _row_sum16(name, a):
    nl, _, w = a.shape
    tn = _tile(w, 4096)

    def body(a_ref, o_ref):
        o_ref[...] = jnp.sum(a_ref[...], axis=0, keepdims=True)

    return pl.pallas_call(
        body, name=name, grid=(nl, w // tn),
        in_specs=[pl.BlockSpec((None, 16, tn), lambda l, n: (l, 0, n))],
        out_specs=pl.BlockSpec((None, 1, tn), lambda l, n: (l, 0, n)),
        out_shape=jax.ShapeDtypeStruct((nl, 1, w), F32),
        compiler_params=_params(("parallel", "parallel")),
    )(a)


def _pack(arrays, row_align):
    flat = jnp.concatenate([a.reshape(-1).astype(F32) for a in arrays])
    quantum = row_align * LANE
    padded = -(-flat.shape[0] // quantum) * quantum
    return jnp.pad(flat, (0, padded - flat.shape[0])).reshape(-1, LANE)


def _unpack(packed, shapes):
    flat = packed.reshape(-1)
    out, off = [], 0
    for s in shapes:
        n = math.prod(s)
        out.append(flat[off:off + n].reshape(s))
        off += n
    return out


def _grid_pos_tables(n_tokens, dim):
    quarter = dim // 4
    omega = 1.0 / (POS_BASE ** (jnp.arange(quarter, dtype=F32) / quarter))

    def enc(n):
        ang = jnp.arange(n, dtype=F32).reshape(-1, 1) * omega[None, :]
        return jnp.concatenate([jnp.sin(ang), jnp.cos(ang)], axis=-1)

    return enc(n_tokens // GRID_W), enc(GRID_W)


def _ffn_forward(tag, v, x_in, up4, dn4, layer, cw, cb):
    h = _mm_cols(f"ffn_up_{tag}", v, up4, layer, ACT_DTYPE)
    a = _conv_swiglu_fwd(f"ffn_conv_{tag}", h, cw, cb)
    f = _mm_rows(f"ffn_down_{tag}", a, dn4, layer, F32)
    return h, a, f


def _ffn_backward(tag, dx_out, fb, x_mid, v, h, a, up4, dn4, layer, cw, cb, gate5, gam3, gam2, scale4p1):
    def post(dxo, f, gam, gate):
        dy, dgate, dgam = _postnorm_bwd(dxo, f.astype(F32), gam, gate)
        return (dy,), (dgate, dgam)

    (df,), (dgate5, dgam3) = _rowwise(f"ffn_post_bwd_{tag}", post, dx_out.shape[0],
                                      [(dx_out, False), (fb, False)], [gam3, gate5],
                                      [(dx_out.shape[1], ACT_DTYPE, False)], [dx_out.shape[1]] * 2)
    da = _mm_rows_nt(f"ffn_down_dx_{tag}", df, dn4, layer, ACT_DTYPE)
    g_dn = _mm_rows_tn(f"ffn_down_dw_{tag}", a, df, dn4.shape[-2], ACT_DTYPE)
    dhv, dhg, dwv, dwg, dbv, dbg = _conv_swiglu_bwd(f"ffn_conv_bwd_{tag}", da, h, cw, cb)
    dh = (dhv, dhg)
    dcw = jnp.concatenate([dwv, dwg], axis=1)
    dcb = jnp.concatenate([dbv, dbg], axis=1)
    dv = _mm_cols_nt(f"ffn_up_dx_{tag}", dh, up4, layer, F32)
    g_up = _mm_cols_tn(f"ffn_up_dw_{tag}", v, dh, up4.shape[-1], ACT_DTYPE)

    def pre(dvv, x, dxo, gam, s1):
        dx, dshift, dscale, dgam = _prenorm_bwd(dvv, x, gam, s1)
        return (dxo + dx,), (dshift, dscale, dgam)

    d = dx_out.shape[1]
    (dx_mid,), (dshift3, dscale4, dgam2) = _rowwise(
        f"ffn_pre_bwd_{tag}", pre, dx_out.shape[0], [(dv, False), (x_mid, False), (dx_out, False)],
        [gam2, scale4p1], [(d, F32, False)], [d] * 3)
    return dx_mid, g_up, g_dn, dcw, dcb, (dshift3, dscale4, dgate5), (dgam2, dgam3)


def kernel(x, c, ctx, c_ctx, ada_w, ada_b, norm_g, s5_lam_re, s5_lam_im, s5_log_step, s5_b_re, s5_b_im, s5_c_re, s5_c_im, s5_d, s5_glu_w, pool_w, pool_scale, ffn_up, ffn_conv, ffn_conv_b, ffn_down, loss_target, m_c_ctx, m_ada_w, m_ada_b, m_norm_g, m_s5_lam_re, m_s5_lam_im, m_s5_log_step, m_s5_b_re, m_s5_b_im, m_s5_c_re, m_s5_c_im, m_s5_d, m_s5_glu_w, m_pool_w, m_pool_scale, m_ffn_up, m_ffn_conv, m_ffn_conv_b, m_ffn_down, v_c_ctx, v_ada_w, v_ada_b, v_norm_g, v_s5_lam_re, v_s5_lam_im, v_s5_log_step, v_s5_b_re, v_s5_b_im, v_s5_c_re, v_s5_c_im, v_s5_d, v_s5_glu_w, v_pool_w, v_pool_scale, v_ffn_up, v_ffn_conv, v_ffn_conv_b, v_ffn_down):
    ix, iy, ic = _coords()
    chip = 2 * ix + iy
    me = 2 * chip + ic
    _, rows, d = x.shape
    rows_c = ctx.shape[1]
    nl = ada_w.shape[0]
    assert nl == 2 and s5_glu_w.shape[0] == 1 and pool_w.shape[0] == 1
    a_cols = ada_w.shape[2]
    f2s = ffn_up.shape[2]
    f2 = N_CHIP * f2s
    ds = d // N_CHIP
    ng = len(POOL_WINDOWS)
    ch = d // ng
    ps = pool_w.shape[2]

    core = jnp.reshape(ic, (1,)).astype(jnp.int32)
    chip_id = jnp.reshape(chip, (1,)).astype(jnp.int32)
    pos = jnp.concatenate([core, chip_id])
    shards = {"up": _view2d(ffn_up), "down": _view2d(ffn_down), "glu": s5_glu_w[0], "pool": _view2d(pool_w[0])}
    gather_bufs = [_cast_own_half(f"cast_{n}", w, pos) for n, w in shards.items()]

    c_all = _all_gather8("gather_cond", _pack([c], SUB))
    c_all = c_all.reshape(N_DEV, -1)[:, :d]
    c16 = jnp.concatenate([c_all, jnp.broadcast_to(c_ctx[None, :], (N_DEV, d))], axis=0)
    ada_b_mine = lax.dynamic_slice_in_dim(ada_b, chip * a_cols, a_cols, axis=1)
    mod_part = _ada_forward(c16, ada_w, ada_b_mine[:, None, :])
    narrow_shapes = [mod_part.shape, norm_g.shape, pool_scale.shape, ffn_conv.shape]
    narrow = _all_gather8("gather_narrow", _pack([mod_part, norm_g, pool_scale, ffn_conv], SUB))
    per_chip = [_unpack(narrow[2 * s], narrow_shapes) for s in range(N_CHIP)]
    mod_all = jnp.concatenate([p[0] for p in per_chip], axis=-1)
    gam = jnp.concatenate([p[1] for p in per_chip], axis=-1)
    pscale = jnp.concatenate([p[2] for p in per_chip], axis=-1)
    conv_w = jnp.concatenate([p[3] for p in per_chip], axis=-1)
    mod_mine = lax.dynamic_index_in_dim(mod_all, me, axis=1, keepdims=False)
    mod_ctx = mod_all[0, N_DEV]

    def mods(vec):
        s0, s1, g2, s3, s4, g5 = [vec[k * d:(k + 1) * d][None, :] for k in range(N_MOD)]
        return s0, 1.0 + s1, g2, s3, 1.0 + s4, g5

    m0, m1, mc = mods(mod_mine[0]), mods(mod_mine[1]), mods(mod_ctx)
    gains = [[gam[l, k][None, :] for k in range(4)] for l in range(nl)]
    conv_b = ffn_conv_b[:, None, :]

    row_tab, col_tab = _grid_pos_tables(rows, d)
    per_tile = _tile(rows // N_SEG, 256, SUB) // GRID_W
    rep = SUB // per_tile
    row_tab = jnp.repeat(row_tab, rep, axis=0)

    def init(xv, rt, ct, g0, shift, s1):
        pe_r = jnp.concatenate(
            [jnp.broadcast_to(rt[q * rep:q * rep + 1], (GRID_W, d // 2)) for q in range(per_tile)], axis=0)
        pe_c = jnp.concatenate([ct] * per_tile, axis=0)
        x0 = xv + jnp.concatenate([pe_r, pe_c], axis=1)
        return (x0, _prenorm(x0, g0, shift, s1)), ()

    (x0, u), _ = _rowwise("init", init, rows, [(x[0], False)],
                          [(row_tab, SUB), col_tab, gains[0][0], m0[0], m0[1]],
                          [(d, F32, False), (d, ACT_DTYPE, True)])
    (uc,), _ = _rowwise("ctx_prenorm", lambda cv, g0, shift, s1: ((_prenorm(cv, g0, shift, s1),), ()),
                        rows_c, [(ctx[0], False)], [gains[0][0], mc[0], mc[1]], [(d, ACT_DTYPE, True)])
    s5_params = (s5_lam_re[0], s5_lam_im[0], s5_log_step[0], s5_b_re[0], s5_b_im[0], s5_c_re[0], s5_c_im[0])
    (bblk, cblk, atile), s5_vjp = jax.vjp(_s5_prepare, *s5_params)
    buf_up, buf_dn, buf_glu, buf_pool = gather_bufs
    rows_of = {n: w.shape[0] for n, w in shards.items()}
    y, (buf_up, buf_glu) = _s5_forward(u, uc, bblk, cblk, atile, s5_d, [buf_up, buf_glu],
                                       _gather_ici_plan([rows_of["up"], rows_of["glu"]]))
    up4 = _swap_gathered("gather_up", buf_up, core, rows_of["up"]).reshape((N_CHIP,) + ffn_up.shape)
    glu4 = _swap_gathered("gather_glu", buf_glu, core, rows_of["glu"])
    (z,), _ = _rowwise("gelu", lambda yv: ((_gelu(yv),), ()), rows, [(y, True)], [], [(d, ACT_DTYPE, False)])
    zz = _mm_cols("glu_proj", z, glu4[:, None], 0, ACT_DTYPE)

    def glu_out(zzv, xv, gate2, g1, g2, shift3, s4):
        zf = zzv.astype(F32)
        o = zf[:, :d] * _sigmoid(zf[:, d:])
        x1 = xv + gate2 * (o * _rstd(o) * g1)
        return (x1, _prenorm(x1, g2, shift3, s4)), ()

    (x1, v0), _ = _rowwise("glu_resid", glu_out, rows, [(zz, False), (x0, False)],
                           [m0[2], gains[0][1], gains[0][2], m0[3], m0[4]], [(d, F32, False), (d, ACT_DTYPE, False)])
    h0, (buf_dn, buf_pool) = _mm_cols(
        "ffn_up_l0", v0, up4, 0, ACT_DTYPE,
        ride=([buf_dn, buf_pool], _gather_ici_plan([rows_of["down"], rows_of["pool"]])))
    dn4 = _swap_gathered("gather_down", buf_dn, core, rows_of["down"]).reshape((N_CHIP,) + ffn_down.shape)
    pool_full = _swap_gathered("gather_pool", buf_pool, core, rows_of["pool"])
    pool_full = pool_full.reshape(N_CHIP, ng, ps, ch).transpose(1, 0, 2, 3).reshape(ng, ch, ch)
    a0 = _conv_swiglu_fwd("ffn_conv_l0", h0, conv_w[0], conv_b[0])
    f0 = _mm_rows("ffn_down_l0", a0, dn4, 0, F32)

    def ffn_out(fv, xv, gate5, g3, g0n, shift0, s1):
        x2 = xv + gate5 * (fv * _rstd(fv) * g3)
        return (x2, _prenorm(x2, g0n, shift0, s1), fv), ()

    (x2, u1, fb0), _ = _rowwise("ffn_resid_l0", ffn_out, rows, [(f0, False), (x1, False)],
                                [m0[5], gains[0][3], gains[1][0], m1[0], m1[1]],
                                [(d, F32, False), (d, ACT_DTYPE, False), (d, ACT_DTYPE, False)])

    p1 = _pool_band("pool_band", u1, False, ACT_DTYPE)
    yr = _mm_grp("pool_proj", p1, pool_full, NN, F32)

    def pool_out(yv, xv, ps_, gate2, g1, g2, shift3, s4):
        o = yv * ps_
        x1n = xv + gate2 * (o * _rstd(o) * g1)
        return (x1n, _prenorm(x1n, g2, shift3, s4), yv), ()

    (x3, v1, yb), _ = _rowwise("pool_resid", pool_out, rows, [(yr, False), (x2, False)],
                               [pscale, m1[2], gains[1][1], gains[1][2], m1[3], m1[4]],
                               [(d, F32, False), (d, ACT_DTYPE, False), (d, ACT_DTYPE, False)])
    h1, a1, f1 = _ffn_forward("l1", v1, x3, up4, dn4, 1, conv_w[1], conv_b[1])

    def loss_head(fv, xv, tv, gate5, g3):
        err = xv + gate5 * (fv * _rstd(fv) * g3) - tv
        return (err * (1.0 / d), fv), (err * err,)

    (dx4, fb1), (sq,) = _rowwise("loss_head", loss_head, rows, [(f1, False), (x3, False), (loss_target[0], False)],
                                 [m1[5], gains[1][3]], [(d, F32, False), (d, ACT_DTYPE, False)], [d])
    loss = lax.psum(0.5 * jnp.sum(sq) / d, ("x", "y", "c"))

    dx3, g_up1, g_dn1, dcw1, dcb1, dmod_ffn1, (dgam12, dgam13) = _ffn_backward(
        "l1", dx4, fb1, x3, v1, h1, a1, up4, dn4, 1, conv_w[1], conv_b[1], m1[5], gains[1][3], gains[1][2], m1[4])

    def pool_post(dxo, yv, ps_, g1, gate2):
        yraw = yv.astype(F32)
        dy, dgate, dgam = _postnorm_bwd(dxo, yraw * ps_, g1, gate2)
        return (dy * ps_,), (dgate, dgam, dy * yraw)

    (dyr,), (dgate2_1, dgam11, dpscale) = _rowwise("pool_post_bwd", pool_post, rows, [(dx3, False), (yb, False)],
                                                   [pscale, gains[1][1], m1[2]], [(d, ACT_DTYPE, False)], [d] * 3)
    dp1 = _mm_grp("pool_proj_dx", dyr, pool_full, NT, ACT_DTYPE)
    g_pool = _mm_grp_tn("pool_proj_dw", p1, dyr, ng, ACT_DTYPE)
    du1 = _pool_band("pool_band_bwd", dp1, True, F32)

    def pre_bwd(duv, xv, dxo, g0, s1):
        dx, dshift, dscale, dgam = _prenorm_bwd(duv, xv, g0, s1)
        return (dxo + dx,), (dshift, dscale, dgam)

    (dx2,), (dshift0_1, dscale1_1, dgam10) = _rowwise(
        "pool_pre_bwd", pre_bwd, rows, [(du1, False), (x2, False), (dx3, False)],
        [gains[1][0], m1[1]], [(d, F32, False)], [d] * 3)

    dx1, g_up0, g_dn0, dcw0, dcb0, dmod_ffn0, (dgam02, dgam03) = _ffn_backward(
        "l0", dx2, fb0, x1, v0, h0, a0, up4, dn4, 0, conv_w[0], conv_b[0], m0[5], gains[0][3], gains[0][2], m0[4])

    def glu_post(dxo, zzv, g1, gate2):
        zf = zzv.astype(F32)
        val, s = zf[:, :d], _sigmoid(zf[:, d:])
        do, dgate, dgam = _postnorm_bwd(dxo, val * s, g1, gate2)
        return (jnp.concatenate([do * s, do * val * (s * (1.0 - s))], axis=1),), (dgate, dgam)

    (dzz,), (dgate2_0, dgam01) = _rowwise("glu_post_bwd", glu_post, rows, [(dx1, False), (zz, False)],
                                          [gains[0][1], m0[2]], [(2 * d, ACT_DTYPE, False)], [d] * 2)
    dz = _mm_cols_nt("glu_proj_dx", dzz, glu4[:, None], 0, F32)
    g_glu = _mm_cols_tn("glu_proj_dw", z, dzz, glu4.shape[-1], ACT_DTYPE)
    (dy,), _ = _rowwise("gelu_bwd", lambda dzv, yv: ((dzv * _gelu_grad(yv),), ()), rows,
                        [(dz, False), (y, True)], [], [(d, ACT_DTYPE, True)])
    g_pool4 = g_pool.reshape(ng, N_CHIP, ps, ch).transpose(1, 0, 2, 3).reshape(N_CHIP, ng * ps, ch)
    big = {"up0": g_up0, "up1": g_up1, "dn0": g_dn0, "dn1": g_dn1, "glu": g_glu, "pool": g_pool4}
    pairs = [_pair_sum(f"pair_{n}", g, core) for n, g in big.items()]
    ride_shapes, ride_copies = _chip_exchange_plan(pairs)
    (du0, duc, d_bblk, d_cblk, d_atile, d_dsk), others = _s5_backward(
        u, uc, dy, bblk, cblk, atile, s5_d, pairs, ride_shapes, ride_copies)
    (gx,), (dshift0_0, dscale1_0, dgam00) = _rowwise(
        "s5_pre_bwd", pre_bwd, rows, [(du0, True), (x0, False), (dx1, False)],
        [gains[0][0], m0[1]], [(d, F32, False)], [d] * 3)

    def ctx_bwd(duv, cv, g0, s1):
        _, dshift, dscale, dgam = _prenorm_bwd(duv, cv, g0, s1)
        return (), (dshift, dscale, dgam)

    _, (dshift_c, dscale_c, dgam00c) = _rowwise("ctx_pre_bwd", ctx_bwd, rows_c, [(duc, True), (ctx[0], False)],
                                                [gains[0][0], mc[1]], [], [d] * 3)
    g_s5 = s5_vjp((d_bblk, d_cblk, d_atile))

    zero_d = jnp.zeros((1, d), F32)
    dmod_lat = jnp.stack([
        jnp.concatenate([dshift0_0, dscale1_0, dgate2_0, *dmod_ffn0], axis=1),
        jnp.concatenate([dshift0_1, dscale1_1, dgate2_1, *dmod_ffn1], axis=1)])
    dmod_ctx = jnp.stack([jnp.concatenate([dshift_c, dscale_c] + [zero_d] * 4, axis=1),
                          jnp.zeros((1, N_MOD * d), F32)])
    dmod_shape = (nl, 2, N_MOD * d)
    dmod_all = _all_gather8("gather_dmod", _pack([jnp.concatenate([dmod_lat, dmod_ctx], axis=1)], SUB))
    dmod_all = jnp.stack([_unpack(dmod_all[k], [dmod_shape])[0] for k in range(N_DEV)])
    dmod16 = jnp.concatenate([dmod_all[:, :, 0], dmod_all[:, :, 1]], axis=0).transpose(1, 0, 2)
    dmod16_mine = lax.dynamic_slice_in_dim(dmod16, chip * a_cols, a_cols, axis=2)
    g_ada_w, g_cctx_part = _ada_backward(c16, dmod16_mine, ada_w)
    g_ada_b = _row_sum16("ada_bias_grad", dmod16)[:, 0]

    d_gam = jnp.stack([jnp.concatenate([dgam00 + dgam00c, dgam01, dgam02, dgam03], axis=0),
                       jnp.concatenate([dgam10, dgam11, dgam12, dgam13], axis=0)])
    small = [d_gam, 0.5 * g_cctx_part, *g_s5, jnp.sum(d_dsk, axis=0, keepdims=True), dpscale,
             jnp.stack([dcw0, dcw1]), jnp.stack([dcb0[0], dcb1[0]])]
    small_shapes = [s.shape for s in small]
    packed = _pack(small, N_DEV * SUB)
    summed = _all_reduce8("reduce_small", packed.reshape(N_DEV, -1, LANE)).reshape(-1, LANE)
    (r_gam, r_cctx, r_lam_re, r_lam_im, r_log_step, r_b_re, r_b_im, r_c_re, r_c_im,
     r_dsk, r_pscale, r_conv, r_convb) = _unpack(summed, small_shapes)
    g_norm = lax.dynamic_slice_in_dim(r_gam, chip * ds, ds, axis=2)
    g_pscale = lax.dynamic_slice_in_dim(r_pscale, chip * ds, ds, axis=1)
    g_conv = lax.dynamic_slice_in_dim(r_conv, chip * f2s, f2s, axis=2)

    red = {n: _chip_sum(f"chip_sum_{n}", p, o, chip_id) for n, p, o in zip(big, pairs, others)}
    d_rows, dn_rows = ffn_up.shape[1], ffn_down.shape[1]
    g_up = _swap_halves("swap_up0", red["up0"], core, None, nl * d_rows, 0)
    g_up = _swap_halves("swap_up1", red["up1"], core, g_up, nl * d_rows, d_rows).reshape(ffn_up.shape)
    g_dn = _swap_halves("swap_dn0", red["dn0"], core, None, nl * dn_rows, 0)
    g_dn = _swap_halves("swap_dn1", red["dn1"], core, g_dn, nl * dn_rows, dn_rows).reshape(ffn_down.shape)
    g_glu_f = _swap_halves("swap_glu", red["glu"], core, None, d, 0)
    g_pool_f = _swap_halves("swap_pool", red["pool"], core, None, ng * ps, 0)

    grads = {
        "c_ctx": r_cctx[0], "ada_w": g_ada_w, "ada_b": g_ada_b, "norm_g": g_norm,
        "s5_lam_re": r_lam_re[None], "s5_lam_im": r_lam_im[None], "s5_log_step": r_log_step[None],
        "s5_b_re": r_b_re[None], "s5_b_im": r_b_im[None], "s5_c_re": r_c_re[None], "s5_c_im": r_c_im[None],
        "s5_d": r_dsk, "s5_glu_w": g_glu_f[None], "pool_w": g_pool_f.reshape(pool_w.shape),
        "pool_scale": g_pscale, "ffn_up": g_up, "ffn_conv": g_conv, "ffn_conv_b": r_convb, "ffn_down": g_dn,
    }
    weights = {
        "c_ctx": (c_ctx, m_c_ctx, v_c_ctx), "ada_w": (ada_w, m_ada_w, v_ada_w), "ada_b": (ada_b, m_ada_b, v_ada_b),
        "norm_g": (norm_g, m_norm_g, v_norm_g), "s5_lam_re": (s5_lam_re, m_s5_lam_re, v_s5_lam_re),
        "s5_lam_im": (s5_lam_im, m_s5_lam_im, v_s5_lam_im), "s5_log_step": (s5_log_step, m_s5_log_step, v_s5_log_step),
        "s5_b_re": (s5_b_re, m_s5_b_re, v_s5_b_re), "s5_b_im": (s5_b_im, m_s5_b_im, v_s5_b_im),
        "s5_c_re": (s5_c_re, m_s5_c_re, v_s5_c_re), "s5_c_im": (s5_c_im, m_s5_c_im, v_s5_c_im),
        "s5_d": (s5_d, m_s5_d, v_s5_d), "s5_glu_w": (s5_glu_w, m_s5_glu_w, v_s5_glu_w),
        "pool_w": (pool_w, m_pool_w, v_pool_w), "pool_scale": (pool_scale, m_pool_scale, v_pool_scale),
        "ffn_up": (ffn_up, m_ffn_up, v_ffn_up), "ffn_conv": (ffn_conv, m_ffn_conv, v_ffn_conv),
        "ffn_conv_b": (ffn_conv_b, m_ffn_conv_b, v_ffn_conv_b), "ffn_down": (ffn_down, m_ffn_down, v_ffn_down),
    }
    names = list(weights)
    large = ("ada_w", "s5_glu_w", "pool_w", "ffn_up", "ffn_down")
    delta, new_m, new_v = {}, {}, {}
    for n in names:
        w, m, v = weights[n]
        if n in large:
            delta[n], new_m[n], new_v[n] = _adamw(f"adamw_{n}", w, grads[n], m, v)
        else:
            shape = w.shape
            view = (1, shape[0]) if w.ndim == 1 else shape
            res = _adamw(f"adamw_{n}", *[t.reshape(view) for t in (w, grads[n], m, v)])
            delta[n], new_m[n], new_v[n] = [t.reshape(shape) for t in res]

    return (loss, gx[None], *[grads[n] for n in names], *[delta[n] for n in names],
            *[new_m[n] for n in names], *[new_v[n] for n in names])
```

```python
import math

import jax
import jax.numpy as jnp
from jax import lax
from jax.experimental import pallas as pl
from jax.experimental.pallas import tpu as pltpu

F32 = jnp.float32
MXU_DTYPE = jnp.bfloat16
ACT_DTYPE = jnp.bfloat16

LANE = 128
SUB = 8
PACK = 16
VMEM_LIMIT = 56 * 1024 * 1024
ELEMWISE_BLOCK = 1 << 18

N_DEV = 8
N_CHIP = 4
N_SEG = SUB
S5_GROUP = 16
S5_STATE = 64
S5_CB = LANE
S5_H = (S5_CB // S5_GROUP) * S5_STATE
S5_NS = 2 * S5_H
POOL_WINDOWS = (2, 4, 8, 16)
POOL_HALO = 16
GRID_W = 64
POS_BASE = 10000.0
RMS_EPS = 1e-6
N_MOD = 6

ADAM_LR = 0.001
ADAM_B1 = 0.9
ADAM_B2 = 0.999
ADAM_EPS = 1e-08
ADAM_WD = 0.01
ADAM_STEP = 10

NN = (((1,), (0,)), ((), ()))
NT = (((1,), (1,)), ((), ()))
TN = (((0,), (0,)), ((), ()))
MESH = pl.DeviceIdType.MESH


def _tile(n, cap, align=LANE):
    best = None
    for t in range(align, min(n, cap) + 1, align):
        if n % t == 0:
            best = t
    return n if best is None else best


def _params(sem=None):
    return pltpu.CompilerParams(dimension_semantics=sem, vmem_limit_bytes=VMEM_LIMIT)


def _dot(a, b, dims=NN):
    return lax.dot_general(a.astype(MXU_DTYPE), b.astype(MXU_DTYPE), dims, preferred_element_type=F32)


def _sigmoid(x):
    return 0.5 * jnp.tanh(0.5 * x) + 0.5


_GELU_C = math.sqrt(2.0 / math.pi)
_GELU_K = 0.044715


def _gelu(x):
    return 0.5 * x * (1.0 + jnp.tanh(_GELU_C * (x + _GELU_K * x * x * x)))


def _gelu_grad(x):
    t = jnp.tanh(_GELU_C * (x + _GELU_K * x * x * x))
    return 0.5 * (1.0 + t) + 0.5 * x * (1.0 - t * t) * _GELU_C * (1.0 + 3.0 * _GELU_K * x * x)


def _rstd(x):
    return lax.rsqrt(jnp.mean(x * x, axis=-1, keepdims=True) + RMS_EPS)


def _norm_bwd(dxh, xh, r):
    return r * (dxh - xh * jnp.mean(dxh * xh, axis=-1, keepdims=True))


def _rowwise(name, fn, rows, tiled, vecs, outs, accs=()):
    seg = rows // N_SEG
    tm = _tile(seg, 256, SUB)
    nt, ntp = rows // tm, seg // tm
    n_t, n_v, n_o, n_a = len(tiled), len(vecs), len(outs), len(accs)

    def spec(width, perm):
        if perm:
            return pl.BlockSpec((tm, width), lambda i: (i % ntp, i // ntp))
        return pl.BlockSpec((tm, width), lambda i: (i, 0))

    args, in_specs = [], []
    for arr, perm in tiled:
        width = arr.shape[-1]
        args.append(arr.reshape(seg, N_SEG * width) if perm else arr)
        in_specs.append(spec(width, perm))
    for v in vecs:
        if isinstance(v, tuple):
            args.append(v[0])
            in_specs.append(pl.BlockSpec((v[1], v[0].shape[1]), lambda i: (i, 0)))
        else:
            args.append(v)
            in_specs.append(pl.BlockSpec(v.shape, lambda i: (0, 0)))
    out_shape, out_specs = [], []
    for width, dtype, perm in outs:
        out_shape.append(jax.ShapeDtypeStruct((seg, N_SEG * width) if perm else (rows, width), dtype))
        out_specs.append(spec(width, perm))
    for width in accs:
        out_shape.append(jax.ShapeDtypeStruct((SUB, width), F32))
        out_specs.append(pl.BlockSpec((SUB, width), lambda i: (0, 0)))

    def body(*refs):
        vals = [r[...] for r in refs[:n_t + n_v]]
        o_refs = refs[n_t + n_v:n_t + n_v + n_o]
        a_refs = refs[n_t + n_v + n_o:]
        o_vals, a_vals = fn(*vals)
        for r, v in zip(o_refs, o_vals):
            r[...] = v.astype(r.dtype)
        if n_a:
            @pl.when(pl.program_id(0) == 0)
            def _():
                for r in a_refs:
                    r[...] = jnp.zeros_like(r)
            for r, v in zip(a_refs, a_vals):
                r[...] += v.reshape(tm // SUB, SUB, v.shape[-1]).sum(axis=0)

    res = pl.pallas_call(
        body, name=name, grid=(nt,), in_specs=in_specs, out_specs=out_specs, out_shape=out_shape,
        compiler_params=_params(("arbitrary",)),
    )(*args)
    res = list(res)
    for k, (width, _, perm) in enumerate(outs):
        if perm:
            res[k] = res[k].reshape(rows, width)
    return res[:n_o], [jnp.sum(a, axis=0, keepdims=True) for a in res[n_o:]]


def _prenorm(x, gam, shift, scale1):
    r = _rstd(x)
    return (x * r) * gam * scale1 + shift


def _prenorm_bwd(du, x, gam, scale1):
    r = _rstd(x)
    xh = x * r
    dxn = du * scale1
    dx = _norm_bwd(dxn * gam, xh, r)
    return dx, du, du * (xh * gam), dxn * xh


def _postnorm_bwd(dxo, y, gam, gate):
    r = _rstd(y)
    yh = y * r
    dyn = dxo * gate
    dy = _norm_bwd(dyn * gam, yh, r)
    return dy, dxo * (yh * gam), dyn * yh


def _matmul(name, a, b, dims, grid, a_spec, b_spec, o_spec, out_shape, out_dtype, acc_shape,
            pair=None, ride=None):
    nk = grid[2]
    ins = list(a if pair and pair[0] == "a" else [a]) + list(b if pair and pair[0] == "b" else [b])
    specs = list(a_spec if pair and pair[0] == "a" else [a_spec]) + list(b_spec if pair and pair[0] == "b" else [b_spec])
    n_in = len(ins)
    bufs, copies = ride if ride else ((), None)
    n_ride = len(bufs)

    def body(*refs):
        in_refs = refs[:n_in]
        o_ref = refs[n_in + n_ride]
        ride_refs = refs[n_in + n_ride + 1:n_in + 2 * n_ride + 1]
        scratch = refs[n_in + 2 * n_ride + 1:]
        pid = [pl.program_id(ax) for ax in range(3)]
        if n_ride:
            send_sems, recv_sems = scratch[-2:]

            @pl.when((pid[0] == 0) & (pid[1] == 0) & (pid[2] == 0))
            def _():
                for cp in copies(ride_refs, ride_refs, send_sems, recv_sems):
                    cp.start()

        vals = [r[...] for r in in_refs]
        if pair:
            first = pid[pair[1]] < pair[2]
            picked = jnp.where(first, vals[0], vals[1]) if pair[0] == "a" else jnp.where(first, vals[1], vals[2])
            vals = [picked, vals[2]] if pair[0] == "a" else [vals[0], picked]
        part = _dot(vals[0], vals[1], dims)
        if nk == 1:
            o_ref[...] = part.astype(o_ref.dtype)
        else:
            acc_ref = scratch[0]

            @pl.when(pid[2] == 0)
            def _():
                acc_ref[...] = part

            @pl.when(pid[2] > 0)
            def _():
                acc_ref[...] += part

            @pl.when(pid[2] == nk - 1)
            def _():
                o_ref[...] = acc_ref[...].astype(o_ref.dtype)

        if n_ride:
            @pl.when((pid[0] == grid[0] - 1) & (pid[1] == grid[1] - 1) & (pid[2] == nk - 1))
            def _():
                for cp in copies(ride_refs, ride_refs, send_sems, recv_sems):
                    cp.wait()

    scratch_shapes = [] if nk == 1 else [pltpu.VMEM(acc_shape, F32)]
    out_shapes = [jax.ShapeDtypeStruct(out_shape, out_dtype)]
    if not n_ride:
        return pl.pallas_call(
            body, name=name, grid=grid, in_specs=specs, out_specs=[o_spec], out_shape=out_shapes,
            scratch_shapes=scratch_shapes, compiler_params=_params(("parallel", "parallel", "arbitrary")),
        )(*ins)[0]
    n_sem = 3 * n_ride
    res = pl.pallas_call(
        body, name=name, grid=grid, in_specs=specs + [ANY_SPEC] * n_ride,
        out_specs=[o_spec] + [ANY_SPEC] * n_ride,
        out_shape=out_shapes + [jax.ShapeDtypeStruct(x.shape, x.dtype) for x in bufs],
        input_output_aliases={n_in + i: 1 + i for i in range(n_ride)},
        scratch_shapes=scratch_shapes + [pltpu.SemaphoreType.DMA((n_sem,)), pltpu.SemaphoreType.DMA((n_sem,))],
        compiler_params=pltpu.CompilerParams(
            dimension_semantics=("arbitrary",) * 3, vmem_limit_bytes=VMEM_LIMIT, has_side_effects=True),
    )(*ins, *bufs)
    return res[0], res[1:]


def _mm_cols(name, a, w4, layer, out_dtype, ride=None):
    m, k = a.shape
    ns = w4.shape[-1]
    tm, tn = _tile(m, 1024), _tile(ns, 1536)
    nps = ns // tn
    return _matmul(
        name, a, w4, NN, (m // tm, N_CHIP * nps, 1),
        pl.BlockSpec((tm, k), lambda i, n, kk: (i, 0)),
        pl.BlockSpec((None, None, k, tn), lambda i, n, kk: (n // nps, layer, 0, n % nps)),
        pl.BlockSpec((tm, tn), lambda i, n, kk: (i, n)),
        (m, N_CHIP * ns), out_dtype, None, ride=ride)


def _halves_specs(g, rows_blk, cols_blk, tiles_half, row_of, col_of):
    if not isinstance(g, tuple):
        return g, pl.BlockSpec((rows_blk, cols_blk), lambda *p: (row_of(p), col_of(p)))
    left = pl.BlockSpec((rows_blk, cols_blk), lambda *p: (row_of(p), jnp.minimum(col_of(p), tiles_half - 1)))
    right = pl.BlockSpec((rows_blk, cols_blk), lambda *p: (row_of(p), jnp.maximum(col_of(p) - tiles_half, 0)))
    return g, (left, right)


def _mm_cols_nt(name, g, w4, layer, out_dtype):
    m = (g[0] if isinstance(g, tuple) else g).shape[0]
    k, ns = w4.shape[-2:]
    tm, tk = _tile(m, 1024), _tile(ns, 1536)
    kps = ns // tk
    half = N_CHIP * kps // 2
    g, g_spec = _halves_specs(g, tm, tk, half, lambda p: p[0], lambda p: p[2])
    return _matmul(
        name, g, w4, NT, (m // tm, 1, N_CHIP * kps), g_spec,
        pl.BlockSpec((None, None, k, tk), lambda i, n, kk: (kk // kps, layer, 0, kk % kps)),
        pl.BlockSpec((tm, k), lambda i, n, kk: (i, 0)),
        (m, k), out_dtype, (tm, k), pair=("a", 2, half) if isinstance(g, tuple) else None)


def _mm_cols_tn(name, a, g, ns, out_dtype):
    m, k = a.shape
    tkm, tmk, tn = _tile(m, 2048), _tile(k, 1024), _tile(ns, 1536)
    nps = ns // tn
    half = N_CHIP * nps // 2
    g, g_spec = _halves_specs(g, tkm, tn, half, lambda p: p[2], lambda p: p[1])
    return _matmul(
        name, a, g, TN, (k // tmk, N_CHIP * nps, m // tkm),
        pl.BlockSpec((tkm, tmk), lambda i, n, kk: (kk, i)), g_spec,
        pl.BlockSpec((None, tmk, tn), lambda i, n, kk: (n // nps, i, n % nps)),
        (N_CHIP, k, ns), out_dtype, (tmk, tn), pair=("b", 1, half) if isinstance(g, tuple) else None)


def _mm_rows(name, a, w4, layer, out_dtype):
    m = a.shape[0]
    rs, n = w4.shape[-2:]
    tm, tk = _tile(m, 1024), _tile(rs, 1536)
    kps = rs // tk
    return _matmul(
        name, a, w4, NN, (m // tm, 1, N_CHIP * kps),
        pl.BlockSpec((tm, tk), lambda i, j, kk: (i, kk)),
        pl.BlockSpec((None, None, tk, n), lambda i, j, kk: (kk // kps, layer, kk % kps, 0)),
        pl.BlockSpec((tm, n), lambda i, j, kk: (i, 0)),
        (m, n), out_dtype, (tm, n))


def _mm_rows_nt(name, g, w4, layer, out_dtype):
    m, n = g.shape
    rs = w4.shape[-2]
    tm, tn = _tile(m, 1024), _tile(rs, 1536)
    nps = rs // tn
    return _matmul(
        name, g, w4, NT, (m // tm, N_CHIP * nps, 1),
        pl.BlockSpec((tm, n), lambda i, j, kk: (i, 0)),
        pl.BlockSpec((None, None, tn, n), lambda i, j, kk: (j // nps, layer, j % nps, 0)),
        pl.BlockSpec((tm, tn), lambda i, j, kk: (i, j)),
        (m, N_CHIP * rs), out_dtype, None)


def _mm_rows_tn(name, a, g, rs, out_dtype):
    m = a.shape[0]
    n = g.shape[1]
    tkm, tmr, tn = _tile(m, 2048), _tile(rs, 1536), _tile(n, 1024)
    mps = rs // tmr
    return _matmul(
        name, a, g, TN, (N_CHIP * mps, n // tn, m // tkm),
        pl.BlockSpec((tkm, tmr), lambda i, j, kk: (kk, i)),
        pl.BlockSpec((tkm, tn), lambda i, j, kk: (kk, j)),
        pl.BlockSpec((None, tmr, tn), lambda i, j, kk: (i // mps, i % mps, j)),
        (N_CHIP, rs, n), out_dtype, (tmr, tn))


def _mm_grp(name, a, w, dims, out_dtype):
    m = a.shape[0]
    ng, ch = w.shape[:2]
    tm = _tile(m, 1024)
    return _matmul(
        name, a, w, dims, (m // tm, ng, 1),
        pl.BlockSpec((tm, ch), lambda i, g, kk: (i, g)),
        pl.BlockSpec((None, ch, ch), lambda i, g, kk: (g, 0, 0)),
        pl.BlockSpec((tm, ch), lambda i, g, kk: (i, g)),
        (m, ng * ch), out_dtype, None)


def _mm_grp_tn(name, a, g, ng, out_dtype):
    m = a.shape[0]
    ch = a.shape[1] // ng
    tk = _tile(m, 2048)
    return _matmul(
        name, a, g, TN, (ng, 1, m // tk),
        pl.BlockSpec((tk, ch), lambda i, j, kk: (kk, i)),
        pl.BlockSpec((tk, ch), lambda i, j, kk: (kk, i)),
        pl.BlockSpec((None, ch, ch), lambda i, j, kk: (i, 0, 0)),
        (ng, ch, ch), out_dtype, (ch, ch))


def _view2d(a):
    return a.reshape(-1, a.shape[-1])


def _elementwise(name, fn, ins, out_dtypes):
    r, c = ins[0].shape
    lanes = -(-c // LANE) * LANE
    tr = _tile(r, max(PACK, ELEMWISE_BLOCK // lanes), PACK)
    spec = pl.BlockSpec((tr, c), lambda i: (i, 0))

    def body(*refs):
        outs = fn(*[x[...] for x in refs[:len(ins)]])
        for o_ref, o in zip(refs[len(ins):], outs):
            o_ref[...] = o.astype(o_ref.dtype)

    return pl.pallas_call(
        body, name=name, grid=(r // tr,), in_specs=[spec] * len(ins), out_specs=[spec] * len(out_dtypes),
        out_shape=[jax.ShapeDtypeStruct((r, c), d) for d in out_dtypes],
        compiler_params=_params(("parallel",)),
    )(*ins)


def _adamw_math(w, g, m, v):
    m = ADAM_B1 * m + (1.0 - ADAM_B1) * g
    v = ADAM_B2 * v + (1.0 - ADAM_B2) * (g * g)
    m_hat = m / (1.0 - ADAM_B1 ** ADAM_STEP)
    v_hat = v / (1.0 - ADAM_B2 ** ADAM_STEP)
    delta = -ADAM_LR * (m_hat / (jnp.sqrt(v_hat) + ADAM_EPS) + ADAM_WD * w)
    return delta, m, v


def _adamw(name, w, g, m, v):
    shape = w.shape
    outs = _elementwise(name, _adamw_math, [_view2d(w), _view2d(g), _view2d(m), _view2d(v)], [F32, F32, F32])
    return [o.reshape(shape) for o in outs]


def _coords():
    return lax.axis_index("x"), lax.axis_index("y"), lax.axis_index("c")


def _peer(x, y, c, k):
    return (x ^ (k >> 2), y ^ ((k >> 1) & 1), c ^ (k & 1))


def _all_gather8(name, block):
    r = block.shape[0]

    def body(x_ref, out_ref, send_sems, recv_sems):
        x, y, c = _coords()
        me = 4 * x + 2 * y + c
        out_ref[me] = x_ref[...]
        copies = []
        for k in range(1, N_DEV):
            cp = pltpu.make_async_remote_copy(
                src_ref=x_ref, dst_ref=out_ref.at[me], send_sem=send_sems.at[k], recv_sem=recv_sems.at[k],
                device_id=_peer(x, y, c, k), device_id_type=MESH)
            cp.start()
            copies.append(cp)
        for cp in copies:
            cp.wait()

    return pl.pallas_call(
        body, name=name,
        in_specs=[pl.BlockSpec(memory_space=pltpu.VMEM)], out_specs=pl.BlockSpec(memory_space=pltpu.VMEM),
        out_shape=jax.ShapeDtypeStruct((N_DEV, r, LANE), F32),
        scratch_shapes=[pltpu.SemaphoreType.DMA((N_DEV,)), pltpu.SemaphoreType.DMA((N_DEV,))],
        compiler_params=pltpu.CompilerParams(vmem_limit_bytes=VMEM_LIMIT),
    )(block)


def _all_reduce8(name, parts):
    r = parts.shape[1]

    def body(p_ref, out_ref, rbuf, send1, recv1, send2, recv2):
        x, y, c = _coords()
        me = 4 * x + 2 * y + c
        first = []
        for k in range(1, N_DEV):
            px, py, pc = _peer(x, y, c, k)
            cp = pltpu.make_async_remote_copy(
                src_ref=p_ref.at[4 * px + 2 * py + pc], dst_ref=rbuf.at[me], send_sem=send1.at[k],
                recv_sem=recv1.at[k], device_id=(px, py, pc), device_id_type=MESH)
            cp.start()
            first.append(cp)
        rbuf[me] = p_ref[me]
        for cp in first:
            cp.wait()
        acc = rbuf[0]
        for d in range(1, N_DEV):
            acc = acc + rbuf[d]
        out_ref[me] = acc
        second = []
        for k in range(1, N_DEV):
            cp = pltpu.make_async_remote_copy(
                src_ref=out_ref.at[me], dst_ref=out_ref.at[me], send_sem=send2.at[k], recv_sem=recv2.at[k],
                device_id=_peer(x, y, c, k), device_id_type=MESH)
            cp.start()
            second.append(cp)
        for cp in second:
            cp.wait()

    return pl.pallas_call(
        body, name=name,
        in_specs=[pl.BlockSpec(memory_space=pltpu.VMEM)], out_specs=pl.BlockSpec(memory_space=pltpu.VMEM),
        out_shape=jax.ShapeDtypeStruct((N_DEV, r, LANE), F32),
        scratch_shapes=[pltpu.VMEM((N_DEV, r, LANE), F32)] + [pltpu.SemaphoreType.DMA((N_DEV,))] * 4,
        compiler_params=pltpu.CompilerParams(vmem_limit_bytes=VMEM_LIMIT),
    )(parts)


ANY_SPEC = pl.BlockSpec(memory_space=pl.ANY)
COMM_BLOCK_BYTES = 4 << 20


def _staged_call(name, body, core, grid, in_specs, ins, out_shape, scratch, aliases=None):
    return pl.pallas_call(
        body, name=name,
        grid_spec=pltpu.PrefetchScalarGridSpec(
            num_scalar_prefetch=1, grid=grid, in_specs=in_specs, out_specs=ANY_SPEC, scratch_shapes=scratch),
        out_shape=out_shape, input_output_aliases=aliases or {},
        compiler_params=pltpu.CompilerParams(
            dimension_semantics=("arbitrary",) * len(grid), vmem_limit_bytes=VMEM_LIMIT, has_side_effects=True),
    )(core, *ins)


def _rows_tile(rows, cols, itemsize):
    return _tile(rows, max(PACK, COMM_BLOCK_BYTES // (cols * itemsize)), PACK)


def _chip_peer(x, y, c, k):
    return (x ^ (k >> 1), y ^ (k & 1), c)


def _cast_own_half(name, w, pos):
    r, cols = w.shape
    h = r // 2
    tr = _tile(h, max(PACK, ELEMWISE_BLOCK // cols), PACK)
    nb = h // tr

    def body(p_ref, w_ref, o_ref):
        o_ref[...] = w_ref[...].astype(o_ref.dtype)

    return pl.pallas_call(
        body, name=name,
        grid_spec=pltpu.PrefetchScalarGridSpec(
            num_scalar_prefetch=1, grid=(nb,),
            in_specs=[pl.BlockSpec((tr, cols), lambda j, p: (p[0] * nb + j, 0))],
            out_specs=pl.BlockSpec((tr, cols), lambda j, p: ((2 * p[1] + p[0]) * nb + j, 0))),
        out_shape=jax.ShapeDtypeStruct((N_CHIP * r, cols), MXU_DTYPE),
        compiler_params=_params(("parallel",)),
    )(pos, w)


def _gather_ici_plan(shard_rows):
    def copies(refs, _, send_sems, recv_sems):
        x, y, c = _coords()
        out = []
        for i, (ref, r) in enumerate(zip(refs, shard_rows)):
            h = r // 2
            mine = ref.at[pl.ds(pl.multiple_of((2 * x + y) * r + c * h, PACK), h)]
            for k in (1, 2, 3):
                n = 3 * i + k - 1
                out.append(pltpu.make_async_remote_copy(
                    src_ref=mine, dst_ref=mine, send_sem=send_sems.at[n], recv_sem=recv_sems.at[n],
                    device_id=_chip_peer(x, y, c, k), device_id_type=MESH))
        return out

    return copies


def _swap_gathered(name, part, core, r):
    cols = part.shape[1]
    h = r // 2
    full = jax.ShapeDtypeStruct(part.shape, part.dtype)
    tr2 = _rows_tile(h, cols, 2)
    nb2 = h // tr2

    def swap_body(c_ref, mine_ref, full_ref, send_sem, recv_sem):
        s, j = pl.program_id(0), pl.program_id(1)
        x, y, c = _coords()
        dst = full_ref.at[pl.ds(pl.multiple_of(s * r + c * h + j * tr2, PACK), tr2)]
        cp = pltpu.make_async_remote_copy(
            src_ref=mine_ref, dst_ref=dst, send_sem=send_sem, recv_sem=recv_sem,
            device_id=(x, y, 1 - c), device_id_type=MESH)
        cp.start()
        cp.wait_send()

        @pl.when((s == N_CHIP - 1) & (j == nb2 - 1))
        def _():
            landed = full_ref.at[pl.ds(0, N_CHIP * h)]
            pltpu.make_async_remote_copy(
                src_ref=landed, dst_ref=landed, send_sem=send_sem, recv_sem=recv_sem,
                device_id=(x, y, 1 - c), device_id_type=MESH).wait_recv()

    return _staged_call(
        name + "_d2d", swap_body, core, (N_CHIP, nb2),
        [pl.BlockSpec((tr2, cols), lambda s, j, c_ref: ((2 * s + c_ref[0]) * nb2 + j, 0))], [part], full,
        [pltpu.SemaphoreType.DMA(()), pltpu.SemaphoreType.DMA(())], aliases={1: 0}).reshape(N_CHIP, r, cols)


def _pair_sum(name, g, core):
    n, r, cols = g.shape
    h = r // 2
    tr = _rows_tile(h, cols, g.dtype.itemsize)
    nb = h // tr
    half = jax.ShapeDtypeStruct((n, h, cols), g.dtype)

    def send_body(c_ref, g_ref, got_ref, send_sem, recv_sem):
        s, j = pl.program_id(0), pl.program_id(1)
        x, y, c = _coords()
        dst = got_ref.at[pl.ds(pl.multiple_of(s * h + j * tr, PACK), tr)]
        cp = pltpu.make_async_remote_copy(
            src_ref=g_ref, dst_ref=dst, send_sem=send_sem, recv_sem=recv_sem,
            device_id=(x, y, 1 - c), device_id_type=MESH)
        cp.start()
        cp.wait_send()

        @pl.when((s == n - 1) & (j == nb - 1))
        def _():
            pltpu.make_async_remote_copy(
                src_ref=got_ref, dst_ref=got_ref, send_sem=send_sem, recv_sem=recv_sem,
                device_id=(x, y, 1 - c), device_id_type=MESH).wait_recv()

    got = _staged_call(
        name + "_send", send_body, core, (n, nb),
        [pl.BlockSpec((tr, cols), lambda s, j, c_ref: ((2 * s + 1 - c_ref[0]) * nb + j, 0))],
        [g.reshape(n * r, cols)], jax.ShapeDtypeStruct((n * h, cols), g.dtype),
        [pltpu.SemaphoreType.DMA(()), pltpu.SemaphoreType.DMA(())]).reshape(n, h, cols)

    def add_body(c_ref, own_ref, got_ref, o_ref):
        o_ref[...] = (own_ref[...].astype(F32) + got_ref[...].astype(F32)).astype(o_ref.dtype)

    blk = pl.BlockSpec((None, tr, cols), lambda s, j, c_ref: (s, j, 0))
    return pl.pallas_call(
        add_body, name=name + "_add",
        grid_spec=pltpu.PrefetchScalarGridSpec(
            num_scalar_prefetch=1, grid=(n, nb),
            in_specs=[pl.BlockSpec((None, tr, cols), lambda s, j, c_ref: (s, c_ref[0] * nb + j, 0)), blk],
            out_specs=blk),
        out_shape=half, compiler_params=_params(("parallel", "parallel")),
    )(core, g, got)


def _chip_exchange_plan(pairs):
    shapes = [jax.ShapeDtypeStruct((N_CHIP - 1,) + p.shape[1:], p.dtype) for p in pairs]

    def copies(in_refs, out_refs, send_sems, recv_sems):
        x, y, c = _coords()
        out = []
        for i, (src, dst) in enumerate(zip(in_refs, out_refs)):
            for k in (1, 2, 3):
                px, py, pc = _chip_peer(x, y, c, k)
                n = 3 * i + k - 1
                out.append(pltpu.make_async_remote_copy(
                    src_ref=src.at[2 * px + py], dst_ref=dst.at[k - 1], send_sem=send_sems.at[n],
                    recv_sem=recv_sems.at[n], device_id=(px, py, pc), device_id_type=MESH))
        return out

    return shapes, copies


def _sum_and_swap(name, pair, got, pos, into, total_rows, base):
    _, h, cols = pair.shape
    tr = _rows_tile(h, cols, 4)
    nb = h // tr

    def body(p_ref, own_ref, a_ref, b_ref, c_ref, *rest):
        full_ref, red_ref, send_sem, recv_sem, local_sem = rest[-5:]
        j = pl.program_id(0)
        x, y, c = _coords()
        red_ref[...] = ((own_ref[...].astype(F32) + a_ref[...].astype(F32)) + b_ref[...].astype(F32)) + c_ref[...].astype(F32)
        dst = full_ref.at[pl.ds(pl.multiple_of(base + c * h + j * tr, SUB), tr)]
        local = pltpu.make_async_copy(red_ref, dst, local_sem)
        remote = pltpu.make_async_remote_copy(
            src_ref=red_ref, dst_ref=dst, send_sem=send_sem, recv_sem=recv_sem,
            device_id=(x, y, 1 - c), device_id_type=MESH)
        local.start()
        remote.start()
        remote.wait_send()
        local.wait()

        @pl.when(j == nb - 1)
        def _():
            landed = full_ref.at[pl.ds(0, h)]
            pltpu.make_async_remote_copy(
                src_ref=landed, dst_ref=landed, send_sem=send_sem, recv_sem=recv_sem,
                device_id=(x, y, 1 - c), device_id_type=MESH).wait_recv()

    in_specs = [pl.BlockSpec((None, tr, cols), lambda j, p: (p[1], j, 0))]
    in_specs += [pl.BlockSpec((None, tr, cols), (lambda j, p, k=k: (k, j, 0))) for k in range(N_CHIP - 1)]
    ins = [pair, got, got, got]
    aliases = None
    if into is not None:
        in_specs.append(ANY_SPEC)
        ins.append(into)
        aliases = {5: 0}
    return _staged_call(
        name, body, pos, (nb,), in_specs, ins, jax.ShapeDtypeStruct((total_rows, cols), F32),
        [pltpu.VMEM((tr, cols), F32)] + [pltpu.SemaphoreType.DMA(())] * 3, aliases)


def _cmul(ar, ai, br, bi):
    return ar * br - ai * bi, ar * bi + ai * br


def _cpow(ar, ai, n):
    rr, ri = jnp.ones_like(ar), jnp.zeros_like(ai)
    br, bi = ar, ai
    while n:
        if n & 1:
            rr, ri = _cmul(rr, ri, br, bi)
        br, bi = _cmul(br, bi, br, bi)
        n >>= 1
    return rr, ri


def _tile_rows(t):
    if isinstance(t, int):
        return pl.ds(t * SUB, SUB)
    return pl.ds(pl.multiple_of(t * SUB, SUB), SUB)


SCAN_UNROLL = 4


def _unrolled_loop(n, body, carry):
    trips = n // SCAN_UNROLL

    def trip(o, c):
        for k in range(SCAN_UNROLL):
            c = body(o * SCAN_UNROLL + k, c)
        return c

    carry = lax.fori_loop(0, trips, trip, carry)
    for i in range(trips * SCAN_UNROLL, n):
        carry = body(i, carry)
    return carry


def _scan_setup(buf, steps, ar, ai, h0r, h0i, rev):
    def total(i, carry):
        sr, si = carry
        rows = _tile_rows(steps - 1 - i if rev else i)
        pr, pi = _cmul(ar, ai, sr, si)
        return pr + buf[rows, 0:S5_H], pi + buf[rows, S5_H:S5_NS]

    zero = jnp.zeros((SUB, S5_H), F32)
    tot_r, tot_i = _unrolled_loop(steps, total, (zero, zero))
    pw_r, pw_i = _cpow(ar[0:1], ai[0:1], steps)
    row = lax.broadcasted_iota(jnp.int32, (SUB, S5_H), 0)
    cur_r, cur_i = h0r, h0i
    init_r, init_i = zero, zero
    for s in (range(N_SEG - 1, -1, -1) if rev else range(N_SEG)):
        init_r = jnp.where(row == s, cur_r, init_r)
        init_i = jnp.where(row == s, cur_i, init_i)
        nr, ni = _cmul(pw_r, pw_i, cur_r, cur_i)
        cur_r, cur_i = nr + tot_r[s:s + 1], ni + tot_i[s:s + 1]
    return init_r, init_i, cur_r, cur_i


def _scan(buf, steps, ar, ai, h0r, h0i, rev, store):
    init_r, init_i, fin_r, fin_i = _scan_setup(buf, steps, ar, ai, h0r, h0i, rev)
    if store:
        def step(i, carry):
            hr, hi = carry
            rows = _tile_rows(steps - 1 - i if rev else i)
            pr, pi = _cmul(ar, ai, hr, hi)
            hr, hi = pr + buf[rows, 0:S5_H], pi + buf[rows, S5_H:S5_NS]
            buf[rows, 0:S5_H] = hr
            buf[rows, S5_H:S5_NS] = hi
            return hr, hi

        _unrolled_loop(steps, step, (init_r, init_i))
    return fin_r, fin_i


def _adjoint_scan(gbuf, hbuf, steps, ar, ai, l0r, l0i, hin_r, hin_i, rev):
    ci = -ai
    arev = not rev
    init_r, init_i, fin_r, fin_i = _scan_setup(gbuf, steps, ar, ci, l0r, l0i, arev)
    zero = jnp.zeros((SUB, S5_H), F32)

    def update(t, hp_r, hp_i, carry):
        lr, li, dr, di = carry
        rows = _tile_rows(t)
        pr, pi = _cmul(ar, ci, lr, li)
        lr, li = pr + gbuf[rows, 0:S5_H], pi + gbuf[rows, S5_H:S5_NS]
        gbuf[rows, 0:S5_H] = lr
        gbuf[rows, S5_H:S5_NS] = li
        return lr, li, dr + lr * hp_r + li * hp_i, di + li * hp_r - lr * hp_i

    def step(i, carry):
        t = steps - 1 - i if rev is False else i
        prev = _tile_rows(t - 1 if rev is False else t + 1)
        return update(t, hbuf[prev, 0:S5_H], hbuf[prev, S5_H:S5_NS], carry)

    carry = _unrolled_loop(steps - 1, step, (init_r, init_i, zero, zero))
    row = lax.broadcasted_iota(jnp.int32, (SUB, S5_H), 0)
    if rev:
        last, edge, shift, t = _tile_rows(0), N_SEG - 1, SUB - 1, steps - 1
    else:
        last, edge, shift, t = _tile_rows(steps - 1), 0, 1, 0
    hp_r = jnp.where(row == edge, hin_r, pltpu.roll(hbuf[last, 0:S5_H], shift, 0))
    hp_i = jnp.where(row == edge, hin_i, pltpu.roll(hbuf[last, S5_H:S5_NS], shift, 0))
    _, _, dr, di = update(t, hp_r, hp_i, carry)
    return fin_r, fin_i, dr, di


def _s5_chunk(rows):
    return _tile(rows, 512, PACK)


def _s5_forward(u, uc, bblk, cblk, atile, dsk, ride_bufs, ride_copies):
    rows, d = u.shape
    rows_c = uc.shape[0]
    nj = d // S5_CB
    steps, steps_c = rows // N_SEG, rows_c // N_SEG
    rc = _s5_chunk(rows)
    n_ride = len(ride_bufs)
    n_sem = 3 * n_ride

    def body(u_ref, uc_ref, b_ref, c_ref, a_ref, d_ref, *rest):
        y_ref = rest[n_ride]
        ride = rest[n_ride + 1:2 * n_ride + 1]
        buf, bufc, send_sems, recv_sems = rest[2 * n_ride + 1:]

        @pl.when(pl.program_id(0) == 0)
        def _():
            for cp in ride_copies(ride, ride, send_sems, recv_sems):
                cp.start()

        zero = jnp.zeros((1, S5_H), F32)
        for dr in (0, 1):
            rev = dr == 1
            ar, ai = a_ref[dr, :, 0:S5_H], a_ref[dr, :, S5_H:S5_NS]
            bm, cm = b_ref[dr].astype(MXU_DTYPE), c_ref[dr].astype(MXU_DTYPE)
            bufc[...] = _dot(uc_ref[...], bm)
            fin_r, fin_i = _scan(bufc, steps_c, ar, ai, zero, zero, rev, False)

            def project(r, _):
                rs = pl.ds(pl.multiple_of(r * rc, rc), rc)
                buf[rs, :] = _dot(u_ref[rs, :], bm)
                return 0

            lax.fori_loop(0, rows // rc, project, 0)
            _scan(buf, steps, ar, ai, fin_r, fin_i, rev, True)

            def readout(r, _):
                rs = pl.ds(pl.multiple_of(r * rc, rc), rc)
                yv = _dot(buf[rs, :], cm)
                if dr == 0:
                    y_ref[rs, :] = u_ref[rs, :].astype(F32) * d_ref[...] + yv
                else:
                    y_ref[rs, :] += yv
                return 0

            lax.fori_loop(0, rows // rc, readout, 0)

        @pl.when(pl.program_id(0) == nj - 1)
        def _():
            for cp in ride_copies(ride, ride, send_sems, recv_sems):
                cp.wait()

    res = pl.pallas_call(
        body, name="s5_forward", grid=(nj,),
        in_specs=[
            pl.BlockSpec((rows, S5_CB), lambda j: (0, j)),
            pl.BlockSpec((rows_c, S5_CB), lambda j: (0, j)),
            pl.BlockSpec((2, None, S5_CB, S5_NS), lambda j: (0, j, 0, 0)),
            pl.BlockSpec((2, None, S5_NS, S5_CB), lambda j: (0, j, 0, 0)),
            pl.BlockSpec((2, None, SUB, S5_NS), lambda j: (0, j, 0, 0)),
            pl.BlockSpec((1, S5_CB), lambda j: (0, j)),
        ] + [ANY_SPEC] * n_ride,
        out_specs=[pl.BlockSpec((rows, S5_CB), lambda j: (0, j))] + [ANY_SPEC] * n_ride,
        out_shape=[jax.ShapeDtypeStruct((rows, d), F32)]
        + [jax.ShapeDtypeStruct(b.shape, b.dtype) for b in ride_bufs],
        input_output_aliases={6 + i: 1 + i for i in range(n_ride)},
        scratch_shapes=[pltpu.VMEM((rows, S5_NS), F32), pltpu.VMEM((rows_c, S5_NS), F32),
                        pltpu.SemaphoreType.DMA((n_sem,)), pltpu.SemaphoreType.DMA((n_sem,))],
        compiler_params=pltpu.CompilerParams(
            dimension_semantics=("arbitrary",), vmem_limit_bytes=VMEM_LIMIT, has_side_effects=True),
    )(u, uc, bblk, cblk, atile, dsk, *ride_bufs)
    return res[0], res[1:]


def _s5_backward(u, uc, dy, bblk, cblk, atile, dsk, ride_ins, ride_shapes, ride_copies):
    rows, d = u.shape
    rows_c = uc.shape[0]
    nj = d // S5_CB
    steps, steps_c = rows // N_SEG, rows_c // N_SEG
    rc = _s5_chunk(rows)
    nchunk = rows // rc
    n_ride = len(ride_ins)
    n_sem = 3 * n_ride

    def body(u_ref, uc_ref, dy_ref, b_ref, c_ref, a_ref, d_ref, *rest):
        ride_in = rest[:n_ride]
        du_ref, duc_ref, db_ref, dc_ref, da_ref, dd_ref = rest[n_ride:n_ride + 6]
        ride_out = rest[n_ride + 6:2 * n_ride + 6]
        hbuf, gbuf, hcbuf, gcbuf, send_sems, recv_sems = rest[2 * n_ride + 6:]

        @pl.when(pl.program_id(0) == 0)
        def _():
            for cp in ride_copies(ride_in, ride_out, send_sems, recv_sems):
                cp.start()

        zero = jnp.zeros((1, S5_H), F32)
        db_ref[...] = jnp.zeros_like(db_ref)
        dc_ref[...] = jnp.zeros_like(dc_ref)
        dd_ref[...] = jnp.zeros_like(dd_ref)
        for dr in (0, 1):
            rev = dr == 1
            ar, ai = a_ref[dr, :, 0:S5_H], a_ref[dr, :, S5_H:S5_NS]
            bm, cm = b_ref[dr].astype(MXU_DTYPE), c_ref[dr].astype(MXU_DTYPE)
            hcbuf[...] = _dot(uc_ref[...], bm)
            hin_r, hin_i = _scan(hcbuf, steps_c, ar, ai, zero, zero, rev, True)

            def project(r, _):
                rs = pl.ds(pl.multiple_of(r * rc, rc), rc)
                hbuf[rs, :] = _dot(u_ref[rs, :], bm)
                return 0

            lax.fori_loop(0, nchunk, project, 0)
            _scan(hbuf, steps, ar, ai, hin_r, hin_i, rev, True)

            def readout_bwd(r, _):
                rs = pl.ds(pl.multiple_of(r * rc, rc), rc)
                dyv = dy_ref[rs, :]
                gbuf[rs, :] = _dot(dyv, cm, NT)
                dc_ref[dr] += _dot(hbuf[rs, :], dyv, TN)
                return 0

            lax.fori_loop(0, nchunk, readout_bwd, 0)
            lf_r, lf_i, dar, dai = _adjoint_scan(gbuf, hbuf, steps, ar, ai, zero, zero, hin_r, hin_i, rev)
            gcbuf[...] = jnp.zeros_like(gcbuf)
            _, _, dar_c, dai_c = _adjoint_scan(gcbuf, hcbuf, steps_c, ar, ai, lf_r, lf_i, zero, zero, rev)
            da_ref[dr, :, 0:S5_H] = dar + dar_c
            da_ref[dr, :, S5_H:S5_NS] = dai + dai_c

            def project_bwd(r, _):
                rs = pl.ds(pl.multiple_of(r * rc, rc), rc)
                lam = gbuf[rs, :]
                uv = u_ref[rs, :]
                part = _dot(lam, bm, NT)
                db_ref[dr] += _dot(uv, lam, TN)
                if dr == 0:
                    dyv = dy_ref[rs, :].astype(F32)
                    du_ref[rs, :] = part + dyv * d_ref[...]
                    dd_ref[...] += (dyv * uv.astype(F32)).reshape(rc // SUB, SUB, S5_CB).sum(axis=0)
                else:
                    du_ref[rs, :] += part
                return 0

            lax.fori_loop(0, nchunk, project_bwd, 0)
            lam_c = gcbuf[...]
            part_c = _dot(lam_c, bm, NT)
            db_ref[dr] += _dot(uc_ref[...], lam_c, TN)
            if dr == 0:
                duc_ref[...] = part_c
            else:
                duc_ref[...] += part_c

        @pl.when(pl.program_id(0) == nj - 1)
        def _():
            for cp in ride_copies(ride_in, ride_out, send_sems, recv_sems):
                cp.wait()

    blk = lambda r: pl.BlockSpec((r, S5_CB), lambda j: (0, j))
    res = pl.pallas_call(
        body, name="s5_backward", grid=(nj,),
        in_specs=[
            blk(rows), blk(rows_c), blk(rows),
            pl.BlockSpec((2, None, S5_CB, S5_NS), lambda j: (0, j, 0, 0)),
            pl.BlockSpec((2, None, S5_NS, S5_CB), lambda j: (0, j, 0, 0)),
            pl.BlockSpec((2, None, SUB, S5_NS), lambda j: (0, j, 0, 0)),
            pl.BlockSpec((1, S5_CB), lambda j: (0, j)),
        ] + [ANY_SPEC] * n_ride,
        out_specs=[
            blk(rows), blk(rows_c),
            pl.BlockSpec((2, None, S5_CB, S5_NS), lambda j: (0, j, 0, 0)),
            pl.BlockSpec((2, None, S5_NS, S5_CB), lambda j: (0, j, 0, 0)),
            pl.BlockSpec((2, None, SUB, S5_NS), lambda j: (0, j, 0, 0)),
            pl.BlockSpec((SUB, S5_CB), lambda j: (0, j)),
        ] + [ANY_SPEC] * n_ride,
        out_shape=[
            jax.ShapeDtypeStruct((rows, d), F32), jax.ShapeDtypeStruct((rows_c, d), F32),
            jax.ShapeDtypeStruct(bblk.shape, F32), jax.ShapeDtypeStruct(cblk.shape, F32),
            jax.ShapeDtypeStruct(atile.shape, F32), jax.ShapeDtypeStruct((SUB, d), F32),
        ] + list(ride_shapes),
        scratch_shapes=[pltpu.VMEM((rows, S5_NS), F32), pltpu.VMEM((rows, S5_NS), F32),
                        pltpu.VMEM((rows_c, S5_NS), F32), pltpu.VMEM((rows_c, S5_NS), F32),
                        pltpu.SemaphoreType.DMA((n_sem,)), pltpu.SemaphoreType.DMA((n_sem,))],
        compiler_params=pltpu.CompilerParams(
            dimension_semantics=("arbitrary",), vmem_limit_bytes=VMEM_LIMIT, has_side_effects=True),
    )(u, uc, dy, bblk, cblk, atile, dsk, *ride_ins)
    return res[:6], res[6:]


def _s5_prepare(lam_re, lam_im, log_step, b_re, b_im, c_re, c_im):
    nd, g, p = lam_re.shape
    gb = S5_CB // S5_GROUP
    nj = g // gb
    dt = jnp.exp(log_step)[..., None]
    mag = jnp.exp(lam_re * dt)
    abar_re = mag * jnp.cos(lam_im * dt)
    abar_im = mag * jnp.sin(lam_im * dt)
    nr, ni = abar_re - 1.0, abar_im
    den = lam_re * lam_re + lam_im * lam_im
    fr = (nr * lam_re + ni * lam_im) / den
    fi = (ni * lam_re - nr * lam_im) / den
    bbar_re = fr[..., None] * b_re - fi[..., None] * b_im
    bbar_im = fr[..., None] * b_im + fi[..., None] * b_re
    eye = jnp.eye(gb, dtype=bool)

    def diag_in(w):
        w = w.reshape(nd, nj, gb, p, S5_GROUP).transpose(0, 1, 2, 4, 3)
        w = jnp.where(eye[None, None, :, None, :, None], w[:, :, :, :, None, :], 0.0)
        return w.reshape(nd, nj, gb * S5_GROUP, gb * p)

    def diag_out(w):
        w = w.reshape(nd, nj, gb, S5_GROUP, p).transpose(0, 1, 2, 4, 3)
        w = jnp.where(eye[None, None, :, None, :, None], w[:, :, :, :, None, :], 0.0)
        return w.reshape(nd, nj, gb * p, gb * S5_GROUP)

    bblk = jnp.concatenate([diag_in(bbar_re), diag_in(bbar_im)], axis=-1)
    cblk = jnp.concatenate([diag_out(c_re), -diag_out(c_im)], axis=-2)
    a2 = jnp.concatenate([abar_re.reshape(nd, nj, gb * p), abar_im.reshape(nd, nj, gb * p)], axis=-1)
    atile = jnp.broadcast_to(a2[:, :, None, :], (nd, nj, SUB, 2 * gb * p))
    return bblk, cblk, atile


def _shifted(x, prev_row, next_row):
    n = x.shape[0]
    row = lax.broadcasted_iota(jnp.int32, (SUB, x.shape[1]), 0)
    xp = pltpu.roll(x, 1, 0)
    xn = pltpu.roll(x, n - 1, 0)
    xp = jnp.concatenate([jnp.where(row == 0, prev_row, xp[:SUB]), xp[SUB:]], axis=0)
    xn = jnp.concatenate([xn[:n - SUB], jnp.where(row == SUB - 1, next_row, xn[n - SUB:])], axis=0)
    return xp, xn


def _edge_rows(ref, r0, n, total, group):
    lo = pl.multiple_of(jnp.maximum(r0 - group, 0), group)
    hi = pl.multiple_of(jnp.minimum(r0 + n, total - group), group)
    prev_row = ref[pl.ds(lo, group), :].astype(F32)[group - 1:group] * (r0 > 0).astype(F32)
    next_row = ref[pl.ds(hi, group), :].astype(F32)[0:1] * (r0 + n < total).astype(F32)
    return prev_row, next_row


def _conv_rows(ref, r0, n, total, w_ref, b_ref):
    x = ref[pl.ds(r0, n), :].astype(F32)
    xp, xn = _shifted(x, *_edge_rows(ref, r0, n, total, PACK))
    hc = w_ref[0:1, :] * xp + w_ref[1:2, :] * x + w_ref[2:3, :] * xn + b_ref[...]
    return hc, xp, x, xn


def _conv_specs(rows, f, tc):
    nt = f // tc
    val = lambda r: pl.BlockSpec((r, tc), lambda j: (0, j))
    gate = lambda r: pl.BlockSpec((r, tc), lambda j: (0, j + nt))
    return val, gate


def _conv_swiglu_fwd(name, h, cw, cb):
    rows, f2 = h.shape
    f = f2 // 2
    tc = _tile(f, 256)
    rc = _tile(rows, 256, PACK)
    val, gate = _conv_specs(rows, f, tc)

    def body(hv_ref, hg_ref, wv_ref, wg_ref, bv_ref, bg_ref, a_ref):
        def chunk(r, _):
            r0 = pl.multiple_of(r * rc, rc)
            hv = _conv_rows(hv_ref, r0, rc, rows, wv_ref, bv_ref)[0]
            hg = _conv_rows(hg_ref, r0, rc, rows, wg_ref, bg_ref)[0]
            a_ref[pl.ds(r0, rc), :] = (hg * _sigmoid(hg) * hv).astype(a_ref.dtype)
            return 0

        lax.fori_loop(0, rows // rc, chunk, 0)

    return pl.pallas_call(
        body, name=name, grid=(f // tc,),
        in_specs=[val(rows), gate(rows), val(3), gate(3), val(1), gate(1)],
        out_specs=val(rows), out_shape=jax.ShapeDtypeStruct((rows, f), ACT_DTYPE),
        compiler_params=_params(("parallel",)),
    )(h, h, cw, cw, cb, cb)


def _conv_swiglu_bwd(name, da, h, cw, cb):
    rows, f2 = h.shape
    f = f2 // 2
    tc = _tile(f, 256)
    rc = _tile(rows, 256, PACK)
    val, gate = _conv_specs(rows, f, tc)

    def body(da_ref, hv_ref, hg_ref, wv_ref, wg_ref, bv_ref, bg_ref,
             dhv_ref, dhg_ref, dwv_ref, dwg_ref, dbv_ref, dbg_ref, sv, sg):
        def first(r, carry):
            r0 = pl.multiple_of(r * rc, rc)
            rs = pl.ds(r0, rc)
            hv, vp, vx, vn = _conv_rows(hv_ref, r0, rc, rows, wv_ref, bv_ref)
            hg, gp, gx, gn = _conv_rows(hg_ref, r0, rc, rows, wg_ref, bg_ref)
            d = da_ref[rs, :].astype(F32)
            s = _sigmoid(hg)
            dv = d * (hg * s)
            dg = d * hv * (s * (1.0 + hg * (1.0 - s)))
            sv[rs, :] = dv
            sg[rs, :] = dg
            sums = [dv * vp, dv * vx, dv * vn, dv, dg * gp, dg * gx, dg * gn, dg]
            return tuple(c + jnp.sum(x, axis=0, keepdims=True) for c, x in zip(carry, sums))

        zero = jnp.zeros((1, tc), F32)
        acc = lax.fori_loop(0, rows // rc, first, (zero,) * 8)
        for k in range(3):
            dwv_ref[k:k + 1, :] = acc[k]
            dwg_ref[k:k + 1, :] = acc[4 + k]
        dbv_ref[...] = acc[3]
        dbg_ref[...] = acc[7]

        def second(r, _):
            r0 = pl.multiple_of(r * rc, rc)
            rs = pl.ds(r0, rc)
            for s_ref, w_ref, o_ref in ((sv, wv_ref, dhv_ref), (sg, wg_ref, dhg_ref)):
                x = s_ref[rs, :]
                xp, xn = _shifted(x, *_edge_rows(s_ref, r0, rc, rows, SUB))
                o_ref[rs, :] = (w_ref[0:1, :] * xn + w_ref[1:2, :] * x + w_ref[2:3, :] * xp).astype(o_ref.dtype)
            return 0

        lax.fori_loop(0, rows // rc, second, 0)

    res = pl.pallas_call(
        body, name=name, grid=(f // tc,),
        in_specs=[val(rows), val(rows), gate(rows), val(3), gate(3), val(1), gate(1)],
        out_specs=[val(rows), val(rows), val(3), val(3), val(1), val(1)],
        out_shape=[jax.ShapeDtypeStruct((rows, f), ACT_DTYPE)] * 2
        + [jax.ShapeDtypeStruct((3, f), F32)] * 2 + [jax.ShapeDtypeStruct((1, f), F32)] * 2,
        scratch_shapes=[pltpu.VMEM((rows, tc), F32), pltpu.VMEM((rows, tc), F32)],
        compiler_params=_params(("parallel",)),
    )(da, h, h, cw, cw, cb, cb)
    return res


def _pool_band(name, x, transpose, out_dtype):
    rows, d = x.shape
    ng = len(POOL_WINDOWS)
    ch = d // ng
    tm = _tile(rows, 256, PACK)
    win = tm + 2 * POOL_HALO
    assert win <= rows

    def body(x_ref, o_ref):
        half = lax.shift_left(jnp.int32(1), pl.program_id(0))
        t0 = pl.program_id(1) * tm
        ws = pl.multiple_of(jnp.clip(t0 - POOL_HALO, 0, rows - win), PACK)
        i = t0 + lax.broadcasted_iota(jnp.int32, (tm, win), 0)
        j = ws + lax.broadcasted_iota(jnp.int32, (tm, win), 1)

        def inv_count(t):
            hi = jnp.minimum(t + half - 1, rows - 1)
            lo = jnp.maximum(t - half, 0)
            return 1.0 / (hi - lo + 1).astype(F32)

        xw = x_ref[pl.ds(ws, win), :]
        xt = x_ref[pl.ds(pl.multiple_of(t0, PACK), tm), :].astype(F32)
        if transpose:
            band = (j - half <= i) & (i <= j + half - 1)
            tw = ws + lax.broadcasted_iota(jnp.int32, (win, 1), 0)
            o = _dot(band.astype(MXU_DTYPE), xw.astype(F32) * inv_count(tw)) - xt
        else:
            band = (i - half <= j) & (j <= i + half - 1)
            tt = t0 + lax.broadcasted_iota(jnp.int32, (tm, 1), 0)
            o = _dot(band.astype(MXU_DTYPE), xw) * inv_count(tt) - xt
        o_ref[...] = o.astype(o_ref.dtype)

    return pl.pallas_call(
        body, name=name, grid=(ng, rows // tm),
        in_specs=[pl.BlockSpec((rows, ch), lambda g, i: (0, g))],
        out_specs=pl.BlockSpec((tm, ch), lambda g, i: (i, g)),
        out_shape=jax.ShapeDtypeStruct((rows, d), out_dtype),
        compiler_params=_params(("parallel", "arbitrary")),
    )(x)


def _ada_forward(c16, ada_w, ada_b):
    nl, d, cols = ada_w.shape
    tn = _tile(cols, 512)

    def body(c_ref, w_ref, b_ref, o_ref):
        cv = c_ref[...]
        o_ref[...] = _dot(cv * _sigmoid(cv), w_ref[...]) + b_ref[...]

    return pl.pallas_call(
        body, name="ada_forward", grid=(nl, cols // tn),
        in_specs=[pl.BlockSpec((16, d), lambda l, n: (0, 0)),
                  pl.BlockSpec((None, d, tn), lambda l, n: (l, 0, n)),
                  pl.BlockSpec((None, 1, tn), lambda l, n: (l, 0, n))],
        out_specs=pl.BlockSpec((None, 16, tn), lambda l, n: (l, 0, n)),
        out_shape=jax.ShapeDtypeStruct((nl, 16, cols), F32),
        compiler_params=_params(("parallel", "parallel")),
    )(c16, ada_w, ada_b)


def _ada_backward(c16, dmod, ada_w):
    nl, d, cols = ada_w.shape
    tn = _tile(cols, 512)
    nn = cols // tn

    def body(c_ref, g_ref, w_ref, gw_ref, gc_ref):
        cv = c_ref[...]
        s = _sigmoid(cv)
        gv = g_ref[...]
        gw_ref[...] = _dot(cv * s, gv, TN)
        dcond = _dot(gv, w_ref[...], NT)
        row = lax.broadcasted_iota(jnp.int32, dcond.shape, 0)
        dctx = jnp.sum(jnp.where(row >= 8, dcond * (s * (1.0 + cv * (1.0 - s))), 0.0), axis=0, keepdims=True)

        @pl.when((pl.program_id(0) == 0) & (pl.program_id(1) == 0))
        def _():
            gc_ref[...] = jnp.zeros_like(gc_ref)

        gc_ref[...] += dctx

    return pl.pallas_call(
        body, name="ada_backward", grid=(nl, nn),
        in_specs=[pl.BlockSpec((16, d), lambda l, n: (0, 0)),
                  pl.BlockSpec((None, 16, tn), lambda l, n: (l, 0, n)),
                  pl.BlockSpec((None, d, tn), lambda l, n: (l, 0, n))],
        out_specs=[pl.BlockSpec((None, d, tn), lambda l, n: (l, 0, n)),
                   pl.BlockSpec((1, d), lambda l, n: (0, 0))],
        out_shape=[jax.ShapeDtypeStruct((nl, d, cols), F32), jax.ShapeDtypeStruct((1, d), F32)],
        compiler_params=_params(("arbitrary", "arbitrary")),
    )(c16, dmod, ada_w)


def _row_sum16(name, a):
    nl, _, w = a.shape
    tn = _tile(w, 4096)

    def body(a_ref, o_ref):
        o_ref[...] = jnp.sum(a_ref[...], axis=0, keepdims=True)

    return pl.pallas_call(
        body, name=name, grid=(nl, w // tn),
        in_specs=[pl.BlockSpec((None, 16, tn), lambda l, n: (l, 0, n))],
        out_specs=pl.BlockSpec((None, 1, tn), lambda l, n: (l, 0, n)),
        out_shape=jax.ShapeDtypeStruct((nl, 1, w), F32),
        compiler_params=_params(("parallel", "parallel")),
    )(a)


def _pack(arrays, row_align):
    flat = jnp.concatenate([a.reshape(-1).astype(F32) for a in arrays])
    quantum = row_align * LANE
    padded = -(-flat.shape[0] // quantum) * quantum
    return jnp.pad(flat, (0, padded - flat.shape[0])).reshape(-1, LANE)


def _unpack(packed, shapes):
    flat = packed.reshape(-1)
    out, off = [], 0
    for s in shapes:
        n = math.prod(s)
        out.append(flat[off:off + n].reshape(s))
        off += n
    return out


def _grid_pos_tables(n_tokens, dim):
    quarter = dim // 4
    omega = 1.0 / (POS_BASE ** (jnp.arange(quarter, dtype=F32) / quarter))

    def enc(n):
        ang = jnp.arange(n, dtype=F32).reshape(-1, 1) * omega[None, :]
        return jnp.concatenate([jnp.sin(ang), jnp.cos(ang)], axis=-1)

    return enc(n_tokens // GRID_W), enc(GRID_W)


def _ffn_forward(tag, v, x_in, up4, dn4, layer, cw, cb):
    h = _mm_cols(f"ffn_up_{tag}", v, up4, layer, ACT_DTYPE)
    a = _conv_swiglu_fwd(f"ffn_conv_{tag}", h, cw, cb)
    f = _mm_rows(f"ffn_down_{tag}", a, dn4, layer, F32)
    return h, a, f


def _ffn_backward(tag, dx_out, fb, x_mid, v, h, a, up4, dn4, layer, cw, cb, gate5, gam3, gam2, scale4p1):
    def post(dxo, f, gam, gate):
        dy, dgate, dgam = _postnorm_bwd(dxo, f.astype(F32), gam, gate)
        return (dy,), (dgate, dgam)

    (df,), (dgate5, dgam3) = _rowwise(f"ffn_post_bwd_{tag}", post, dx_out.shape[0],
                                      [(dx_out, False), (fb, False)], [gam3, gate5],
                                      [(dx_out.shape[1], ACT_DTYPE, False)], [dx_out.shape[1]] * 2)
    da = _mm_rows_nt(f"ffn_down_dx_{tag}", df, dn4, layer, ACT_DTYPE)
    g_dn = _mm_rows_tn(f"ffn_down_dw_{tag}", a, df, dn4.shape[-2], ACT_DTYPE)
    dhv, dhg, dwv, dwg, dbv, dbg = _conv_swiglu_bwd(f"ffn_conv_bwd_{tag}", da, h, cw, cb)
    dh = (dhv, dhg)
    dcw = jnp.concatenate([dwv, dwg], axis=1)
    dcb = jnp.concatenate([dbv, dbg], axis=1)
    dv = _mm_cols_nt(f"ffn_up_dx_{tag}", dh, up4, layer, F32)
    g_up = _mm_cols_tn(f"ffn_up_dw_{tag}", v, dh, up4.shape[-1], ACT_DTYPE)

    def pre(dvv, x, dxo, gam, s1):
        dx, dshift, dscale, dgam = _prenorm_bwd(dvv, x, gam, s1)
        return (dxo + dx,), (dshift, dscale, dgam)

    d = dx_out.shape[1]
    (dx_mid,), (dshift3, dscale4, dgam2) = _rowwise(
        f"ffn_pre_bwd_{tag}", pre, dx_out.shape[0], [(dv, False), (x_mid, False), (dx_out, False)],
        [gam2, scale4p1], [(d, F32, False)], [d] * 3)
    return dx_mid, g_up, g_dn, dcw, dcb, (dshift3, dscale4, dgate5), (dgam2, dgam3)


def kernel(x, c, ctx, c_ctx, ada_w, ada_b, norm_g, s5_lam_re, s5_lam_im, s5_log_step, s5_b_re, s5_b_im, s5_c_re, s5_c_im, s5_d, s5_glu_w, pool_w, pool_scale, ffn_up, ffn_conv, ffn_conv_b, ffn_down, loss_target, m_c_ctx, m_ada_w, m_ada_b, m_norm_g, m_s5_lam_re, m_s5_lam_im, m_s5_log_step, m_s5_b_re, m_s5_b_im, m_s5_c_re, m_s5_c_im, m_s5_d, m_s5_glu_w, m_pool_w, m_pool_scale, m_ffn_up, m_ffn_conv, m_ffn_conv_b, m_ffn_down, v_c_ctx, v_ada_w, v_ada_b, v_norm_g, v_s5_lam_re, v_s5_lam_im, v_s5_log_step, v_s5_b_re, v_s5_b_im, v_s5_c_re, v_s5_c_im, v_s5_d, v_s5_glu_w, v_pool_w, v_pool_scale, v_ffn_up, v_ffn_conv, v_ffn_conv_b, v_ffn_down):
    ix, iy, ic = _coords()
    chip = 2 * ix + iy
    me = 2 * chip + ic
    _, rows, d = x.shape
    rows_c = ctx.shape[1]
    nl = ada_w.shape[0]
    assert nl == 2 and s5_glu_w.shape[0] == 1 and pool_w.shape[0] == 1
    a_cols = ada_w.shape[2]
    f2s = ffn_up.shape[2]
    f2 = N_CHIP * f2s
    ds = d // N_CHIP
    ng = len(POOL_WINDOWS)
    ch = d // ng
    ps = pool_w.shape[2]

    core = jnp.reshape(ic, (1,)).astype(jnp.int32)
    chip_id = jnp.reshape(chip, (1,)).astype(jnp.int32)
    pos = jnp.concatenate([core, chip_id])
    shards = {"up": _view2d(ffn_up), "down": _view2d(ffn_down), "glu": s5_glu_w[0], "pool": _view2d(pool_w[0])}
    gather_bufs = [_cast_own_half(f"cast_{n}", w, pos) for n, w in shards.items()]

    c_all = _all_gather8("gather_cond", _pack([c], SUB))
    c_all = c_all.reshape(N_DEV, -1)[:, :d]
    c16 = jnp.concatenate([c_all, jnp.broadcast_to(c_ctx[None, :], (N_DEV, d))], axis=0)
    ada_b_mine = lax.dynamic_slice_in_dim(ada_b, chip * a_cols, a_cols, axis=1)
    mod_part = _ada_forward(c16, ada_w, ada_b_mine[:, None, :])
    narrow_shapes = [mod_part.shape, norm_g.shape, pool_scale.shape, ffn_conv.shape]
    narrow = _all_gather8("gather_narrow", _pack([mod_part, norm_g, pool_scale, ffn_conv], SUB))
    per_chip = [_unpack(narrow[2 * s], narrow_shapes) for s in range(N_CHIP)]
    mod_all = jnp.concatenate([p[0] for p in per_chip], axis=-1)
    gam = jnp.concatenate([p[1] for p in per_chip], axis=-1)
    pscale = jnp.concatenate([p[2] for p in per_chip], axis=-1)
    conv_w = jnp.concatenate([p[3] for p in per_chip], axis=-1)
    mod_mine = lax.dynamic_index_in_dim(mod_all, me, axis=1, keepdims=False)
    mod_ctx = mod_all[0, N_DEV]

    def mods(vec):
        s0, s1, g2, s3, s4, g5 = [vec[k * d:(k + 1) * d][None, :] for k in range(N_MOD)]
        return s0, 1.0 + s1, g2, s3, 1.0 + s4, g5

    m0, m1, mc = mods(mod_mine[0]), mods(mod_mine[1]), mods(mod_ctx)
    gains = [[gam[l, k][None, :] for k in range(4)] for l in range(nl)]
    conv_b = ffn_conv_b[:, None, :]

    row_tab, col_tab = _grid_pos_tables(rows, d)
    per_tile = _tile(rows // N_SEG, 256, SUB) // GRID_W
    rep = SUB // per_tile
    row_tab = jnp.repeat(row_tab, rep, axis=0)

    def init(xv, rt, ct, g0, shift, s1):
        pe_r = jnp.concatenate(
            [jnp.broadcast_to(rt[q * rep:q * rep + 1], (GRID_W, d // 2)) for q in range(per_tile)], axis=0)
        pe_c = jnp.concatenate([ct] * per_tile, axis=0)
        x0 = xv + jnp.concatenate([pe_r, pe_c], axis=1)
        return (x0, _prenorm(x0, g0, shift, s1)), ()

    (x0, u), _ = _rowwise("init", init, rows, [(x[0], False)],
                          [(row_tab, SUB), col_tab, gains[0][0], m0[0], m0[1]],
                          [(d, F32, False), (d, ACT_DTYPE, True)])
    (uc,), _ = _rowwise("ctx_prenorm", lambda cv, g0, shift, s1: ((_prenorm(cv, g0, shift, s1),), ()),
                        rows_c, [(ctx[0], False)], [gains[0][0], mc[0], mc[1]], [(d, ACT_DTYPE, True)])
    s5_params = (s5_lam_re[0], s5_lam_im[0], s5_log_step[0], s5_b_re[0], s5_b_im[0], s5_c_re[0], s5_c_im[0])
    (bblk, cblk, atile), s5_vjp = jax.vjp(_s5_prepare, *s5_params)
    buf_up, buf_dn, buf_glu, buf_pool = gather_bufs
    rows_of = {n: w.shape[0] for n, w in shards.items()}
    y, (buf_up, buf_glu) = _s5_forward(u, uc, bblk, cblk, atile, s5_d, [buf_up, buf_glu],
                                       _gather_ici_plan([rows_of["up"], rows_of["glu"]]))
    up4 = _swap_gathered("gather_up", buf_up, core, rows_of["up"]).reshape((N_CHIP,) + ffn_up.shape)
    glu4 = _swap_gathered("gather_glu", buf_glu, core, rows_of["glu"])
    (z,), _ = _rowwise("gelu", lambda yv: ((_gelu(yv),), ()), rows, [(y, True)], [], [(d, ACT_DTYPE, False)])
    zz = _mm_cols("glu_proj", z, glu4[:, None], 0, ACT_DTYPE)

    def glu_out(zzv, xv, gate2, g1, g2, shift3, s4):
        zf = zzv.astype(F32)
        o = zf[:, :d] * _sigmoid(zf[:, d:])
        x1 = xv + gate2 * (o * _rstd(o) * g1)
        return (x1, _prenorm(x1, g2, shift3, s4)), ()

    (x1, v0), _ = _rowwise("glu_resid", glu_out, rows, [(zz, False), (x0, False)],
                           [m0[2], gains[0][1], gains[0][2], m0[3], m0[4]], [(d, F32, False), (d, ACT_DTYPE, False)])
    h0, (buf_dn, buf_pool) = _mm_cols(
        "ffn_up_l0", v0, up4, 0, ACT_DTYPE,
        ride=([buf_dn, buf_pool], _gather_ici_plan([rows_of["down"], rows_of["pool"]])))
    dn4 = _swap_gathered("gather_down", buf_dn, core, rows_of["down"]).reshape((N_CHIP,) + ffn_down.shape)
    pool_full = _swap_gathered("gather_pool", buf_pool, core, rows_of["pool"])
    pool_full = pool_full.reshape(N_CHIP, ng, ps, ch).transpose(1, 0, 2, 3).reshape(ng, ch, ch)
    a0 = _conv_swiglu_fwd("ffn_conv_l0", h0, conv_w[0], conv_b[0])
    f0 = _mm_rows("ffn_down_l0", a0, dn4, 0, F32)

    def ffn_out(fv, xv, gate5, g3, g0n, shift0, s1):
        x2 = xv + gate5 * (fv * _rstd(fv) * g3)
        return (x2, _prenorm(x2, g0n, shift0, s1), fv), ()

    (x2, u1, fb0), _ = _rowwise("ffn_resid_l0", ffn_out, rows, [(f0, False), (x1, False)],
                                [m0[5], gains[0][3], gains[1][0], m1[0], m1[1]],
                                [(d, F32, False), (d, ACT_DTYPE, False), (d, ACT_DTYPE, False)])

    p1 = _pool_band("pool_band", u1, False, ACT_DTYPE)
    yr = _mm_grp("pool_proj", p1, pool_full, NN, F32)

    def pool_out(yv, xv, ps_, gate2, g1, g2, shift3, s4):
        o = yv * ps_
        x1n = xv + gate2 * (o * _rstd(o) * g1)
        return (x1n, _prenorm(x1n, g2, shift3, s4), yv), ()

    (x3, v1, yb), _ = _rowwise("pool_resid", pool_out, rows, [(yr, False), (x2, False)],
                               [pscale, m1[2], gains[1][1], gains[1][2], m1[3], m1[4]],
                               [(d, F32, False), (d, ACT_DTYPE, False), (d, ACT_DTYPE, False)])
    h1, a1, f1 = _ffn_forward("l1", v1, x3, up4, dn4, 1, conv_w[1], conv_b[1])

    def loss_head(fv, xv, tv, gate5, g3):
        err = xv + gate5 * (fv * _rstd(fv) * g3) - tv
        return (err * (1.0 / d), fv), (err * err,)

    (dx4, fb1), (sq,) = _rowwise("loss_head", loss_head, rows, [(f1, False), (x3, False), (loss_target[0], False)],
                                 [m1[5], gains[1][3]], [(d, F32, False), (d, ACT_DTYPE, False)], [d])
    loss = lax.psum(0.5 * jnp.sum(sq) / d, ("x", "y", "c"))

    dx3, g_up1, g_dn1, dcw1, dcb1, dmod_ffn1, (dgam12, dgam13) = _ffn_backward(
        "l1", dx4, fb1, x3, v1, h1, a1, up4, dn4, 1, conv_w[1], conv_b[1], m1[5], gains[1][3], gains[1][2], m1[4])

    def pool_post(dxo, yv, ps_, g1, gate2):
        yraw = yv.astype(F32)
        dy, dgate, dgam = _postnorm_bwd(dxo, yraw * ps_, g1, gate2)
        return (dy * ps_,), (dgate, dgam, dy * yraw)

    (dyr,), (dgate2_1, dgam11, dpscale) = _rowwise("pool_post_bwd", pool_post, rows, [(dx3, False), (yb, False)],
                                                   [pscale, gains[1][1], m1[2]], [(d, ACT_DTYPE, False)], [d] * 3)
    dp1 = _mm_grp("pool_proj_dx", dyr, pool_full, NT, ACT_DTYPE)
    g_pool = _mm_grp_tn("pool_proj_dw", p1, dyr, ng, ACT_DTYPE)
    du1 = _pool_band("pool_band_bwd", dp1, True, F32)

    def pre_bwd(duv, xv, dxo, g0, s1):
        dx, dshift, dscale, dgam = _prenorm_bwd(duv, xv, g0, s1)
        return (dxo + dx,), (dshift, dscale, dgam)

    (dx2,), (dshift0_1, dscale1_1, dgam10) = _rowwise(
        "pool_pre_bwd", pre_bwd, rows, [(du1, False), (x2, False), (dx3, False)],
        [gains[1][0], m1[1]], [(d, F32, False)], [d] * 3)

    dx1, g_up0, g_dn0, dcw0, dcb0, dmod_ffn0, (dgam02, dgam03) = _ffn_backward(
        "l0", dx2, fb0, x1, v0, h0, a0, up4, dn4, 0, conv_w[0], conv_b[0], m0[5], gains[0][3], gains[0][2], m0[4])

    def glu_post(dxo, zzv, g1, gate2):
        zf = zzv.astype(F32)
        val, s = zf[:, :d], _sigmoid(zf[:, d:])
        do, dgate, dgam = _postnorm_bwd(dxo, val * s, g1, gate2)
        return (jnp.concatenate([do * s, do * val * (s * (1.0 - s))], axis=1),), (dgate, dgam)

    (dzz,), (dgate2_0, dgam01) = _rowwise("glu_post_bwd", glu_post, rows, [(dx1, False), (zz, False)],
                                          [gains[0][1], m0[2]], [(2 * d, ACT_DTYPE, False)], [d] * 2)
    dz = _mm_cols_nt("glu_proj_dx", dzz, glu4[:, None], 0, F32)
    g_glu = _mm_cols_tn("glu_proj_dw", z, dzz, glu4.shape[-1], ACT_DTYPE)
    (dy,), _ = _rowwise("gelu_bwd", lambda dzv, yv: ((dzv * _gelu_grad(yv),), ()), rows,
                        [(dz, False), (y, True)], [], [(d, ACT_DTYPE, True)])
    g_pool4 = g_pool.reshape(ng, N_CHIP, ps, ch).transpose(1, 0, 2, 3).reshape(N_CHIP, ng * ps, ch)
    big = {"up0": g_up0, "up1": g_up1, "dn0": g_dn0, "dn1": g_dn1, "glu": g_glu, "pool": g_pool4}
    pairs = [_pair_sum(f"pair_{n}", g, core) for n, g in big.items()]
    ride_shapes, ride_copies = _chip_exchange_plan(pairs)
    (du0, duc, d_bblk, d_cblk, d_atile, d_dsk), others = _s5_backward(
        u, uc, dy, bblk, cblk, atile, s5_d, pairs, ride_shapes, ride_copies)
    (gx,), (dshift0_0, dscale1_0, dgam00) = _rowwise(
        "s5_pre_bwd", pre_bwd, rows, [(du0, True), (x0, False), (dx1, False)],
        [gains[0][0], m0[1]], [(d, F32, False)], [d] * 3)

    def ctx_bwd(duv, cv, g0, s1):
        _, dshift, dscale, dgam = _prenorm_bwd(duv, cv, g0, s1)
        return (), (dshift, dscale, dgam)

    _, (dshift_c, dscale_c, dgam00c) = _rowwise("ctx_pre_bwd", ctx_bwd, rows_c, [(duc, True), (ctx[0], False)],
                                                [gains[0][0], mc[1]], [], [d] * 3)
    g_s5 = s5_vjp((d_bblk, d_cblk, d_atile))

    zero_d = jnp.zeros((1, d), F32)
    dmod_lat = jnp.stack([
        jnp.concatenate([dshift0_0, dscale1_0, dgate2_0, *dmod_ffn0], axis=1),
        jnp.concatenate([dshift0_1, dscale1_1, dgate2_1, *dmod_ffn1], axis=1)])
    dmod_ctx = jnp.stack([jnp.concatenate([dshift_c, dscale_c] + [zero_d] * 4, axis=1),
                          jnp.zeros((1, N_MOD * d), F32)])
    dmod_shape = (nl, 2, N_MOD * d)
    dmod_all = _all_gather8("gather_dmod", _pack([jnp.concatenate([dmod_lat, dmod_ctx], axis=1)], SUB))
    dmod_all = jnp.stack([_unpack(dmod_all[k], [dmod_shape])[0] for k in range(N_DEV)])
    dmod16 = jnp.concatenate([dmod_all[:, :, 0], dmod_all[:, :, 1]], axis=0).transpose(1, 0, 2)
    dmod16_mine = lax.dynamic_slice_in_dim(dmod16, chip * a_cols, a_cols, axis=2)
    g_ada_w, g_cctx_part = _ada_backward(c16, dmod16_mine, ada_w)
    g_ada_b = _row_sum16("ada_bias_grad", dmod16)[:, 0]

    d_gam = jnp.stack([jnp.concatenate([dgam00 + dgam00c, dgam01, dgam02, dgam03], axis=0),
                       jnp.concatenate([dgam10, dgam11, dgam12, dgam13], axis=0)])
    small = [d_gam, 0.5 * g_cctx_part, *g_s5, jnp.sum(d_dsk, axis=0, keepdims=True), dpscale,
             jnp.stack([dcw0, dcw1]), jnp.stack([dcb0[0], dcb1[0]])]
    small_shapes = [s.shape for s in small]
    packed = _pack(small, N_DEV * SUB)
    summed = _all_reduce8("reduce_small", packed.reshape(N_DEV, -1, LANE)).reshape(-1, LANE)
    (r_gam, r_cctx, r_lam_re, r_lam_im, r_log_step, r_b_re, r_b_im, r_c_re, r_c_im,
     r_dsk, r_pscale, r_conv, r_convb) = _unpack(summed, small_shapes)
    g_norm = lax.dynamic_slice_in_dim(r_gam, chip * ds, ds, axis=2)
    g_pscale = lax.dynamic_slice_in_dim(r_pscale, chip * ds, ds, axis=1)
    g_conv = lax.dynamic_slice_in_dim(r_conv, chip * f2s, f2s, axis=2)

    parts = dict(zip(big, zip(pairs, others)))
    d_rows, dn_rows = ffn_up.shape[1], ffn_down.shape[1]
    g_up = _sum_and_swap("reduce_up0", *parts["up0"], pos, None, nl * d_rows, 0)
    g_up = _sum_and_swap("reduce_up1", *parts["up1"], pos, g_up, nl * d_rows, d_rows).reshape(ffn_up.shape)
    g_dn = _sum_and_swap("reduce_dn0", *parts["dn0"], pos, None, nl * dn_rows, 0)
    g_dn = _sum_and_swap("reduce_dn1", *parts["dn1"], pos, g_dn, nl * dn_rows, dn_rows).reshape(ffn_down.shape)
    g_glu_f = _sum_and_swap("reduce_glu", *parts["glu"], pos, None, d, 0)
    g_pool_f = _sum_and_swap("reduce_pool", *parts["pool"], pos, None, ng * ps, 0)

    grads = {
        "c_ctx": r_cctx[0], "ada_w": g_ada_w, "ada_b": g_ada_b, "norm_g": g_norm,
        "s5_lam_re": r_lam_re[None], "s5_lam_im": r_lam_im[None], "s5_log_step": r_log_step[None],
        "s5_b_re": r_b_re[None], "s5_b_im": r_b_im[None], "s5_c_re": r_c_re[None], "s5_c_im": r_c_im[None],
        "s5_d": r_dsk, "s5_glu_w": g_glu_f[None], "pool_w": g_pool_f.reshape(pool_w.shape),
        "pool_scale": g_pscale, "ffn_up": g_up, "ffn_conv": g_conv, "ffn_conv_b": r_convb, "ffn_down": g_dn,
    }
    weights = {
        "c_ctx": (c_ctx, m_c_ctx, v_c_ctx), "ada_w": (ada_w, m_ada_w, v_ada_w), "ada_b": (ada_b, m_ada_b, v_ada_b),
        "norm_g": (norm_g, m_norm_g, v_norm_g), "s5_lam_re": (s5_lam_re, m_s5_lam_re, v_s5_lam_re),
        "s5_lam_im": (s5_lam_im, m_s5_lam_im, v_s5_lam_im), "s5_log_step": (s5_log_step, m_s5_log_step, v_s5_log_step),
        "s5_b_re": (s5_b_re, m_s5_b_re, v_s5_b_re), "s5_b_im": (s5_b_im, m_s5_b_im, v_s5_b_im),
        "s5_c_re": (s5_c_re, m_s5_c_re, v_s5_c_re), "s5_c_im": (s5_c_im, m_s5_c_im, v_s5_c_im),
        "s5_d": (s5_d, m_s5_d, v_s5_d), "s5_glu_w": (s5_glu_w, m_s5_glu_w, v_s5_glu_w),
        "pool_w": (pool_w, m_pool_w, v_pool_w), "pool_scale": (pool_scale, m_pool_scale, v_pool_scale),
        "ffn_up": (ffn_up, m_ffn_up, v_ffn_up), "ffn_conv": (ffn_conv, m_ffn_conv, v_ffn_conv),
        "ffn_conv_b": (ffn_conv_b, m_ffn_conv_b, v_ffn_conv_b), "ffn_down": (ffn_down, m_ffn_down, v_ffn_down),
    }
    names = list(weights)
    large = ("ada_w", "s5_glu_w", "pool_w", "ffn_up", "ffn_down")
    delta, new_m, new_v = {}, {}, {}
    for n in names:
        w, m, v = weights[n]
        if n in large:
            delta[n], new_m[n], new_v[n] = _adamw(f"adamw_{n}", w, grads[n], m, v)
        else:
            shape = w.shape
            view = (1, shape[0]) if w.ndim == 1 else shape
            res = _adamw(f"adamw_{n}", *[t.reshape(view) for t in (w, grads[n], m, v)])
            delta[n], new_m[n], new_v[n] = [t.reshape(shape) for t in res]

    return (loss, gx[None], *[grads[n] for n in names], *[delta[n] for n in names],
            *[new_m[n] for n in names], *[new_v[n] for n in names])
```

```python
import math

import jax
import jax.numpy as jnp
from jax import lax
from jax.experimental import pallas as pl
from jax.experimental.pallas import tpu as pltpu

F32 = jnp.float32
MXU_DTYPE = jnp.bfloat16
ACT_DTYPE = jnp.bfloat16

LANE = 128
SUB = 8
PACK = 16
VMEM_LIMIT = 56 * 1024 * 1024
ELEMWISE_BLOCK = 1 << 18

N_DEV = 8
N_CHIP = 4
N_SEG = SUB
S5_GROUP = 16
S5_STATE = 64
S5_CB = LANE
S5_H = (S5_CB // S5_GROUP) * S5_STATE
S5_NS = 2 * S5_H
POOL_WINDOWS = (2, 4, 8, 16)
POOL_HALO = 16
GRID_W = 64
POS_BASE = 10000.0
RMS_EPS = 1e-6
N_MOD = 6

ADAM_LR = 0.001
ADAM_B1 = 0.9
ADAM_B2 = 0.999
ADAM_EPS = 1e-08
ADAM_WD = 0.01
ADAM_STEP = 10

NN = (((1,), (0,)), ((), ()))
NT = (((1,), (1,)), ((), ()))
TN = (((0,), (0,)), ((), ()))
MESH = pl.DeviceIdType.MESH


def _tile(n, cap, align=LANE):
    best = None
    for t in range(align, min(n, cap) + 1, align):
        if n % t == 0:
            best = t
    return n if best is None else best


def _params(sem=None):
    return pltpu.CompilerParams(dimension_semantics=sem, vmem_limit_bytes=VMEM_LIMIT)


def _dot(a, b, dims=NN):
    return lax.dot_general(a.astype(MXU_DTYPE), b.astype(MXU_DTYPE), dims, preferred_element_type=F32)


def _sigmoid(x):
    return 0.5 * jnp.tanh(0.5 * x) + 0.5


_GELU_C = math.sqrt(2.0 / math.pi)
_GELU_K = 0.044715


def _gelu(x):
    return 0.5 * x * (1.0 + jnp.tanh(_GELU_C * (x + _GELU_K * x * x * x)))


def _gelu_grad(x):
    t = jnp.tanh(_GELU_C * (x + _GELU_K * x * x * x))
    return 0.5 * (1.0 + t) + 0.5 * x * (1.0 - t * t) * _GELU_C * (1.0 + 3.0 * _GELU_K * x * x)


def _rstd(x):
    return lax.rsqrt(jnp.mean(x * x, axis=-1, keepdims=True) + RMS_EPS)


def _norm_bwd(dxh, xh, r):
    return r * (dxh - xh * jnp.mean(dxh * xh, axis=-1, keepdims=True))


def _rowwise(name, fn, rows, tiled, vecs, outs, accs=()):
    seg = rows // N_SEG
    tm = _tile(seg, 256, SUB)
    nt, ntp = rows // tm, seg // tm
    n_t, n_v, n_o, n_a = len(tiled), len(vecs), len(outs), len(accs)

    def spec(width, perm):
        if perm:
            return pl.BlockSpec((tm, width), lambda i: (i % ntp, i // ntp))
        return pl.BlockSpec((tm, width), lambda i: (i, 0))

    args, in_specs = [], []
    for arr, perm in tiled:
        width = arr.shape[-1]
        args.append(arr.reshape(seg, N_SEG * width) if perm else arr)
        in_specs.append(spec(width, perm))
    for v in vecs:
        if isinstance(v, tuple):
            args.append(v[0])
            in_specs.append(pl.BlockSpec((v[1], v[0].shape[1]), lambda i: (i, 0)))
        else:
            args.append(v)
            in_specs.append(pl.BlockSpec(v.shape, lambda i: (0, 0)))
    out_shape, out_specs = [], []
    for width, dtype, perm in outs:
        out_shape.append(jax.ShapeDtypeStruct((seg, N_SEG * width) if perm else (rows, width), dtype))
        out_specs.append(spec(width, perm))
    for width in accs:
        out_shape.append(jax.ShapeDtypeStruct((SUB, width), F32))
        out_specs.append(pl.BlockSpec((SUB, width), lambda i: (0, 0)))

    def body(*refs):
        vals = [r[...] for r in refs[:n_t + n_v]]
        o_refs = refs[n_t + n_v:n_t + n_v + n_o]
        a_refs = refs[n_t + n_v + n_o:]
        o_vals, a_vals = fn(*vals)
        for r, v in zip(o_refs, o_vals):
            r[...] = v.astype(r.dtype)
        if n_a:
            @pl.when(pl.program_id(0) == 0)
            def _():
                for r in a_refs:
                    r[...] = jnp.zeros_like(r)
            for r, v in zip(a_refs, a_vals):
                r[...] += v.reshape(tm // SUB, SUB, v.shape[-1]).sum(axis=0)

    res = pl.pallas_call(
        body, name=name, grid=(nt,), in_specs=in_specs, out_specs=out_specs, out_shape=out_shape,
        compiler_params=_params(("arbitrary",)),
    )(*args)
    res = list(res)
    for k, (width, _, perm) in enumerate(outs):
        if perm:
            res[k] = res[k].reshape(rows, width)
    return res[:n_o], [jnp.sum(a, axis=0, keepdims=True) for a in res[n_o:]]


def _prenorm(x, gam, shift, scale1):
    r = _rstd(x)
    return (x * r) * gam * scale1 + shift


def _prenorm_bwd(du, x, gam, scale1):
    r = _rstd(x)
    xh = x * r
    dxn = du * scale1
    dx = _norm_bwd(dxn * gam, xh, r)
    return dx, du, du * (xh * gam), dxn * xh


def _postnorm_bwd(dxo, y, gam, gate):
    r = _rstd(y)
    yh = y * r
    dyn = dxo * gate
    dy = _norm_bwd(dyn * gam, yh, r)
    return dy, dxo * (yh * gam), dyn * yh


def _matmul(name, a, b, dims, grid, a_spec, b_spec, o_spec, out_shape, out_dtype, acc_shape,
            pair=None, ride=None):
    nk = grid[2]
    ins = list(a if pair and pair[0] == "a" else [a]) + list(b if pair and pair[0] == "b" else [b])
    specs = list(a_spec if pair and pair[0] == "a" else [a_spec]) + list(b_spec if pair and pair[0] == "b" else [b_spec])
    n_in = len(ins)
    bufs, copies = ride if ride else ((), None)
    n_ride = len(bufs)

    def body(*refs):
        in_refs = refs[:n_in]
        o_ref = refs[n_in + n_ride]
        ride_refs = refs[n_in + n_ride + 1:n_in + 2 * n_ride + 1]
        scratch = refs[n_in + 2 * n_ride + 1:]
        pid = [pl.program_id(ax) for ax in range(3)]
        if n_ride:
            send_sems, recv_sems = scratch[-2:]

            @pl.when((pid[0] == 0) & (pid[1] == 0) & (pid[2] == 0))
            def _():
                for cp in copies(ride_refs, ride_refs, send_sems, recv_sems):
                    cp.start()

        vals = [r[...] for r in in_refs]
        if pair:
            first = pid[pair[1]] < pair[2]
            picked = jnp.where(first, vals[0], vals[1]) if pair[0] == "a" else jnp.where(first, vals[1], vals[2])
            vals = [picked, vals[2]] if pair[0] == "a" else [vals[0], picked]
        part = _dot(vals[0], vals[1], dims)
        if nk == 1:
            o_ref[...] = part.astype(o_ref.dtype)
        else:
            acc_ref = scratch[0]

            @pl.when(pid[2] == 0)
            def _():
                acc_ref[...] = part

            @pl.when(pid[2] > 0)
            def _():
                acc_ref[...] += part

            @pl.when(pid[2] == nk - 1)
            def _():
                o_ref[...] = acc_ref[...].astype(o_ref.dtype)

        if n_ride:
            @pl.when((pid[0] == grid[0] - 1) & (pid[1] == grid[1] - 1) & (pid[2] == nk - 1))
            def _():
                for cp in copies(ride_refs, ride_refs, send_sems, recv_sems):
                    cp.wait()

    scratch_shapes = [] if nk == 1 else [pltpu.VMEM(acc_shape, F32)]
    out_shapes = [jax.ShapeDtypeStruct(out_shape, out_dtype)]
    if not n_ride:
        return pl.pallas_call(
            body, name=name, grid=grid, in_specs=specs, out_specs=[o_spec], out_shape=out_shapes,
            scratch_shapes=scratch_shapes, compiler_params=_params(("parallel", "parallel", "arbitrary")),
        )(*ins)[0]
    n_sem = 3 * n_ride
    res = pl.pallas_call(
        body, name=name, grid=grid, in_specs=specs + [ANY_SPEC] * n_ride,
        out_specs=[o_spec] + [ANY_SPEC] * n_ride,
        out_shape=out_shapes + [jax.ShapeDtypeStruct(x.shape, x.dtype) for x in bufs],
        input_output_aliases={n_in + i: 1 + i for i in range(n_ride)},
        scratch_shapes=scratch_shapes + [pltpu.SemaphoreType.DMA((n_sem,)), pltpu.SemaphoreType.DMA((n_sem,))],
        compiler_params=pltpu.CompilerParams(
            dimension_semantics=("arbitrary",) * 3, vmem_limit_bytes=VMEM_LIMIT, has_side_effects=True),
    )(*ins, *bufs)
    return res[0], res[1:]


def _mm_cols(name, a, w4, layer, out_dtype, ride=None):
    m, k = a.shape
    ns = w4.shape[-1]
    tm, tn = _tile(m, 1024), _tile(ns, 1536)
    nps = ns // tn
    return _matmul(
        name, a, w4, NN, (m // tm, N_CHIP * nps, 1),
        pl.BlockSpec((tm, k), lambda i, n, kk: (i, 0)),
        pl.BlockSpec((None, None, k, tn), lambda i, n, kk: (n // nps, layer, 0, n % nps)),
        pl.BlockSpec((tm, tn), lambda i, n, kk: (i, n)),
        (m, N_CHIP * ns), out_dtype, None, ride=ride)


def _halves_specs(g, rows_blk, cols_blk, tiles_half, row_of, col_of):
    if not isinstance(g, tuple):
        return g, pl.BlockSpec((rows_blk, cols_blk), lambda *p: (row_of(p), col_of(p)))
    left = pl.BlockSpec((rows_blk, cols_blk), lambda *p: (row_of(p), jnp.minimum(col_of(p), tiles_half - 1)))
    right = pl.BlockSpec((rows_blk, cols_blk), lambda *p: (row_of(p), jnp.maximum(col_of(p) - tiles_half, 0)))
    return g, (left, right)


def _mm_cols_nt(name, g, w4, layer, out_dtype):
    m = (g[0] if isinstance(g, tuple) else g).shape[0]
    k, ns = w4.shape[-2:]
    tm, tk = _tile(m, 1024), _tile(ns, 1536)
    kps = ns // tk
    half = N_CHIP * kps // 2
    g, g_spec = _halves_specs(g, tm, tk, half, lambda p: p[0], lambda p: p[2])
    return _matmul(
        name, g, w4, NT, (m // tm, 1, N_CHIP * kps), g_spec,
        pl.BlockSpec((None, None, k, tk), lambda i, n, kk: (kk // kps, layer, 0, kk % kps)),
        pl.BlockSpec((tm, k), lambda i, n, kk: (i, 0)),
        (m, k), out_dtype, (tm, k), pair=("a", 2, half) if isinstance(g, tuple) else None)


def _mm_cols_tn(name, a, g, ns, out_dtype):
    m, k = a.shape
    tkm, tmk, tn = _tile(m, 2048), _tile(k, 1024), _tile(ns, 1536)
    nps = ns // tn
    half = N_CHIP * nps // 2
    g, g_spec = _halves_specs(g, tkm, tn, half, lambda p: p[2], lambda p: p[1])
    return _matmul(
        name, a, g, TN, (k // tmk, N_CHIP * nps, m // tkm),
        pl.BlockSpec((tkm, tmk), lambda i, n, kk: (kk, i)), g_spec,
        pl.BlockSpec((None, tmk, tn), lambda i, n, kk: (n // nps, i, n % nps)),
        (N_CHIP, k, ns), out_dtype, (tmk, tn), pair=("b", 1, half) if isinstance(g, tuple) else None)


def _mm_rows(name, a, w4, layer, out_dtype):
    m = a.shape[0]
    rs, n = w4.shape[-2:]
    tm, tk = _tile(m, 1024), _tile(rs, 1536)
    kps = rs // tk
    return _matmul(
        name, a, w4, NN, (m // tm, 1, N_CHIP * kps),
        pl.BlockSpec((tm, tk), lambda i, j, kk: (i, kk)),
        pl.BlockSpec((None, None, tk, n), lambda i, j, kk: (kk // kps, layer, kk % kps, 0)),
        pl.BlockSpec((tm, n), lambda i, j, kk: (i, 0)),
        (m, n), out_dtype, (tm, n))


def _mm_rows_nt(name, g, w4, layer, out_dtype):
    m, n = g.shape
    rs = w4.shape[-2]
    tm, tn = _tile(m, 1024), _tile(rs, 1536)
    nps = rs // tn
    return _matmul(
        name, g, w4, NT, (m // tm, N_CHIP * nps, 1),
        pl.BlockSpec((tm, n), lambda i, j, kk: (i, 0)),
        pl.BlockSpec((None, None, tn, n), lambda i, j, kk: (j // nps, layer, j % nps, 0)),
        pl.BlockSpec((tm, tn), lambda i, j, kk: (i, j)),
        (m, N_CHIP * rs), out_dtype, None)


def _mm_rows_tn(name, a, g, rs, out_dtype):
    m = a.shape[0]
    n = g.shape[1]
    tkm, tmr, tn = _tile(m, 2048), _tile(rs, 1536), _tile(n, 1024)
    mps = rs // tmr
    return _matmul(
        name, a, g, TN, (N_CHIP * mps, n // tn, m // tkm),
        pl.BlockSpec((tkm, tmr), lambda i, j, kk: (kk, i)),
        pl.BlockSpec((tkm, tn), lambda i, j, kk: (kk, j)),
        pl.BlockSpec((None, tmr, tn), lambda i, j, kk: (i // mps, i % mps, j)),
        (N_CHIP, rs, n), out_dtype, (tmr, tn))


def _mm_grp(name, a, w, dims, out_dtype):
    m = a.shape[0]
    ng, ch = w.shape[:2]
    tm = _tile(m, 1024)
    return _matmul(
        name, a, w, dims, (m // tm, ng, 1),
        pl.BlockSpec((tm, ch), lambda i, g, kk: (i, g)),
        pl.BlockSpec((None, ch, ch), lambda i, g, kk: (g, 0, 0)),
        pl.BlockSpec((tm, ch), lambda i, g, kk: (i, g)),
        (m, ng * ch), out_dtype, None)


def _mm_grp_tn(name, a, g, ng, out_dtype):
    m = a.shape[0]
    ch = a.shape[1] // ng
    tk = _tile(m, 2048)
    return _matmul(
        name, a, g, TN, (ng, 1, m // tk),
        pl.BlockSpec((tk, ch), lambda i, j, kk: (kk, i)),
        pl.BlockSpec((tk, ch), lambda i, j, kk: (kk, i)),
        pl.BlockSpec((None, ch, ch), lambda i, j, kk: (i, 0, 0)),
        (ng, ch, ch), out_dtype, (ch, ch))


def _view2d(a):
    return a.reshape(-1, a.shape[-1])


def _elementwise(name, fn, ins, out_dtypes, ride=None):
    r, c = ins[0].shape
    lanes = -(-c // LANE) * LANE
    tr = _tile(r, max(PACK, ELEMWISE_BLOCK // lanes), PACK)
    spec = pl.BlockSpec((tr, c), lambda i: (i, 0))
    n_in, n_out, nsteps = len(ins), len(out_dtypes), r // tr
    sources, bufs, copies, n_sem = ride if ride else ((), (), None, 0)
    n_src, n_buf = len(sources), len(bufs)

    def body(*refs):
        if ride:
            src_refs = refs[n_in:n_in + n_src]
            buf_refs = refs[n_in + n_src + n_buf + n_out:n_in + n_src + 2 * n_buf + n_out]
            send_sems, recv_sems = refs[-2:]

            @pl.when(pl.program_id(0) == 0)
            def _():
                for cp in copies(src_refs, buf_refs, send_sems, recv_sems):
                    cp.start()

        outs = fn(*[x[...] for x in refs[:n_in]])
        first_out = n_in + n_src + n_buf
        for o_ref, o in zip(refs[first_out:first_out + n_out], outs):
            o_ref[...] = o.astype(o_ref.dtype)

        if ride:
            @pl.when(pl.program_id(0) == nsteps - 1)
            def _():
                for cp in copies(src_refs, buf_refs, send_sems, recv_sems):
                    cp.wait()

    out_shape = [jax.ShapeDtypeStruct((r, c), d) for d in out_dtypes]
    if not ride:
        return pl.pallas_call(
            body, name=name, grid=(nsteps,), in_specs=[spec] * n_in, out_specs=[spec] * n_out,
            out_shape=out_shape, compiler_params=_params(("parallel",)),
        )(*ins)
    return pl.pallas_call(
        body, name=name, grid=(nsteps,),
        in_specs=[spec] * n_in + [ANY_SPEC] * (n_src + n_buf),
        out_specs=[spec] * n_out + [ANY_SPEC] * n_buf,
        out_shape=out_shape + [jax.ShapeDtypeStruct(b.shape, b.dtype) for b in bufs],
        input_output_aliases={n_in + n_src + i: n_out + i for i in range(n_buf)},
        scratch_shapes=[pltpu.SemaphoreType.DMA((n_sem,)), pltpu.SemaphoreType.DMA((n_sem,))],
        compiler_params=pltpu.CompilerParams(
            dimension_semantics=("arbitrary",), vmem_limit_bytes=VMEM_LIMIT, has_side_effects=True),
    )(*ins, *sources, *bufs)


def _adamw_math(w, g, m, v):
    m = ADAM_B1 * m + (1.0 - ADAM_B1) * g
    v = ADAM_B2 * v + (1.0 - ADAM_B2) * (g * g)
    m_hat = m / (1.0 - ADAM_B1 ** ADAM_STEP)
    v_hat = v / (1.0 - ADAM_B2 ** ADAM_STEP)
    delta = -ADAM_LR * (m_hat / (jnp.sqrt(v_hat) + ADAM_EPS) + ADAM_WD * w)
    return delta, m, v


def _adamw(name, w, g, m, v, ride=None):
    shape = w.shape
    outs = _elementwise(name, _adamw_math, [_view2d(w), _view2d(g), _view2d(m), _view2d(v)], [F32, F32, F32], ride)
    return [o.reshape(shape) for o in outs[:3]] + list(outs[3:])


def _coords():
    return lax.axis_index("x"), lax.axis_index("y"), lax.axis_index("c")


def _peer(x, y, c, k):
    return (x ^ (k >> 2), y ^ ((k >> 1) & 1), c ^ (k & 1))


def _all_gather8(name, block):
    r = block.shape[0]

    def body(x_ref, out_ref, send_sems, recv_sems):
        x, y, c = _coords()
        me = 4 * x + 2 * y + c
        out_ref[me] = x_ref[...]
        copies = []
        for k in range(1, N_DEV):
            cp = pltpu.make_async_remote_copy(
                src_ref=x_ref, dst_ref=out_ref.at[me], send_sem=send_sems.at[k], recv_sem=recv_sems.at[k],
                device_id=_peer(x, y, c, k), device_id_type=MESH)
            cp.start()
            copies.append(cp)
        for cp in copies:
            cp.wait()

    return pl.pallas_call(
        body, name=name,
        in_specs=[pl.BlockSpec(memory_space=pltpu.VMEM)], out_specs=pl.BlockSpec(memory_space=pltpu.VMEM),
        out_shape=jax.ShapeDtypeStruct((N_DEV, r, LANE), F32),
        scratch_shapes=[pltpu.SemaphoreType.DMA((N_DEV,)), pltpu.SemaphoreType.DMA((N_DEV,))],
        compiler_params=pltpu.CompilerParams(vmem_limit_bytes=VMEM_LIMIT),
    )(block)


def _scatter8_copies(sources, bufs, send_sems, recv_sems):
    x, y, c = _coords()
    out = []
    for k in range(1, N_DEV):
        px, py, pc = _peer(x, y, c, k)
        out.append(pltpu.make_async_remote_copy(
            src_ref=sources[0].at[4 * px + 2 * py + pc], dst_ref=bufs[0].at[k], send_sem=send_sems.at[k],
            recv_sem=recv_sems.at[k], device_id=(px, py, pc), device_id_type=MESH))
    return out


def _gather8_copies(sources, bufs, send_sems, recv_sems):
    x, y, c = _coords()
    mine = bufs[0].at[4 * x + 2 * y + c]
    return [pltpu.make_async_remote_copy(
        src_ref=mine, dst_ref=mine, send_sem=send_sems.at[k], recv_sem=recv_sems.at[k],
        device_id=_peer(x, y, c, k), device_id_type=MESH) for k in range(1, N_DEV)]


def _sum_slots(name, parts, got, me):
    _, r, lanes = parts.shape

    def body(me_ref, own_ref, got_ref, o_ref):
        acc = own_ref[...]
        for k in range(1, N_DEV):
            acc = acc + got_ref[k]
        o_ref[...] = acc

    return pl.pallas_call(
        body, name=name,
        grid_spec=pltpu.PrefetchScalarGridSpec(
            num_scalar_prefetch=1, grid=(1,),
            in_specs=[pl.BlockSpec((None, r, lanes), lambda i, m: (m[0], 0, 0)),
                      pl.BlockSpec((N_DEV, r, lanes), lambda i, m: (0, 0, 0))],
            out_specs=pl.BlockSpec((None, r, lanes), lambda i, m: (m[0], 0, 0))),
        out_shape=jax.ShapeDtypeStruct(parts.shape, parts.dtype), compiler_params=_params(("arbitrary",)),
    )(me, parts, got)


ANY_SPEC = pl.BlockSpec(memory_space=pl.ANY)
COMM_BLOCK_BYTES = 4 << 20


def _staged_call(name, body, core, grid, in_specs, ins, out_shape, scratch, aliases=None):
    return pl.pallas_call(
        body, name=name,
        grid_spec=pltpu.PrefetchScalarGridSpec(
            num_scalar_prefetch=1, grid=grid, in_specs=in_specs, out_specs=ANY_SPEC, scratch_shapes=scratch),
        out_shape=out_shape, input_output_aliases=aliases or {},
        compiler_params=pltpu.CompilerParams(
            dimension_semantics=("arbitrary",) * len(grid), vmem_limit_bytes=VMEM_LIMIT, has_side_effects=True),
    )(core, *ins)


def _rows_tile(rows, cols, itemsize):
    return _tile(rows, max(PACK, COMM_BLOCK_BYTES // (cols * itemsize)), PACK)


def _chip_peer(x, y, c, k):
    return (x ^ (k >> 1), y ^ (k & 1), c)


def _cast_own_half(name, w, pos):
    r, cols = w.shape
    h = r // 2
    tr = _tile(h, max(PACK, ELEMWISE_BLOCK // cols), PACK)
    nb = h // tr

    def body(p_ref, w_ref, o_ref):
        o_ref[...] = w_ref[...].astype(o_ref.dtype)

    return pl.pallas_call(
        body, name=name,
        grid_spec=pltpu.PrefetchScalarGridSpec(
            num_scalar_prefetch=1, grid=(nb,),
            in_specs=[pl.BlockSpec((tr, cols), lambda j, p: (p[0] * nb + j, 0))],
            out_specs=pl.BlockSpec((tr, cols), lambda j, p: ((2 * p[1] + p[0]) * nb + j, 0))),
        out_shape=jax.ShapeDtypeStruct((N_CHIP * r, cols), MXU_DTYPE),
        compiler_params=_params(("parallel",)),
    )(pos, w)


def _gather_ici_plan(shard_rows):
    def copies(refs, _, send_sems, recv_sems):
        x, y, c = _coords()
        out = []
        for i, (ref, r) in enumerate(zip(refs, shard_rows)):
            h = r // 2
            mine = ref.at[pl.ds(pl.multiple_of((2 * x + y) * r + c * h, PACK), h)]
            for k in (1, 2, 3):
                n = 3 * i + k - 1
                out.append(pltpu.make_async_remote_copy(
                    src_ref=mine, dst_ref=mine, send_sem=send_sems.at[n], recv_sem=recv_sems.at[n],
                    device_id=_chip_peer(x, y, c, k), device_id_type=MESH))
        return out

    return copies


def _swap_gathered(name, part, core, r):
    cols = part.shape[1]
    h = r // 2
    full = jax.ShapeDtypeStruct(part.shape, part.dtype)
    tr2 = _rows_tile(h, cols, 2)
    nb2 = h // tr2

    def swap_body(c_ref, mine_ref, full_ref, send_sem, recv_sem):
        s, j = pl.program_id(0), pl.program_id(1)
        x, y, c = _coords()
        dst = full_ref.at[pl.ds(pl.multiple_of(s * r + c * h + j * tr2, PACK), tr2)]
        cp = pltpu.make_async_remote_copy(
            src_ref=mine_ref, dst_ref=dst, send_sem=send_sem, recv_sem=recv_sem,
            device_id=(x, y, 1 - c), device_id_type=MESH)
        cp.start()
        cp.wait_send()

        @pl.when((s == N_CHIP - 1) & (j == nb2 - 1))
        def _():
            landed = full_ref.at[pl.ds(0, N_CHIP * h)]
            pltpu.make_async_remote_copy(
                src_ref=landed, dst_ref=landed, send_sem=send_sem, recv_sem=recv_sem,
                device_id=(x, y, 1 - c), device_id_type=MESH).wait_recv()

    return _staged_call(
        name + "_d2d", swap_body, core, (N_CHIP, nb2),
        [pl.BlockSpec((tr2, cols), lambda s, j, c_ref: ((2 * s + c_ref[0]) * nb2 + j, 0))], [part], full,
        [pltpu.SemaphoreType.DMA(()), pltpu.SemaphoreType.DMA(())], aliases={1: 0}).reshape(N_CHIP, r, cols)


def _pair_sum(name, g, core):
    n, r, cols = g.shape
    h = r // 2
    tr = _rows_tile(h, cols, g.dtype.itemsize)
    nb = h // tr
    half = jax.ShapeDtypeStruct((n, h, cols), g.dtype)

    def send_body(c_ref, g_ref, got_ref, send_sem, recv_sem):
        s, j = pl.program_id(0), pl.program_id(1)
        x, y, c = _coords()
        dst = got_ref.at[pl.ds(pl.multiple_of(s * h + j * tr, PACK), tr)]
        cp = pltpu.make_async_remote_copy(
            src_ref=g_ref, dst_ref=dst, send_sem=send_sem, recv_sem=recv_sem,
            device_id=(x, y, 1 - c), device_id_type=MESH)
        cp.start()
        cp.wait_send()

        @pl.when((s == n - 1) & (j == nb - 1))
        def _():
            pltpu.make_async_remote_copy(
                src_ref=got_ref, dst_ref=got_ref, send_sem=send_sem, recv_sem=recv_sem,
                device_id=(x, y, 1 - c), device_id_type=MESH).wait_recv()

    got = _staged_call(
        name + "_send", send_body, core, (n, nb),
        [pl.BlockSpec((tr, cols), lambda s, j, c_ref: ((2 * s + 1 - c_ref[0]) * nb + j, 0))],
        [g.reshape(n * r, cols)], jax.ShapeDtypeStruct((n * h, cols), g.dtype),
        [pltpu.SemaphoreType.DMA(()), pltpu.SemaphoreType.DMA(())]).reshape(n, h, cols)

    def add_body(c_ref, own_ref, got_ref, o_ref):
        o_ref[...] = (own_ref[...].astype(F32) + got_ref[...].astype(F32)).astype(o_ref.dtype)

    blk = pl.BlockSpec((None, tr, cols), lambda s, j, c_ref: (s, j, 0))
    return pl.pallas_call(
        add_body, name=name + "_add",
        grid_spec=pltpu.PrefetchScalarGridSpec(
            num_scalar_prefetch=1, grid=(n, nb),
            in_specs=[pl.BlockSpec((None, tr, cols), lambda s, j, c_ref: (s, c_ref[0] * nb + j, 0)), blk],
            out_specs=blk),
        out_shape=half, compiler_params=_params(("parallel", "parallel")),
    )(core, g, got)


def _chip_exchange_plan(pairs):
    shapes = [jax.ShapeDtypeStruct((N_CHIP - 1,) + p.shape[1:], p.dtype) for p in pairs]

    def copies(in_refs, out_refs, send_sems, recv_sems):
        x, y, c = _coords()
        out = []
        for i, (src, dst) in enumerate(zip(in_refs, out_refs)):
            for k in (1, 2, 3):
                px, py, pc = _chip_peer(x, y, c, k)
                n = 3 * i + k - 1
                out.append(pltpu.make_async_remote_copy(
                    src_ref=src.at[2 * px + py], dst_ref=dst.at[k - 1], send_sem=send_sems.at[n],
                    recv_sem=recv_sems.at[n], device_id=(px, py, pc), device_id_type=MESH))
        return out

    return shapes, copies


def _sum_and_swap(name, pair, got, pos, into, total_rows, base):
    _, h, cols = pair.shape
    tr = _rows_tile(h, cols, 4)
    nb = h // tr

    def body(p_ref, own_ref, a_ref, b_ref, c_ref, *rest):
        full_ref, red_ref, send_sem, recv_sem, local_sem = rest[-5:]
        j = pl.program_id(0)
        x, y, c = _coords()
        red_ref[...] = ((own_ref[...].astype(F32) + a_ref[...].astype(F32)) + b_ref[...].astype(F32)) + c_ref[...].astype(F32)
        dst = full_ref.at[pl.ds(pl.multiple_of(base + c * h + j * tr, SUB), tr)]
        local = pltpu.make_async_copy(red_ref, dst, local_sem)
        remote = pltpu.make_async_remote_copy(
            src_ref=red_ref, dst_ref=dst, send_sem=send_sem, recv_sem=recv_sem,
            device_id=(x, y, 1 - c), device_id_type=MESH)
        local.start()
        remote.start()
        remote.wait_send()
        local.wait()

        @pl.when(j == nb - 1)
        def _():
            landed = full_ref.at[pl.ds(0, h)]
            pltpu.make_async_remote_copy(
                src_ref=landed, dst_ref=landed, send_sem=send_sem, recv_sem=recv_sem,
                device_id=(x, y, 1 - c), device_id_type=MESH).wait_recv()

    in_specs = [pl.BlockSpec((None, tr, cols), lambda j, p: (p[1], j, 0))]
    in_specs += [pl.BlockSpec((None, tr, cols), (lambda j, p, k=k: (k, j, 0))) for k in range(N_CHIP - 1)]
    ins = [pair, got, got, got]
    aliases = None
    if into is not None:
        in_specs.append(ANY_SPEC)
        ins.append(into)
        aliases = {5: 0}
    return _staged_call(
        name, body, pos, (nb,), in_specs, ins, jax.ShapeDtypeStruct((total_rows, cols), F32),
        [pltpu.VMEM((tr, cols), F32)] + [pltpu.SemaphoreType.DMA(())] * 3, aliases)


def _cmul(ar, ai, br, bi):
    return ar * br - ai * bi, ar * bi + ai * br


def _cpow(ar, ai, n):
    rr, ri = jnp.ones_like(ar), jnp.zeros_like(ai)
    br, bi = ar, ai
    while n:
        if n & 1:
            rr, ri = _cmul(rr, ri, br, bi)
        br, bi = _cmul(br, bi, br, bi)
        n >>= 1
    return rr, ri


def _tile_rows(t):
    if isinstance(t, int):
        return pl.ds(t * SUB, SUB)
    return pl.ds(pl.multiple_of(t * SUB, SUB), SUB)


SCAN_UNROLL = 4


def _unrolled_loop(n, body, carry):
    trips = n // SCAN_UNROLL

    def trip(o, c):
        for k in range(SCAN_UNROLL):
            c = body(o * SCAN_UNROLL + k, c)
        return c

    carry = lax.fori_loop(0, trips, trip, carry)
    for i in range(trips * SCAN_UNROLL, n):
        carry = body(i, carry)
    return carry


def _scan_setup(buf, steps, ar, ai, h0r, h0i, rev):
    def total(i, carry):
        sr, si = carry
        rows = _tile_rows(steps - 1 - i if rev else i)
        pr, pi = _cmul(ar, ai, sr, si)
        return pr + buf[rows, 0:S5_H], pi + buf[rows, S5_H:S5_NS]

    zero = jnp.zeros((SUB, S5_H), F32)
    tot_r, tot_i = _unrolled_loop(steps, total, (zero, zero))
    pw_r, pw_i = _cpow(ar[0:1], ai[0:1], steps)
    row = lax.broadcasted_iota(jnp.int32, (SUB, S5_H), 0)
    cur_r, cur_i = h0r, h0i
    init_r, init_i = zero, zero
    for s in (range(N_SEG - 1, -1, -1) if rev else range(N_SEG)):
        init_r = jnp.where(row == s, cur_r, init_r)
        init_i = jnp.where(row == s, cur_i, init_i)
        nr, ni = _cmul(pw_r, pw_i, cur_r, cur_i)
        cur_r, cur_i = nr + tot_r[s:s + 1], ni + tot_i[s:s + 1]
    return init_r, init_i, cur_r, cur_i


def _scan(buf, steps, ar, ai, h0r, h0i, rev, store):
    init_r, init_i, fin_r, fin_i = _scan_setup(buf, steps, ar, ai, h0r, h0i, rev)
    if store:
        def step(i, carry):
            hr, hi = carry
            rows = _tile_rows(steps - 1 - i if rev else i)
            pr, pi = _cmul(ar, ai, hr, hi)
            hr, hi = pr + buf[rows, 0:S5_H], pi + buf[rows, S5_H:S5_NS]
            buf[rows, 0:S5_H] = hr
            buf[rows, S5_H:S5_NS] = hi
            return hr, hi

        _unrolled_loop(steps, step, (init_r, init_i))
    return fin_r, fin_i


def _adjoint_scan(gbuf, hbuf, steps, ar, ai, l0r, l0i, hin_r, hin_i, rev):
    ci = -ai
    arev = not rev
    init_r, init_i, fin_r, fin_i = _scan_setup(gbuf, steps, ar, ci, l0r, l0i, arev)
    zero = jnp.zeros((SUB, S5_H), F32)

    def update(t, hp_r, hp_i, carry):
        lr, li, dr, di = carry
        rows = _tile_rows(t)
        pr, pi = _cmul(ar, ci, lr, li)
        lr, li = pr + gbuf[rows, 0:S5_H], pi + gbuf[rows, S5_H:S5_NS]
        gbuf[rows, 0:S5_H] = lr
        gbuf[rows, S5_H:S5_NS] = li
        return lr, li, dr + lr * hp_r + li * hp_i, di + li * hp_r - lr * hp_i

    def step(i, carry):
        t = steps - 1 - i if rev is False else i
        prev = _tile_rows(t - 1 if rev is False else t + 1)
        return update(t, hbuf[prev, 0:S5_H], hbuf[prev, S5_H:S5_NS], carry)

    carry = _unrolled_loop(steps - 1, step, (init_r, init_i, zero, zero))
    row = lax.broadcasted_iota(jnp.int32, (SUB, S5_H), 0)
    if rev:
        last, edge, shift, t = _tile_rows(0), N_SEG - 1, SUB - 1, steps - 1
    else:
        last, edge, shift, t = _tile_rows(steps - 1), 0, 1, 0
    hp_r = jnp.where(row == edge, hin_r, pltpu.roll(hbuf[last, 0:S5_H], shift, 0))
    hp_i = jnp.where(row == edge, hin_i, pltpu.roll(hbuf[last, S5_H:S5_NS], shift, 0))
    _, _, dr, di = update(t, hp_r, hp_i, carry)
    return fin_r, fin_i, dr, di


def _s5_chunk(rows):
    return _tile(rows, 512, PACK)


def _s5_forward(u, uc, bblk, cblk, atile, dsk, ride_bufs, ride_copies):
    rows, d = u.shape
    rows_c = uc.shape[0]
    nj = d // S5_CB
    steps, steps_c = rows // N_SEG, rows_c // N_SEG
    rc = _s5_chunk(rows)
    n_ride = len(ride_bufs)
    n_sem = 3 * n_ride

    def body(u_ref, uc_ref, b_ref, c_ref, a_ref, d_ref, *rest):
        y_ref = rest[n_ride]
        ride = rest[n_ride + 1:2 * n_ride + 1]
        buf, bufc, send_sems, recv_sems = rest[2 * n_ride + 1:]

        @pl.when(pl.program_id(0) == 0)
        def _():
            for cp in ride_copies(ride, ride, send_sems, recv_sems):
                cp.start()

        zero = jnp.zeros((1, S5_H), F32)
        for dr in (0, 1):
            rev = dr == 1
            ar, ai = a_ref[dr, :, 0:S5_H], a_ref[dr, :, S5_H:S5_NS]
            bm, cm = b_ref[dr].astype(MXU_DTYPE), c_ref[dr].astype(MXU_DTYPE)
            bufc[...] = _dot(uc_ref[...], bm)
            fin_r, fin_i = _scan(bufc, steps_c, ar, ai, zero, zero, rev, False)

            def project(r, _):
                rs = pl.ds(pl.multiple_of(r * rc, rc), rc)
                buf[rs, :] = _dot(u_ref[rs, :], bm)
                return 0

            lax.fori_loop(0, rows // rc, project, 0)
            _scan(buf, steps, ar, ai, fin_r, fin_i, rev, True)

            def readout(r, _):
                rs = pl.ds(pl.multiple_of(r * rc, rc), rc)
                yv = _dot(buf[rs, :], cm)
                if dr == 0:
                    y_ref[rs, :] = u_ref[rs, :].astype(F32) * d_ref[...] + yv
                else:
                    y_ref[rs, :] += yv
                return 0

            lax.fori_loop(0, rows // rc, readout, 0)

        @pl.when(pl.program_id(0) == nj - 1)
        def _():
            for cp in ride_copies(ride, ride, send_sems, recv_sems):
                cp.wait()

    res = pl.pallas_call(
        body, name="s5_forward", grid=(nj,),
        in_specs=[
            pl.BlockSpec((rows, S5_CB), lambda j: (0, j)),
            pl.BlockSpec((rows_c, S5_CB), lambda j: (0, j)),
            pl.BlockSpec((2, None, S5_CB, S5_NS), lambda j: (0, j, 0, 0)),
            pl.BlockSpec((2, None, S5_NS, S5_CB), lambda j: (0, j, 0, 0)),
            pl.BlockSpec((2, None, SUB, S5_NS), lambda j: (0, j, 0, 0)),
            pl.BlockSpec((1, S5_CB), lambda j: (0, j)),
        ] + [ANY_SPEC] * n_ride,
        out_specs=[pl.BlockSpec((rows, S5_CB), lambda j: (0, j))] + [ANY_SPEC] * n_ride,
        out_shape=[jax.ShapeDtypeStruct((rows, d), F32)]
        + [jax.ShapeDtypeStruct(b.shape, b.dtype) for b in ride_bufs],
        input_output_aliases={6 + i: 1 + i for i in range(n_ride)},
        scratch_shapes=[pltpu.VMEM((rows, S5_NS), F32), pltpu.VMEM((rows_c, S5_NS), F32),
                        pltpu.SemaphoreType.DMA((n_sem,)), pltpu.SemaphoreType.DMA((n_sem,))],
        compiler_params=pltpu.CompilerParams(
            dimension_semantics=("arbitrary",), vmem_limit_bytes=VMEM_LIMIT, has_side_effects=True),
    )(u, uc, bblk, cblk, atile, dsk, *ride_bufs)
    return res[0], res[1:]


def _s5_backward(u, uc, dy, bblk, cblk, atile, dsk, ride_ins, ride_shapes, ride_copies):
    rows, d = u.shape
    rows_c = uc.shape[0]
    nj = d // S5_CB
    steps, steps_c = rows // N_SEG, rows_c // N_SEG
    rc = _s5_chunk(rows)
    nchunk = rows // rc
    n_ride = len(ride_ins)
    n_sem = 3 * n_ride

    def body(u_ref, uc_ref, dy_ref, b_ref, c_ref, a_ref, d_ref, *rest):
        ride_in = rest[:n_ride]
        du_ref, duc_ref, db_ref, dc_ref, da_ref, dd_ref = rest[n_ride:n_ride + 6]
        ride_out = rest[n_ride + 6:2 * n_ride + 6]
        hbuf, gbuf, hcbuf, gcbuf, send_sems, recv_sems = rest[2 * n_ride + 6:]

        @pl.when(pl.program_id(0) == 0)
        def _():
            for cp in ride_copies(ride_in, ride_out, send_sems, recv_sems):
                cp.start()

        zero = jnp.zeros((1, S5_H), F32)
        db_ref[...] = jnp.zeros_like(db_ref)
        dc_ref[...] = jnp.zeros_like(dc_ref)
        dd_ref[...] = jnp.zeros_like(dd_ref)
        for dr in (0, 1):
            rev = dr == 1
            ar, ai = a_ref[dr, :, 0:S5_H], a_ref[dr, :, S5_H:S5_NS]
            bm, cm = b_ref[dr].astype(MXU_DTYPE), c_ref[dr].astype(MXU_DTYPE)
            hcbuf[...] = _dot(uc_ref[...], bm)
            hin_r, hin_i = _scan(hcbuf, steps_c, ar, ai, zero, zero, rev, True)

            def project(r, _):
                rs = pl.ds(pl.multiple_of(r * rc, rc), rc)
                hbuf[rs, :] = _dot(u_ref[rs, :], bm)
                return 0

            lax.fori_loop(0, nchunk, project, 0)
            _scan(hbuf, steps, ar, ai, hin_r, hin_i, rev, True)

            def readout_bwd(r, _):
                rs = pl.ds(pl.multiple_of(r * rc, rc), rc)
                dyv = dy_ref[rs, :]
                gbuf[rs, :] = _dot(dyv, cm, NT)
                dc_ref[dr] += _dot(hbuf[rs, :], dyv, TN)
                return 0

            lax.fori_loop(0, nchunk, readout_bwd, 0)
            lf_r, lf_i, dar, dai = _adjoint_scan(gbuf, hbuf, steps, ar, ai, zero, zero, hin_r, hin_i, rev)
            gcbuf[...] = jnp.zeros_like(gcbuf)
            _, _, dar_c, dai_c = _adjoint_scan(gcbuf, hcbuf, steps_c, ar, ai, lf_r, lf_i, zero, zero, rev)
            da_ref[dr, :, 0:S5_H] = dar + dar_c
            da_ref[dr, :, S5_H:S5_NS] = dai + dai_c

            def project_bwd(r, _):
                rs = pl.ds(pl.multiple_of(r * rc, rc), rc)
                lam = gbuf[rs, :]
                uv = u_ref[rs, :]
                part = _dot(lam, bm, NT)
                db_ref[dr] += _dot(uv, lam, TN)
                if dr == 0:
                    dyv = dy_ref[rs, :].astype(F32)
                    du_ref[rs, :] = part + dyv * d_ref[...]
                    dd_ref[...] += (dyv * uv.astype(F32)).reshape(rc // SUB, SUB, S5_CB).sum(axis=0)
                else:
                    du_ref[rs, :] += part
                return 0

            lax.fori_loop(0, nchunk, project_bwd, 0)
            lam_c = gcbuf[...]
            part_c = _dot(lam_c, bm, NT)
            db_ref[dr] += _dot(uc_ref[...], lam_c, TN)
            if dr == 0:
                duc_ref[...] = part_c
            else:
                duc_ref[...] += part_c

        @pl.when(pl.program_id(0) == nj - 1)
        def _():
            for cp in ride_copies(ride_in, ride_out, send_sems, recv_sems):
                cp.wait()

    blk = lambda r: pl.BlockSpec((r, S5_CB), lambda j: (0, j))
    res = pl.pallas_call(
        body, name="s5_backward", grid=(nj,),
        in_specs=[
            blk(rows), blk(rows_c), blk(rows),
            pl.BlockSpec((2, None, S5_CB, S5_NS), lambda j: (0, j, 0, 0)),
            pl.BlockSpec((2, None, S5_NS, S5_CB), lambda j: (0, j, 0, 0)),
            pl.BlockSpec((2, None, SUB, S5_NS), lambda j: (0, j, 0, 0)),
            pl.BlockSpec((1, S5_CB), lambda j: (0, j)),
        ] + [ANY_SPEC] * n_ride,
        out_specs=[
            blk(rows), blk(rows_c),
            pl.BlockSpec((2, None, S5_CB, S5_NS), lambda j: (0, j, 0, 0)),
            pl.BlockSpec((2, None, S5_NS, S5_CB), lambda j: (0, j, 0, 0)),
            pl.BlockSpec((2, None, SUB, S5_NS), lambda j: (0, j, 0, 0)),
            pl.BlockSpec((SUB, S5_CB), lambda j: (0, j)),
        ] + [ANY_SPEC] * n_ride,
        out_shape=[
            jax.ShapeDtypeStruct((rows, d), F32), jax.ShapeDtypeStruct((rows_c, d), F32),
            jax.ShapeDtypeStruct(bblk.shape, F32), jax.ShapeDtypeStruct(cblk.shape, F32),
            jax.ShapeDtypeStruct(atile.shape, F32), jax.ShapeDtypeStruct((SUB, d), F32),
        ] + list(ride_shapes),
        scratch_shapes=[pltpu.VMEM((rows, S5_NS), F32), pltpu.VMEM((rows, S5_NS), F32),
                        pltpu.VMEM((rows_c, S5_NS), F32), pltpu.VMEM((rows_c, S5_NS), F32),
                        pltpu.SemaphoreType.DMA((n_sem,)), pltpu.SemaphoreType.DMA((n_sem,))],
        compiler_params=pltpu.CompilerParams(
            dimension_semantics=("arbitrary",), vmem_limit_bytes=VMEM_LIMIT, has_side_effects=True),
    )(u, uc, dy, bblk, cblk, atile, dsk, *ride_ins)
    return res[:6], res[6:]


def _s5_prepare(lam_re, lam_im, log_step, b_re, b_im, c_re, c_im):
    nd, g, p = lam_re.shape
    gb = S5_CB // S5_GROUP
    nj = g // gb
    dt = jnp.exp(log_step)[..., None]
    mag = jnp.exp(lam_re * dt)
    abar_re = mag * jnp.cos(lam_im * dt)
    abar_im = mag * jnp.sin(lam_im * dt)
    nr, ni = abar_re - 1.0, abar_im
    den = lam_re * lam_re + lam_im * lam_im
    fr = (nr * lam_re + ni * lam_im) / den
    fi = (ni * lam_re - nr * lam_im) / den
    bbar_re = fr[..., None] * b_re - fi[..., None] * b_im
    bbar_im = fr[..., None] * b_im + fi[..., None] * b_re
    eye = jnp.eye(gb, dtype=bool)

    def diag_in(w):
        w = w.reshape(nd, nj, gb, p, S5_GROUP).transpose(0, 1, 2, 4, 3)
        w = jnp.where(eye[None, None, :, None, :, None], w[:, :, :, :, None, :], 0.0)
        return w.reshape(nd, nj, gb * S5_GROUP, gb * p)

    def diag_out(w):
        w = w.reshape(nd, nj, gb, S5_GROUP, p).transpose(0, 1, 2, 4, 3)
        w = jnp.where(eye[None, None, :, None, :, None], w[:, :, :, :, None, :], 0.0)
        return w.reshape(nd, nj, gb * p, gb * S5_GROUP)

    bblk = jnp.concatenate([diag_in(bbar_re), diag_in(bbar_im)], axis=-1)
    cblk = jnp.concatenate([diag_out(c_re), -diag_out(c_im)], axis=-2)
    a2 = jnp.concatenate([abar_re.reshape(nd, nj, gb * p), abar_im.reshape(nd, nj, gb * p)], axis=-1)
    atile = jnp.broadcast_to(a2[:, :, None, :], (nd, nj, SUB, 2 * gb * p))
    return bblk, cblk, atile


def _shifted(x, prev_row, next_row):
    n = x.shape[0]
    row = lax.broadcasted_iota(jnp.int32, (SUB, x.shape[1]), 0)
    xp = pltpu.roll(x, 1, 0)
    xn = pltpu.roll(x, n - 1, 0)
    xp = jnp.concatenate([jnp.where(row == 0, prev_row, xp[:SUB]), xp[SUB:]], axis=0)
    xn = jnp.concatenate([xn[:n - SUB], jnp.where(row == SUB - 1, next_row, xn[n - SUB:])], axis=0)
    return xp, xn


def _edge_rows(ref, r0, n, total, group):
    lo = pl.multiple_of(jnp.maximum(r0 - group, 0), group)
    hi = pl.multiple_of(jnp.minimum(r0 + n, total - group), group)
    prev_row = ref[pl.ds(lo, group), :].astype(F32)[group - 1:group] * (r0 > 0).astype(F32)
    next_row = ref[pl.ds(hi, group), :].astype(F32)[0:1] * (r0 + n < total).astype(F32)
    return prev_row, next_row


def _conv_rows(ref, r0, n, total, w_ref, b_ref):
    x = ref[pl.ds(r0, n), :].astype(F32)
    xp, xn = _shifted(x, *_edge_rows(ref, r0, n, total, PACK))
    hc = w_ref[0:1, :] * xp + w_ref[1:2, :] * x + w_ref[2:3, :] * xn + b_ref[...]
    return hc, xp, x, xn


def _conv_specs(rows, f, tc):
    nt = f // tc
    val = lambda r: pl.BlockSpec((r, tc), lambda j: (0, j))
    gate = lambda r: pl.BlockSpec((r, tc), lambda j: (0, j + nt))
    return val, gate


def _conv_swiglu_fwd(name, h, cw, cb):
    rows, f2 = h.shape
    f = f2 // 2
    tc = _tile(f, 256)
    rc = _tile(rows, 256, PACK)
    val, gate = _conv_specs(rows, f, tc)

    def body(hv_ref, hg_ref, wv_ref, wg_ref, bv_ref, bg_ref, a_ref):
        def chunk(r, _):
            r0 = pl.multiple_of(r * rc, rc)
            hv = _conv_rows(hv_ref, r0, rc, rows, wv_ref, bv_ref)[0]
            hg = _conv_rows(hg_ref, r0, rc, rows, wg_ref, bg_ref)[0]
            a_ref[pl.ds(r0, rc), :] = (hg * _sigmoid(hg) * hv).astype(a_ref.dtype)
            return 0

        lax.fori_loop(0, rows // rc, chunk, 0)

    return pl.pallas_call(
        body, name=name, grid=(f // tc,),
        in_specs=[val(rows), gate(rows), val(3), gate(3), val(1), gate(1)],
        out_specs=val(rows), out_shape=jax.ShapeDtypeStruct((rows, f), ACT_DTYPE),
        compiler_params=_params(("parallel",)),
    )(h, h, cw, cw, cb, cb)


def _conv_swiglu_bwd(name, da, h, cw, cb):
    rows, f2 = h.shape
    f = f2 // 2
    tc = _tile(f, 256)
    rc = _tile(rows, 256, PACK)
    val, gate = _conv_specs(rows, f, tc)

    def body(da_ref, hv_ref, hg_ref, wv_ref, wg_ref, bv_ref, bg_ref,
             dhv_ref, dhg_ref, dwv_ref, dwg_ref, dbv_ref, dbg_ref, sv, sg):
        def first(r, carry):
            r0 = pl.multiple_of(r * rc, rc)
            rs = pl.ds(r0, rc)
            hv, vp, vx, vn = _conv_rows(hv_ref, r0, rc, rows, wv_ref, bv_ref)
            hg, gp, gx, gn = _conv_rows(hg_ref, r0, rc, rows, wg_ref, bg_ref)
            d = da_ref[rs, :].astype(F32)
            s = _sigmoid(hg)
            dv = d * (hg * s)
            dg = d * hv * (s * (1.0 + hg * (1.0 - s)))
            sv[rs, :] = dv
            sg[rs, :] = dg
            sums = [dv * vp, dv * vx, dv * vn, dv, dg * gp, dg * gx, dg * gn, dg]
            return tuple(c + jnp.sum(x, axis=0, keepdims=True) for c, x in zip(carry, sums))

        zero = jnp.zeros((1, tc), F32)
        acc = lax.fori_loop(0, rows // rc, first, (zero,) * 8)
        for k in range(3):
            dwv_ref[k:k + 1, :] = acc[k]
            dwg_ref[k:k + 1, :] = acc[4 + k]
        dbv_ref[...] = acc[3]
        dbg_ref[...] = acc[7]

        def second(r, _):
            r0 = pl.multiple_of(r * rc, rc)
            rs = pl.ds(r0, rc)
            for s_ref, w_ref, o_ref in ((sv, wv_ref, dhv_ref), (sg, wg_ref, dhg_ref)):
                x = s_ref[rs, :]
                xp, xn = _shifted(x, *_edge_rows(s_ref, r0, rc, rows, SUB))
                o_ref[rs, :] = (w_ref[0:1, :] * xn + w_ref[1:2, :] * x + w_ref[2:3, :] * xp).astype(o_ref.dtype)
            return 0

        lax.fori_loop(0, rows // rc, second, 0)

    res = pl.pallas_call(
        body, name=name, grid=(f // tc,),
        in_specs=[val(rows), val(rows), gate(rows), val(3), gate(3), val(1), gate(1)],
        out_specs=[val(rows), val(rows), val(3), val(3), val(1), val(1)],
        out_shape=[jax.ShapeDtypeStruct((rows, f), ACT_DTYPE)] * 2
        + [jax.ShapeDtypeStruct((3, f), F32)] * 2 + [jax.ShapeDtypeStruct((1, f), F32)] * 2,
        scratch_shapes=[pltpu.VMEM((rows, tc), F32), pltpu.VMEM((rows, tc), F32)],
        compiler_params=_params(("parallel",)),
    )(da, h, h, cw, cw, cb, cb)
    return res


def _pool_band(name, x, transpose, out_dtype):
    rows, d = x.shape
    ng = len(POOL_WINDOWS)
    ch = d // ng
    tm = _tile(rows, 256, PACK)
    win = tm + 2 * POOL_HALO
    assert win <= rows

    def body(x_ref, o_ref):
        half = lax.shift_left(jnp.int32(1), pl.program_id(0))
        t0 = pl.program_id(1) * tm
        ws = pl.multiple_of(jnp.clip(t0 - POOL_HALO, 0, rows - win), PACK)
        i = t0 + lax.broadcasted_iota(jnp.int32, (tm, win), 0)
        j = ws + lax.broadcasted_iota(jnp.int32, (tm, win), 1)

        def inv_count(t):
            hi = jnp.minimum(t + half - 1, rows - 1)
            lo = jnp.maximum(t - half, 0)
            return 1.0 / (hi - lo + 1).astype(F32)

        xw = x_ref[pl.ds(ws, win), :]
        xt = x_ref[pl.ds(pl.multiple_of(t0, PACK), tm), :].astype(F32)
        if transpose:
            band = (j - half <= i) & (i <= j + half - 1)
            tw = ws + lax.broadcasted_iota(jnp.int32, (win, 1), 0)
            o = _dot(band.astype(MXU_DTYPE), xw.astype(F32) * inv_count(tw)) - xt
        else:
            band = (i - half <= j) & (j <= i + half - 1)
            tt = t0 + lax.broadcasted_iota(jnp.int32, (tm, 1), 0)
            o = _dot(band.astype(MXU_DTYPE), xw) * inv_count(tt) - xt
        o_ref[...] = o.astype(o_ref.dtype)

    return pl.pallas_call(
        body, name=name, grid=(ng, rows // tm),
        in_specs=[pl.BlockSpec((rows, ch), lambda g, i: (0, g))],
        out_specs=pl.BlockSpec((tm, ch), lambda g, i: (i, g)),
        out_shape=jax.ShapeDtypeStruct((rows, d), out_dtype),
        compiler_params=_params(("parallel", "arbitrary")),
    )(x)


def _ada_forward(c16, ada_w, ada_b):
    nl, d, cols = ada_w.shape
    tn = _tile(cols, 512)

    def body(c_ref, w_ref, b_ref, o_ref):
        cv = c_ref[...]
        o_ref[...] = _dot(cv * _sigmoid(cv), w_ref[...]) + b_ref[...]

    return pl.pallas_call(
        body, name="ada_forward", grid=(nl, cols // tn),
        in_specs=[pl.BlockSpec((16, d), lambda l, n: (0, 0)),
                  pl.BlockSpec((None, d, tn), lambda l, n: (l, 0, n)),
                  pl.BlockSpec((None, 1, tn), lambda l, n: (l, 0, n))],
        out_specs=pl.BlockSpec((None, 16, tn), lambda l, n: (l, 0, n)),
        out_shape=jax.ShapeDtypeStruct((nl, 16, cols), F32),
        compiler_params=_params(("parallel", "parallel")),
    )(c16, ada_w, ada_b)


def _ada_backward(c16, dmod, ada_w):
    nl, d, cols = ada_w.shape
    tn = _tile(cols, 512)
    nn = cols // tn

    def body(c_ref, g_ref, w_ref, gw_ref, gc_ref):
        cv = c_ref[...]
        s = _sigmoid(cv)
        gv = g_ref[...]
        gw_ref[...] = _dot(cv * s, gv, TN)
        dcond = _dot(gv, w_ref[...], NT)
        row = lax.broadcasted_iota(jnp.int32, dcond.shape, 0)
        dctx = jnp.sum(jnp.where(row >= 8, dcond * (s * (1.0 + cv * (1.0 - s))), 0.0), axis=0, keepdims=True)

        @pl.when((pl.program_id(0) == 0) & (pl.program_id(1) == 0))
        def _():
            gc_ref[...] = jnp.zeros_like(gc_ref)

        gc_ref[...] += dctx

    return pl.pallas_call(
        body, name="ada_backward", grid=(nl, nn),
        in_specs=[pl.BlockSpec((16, d), lambda l, n: (0, 0)),
                  pl.BlockSpec((None, 16, tn), lambda l, n: (l, 0, n)),
                  pl.BlockSpec((None, d, tn), lambda l, n: (l, 0, n))],
        out_specs=[pl.BlockSpec((None, d, tn), lambda l, n: (l, 0, n)),
                   pl.BlockSpec((1, d), lambda l, n: (0, 0))],
        out_shape=[jax.ShapeDtypeStruct((nl, d, cols), F32), jax.ShapeDtypeStruct((1, d), F32)],
        compiler_params=_params(("arbitrary", "arbitrary")),
    )(c16, dmod, ada_w)


def _row_sum16(name, a):
    nl, _, w = a.shape
    tn = _tile(w, 4096)

    def body(a_ref, o_ref):
        o_ref[...] = jnp.sum(a_ref[...], axis=0, keepdims=True)

    return pl.pallas_call(
        body, name=name, grid=(nl, w // tn),
        in_specs=[pl.BlockSpec((None, 16, tn), lambda l, n: (l, 0, n))],
        out_specs=pl.BlockSpec((None, 1, tn), lambda l, n: (l, 0, n)),
        out_shape=jax.ShapeDtypeStruct((nl, 1, w), F32),
        compiler_params=_params(("parallel", "parallel")),
    )(a)


def _pack(arrays, row_align):
    flat = jnp.concatenate([a.reshape(-1).astype(F32) for a in arrays])
    quantum = row_align * LANE
    padded = -(-flat.shape[0] // quantum) * quantum
    return jnp.pad(flat, (0, padded - flat.shape[0])).reshape(-1, LANE)


def _unpack(packed, shapes):
    flat = packed.reshape(-1)
    out, off = [], 0
    for s in shapes:
        n = math.prod(s)
        out.append(flat[off:off + n].reshape(s))
        off += n
    return out


def _grid_pos_tables(n_tokens, dim):
    quarter = dim // 4
    omega = 1.0 / (POS_BASE ** (jnp.arange(quarter, dtype=F32) / quarter))

    def enc(n):
        ang = jnp.arange(n, dtype=F32).reshape(-1, 1) * omega[None, :]
        return jnp.concatenate([jnp.sin(ang), jnp.cos(ang)], axis=-1)

    return enc(n_tokens // GRID_W), enc(GRID_W)


def _ffn_forward(tag, v, x_in, up4, dn4, layer, cw, cb):
    h = _mm_cols(f"ffn_up_{tag}", v, up4, layer, ACT_DTYPE)
    a = _conv_swiglu_fwd(f"ffn_conv_{tag}", h, cw, cb)
    f = _mm_rows(f"ffn_down_{tag}", a, dn4, layer, F32)
    return h, a, f


def _ffn_backward(tag, dx_out, fb, x_mid, v, h, a, up4, dn4, layer, cw, cb, gate5, gam3, gam2, scale4p1):
    def post(dxo, f, gam, gate):
        dy, dgate, dgam = _postnorm_bwd(dxo, f.astype(F32), gam, gate)
        return (dy,), (dgate, dgam)

    (df,), (dgate5, dgam3) = _rowwise(f"ffn_post_bwd_{tag}", post, dx_out.shape[0],
                                      [(dx_out, False), (fb, False)], [gam3, gate5],
                                      [(dx_out.shape[1], ACT_DTYPE, False)], [dx_out.shape[1]] * 2)
    da = _mm_rows_nt(f"ffn_down_dx_{tag}", df, dn4, layer, ACT_DTYPE)
    g_dn = _mm_rows_tn(f"ffn_down_dw_{tag}", a, df, dn4.shape[-2], ACT_DTYPE)
    dhv, dhg, dwv, dwg, dbv, dbg = _conv_swiglu_bwd(f"ffn_conv_bwd_{tag}", da, h, cw, cb)
    dh = (dhv, dhg)
    dcw = jnp.concatenate([dwv, dwg], axis=1)
    dcb = jnp.concatenate([dbv, dbg], axis=1)
    dv = _mm_cols_nt(f"ffn_up_dx_{tag}", dh, up4, layer, F32)
    g_up = _mm_cols_tn(f"ffn_up_dw_{tag}", v, dh, up4.shape[-1], ACT_DTYPE)

    def pre(dvv, x, dxo, gam, s1):
        dx, dshift, dscale, dgam = _prenorm_bwd(dvv, x, gam, s1)
        return (dxo + dx,), (dshift, dscale, dgam)

    d = dx_out.shape[1]
    (dx_mid,), (dshift3, dscale4, dgam2) = _rowwise(
        f"ffn_pre_bwd_{tag}", pre, dx_out.shape[0], [(dv, False), (x_mid, False), (dx_out, False)],
        [gam2, scale4p1], [(d, F32, False)], [d] * 3)
    return dx_mid, g_up, g_dn, dcw, dcb, (dshift3, dscale4, dgate5), (dgam2, dgam3)


def kernel(x, c, ctx, c_ctx, ada_w, ada_b, norm_g, s5_lam_re, s5_lam_im, s5_log_step, s5_b_re, s5_b_im, s5_c_re, s5_c_im, s5_d, s5_glu_w, pool_w, pool_scale, ffn_up, ffn_conv, ffn_conv_b, ffn_down, loss_target, m_c_ctx, m_ada_w, m_ada_b, m_norm_g, m_s5_lam_re, m_s5_lam_im, m_s5_log_step, m_s5_b_re, m_s5_b_im, m_s5_c_re, m_s5_c_im, m_s5_d, m_s5_glu_w, m_pool_w, m_pool_scale, m_ffn_up, m_ffn_conv, m_ffn_conv_b, m_ffn_down, v_c_ctx, v_ada_w, v_ada_b, v_norm_g, v_s5_lam_re, v_s5_lam_im, v_s5_log_step, v_s5_b_re, v_s5_b_im, v_s5_c_re, v_s5_c_im, v_s5_d, v_s5_glu_w, v_pool_w, v_pool_scale, v_ffn_up, v_ffn_conv, v_ffn_conv_b, v_ffn_down):
    ix, iy, ic = _coords()
    chip = 2 * ix + iy
    me = 2 * chip + ic
    _, rows, d = x.shape
    rows_c = ctx.shape[1]
    nl = ada_w.shape[0]
    assert nl == 2 and s5_glu_w.shape[0] == 1 and pool_w.shape[0] == 1
    a_cols = ada_w.shape[2]
    f2s = ffn_up.shape[2]
    f2 = N_CHIP * f2s
    ds = d // N_CHIP
    ng = len(POOL_WINDOWS)
    ch = d // ng
    ps = pool_w.shape[2]

    core = jnp.reshape(ic, (1,)).astype(jnp.int32)
    chip_id = jnp.reshape(chip, (1,)).astype(jnp.int32)
    pos = jnp.concatenate([core, chip_id])
    shards = {"up": _view2d(ffn_up), "down": _view2d(ffn_down), "glu": s5_glu_w[0], "pool": _view2d(pool_w[0])}
    gather_bufs = [_cast_own_half(f"cast_{n}", w, pos) for n, w in shards.items()]

    c_all = _all_gather8("gather_cond", _pack([c], SUB))
    c_all = c_all.reshape(N_DEV, -1)[:, :d]
    c16 = jnp.concatenate([c_all, jnp.broadcast_to(c_ctx[None, :], (N_DEV, d))], axis=0)
    ada_b_mine = lax.dynamic_slice_in_dim(ada_b, chip * a_cols, a_cols, axis=1)
    mod_part = _ada_forward(c16, ada_w, ada_b_mine[:, None, :])
    narrow_shapes = [mod_part.shape, norm_g.shape, pool_scale.shape, ffn_conv.shape]
    narrow = _all_gather8("gather_narrow", _pack([mod_part, norm_g, pool_scale, ffn_conv], SUB))
    per_chip = [_unpack(narrow[2 * s], narrow_shapes) for s in range(N_CHIP)]
    mod_all = jnp.concatenate([p[0] for p in per_chip], axis=-1)
    gam = jnp.concatenate([p[1] for p in per_chip], axis=-1)
    pscale = jnp.concatenate([p[2] for p in per_chip], axis=-1)
    conv_w = jnp.concatenate([p[3] for p in per_chip], axis=-1)
    mod_mine = lax.dynamic_index_in_dim(mod_all, me, axis=1, keepdims=False)
    mod_ctx = mod_all[0, N_DEV]

    def mods(vec):
        s0, s1, g2, s3, s4, g5 = [vec[k * d:(k + 1) * d][None, :] for k in range(N_MOD)]
        return s0, 1.0 + s1, g2, s3, 1.0 + s4, g5

    m0, m1, mc = mods(mod_mine[0]), mods(mod_mine[1]), mods(mod_ctx)
    gains = [[gam[l, k][None, :] for k in range(4)] for l in range(nl)]
    conv_b = ffn_conv_b[:, None, :]

    row_tab, col_tab = _grid_pos_tables(rows, d)
    per_tile = _tile(rows // N_SEG, 256, SUB) // GRID_W
    rep = SUB // per_tile
    row_tab = jnp.repeat(row_tab, rep, axis=0)

    def init(xv, rt, ct, g0, shift, s1):
        pe_r = jnp.concatenate(
            [jnp.broadcast_to(rt[q * rep:q * rep + 1], (GRID_W, d // 2)) for q in range(per_tile)], axis=0)
        pe_c = jnp.concatenate([ct] * per_tile, axis=0)
        x0 = xv + jnp.concatenate([pe_r, pe_c], axis=1)
        return (x0, _prenorm(x0, g0, shift, s1)), ()

    (x0, u), _ = _rowwise("init", init, rows, [(x[0], False)],
                          [(row_tab, SUB), col_tab, gains[0][0], m0[0], m0[1]],
                          [(d, F32, False), (d, ACT_DTYPE, True)])
    (uc,), _ = _rowwise("ctx_prenorm", lambda cv, g0, shift, s1: ((_prenorm(cv, g0, shift, s1),), ()),
                        rows_c, [(ctx[0], False)], [gains[0][0], mc[0], mc[1]], [(d, ACT_DTYPE, True)])
    s5_params = (s5_lam_re[0], s5_lam_im[0], s5_log_step[0], s5_b_re[0], s5_b_im[0], s5_c_re[0], s5_c_im[0])
    (bblk, cblk, atile), s5_vjp = jax.vjp(_s5_prepare, *s5_params)
    buf_up, buf_dn, buf_glu, buf_pool = gather_bufs
    rows_of = {n: w.shape[0] for n, w in shards.items()}
    y, (buf_up, buf_glu) = _s5_forward(u, uc, bblk, cblk, atile, s5_d, [buf_up, buf_glu],
                                       _gather_ici_plan([rows_of["up"], rows_of["glu"]]))
    up4 = _swap_gathered("gather_up", buf_up, core, rows_of["up"]).reshape((N_CHIP,) + ffn_up.shape)
    glu4 = _swap_gathered("gather_glu", buf_glu, core, rows_of["glu"])
    (z,), _ = _rowwise("gelu", lambda yv: ((_gelu(yv),), ()), rows, [(y, True)], [], [(d, ACT_DTYPE, False)])
    zz = _mm_cols("glu_proj", z, glu4[:, None], 0, ACT_DTYPE)

    def glu_out(zzv, xv, gate2, g1, g2, shift3, s4):
        zf = zzv.astype(F32)
        o = zf[:, :d] * _sigmoid(zf[:, d:])
        x1 = xv + gate2 * (o * _rstd(o) * g1)
        return (x1, _prenorm(x1, g2, shift3, s4)), ()

    (x1, v0), _ = _rowwise("glu_resid", glu_out, rows, [(zz, False), (x0, False)],
                           [m0[2], gains[0][1], gains[0][2], m0[3], m0[4]], [(d, F32, False), (d, ACT_DTYPE, False)])
    h0, (buf_dn, buf_pool) = _mm_cols(
        "ffn_up_l0", v0, up4, 0, ACT_DTYPE,
        ride=([buf_dn, buf_pool], _gather_ici_plan([rows_of["down"], rows_of["pool"]])))
    dn4 = _swap_gathered("gather_down", buf_dn, core, rows_of["down"]).reshape((N_CHIP,) + ffn_down.shape)
    pool_full = _swap_gathered("gather_pool", buf_pool, core, rows_of["pool"])
    pool_full = pool_full.reshape(N_CHIP, ng, ps, ch).transpose(1, 0, 2, 3).reshape(ng, ch, ch)
    a0 = _conv_swiglu_fwd("ffn_conv_l0", h0, conv_w[0], conv_b[0])
    f0 = _mm_rows("ffn_down_l0", a0, dn4, 0, F32)

    def ffn_out(fv, xv, gate5, g3, g0n, shift0, s1):
        x2 = xv + gate5 * (fv * _rstd(fv) * g3)
        return (x2, _prenorm(x2, g0n, shift0, s1), fv), ()

    (x2, u1, fb0), _ = _rowwise("ffn_resid_l0", ffn_out, rows, [(f0, False), (x1, False)],
                                [m0[5], gains[0][3], gains[1][0], m1[0], m1[1]],
                                [(d, F32, False), (d, ACT_DTYPE, False), (d, ACT_DTYPE, False)])

    p1 = _pool_band("pool_band", u1, False, ACT_DTYPE)
    yr = _mm_grp("pool_proj", p1, pool_full, NN, F32)

    def pool_out(yv, xv, ps_, gate2, g1, g2, shift3, s4):
        o = yv * ps_
        x1n = xv + gate2 * (o * _rstd(o) * g1)
        return (x1n, _prenorm(x1n, g2, shift3, s4), yv), ()

    (x3, v1, yb), _ = _rowwise("pool_resid", pool_out, rows, [(yr, False), (x2, False)],
                               [pscale, m1[2], gains[1][1], gains[1][2], m1[3], m1[4]],
                               [(d, F32, False), (d, ACT_DTYPE, False), (d, ACT_DTYPE, False)])
    h1, a1, f1 = _ffn_forward("l1", v1, x3, up4, dn4, 1, conv_w[1], conv_b[1])

    def loss_head(fv, xv, tv, gate5, g3):
        err = xv + gate5 * (fv * _rstd(fv) * g3) - tv
        return (err * (1.0 / d), fv), (err * err,)

    (dx4, fb1), (sq,) = _rowwise("loss_head", loss_head, rows, [(f1, False), (x3, False), (loss_target[0], False)],
                                 [m1[5], gains[1][3]], [(d, F32, False), (d, ACT_DTYPE, False)], [d])
    loss = lax.psum(0.5 * jnp.sum(sq) / d, ("x", "y", "c"))

    dx3, g_up1, g_dn1, dcw1, dcb1, dmod_ffn1, (dgam12, dgam13) = _ffn_backward(
        "l1", dx4, fb1, x3, v1, h1, a1, up4, dn4, 1, conv_w[1], conv_b[1], m1[5], gains[1][3], gains[1][2], m1[4])

    def pool_post(dxo, yv, ps_, g1, gate2):
        yraw = yv.astype(F32)
        dy, dgate, dgam = _postnorm_bwd(dxo, yraw * ps_, g1, gate2)
        return (dy * ps_,), (dgate, dgam, dy * yraw)

    (dyr,), (dgate2_1, dgam11, dpscale) = _rowwise("pool_post_bwd", pool_post, rows, [(dx3, False), (yb, False)],
                                                   [pscale, gains[1][1], m1[2]], [(d, ACT_DTYPE, False)], [d] * 3)
    dp1 = _mm_grp("pool_proj_dx", dyr, pool_full, NT, ACT_DTYPE)
    g_pool = _mm_grp_tn("pool_proj_dw", p1, dyr, ng, ACT_DTYPE)
    du1 = _pool_band("pool_band_bwd", dp1, True, F32)

    def pre_bwd(duv, xv, dxo, g0, s1):
        dx, dshift, dscale, dgam = _prenorm_bwd(duv, xv, g0, s1)
        return (dxo + dx,), (dshift, dscale, dgam)

    (dx2,), (dshift0_1, dscale1_1, dgam10) = _rowwise(
        "pool_pre_bwd", pre_bwd, rows, [(du1, False), (x2, False), (dx3, False)],
        [gains[1][0], m1[1]], [(d, F32, False)], [d] * 3)

    dx1, g_up0, g_dn0, dcw0, dcb0, dmod_ffn0, (dgam02, dgam03) = _ffn_backward(
        "l0", dx2, fb0, x1, v0, h0, a0, up4, dn4, 0, conv_w[0], conv_b[0], m0[5], gains[0][3], gains[0][2], m0[4])

    def glu_post(dxo, zzv, g1, gate2):
        zf = zzv.astype(F32)
        val, s = zf[:, :d], _sigmoid(zf[:, d:])
        do, dgate, dgam = _postnorm_bwd(dxo, val * s, g1, gate2)
        return (jnp.concatenate([do * s, do * val * (s * (1.0 - s))], axis=1),), (dgate, dgam)

    (dzz,), (dgate2_0, dgam01) = _rowwise("glu_post_bwd", glu_post, rows, [(dx1, False), (zz, False)],
                                          [gains[0][1], m0[2]], [(2 * d, ACT_DTYPE, False)], [d] * 2)
    dz = _mm_cols_nt("glu_proj_dx", dzz, glu4[:, None], 0, F32)
    g_glu = _mm_cols_tn("glu_proj_dw", z, dzz, glu4.shape[-1], ACT_DTYPE)
    (dy,), _ = _rowwise("gelu_bwd", lambda dzv, yv: ((dzv * _gelu_grad(yv),), ()), rows,
                        [(dz, False), (y, True)], [], [(d, ACT_DTYPE, True)])
    g_pool4 = g_pool.reshape(ng, N_CHIP, ps, ch).transpose(1, 0, 2, 3).reshape(N_CHIP, ng * ps, ch)
    big = {"up0": g_up0, "up1": g_up1, "dn0": g_dn0, "dn1": g_dn1, "glu": g_glu, "pool": g_pool4}
    pairs = [_pair_sum(f"pair_{n}", g, core) for n, g in big.items()]
    ride_shapes, ride_copies = _chip_exchange_plan(pairs)
    (du0, duc, d_bblk, d_cblk, d_atile, d_dsk), others = _s5_backward(
        u, uc, dy, bblk, cblk, atile, s5_d, pairs, ride_shapes, ride_copies)
    (gx,), (dshift0_0, dscale1_0, dgam00) = _rowwise(
        "s5_pre_bwd", pre_bwd, rows, [(du0, True), (x0, False), (dx1, False)],
        [gains[0][0], m0[1]], [(d, F32, False)], [d] * 3)

    def ctx_bwd(duv, cv, g0, s1):
        _, dshift, dscale, dgam = _prenorm_bwd(duv, cv, g0, s1)
        return (), (dshift, dscale, dgam)

    _, (dshift_c, dscale_c, dgam00c) = _rowwise("ctx_pre_bwd", ctx_bwd, rows_c, [(duc, True), (ctx[0], False)],
                                                [gains[0][0], mc[1]], [], [d] * 3)
    g_s5 = s5_vjp((d_bblk, d_cblk, d_atile))

    zero_d = jnp.zeros((1, d), F32)
    dmod_lat = jnp.stack([
        jnp.concatenate([dshift0_0, dscale1_0, dgate2_0, *dmod_ffn0], axis=1),
        jnp.concatenate([dshift0_1, dscale1_1, dgate2_1, *dmod_ffn1], axis=1)])
    dmod_ctx = jnp.stack([jnp.concatenate([dshift_c, dscale_c] + [zero_d] * 4, axis=1),
                          jnp.zeros((1, N_MOD * d), F32)])
    dmod_shape = (nl, 2, N_MOD * d)
    dmod_all = _all_gather8("gather_dmod", _pack([jnp.concatenate([dmod_lat, dmod_ctx], axis=1)], SUB))
    dmod_all = jnp.stack([_unpack(dmod_all[k], [dmod_shape])[0] for k in range(N_DEV)])
    dmod16 = jnp.concatenate([dmod_all[:, :, 0], dmod_all[:, :, 1]], axis=0).transpose(1, 0, 2)
    dmod16_mine = lax.dynamic_slice_in_dim(dmod16, chip * a_cols, a_cols, axis=2)
    g_ada_w, g_cctx_part = _ada_backward(c16, dmod16_mine, ada_w)
    g_ada_b = _row_sum16("ada_bias_grad", dmod16)[:, 0]

    d_gam = jnp.stack([jnp.concatenate([dgam00 + dgam00c, dgam01, dgam02, dgam03], axis=0),
                       jnp.concatenate([dgam10, dgam11, dgam12, dgam13], axis=0)])
    small = [d_gam, 0.5 * g_cctx_part, *g_s5, jnp.sum(d_dsk, axis=0, keepdims=True), dpscale,
             jnp.stack([dcw0, dcw1]), jnp.stack([dcb0[0], dcb1[0]])]
    small_shapes = [s.shape for s in small]
    packed = _pack(small, N_DEV * SUB).reshape(N_DEV, -1, LANE)

    parts = dict(zip(big, zip(pairs, others)))
    d_rows, dn_rows = ffn_up.shape[1], ffn_down.shape[1]
    g_up = _sum_and_swap("reduce_up0", *parts["up0"], pos, None, nl * d_rows, 0)
    g_up = _sum_and_swap("reduce_up1", *parts["up1"], pos, g_up, nl * d_rows, d_rows).reshape(ffn_up.shape)
    g_dn = _sum_and_swap("reduce_dn0", *parts["dn0"], pos, None, nl * dn_rows, 0)
    g_dn = _sum_and_swap("reduce_dn1", *parts["dn1"], pos, g_dn, nl * dn_rows, dn_rows).reshape(ffn_down.shape)
    g_glu_f = _sum_and_swap("reduce_glu", *parts["glu"], pos, None, d, 0)
    g_pool_f = _sum_and_swap("reduce_pool", *parts["pool"], pos, None, ng * ps, 0)

    delta, new_m, new_v = {}, {}, {}
    delta["ada_w"], new_m["ada_w"], new_v["ada_w"], landed = _adamw(
        "adamw_ada_w", ada_w, g_ada_w, m_ada_w, v_ada_w,
        ride=([packed], [jnp.zeros(packed.shape, F32)], _scatter8_copies, N_DEV))
    mine = _sum_slots("reduce_small_sum", packed, landed, jnp.reshape(me, (1,)).astype(jnp.int32))
    delta["ffn_up"], new_m["ffn_up"], new_v["ffn_up"], summed = _adamw(
        "adamw_ffn_up", ffn_up, g_up, m_ffn_up, v_ffn_up, ride=([], [mine], _gather8_copies, N_DEV))
    (r_gam, r_cctx, r_lam_re, r_lam_im, r_log_step, r_b_re, r_b_im, r_c_re, r_c_im,
     r_dsk, r_pscale, r_conv, r_convb) = _unpack(summed.reshape(-1, LANE), small_shapes)
    g_norm = lax.dynamic_slice_in_dim(r_gam, chip * ds, ds, axis=2)
    g_pscale = lax.dynamic_slice_in_dim(r_pscale, chip * ds, ds, axis=1)
    g_conv = lax.dynamic_slice_in_dim(r_conv, chip * f2s, f2s, axis=2)

    grads = {
        "c_ctx": r_cctx[0], "ada_w": g_ada_w, "ada_b": g_ada_b, "norm_g": g_norm,
        "s5_lam_re": r_lam_re[None], "s5_lam_im": r_lam_im[None], "s5_log_step": r_log_step[None],
        "s5_b_re": r_b_re[None], "s5_b_im": r_b_im[None], "s5_c_re": r_c_re[None], "s5_c_im": r_c_im[None],
        "s5_d": r_dsk, "s5_glu_w": g_glu_f[None], "pool_w": g_pool_f.reshape(pool_w.shape),
        "pool_scale": g_pscale, "ffn_up": g_up, "ffn_conv": g_conv, "ffn_conv_b": r_convb, "ffn_down": g_dn,
    }
    weights = {
        "c_ctx": (c_ctx, m_c_ctx, v_c_ctx), "ada_w": (ada_w, m_ada_w, v_ada_w), "ada_b": (ada_b, m_ada_b, v_ada_b),
        "norm_g": (norm_g, m_norm_g, v_norm_g), "s5_lam_re": (s5_lam_re, m_s5_lam_re, v_s5_lam_re),
        "s5_lam_im": (s5_lam_im, m_s5_lam_im, v_s5_lam_im), "s5_log_step": (s5_log_step, m_s5_log_step, v_s5_log_step),
        "s5_b_re": (s5_b_re, m_s5_b_re, v_s5_b_re), "s5_b_im": (s5_b_im, m_s5_b_im, v_s5_b_im),
        "s5_c_re": (s5_c_re, m_s5_c_re, v_s5_c_re), "s5_c_im": (s5_c_im, m_s5_c_im, v_s5_c_im),
        "s5_d": (s5_d, m_s5_d, v_s5_d), "s5_glu_w": (s5_glu_w, m_s5_glu_w, v_s5_glu_w),
        "pool_w": (pool_w, m_pool_w, v_pool_w), "pool_scale": (pool_scale, m_pool_scale, v_pool_scale),
        "ffn_up": (ffn_up, m_ffn_up, v_ffn_up), "ffn_conv": (ffn_conv, m_ffn_conv, v_ffn_conv),
        "ffn_conv_b": (ffn_conv_b, m_ffn_conv_b, v_ffn_conv_b), "ffn_down": (ffn_down, m_ffn_down, v_ffn_down),
    }
    names = list(weights)
    large = ("ada_w", "s5_glu_w", "pool_w", "ffn_up", "ffn_down")
    for n in names:
        w, m, v = weights[n]
        if n in delta:
            continue
        if n in large:
            delta[n], new_m[n], new_v[n] = _adamw(f"adamw_{n}", w, grads[n], m, v)
        else:
            shape = w.shape
            view = (1, shape[0]) if w.ndim == 1 else shape
            res = _adamw(f"adamw_{n}", *[t.reshape(view) for t in (w, grads[n], m, v)])
            delta[n], new_m[n], new_v[n] = [t.reshape(shape) for t in res]

    return (loss, gx[None], *[grads[n] for n in names], *[delta[n] for n in names],
            *[new_m[n] for n in names], *[new_v[n] for n in names])
```

```python
import math

import jax
import jax.numpy as jnp
from jax import lax
from jax.experimental import pallas as pl
from jax.experimental.pallas import tpu as pltpu

F32 = jnp.float32
MXU_DTYPE = jnp.bfloat16
ACT_DTYPE = jnp.bfloat16

LANE = 128
SUB = 8
PACK = 16
VMEM_LIMIT = 56 * 1024 * 1024
ELEMWISE_BLOCK = 1 << 18

N_DEV = 8
N_CHIP = 4
N_SEG = SUB
S5_GROUP = 16
S5_STATE = 64
S5_CB = LANE
S5_H = (S5_CB // S5_GROUP) * S5_STATE
S5_NS = 2 * S5_H
POOL_WINDOWS = (2, 4, 8, 16)
POOL_HALO = 16
GRID_W = 64
POS_BASE = 10000.0
RMS_EPS = 1e-6
N_MOD = 6

ADAM_LR = 0.001
ADAM_B1 = 0.9
ADAM_B2 = 0.999
ADAM_EPS = 1e-08
ADAM_WD = 0.01
ADAM_STEP = 10

NN = (((1,), (0,)), ((), ()))
NT = (((1,), (1,)), ((), ()))
TN = (((0,), (0,)), ((), ()))
MESH = pl.DeviceIdType.MESH


def _tile(n, cap, align=LANE):
    best = None
    for t in range(align, min(n, cap) + 1, align):
        if n % t == 0:
            best = t
    return n if best is None else best


def _params(sem=None):
    return pltpu.CompilerParams(dimension_semantics=sem, vmem_limit_bytes=VMEM_LIMIT)


def _dot(a, b, dims=NN):
    return lax.dot_general(a.astype(MXU_DTYPE), b.astype(MXU_DTYPE), dims, preferred_element_type=F32)


def _sigmoid(x):
    return 0.5 * jnp.tanh(0.5 * x) + 0.5


_GELU_C = math.sqrt(2.0 / math.pi)
_GELU_K = 0.044715


def _gelu(x):
    return 0.5 * x * (1.0 + jnp.tanh(_GELU_C * (x + _GELU_K * x * x * x)))


def _gelu_grad(x):
    t = jnp.tanh(_GELU_C * (x + _GELU_K * x * x * x))
    return 0.5 * (1.0 + t) + 0.5 * x * (1.0 - t * t) * _GELU_C * (1.0 + 3.0 * _GELU_K * x * x)


def _rstd(x):
    return lax.rsqrt(jnp.mean(x * x, axis=-1, keepdims=True) + RMS_EPS)


def _norm_bwd(dxh, xh, r):
    return r * (dxh - xh * jnp.mean(dxh * xh, axis=-1, keepdims=True))


def _rowwise(name, fn, rows, tiled, vecs, outs, accs=()):
    seg = rows // N_SEG
    tm = _tile(seg, 256, SUB)
    nt, ntp = rows // tm, seg // tm
    n_t, n_v, n_o, n_a = len(tiled), len(vecs), len(outs), len(accs)

    def spec(width, perm):
        if perm:
            return pl.BlockSpec((tm, width), lambda i: (i % ntp, i // ntp))
        return pl.BlockSpec((tm, width), lambda i: (i, 0))

    args, in_specs = [], []
    for arr, perm in tiled:
        width = arr.shape[-1]
        args.append(arr.reshape(seg, N_SEG * width) if perm else arr)
        in_specs.append(spec(width, perm))
    for v in vecs:
        if isinstance(v, tuple):
            args.append(v[0])
            in_specs.append(pl.BlockSpec((v[1], v[0].shape[1]), lambda i: (i, 0)))
        else:
            args.append(v)
            in_specs.append(pl.BlockSpec(v.shape, lambda i: (0, 0)))
    out_shape, out_specs = [], []
    for width, dtype, perm in outs:
        out_shape.append(jax.ShapeDtypeStruct((seg, N_SEG * width) if perm else (rows, width), dtype))
        out_specs.append(spec(width, perm))
    for width in accs:
        out_shape.append(jax.ShapeDtypeStruct((SUB, width), F32))
        out_specs.append(pl.BlockSpec((SUB, width), lambda i: (0, 0)))

    def body(*refs):
        vals = [r[...] for r in refs[:n_t + n_v]]
        o_refs = refs[n_t + n_v:n_t + n_v + n_o]
        a_refs = refs[n_t + n_v + n_o:]
        o_vals, a_vals = fn(*vals)
        for r, v in zip(o_refs, o_vals):
            r[...] = v.astype(r.dtype)
        if n_a:
            @pl.when(pl.program_id(0) == 0)
            def _():
                for r in a_refs:
                    r[...] = jnp.zeros_like(r)
            for r, v in zip(a_refs, a_vals):
                r[...] += v.reshape(tm // SUB, SUB, v.shape[-1]).sum(axis=0)

    res = pl.pallas_call(
        body, name=name, grid=(nt,), in_specs=in_specs, out_specs=out_specs, out_shape=out_shape,
        compiler_params=_params(("arbitrary",)),
    )(*args)
    res = list(res)
    for k, (width, _, perm) in enumerate(outs):
        if perm:
            res[k] = res[k].reshape(rows, width)
    return res[:n_o], [jnp.sum(a, axis=0, keepdims=True) for a in res[n_o:]]


def _prenorm(x, gam, shift, scale1):
    r = _rstd(x)
    return (x * r) * gam * scale1 + shift


def _prenorm_bwd(du, x, gam, scale1):
    r = _rstd(x)
    xh = x * r
    dxn = du * scale1
    dx = _norm_bwd(dxn * gam, xh, r)
    return dx, du, du * (xh * gam), dxn * xh


def _postnorm_bwd(dxo, y, gam, gate):
    r = _rstd(y)
    yh = y * r
    dyn = dxo * gate
    dy = _norm_bwd(dyn * gam, yh, r)
    return dy, dxo * (yh * gam), dyn * yh


def _matmul(name, a, b, dims, grid, a_spec, b_spec, o_spec, out_shape, out_dtype, acc_shape,
            pair=None, ride=None):
    nk = grid[2]
    ins = list(a if pair and pair[0] == "a" else [a]) + list(b if pair and pair[0] == "b" else [b])
    specs = list(a_spec if pair and pair[0] == "a" else [a_spec]) + list(b_spec if pair and pair[0] == "b" else [b_spec])
    n_in = len(ins)
    bufs, copies = ride if ride else ((), None)
    n_ride = len(bufs)

    def body(*refs):
        in_refs = refs[:n_in]
        o_ref = refs[n_in + n_ride]
        ride_refs = refs[n_in + n_ride + 1:n_in + 2 * n_ride + 1]
        scratch = refs[n_in + 2 * n_ride + 1:]
        pid = [pl.program_id(ax) for ax in range(3)]
        if n_ride:
            send_sems, recv_sems = scratch[-2:]

            @pl.when((pid[0] == 0) & (pid[1] == 0) & (pid[2] == 0))
            def _():
                for cp in copies(ride_refs, ride_refs, send_sems, recv_sems):
                    cp.start()

        vals = [r[...] for r in in_refs]
        if pair:
            first = pid[pair[1]] < pair[2]
            picked = jnp.where(first, vals[0], vals[1]) if pair[0] == "a" else jnp.where(first, vals[1], vals[2])
            vals = [picked, vals[2]] if pair[0] == "a" else [vals[0], picked]
        part = _dot(vals[0], vals[1], dims)
        if nk == 1:
            o_ref[...] = part.astype(o_ref.dtype)
        else:
            acc_ref = scratch[0]

            @pl.when(pid[2] == 0)
            def _():
                acc_ref[...] = part

            @pl.when(pid[2] > 0)
            def _():
                acc_ref[...] += part

            @pl.when(pid[2] == nk - 1)
            def _():
                o_ref[...] = acc_ref[...].astype(o_ref.dtype)

        if n_ride:
            @pl.when((pid[0] == grid[0] - 1) & (pid[1] == grid[1] - 1) & (pid[2] == nk - 1))
            def _():
                for cp in copies(ride_refs, ride_refs, send_sems, recv_sems):
                    cp.wait()

    scratch_shapes = [] if nk == 1 else [pltpu.VMEM(acc_shape, F32)]
    out_shapes = [jax.ShapeDtypeStruct(out_shape, out_dtype)]
    if not n_ride:
        return pl.pallas_call(
            body, name=name, grid=grid, in_specs=specs, out_specs=[o_spec], out_shape=out_shapes,
            scratch_shapes=scratch_shapes, compiler_params=_params(("parallel", "parallel", "arbitrary")),
        )(*ins)[0]
    n_sem = 3 * n_ride
    res = pl.pallas_call(
        body, name=name, grid=grid, in_specs=specs + [ANY_SPEC] * n_ride,
        out_specs=[o_spec] + [ANY_SPEC] * n_ride,
        out_shape=out_shapes + [jax.ShapeDtypeStruct(x.shape, x.dtype) for x in bufs],
        input_output_aliases={n_in + i: 1 + i for i in range(n_ride)},
        scratch_shapes=scratch_shapes + [pltpu.SemaphoreType.DMA((n_sem,)), pltpu.SemaphoreType.DMA((n_sem,))],
        compiler_params=pltpu.CompilerParams(
            dimension_semantics=("arbitrary",) * 3, vmem_limit_bytes=VMEM_LIMIT, has_side_effects=True),
    )(*ins, *bufs)
    return res[0], res[1:]


def _mm_cols(name, a, w4, layer, out_dtype, ride=None):
    m, k = a.shape
    ns = w4.shape[-1]
    tm, tn = _tile(m, 1024), _tile(ns, 1536)
    nps = ns // tn
    return _matmul(
        name, a, w4, NN, (m // tm, N_CHIP * nps, 1),
        pl.BlockSpec((tm, k), lambda i, n, kk: (i, 0)),
        pl.BlockSpec((None, None, k, tn), lambda i, n, kk: (n // nps, layer, 0, n % nps)),
        pl.BlockSpec((tm, tn), lambda i, n, kk: (i, n)),
        (m, N_CHIP * ns), out_dtype, None, ride=ride)


def _halves_specs(g, rows_blk, cols_blk, tiles_half, row_of, col_of):
    if not isinstance(g, tuple):
        return g, pl.BlockSpec((rows_blk, cols_blk), lambda *p: (row_of(p), col_of(p)))
    left = pl.BlockSpec((rows_blk, cols_blk), lambda *p: (row_of(p), jnp.minimum(col_of(p), tiles_half - 1)))
    right = pl.BlockSpec((rows_blk, cols_blk), lambda *p: (row_of(p), jnp.maximum(col_of(p) - tiles_half, 0)))
    return g, (left, right)


def _mm_cols_nt(name, g, w4, layer, out_dtype):
    m = (g[0] if isinstance(g, tuple) else g).shape[0]
    k, ns = w4.shape[-2:]
    tm, tk = _tile(m, 1024), _tile(ns, 1536)
    kps = ns // tk
    half = N_CHIP * kps // 2
    g, g_spec = _halves_specs(g, tm, tk, half, lambda p: p[0], lambda p: p[2])
    return _matmul(
        name, g, w4, NT, (m // tm, 1, N_CHIP * kps), g_spec,
        pl.BlockSpec((None, None, k, tk), lambda i, n, kk: (kk // kps, layer, 0, kk % kps)),
        pl.BlockSpec((tm, k), lambda i, n, kk: (i, 0)),
        (m, k), out_dtype, (tm, k), pair=("a", 2, half) if isinstance(g, tuple) else None)


def _mm_cols_tn(name, a, g, ns, out_dtype):
    m, k = a.shape
    tkm, tmk, tn = _tile(m, 2048), _tile(k, 1024), _tile(ns, 1536)
    nps = ns // tn
    half = N_CHIP * nps // 2
    g, g_spec = _halves_specs(g, tkm, tn, half, lambda p: p[2], lambda p: p[1])
    return _matmul(
        name, a, g, TN, (k // tmk, N_CHIP * nps, m // tkm),
        pl.BlockSpec((tkm, tmk), lambda i, n, kk: (kk, i)), g_spec,
        pl.BlockSpec((None, tmk, tn), lambda i, n, kk: (n // nps, i, n % nps)),
        (N_CHIP, k, ns), out_dtype, (tmk, tn), pair=("b", 1, half) if isinstance(g, tuple) else None)


def _mm_rows(name, a, w4, layer, out_dtype):
    m = a.shape[0]
    rs, n = w4.shape[-2:]
    tm, tk = _tile(m, 1024), _tile(rs, 1536)
    kps = rs // tk
    return _matmul(
        name, a, w4, NN, (m // tm, 1, N_CHIP * kps),
        pl.BlockSpec((tm, tk), lambda i, j, kk: (i, kk)),
        pl.BlockSpec((None, None, tk, n), lambda i, j, kk: (kk // kps, layer, kk % kps, 0)),
        pl.BlockSpec((tm, n), lambda i, j, kk: (i, 0)),
        (m, n), out_dtype, (tm, n))


def _mm_rows_nt(name, g, w4, layer, out_dtype):
    m, n = g.shape
    rs = w4.shape[-2]
    tm, tn = _tile(m, 1024), _tile(rs, 1536)
    nps = rs // tn
    return _matmul(
        name, g, w4, NT, (m // tm, N_CHIP * nps, 1),
        pl.BlockSpec((tm, n), lambda i, j, kk: (i, 0)),
        pl.BlockSpec((None, None, tn, n), lambda i, j, kk: (j // nps, layer, j % nps, 0)),
        pl.BlockSpec((tm, tn), lambda i, j, kk: (i, j)),
        (m, N_CHIP * rs), out_dtype, None)


def _mm_rows_tn(name, a, g, rs, out_dtype):
    m = a.shape[0]
    n = g.shape[1]
    tkm, tmr, tn = _tile(m, 2048), _tile(rs, 1536), _tile(n, 1024)
    mps = rs // tmr
    return _matmul(
        name, a, g, TN, (N_CHIP * mps, n // tn, m // tkm),
        pl.BlockSpec((tkm, tmr), lambda i, j, kk: (kk, i)),
        pl.BlockSpec((tkm, tn), lambda i, j, kk: (kk, j)),
        pl.BlockSpec((None, tmr, tn), lambda i, j, kk: (i // mps, i % mps, j)),
        (N_CHIP, rs, n), out_dtype, (tmr, tn))


def _mm_grp(name, a, w, dims, out_dtype):
    m = a.shape[0]
    ng, ch = w.shape[:2]
    tm = _tile(m, 1024)
    return _matmul(
        name, a, w, dims, (m // tm, ng, 1),
        pl.BlockSpec((tm, ch), lambda i, g, kk: (i, g)),
        pl.BlockSpec((None, ch, ch), lambda i, g, kk: (g, 0, 0)),
        pl.BlockSpec((tm, ch), lambda i, g, kk: (i, g)),
        (m, ng * ch), out_dtype, None)


def _mm_grp_tn(name, a, g, ng, out_dtype):
    m = a.shape[0]
    ch = a.shape[1] // ng
    tk = _tile(m, 2048)
    return _matmul(
        name, a, g, TN, (ng, 1, m // tk),
        pl.BlockSpec((tk, ch), lambda i, j, kk: (kk, i)),
        pl.BlockSpec((tk, ch), lambda i, j, kk: (kk, i)),
        pl.BlockSpec((None, ch, ch), lambda i, j, kk: (i, 0, 0)),
        (ng, ch, ch), out_dtype, (ch, ch))


def _view2d(a):
    return a.reshape(-1, a.shape[-1])


def _elementwise(name, fn, ins, out_dtypes):
    r, c = ins[0].shape
    lanes = -(-c // LANE) * LANE
    tr = _tile(r, max(PACK, ELEMWISE_BLOCK // lanes), PACK)
    spec = pl.BlockSpec((tr, c), lambda i: (i, 0))

    def body(*refs):
        outs = fn(*[x[...] for x in refs[:len(ins)]])
        for o_ref, o in zip(refs[len(ins):], outs):
            o_ref[...] = o.astype(o_ref.dtype)

    return pl.pallas_call(
        body, name=name, grid=(r // tr,), in_specs=[spec] * len(ins), out_specs=[spec] * len(out_dtypes),
        out_shape=[jax.ShapeDtypeStruct((r, c), d) for d in out_dtypes],
        compiler_params=_params(("parallel",)),
    )(*ins)


def _adamw_math(w, g, m, v):
    m = ADAM_B1 * m + (1.0 - ADAM_B1) * g
    v = ADAM_B2 * v + (1.0 - ADAM_B2) * (g * g)
    m_hat = m / (1.0 - ADAM_B1 ** ADAM_STEP)
    v_hat = v / (1.0 - ADAM_B2 ** ADAM_STEP)
    delta = -ADAM_LR * (m_hat / (jnp.sqrt(v_hat) + ADAM_EPS) + ADAM_WD * w)
    return delta, m, v


def _adamw(name, w, g, m, v, emit_grad=False):
    shape = w.shape
    fn = (lambda *t: _adamw_math(*t) + (t[1],)) if emit_grad else _adamw_math
    outs = _elementwise(name, fn, [_view2d(w), _view2d(g), _view2d(m), _view2d(v)], [F32] * (4 if emit_grad else 3))
    return [o.reshape(shape) for o in outs]


def _coords():
    return lax.axis_index("x"), lax.axis_index("y"), lax.axis_index("c")


def _peer(x, y, c, k):
    return (x ^ (k >> 2), y ^ ((k >> 1) & 1), c ^ (k & 1))


def _all_gather8(name, block):
    r = block.shape[0]

    def body(x_ref, out_ref, send_sems, recv_sems):
        x, y, c = _coords()
        me = 4 * x + 2 * y + c
        out_ref[me] = x_ref[...]
        copies = []
        for k in range(1, N_DEV):
            cp = pltpu.make_async_remote_copy(
                src_ref=x_ref, dst_ref=out_ref.at[me], send_sem=send_sems.at[k], recv_sem=recv_sems.at[k],
                device_id=_peer(x, y, c, k), device_id_type=MESH)
            cp.start()
            copies.append(cp)
        for cp in copies:
            cp.wait()

    return pl.pallas_call(
        body, name=name,
        in_specs=[pl.BlockSpec(memory_space=pltpu.VMEM)], out_specs=pl.BlockSpec(memory_space=pltpu.VMEM),
        out_shape=jax.ShapeDtypeStruct((N_DEV, r, LANE), F32),
        scratch_shapes=[pltpu.SemaphoreType.DMA((N_DEV,)), pltpu.SemaphoreType.DMA((N_DEV,))],
        compiler_params=pltpu.CompilerParams(vmem_limit_bytes=VMEM_LIMIT),
    )(block)


def _all_reduce8(name, parts):
    r = parts.shape[1]

    def body(p_ref, out_ref, rbuf, send1, recv1, send2, recv2):
        x, y, c = _coords()
        me = 4 * x + 2 * y + c
        first = []
        for k in range(1, N_DEV):
            px, py, pc = _peer(x, y, c, k)
            cp = pltpu.make_async_remote_copy(
                src_ref=p_ref.at[4 * px + 2 * py + pc], dst_ref=rbuf.at[me], send_sem=send1.at[k],
                recv_sem=recv1.at[k], device_id=(px, py, pc), device_id_type=MESH)
            cp.start()
            first.append(cp)
        rbuf[me] = p_ref[me]
        for cp in first:
            cp.wait()
        acc = rbuf[0]
        for d in range(1, N_DEV):
            acc = acc + rbuf[d]
        out_ref[me] = acc
        second = []
        for k in range(1, N_DEV):
            cp = pltpu.make_async_remote_copy(
                src_ref=out_ref.at[me], dst_ref=out_ref.at[me], send_sem=send2.at[k], recv_sem=recv2.at[k],
                device_id=_peer(x, y, c, k), device_id_type=MESH)
            cp.start()
            second.append(cp)
        for cp in second:
            cp.wait()

    return pl.pallas_call(
        body, name=name,
        in_specs=[pl.BlockSpec(memory_space=pltpu.VMEM)], out_specs=pl.BlockSpec(memory_space=pltpu.VMEM),
        out_shape=jax.ShapeDtypeStruct((N_DEV, r, LANE), F32),
        scratch_shapes=[pltpu.VMEM((N_DEV, r, LANE), F32)] + [pltpu.SemaphoreType.DMA((N_DEV,))] * 4,
        compiler_params=pltpu.CompilerParams(vmem_limit_bytes=VMEM_LIMIT),
    )(parts)


ANY_SPEC = pl.BlockSpec(memory_space=pl.ANY)
COMM_BLOCK_BYTES = 4 << 20


def _staged_call(name, body, core, grid, in_specs, ins, out_shape, scratch, aliases=None):
    return pl.pallas_call(
        body, name=name,
        grid_spec=pltpu.PrefetchScalarGridSpec(
            num_scalar_prefetch=1, grid=grid, in_specs=in_specs, out_specs=ANY_SPEC, scratch_shapes=scratch),
        out_shape=out_shape, input_output_aliases=aliases or {},
        compiler_params=pltpu.CompilerParams(
            dimension_semantics=("arbitrary",) * len(grid), vmem_limit_bytes=VMEM_LIMIT, has_side_effects=True),
    )(core, *ins)


def _rows_tile(rows, cols, itemsize):
    return _tile(rows, max(PACK, COMM_BLOCK_BYTES // (cols * itemsize)), PACK)


def _chip_peer(x, y, c, k):
    return (x ^ (k >> 1), y ^ (k & 1), c)


def _cast_own_half(name, w, pos):
    r, cols = w.shape
    h = r // 2
    tr = _tile(h, max(PACK, ELEMWISE_BLOCK // cols), PACK)
    nb = h // tr

    def body(p_ref, w_ref, o_ref):
        o_ref[...] = w_ref[...].astype(o_ref.dtype)

    return pl.pallas_call(
        body, name=name,
        grid_spec=pltpu.PrefetchScalarGridSpec(
            num_scalar_prefetch=1, grid=(nb,),
            in_specs=[pl.BlockSpec((tr, cols), lambda j, p: (p[0] * nb + j, 0))],
            out_specs=pl.BlockSpec((tr, cols), lambda j, p: ((2 * p[1] + p[0]) * nb + j, 0))),
        out_shape=jax.ShapeDtypeStruct((N_CHIP * r, cols), MXU_DTYPE),
        compiler_params=_params(("parallel",)),
    )(pos, w)


def _gather_ici_plan(shard_rows):
    def copies(refs, _, send_sems, recv_sems):
        x, y, c = _coords()
        out = []
        for i, (ref, r) in enumerate(zip(refs, shard_rows)):
            h = r // 2
            mine = ref.at[pl.ds(pl.multiple_of((2 * x + y) * r + c * h, PACK), h)]
            for k in (1, 2, 3):
                n = 3 * i + k - 1
                out.append(pltpu.make_async_remote_copy(
                    src_ref=mine, dst_ref=mine, send_sem=send_sems.at[n], recv_sem=recv_sems.at[n],
                    device_id=_chip_peer(x, y, c, k), device_id_type=MESH))
        return out

    return copies


def _swap_gathered(name, part, core, r):
    cols = part.shape[1]
    h = r // 2
    full = jax.ShapeDtypeStruct(part.shape, part.dtype)
    tr2 = _rows_tile(h, cols, 2)
    nb2 = h // tr2

    def swap_body(c_ref, mine_ref, full_ref, send_sem, recv_sem):
        s, j = pl.program_id(0), pl.program_id(1)
        x, y, c = _coords()
        dst = full_ref.at[pl.ds(pl.multiple_of(s * r + c * h + j * tr2, PACK), tr2)]
        cp = pltpu.make_async_remote_copy(
            src_ref=mine_ref, dst_ref=dst, send_sem=send_sem, recv_sem=recv_sem,
            device_id=(x, y, 1 - c), device_id_type=MESH)
        cp.start()
        cp.wait_send()

        @pl.when((s == N_CHIP - 1) & (j == nb2 - 1))
        def _():
            landed = full_ref.at[pl.ds(0, N_CHIP * h)]
            pltpu.make_async_remote_copy(
                src_ref=landed, dst_ref=landed, send_sem=send_sem, recv_sem=recv_sem,
                device_id=(x, y, 1 - c), device_id_type=MESH).wait_recv()

    return _staged_call(
        name + "_d2d", swap_body, core, (N_CHIP, nb2),
        [pl.BlockSpec((tr2, cols), lambda s, j, c_ref: ((2 * s + c_ref[0]) * nb2 + j, 0))], [part], full,
        [pltpu.SemaphoreType.DMA(()), pltpu.SemaphoreType.DMA(())], aliases={1: 0}).reshape(N_CHIP, r, cols)


def _pair_sum(name, g, core):
    n, r, cols = g.shape
    h = r // 2
    tr = _rows_tile(h, cols, g.dtype.itemsize)
    nb = h // tr
    half = jax.ShapeDtypeStruct((n, h, cols), g.dtype)

    def send_body(c_ref, g_ref, got_ref, send_sem, recv_sem):
        s, j = pl.program_id(0), pl.program_id(1)
        x, y, c = _coords()
        dst = got_ref.at[pl.ds(pl.multiple_of(s * h + j * tr, PACK), tr)]
        cp = pltpu.make_async_remote_copy(
            src_ref=g_ref, dst_ref=dst, send_sem=send_sem, recv_sem=recv_sem,
            device_id=(x, y, 1 - c), device_id_type=MESH)
        cp.start()
        cp.wait_send()

        @pl.when((s == n - 1) & (j == nb - 1))
        def _():
            pltpu.make_async_remote_copy(
                src_ref=got_ref, dst_ref=got_ref, send_sem=send_sem, recv_sem=recv_sem,
                device_id=(x, y, 1 - c), device_id_type=MESH).wait_recv()

    got = _staged_call(
        name + "_send", send_body, core, (n, nb),
        [pl.BlockSpec((tr, cols), lambda s, j, c_ref: ((2 * s + 1 - c_ref[0]) * nb + j, 0))],
        [g.reshape(n * r, cols)], jax.ShapeDtypeStruct((n * h, cols), g.dtype),
        [pltpu.SemaphoreType.DMA(()), pltpu.SemaphoreType.DMA(())]).reshape(n, h, cols)

    def add_body(c_ref, own_ref, got_ref, o_ref):
        o_ref[...] = (own_ref[...].astype(F32) + got_ref[...].astype(F32)).astype(o_ref.dtype)

    blk = pl.BlockSpec((None, tr, cols), lambda s, j, c_ref: (s, j, 0))
    return pl.pallas_call(
        add_body, name=name + "_add",
        grid_spec=pltpu.PrefetchScalarGridSpec(
            num_scalar_prefetch=1, grid=(n, nb),
            in_specs=[pl.BlockSpec((None, tr, cols), lambda s, j, c_ref: (s, c_ref[0] * nb + j, 0)), blk],
            out_specs=blk),
        out_shape=half, compiler_params=_params(("parallel", "parallel")),
    )(core, g, got)


def _chip_exchange_plan(pairs):
    shapes = [jax.ShapeDtypeStruct((N_CHIP - 1,) + p.shape[1:], p.dtype) for p in pairs]

    def copies(in_refs, out_refs, send_sems, recv_sems):
        x, y, c = _coords()
        out = []
        for i, (src, dst) in enumerate(zip(in_refs, out_refs)):
            for k in (1, 2, 3):
                px, py, pc = _chip_peer(x, y, c, k)
                n = 3 * i + k - 1
                out.append(pltpu.make_async_remote_copy(
                    src_ref=src.at[2 * px + py], dst_ref=dst.at[k - 1], send_sem=send_sems.at[n],
                    recv_sem=recv_sems.at[n], device_id=(px, py, pc), device_id_type=MESH))
        return out

    return shapes, copies


def _sum_and_swap(name, pair, got, pos, into, total_rows, base):
    _, h, cols = pair.shape
    tr = _rows_tile(h, cols, 4)
    nb = h // tr

    def body(p_ref, own_ref, a_ref, b_ref, c_ref, *rest):
        full_ref, red_ref, send_sem, recv_sem, local_sem = rest[-5:]
        j = pl.program_id(0)
        x, y, c = _coords()
        red_ref[...] = ((own_ref[...].astype(F32) + a_ref[...].astype(F32)) + b_ref[...].astype(F32)) + c_ref[...].astype(F32)
        dst = full_ref.at[pl.ds(pl.multiple_of(base + c * h + j * tr, SUB), tr)]
        local = pltpu.make_async_copy(red_ref, dst, local_sem)
        remote = pltpu.make_async_remote_copy(
            src_ref=red_ref, dst_ref=dst, send_sem=send_sem, recv_sem=recv_sem,
            device_id=(x, y, 1 - c), device_id_type=MESH)
        local.start()
        remote.start()
        remote.wait_send()
        local.wait()

        @pl.when(j == nb - 1)
        def _():
            landed = full_ref.at[pl.ds(0, h)]
            pltpu.make_async_remote_copy(
                src_ref=landed, dst_ref=landed, send_sem=send_sem, recv_sem=recv_sem,
                device_id=(x, y, 1 - c), device_id_type=MESH).wait_recv()

    in_specs = [pl.BlockSpec((None, tr, cols), lambda j, p: (p[1], j, 0))]
    in_specs += [pl.BlockSpec((None, tr, cols), (lambda j, p, k=k: (k, j, 0))) for k in range(N_CHIP - 1)]
    ins = [pair, got, got, got]
    aliases = None
    if into is not None:
        in_specs.append(ANY_SPEC)
        ins.append(into)
        aliases = {5: 0}
    return _staged_call(
        name, body, pos, (nb,), in_specs, ins, jax.ShapeDtypeStruct((total_rows, cols), F32),
        [pltpu.VMEM((tr, cols), F32)] + [pltpu.SemaphoreType.DMA(())] * 3, aliases)


def _cmul(ar, ai, br, bi):
    return ar * br - ai * bi, ar * bi + ai * br


def _cpow(ar, ai, n):
    rr, ri = jnp.ones_like(ar), jnp.zeros_like(ai)
    br, bi = ar, ai
    while n:
        if n & 1:
            rr, ri = _cmul(rr, ri, br, bi)
        br, bi = _cmul(br, bi, br, bi)
        n >>= 1
    return rr, ri


def _tile_rows(t):
    if isinstance(t, int):
        return pl.ds(t * SUB, SUB)
    return pl.ds(pl.multiple_of(t * SUB, SUB), SUB)


SCAN_UNROLL = 8


def _unrolled_loop(n, body, carry):
    trips = n // SCAN_UNROLL

    def trip(o, c):
        for k in range(SCAN_UNROLL):
            c = body(o * SCAN_UNROLL + k, c)
        return c

    carry = lax.fori_loop(0, trips, trip, carry)
    for i in range(trips * SCAN_UNROLL, n):
        carry = body(i, carry)
    return carry


def _scan_setup(buf, steps, ar, ai, h0r, h0i, rev):
    def total(i, carry):
        sr, si = carry
        rows = _tile_rows(steps - 1 - i if rev else i)
        pr, pi = _cmul(ar, ai, sr, si)
        return pr + buf[rows, 0:S5_H], pi + buf[rows, S5_H:S5_NS]

    zero = jnp.zeros((SUB, S5_H), F32)
    tot_r, tot_i = _unrolled_loop(steps, total, (zero, zero))
    pw_r, pw_i = _cpow(ar[0:1], ai[0:1], steps)
    row = lax.broadcasted_iota(jnp.int32, (SUB, S5_H), 0)
    cur_r, cur_i = h0r, h0i
    init_r, init_i = zero, zero
    for s in (range(N_SEG - 1, -1, -1) if rev else range(N_SEG)):
        init_r = jnp.where(row == s, cur_r, init_r)
        init_i = jnp.where(row == s, cur_i, init_i)
        nr, ni = _cmul(pw_r, pw_i, cur_r, cur_i)
        cur_r, cur_i = nr + tot_r[s:s + 1], ni + tot_i[s:s + 1]
    return init_r, init_i, cur_r, cur_i


def _scan(buf, steps, ar, ai, h0r, h0i, rev, store):
    init_r, init_i, fin_r, fin_i = _scan_setup(buf, steps, ar, ai, h0r, h0i, rev)
    if store:
        def step(i, carry):
            hr, hi = carry
            rows = _tile_rows(steps - 1 - i if rev else i)
            pr, pi = _cmul(ar, ai, hr, hi)
            hr, hi = pr + buf[rows, 0:S5_H], pi + buf[rows, S5_H:S5_NS]
            buf[rows, 0:S5_H] = hr
            buf[rows, S5_H:S5_NS] = hi
            return hr, hi

        _unrolled_loop(steps, step, (init_r, init_i))
    return fin_r, fin_i


def _adjoint_scan(gbuf, hbuf, steps, ar, ai, l0r, l0i, hin_r, hin_i, rev):
    ci = -ai
    arev = not rev
    init_r, init_i, fin_r, fin_i = _scan_setup(gbuf, steps, ar, ci, l0r, l0i, arev)
    zero = jnp.zeros((SUB, S5_H), F32)

    def update(t, hp_r, hp_i, carry):
        lr, li, dr, di = carry
        rows = _tile_rows(t)
        pr, pi = _cmul(ar, ci, lr, li)
        lr, li = pr + gbuf[rows, 0:S5_H], pi + gbuf[rows, S5_H:S5_NS]
        gbuf[rows, 0:S5_H] = lr
        gbuf[rows, S5_H:S5_NS] = li
        return lr, li, dr + lr * hp_r + li * hp_i, di + li * hp_r - lr * hp_i

    def step(i, carry):
        t = steps - 1 - i if rev is False else i
        prev = _tile_rows(t - 1 if rev is False else t + 1)
        return update(t, hbuf[prev, 0:S5_H], hbuf[prev, S5_H:S5_NS], carry)

    carry = _unrolled_loop(steps - 1, step, (init_r, init_i, zero, zero))
    row = lax.broadcasted_iota(jnp.int32, (SUB, S5_H), 0)
    if rev:
        last, edge, shift, t = _tile_rows(0), N_SEG - 1, SUB - 1, steps - 1
    else:
        last, edge, shift, t = _tile_rows(steps - 1), 0, 1, 0
    hp_r = jnp.where(row == edge, hin_r, pltpu.roll(hbuf[last, 0:S5_H], shift, 0))
    hp_i = jnp.where(row == edge, hin_i, pltpu.roll(hbuf[last, S5_H:S5_NS], shift, 0))
    _, _, dr, di = update(t, hp_r, hp_i, carry)
    return fin_r, fin_i, dr, di


def _s5_chunk(rows):
    return _tile(rows, 512, PACK)


def _s5_forward(u, uc, bblk, cblk, atile, dsk, ride_bufs, ride_copies):
    rows, d = u.shape
    rows_c = uc.shape[0]
    nj = d // S5_CB
    steps, steps_c = rows // N_SEG, rows_c // N_SEG
    rc = _s5_chunk(rows)
    n_ride = len(ride_bufs)
    n_sem = 3 * n_ride

    def body(u_ref, uc_ref, b_ref, c_ref, a_ref, d_ref, *rest):
        y_ref = rest[n_ride]
        ride = rest[n_ride + 1:2 * n_ride + 1]
        buf, bufc, send_sems, recv_sems = rest[2 * n_ride + 1:]

        @pl.when(pl.program_id(0) == 0)
        def _():
            for cp in ride_copies(ride, ride, send_sems, recv_sems):
                cp.start()

        zero = jnp.zeros((1, S5_H), F32)
        for dr in (0, 1):
            rev = dr == 1
            ar, ai = a_ref[dr, :, 0:S5_H], a_ref[dr, :, S5_H:S5_NS]
            bm, cm = b_ref[dr].astype(MXU_DTYPE), c_ref[dr].astype(MXU_DTYPE)
            bufc[...] = _dot(uc_ref[...], bm)
            fin_r, fin_i = _scan(bufc, steps_c, ar, ai, zero, zero, rev, False)

            def project(r, _):
                rs = pl.ds(pl.multiple_of(r * rc, rc), rc)
                buf[rs, :] = _dot(u_ref[rs, :], bm)
                return 0

            lax.fori_loop(0, rows // rc, project, 0)
            _scan(buf, steps, ar, ai, fin_r, fin_i, rev, True)

            def readout(r, _):
                rs = pl.ds(pl.multiple_of(r * rc, rc), rc)
                yv = _dot(buf[rs, :], cm)
                if dr == 0:
                    y_ref[rs, :] = u_ref[rs, :].astype(F32) * d_ref[...] + yv
                else:
                    y_ref[rs, :] += yv
                return 0

            lax.fori_loop(0, rows // rc, readout, 0)

        @pl.when(pl.program_id(0) == nj - 1)
        def _():
            for cp in ride_copies(ride, ride, send_sems, recv_sems):
                cp.wait()

    res = pl.pallas_call(
        body, name="s5_forward", grid=(nj,),
        in_specs=[
            pl.BlockSpec((rows, S5_CB), lambda j: (0, j)),
            pl.BlockSpec((rows_c, S5_CB), lambda j: (0, j)),
            pl.BlockSpec((2, None, S5_CB, S5_NS), lambda j: (0, j, 0, 0)),
            pl.BlockSpec((2, None, S5_NS, S5_CB), lambda j: (0, j, 0, 0)),
            pl.BlockSpec((2, None, SUB, S5_NS), lambda j: (0, j, 0, 0)),
            pl.BlockSpec((1, S5_CB), lambda j: (0, j)),
        ] + [ANY_SPEC] * n_ride,
        out_specs=[pl.BlockSpec((rows, S5_CB), lambda j: (0, j))] + [ANY_SPEC] * n_ride,
        out_shape=[jax.ShapeDtypeStruct((rows, d), F32)]
        + [jax.ShapeDtypeStruct(b.shape, b.dtype) for b in ride_bufs],
        input_output_aliases={6 + i: 1 + i for i in range(n_ride)},
        scratch_shapes=[pltpu.VMEM((rows, S5_NS), F32), pltpu.VMEM((rows_c, S5_NS), F32),
                        pltpu.SemaphoreType.DMA((n_sem,)), pltpu.SemaphoreType.DMA((n_sem,))],
        compiler_params=pltpu.CompilerParams(
            dimension_semantics=("arbitrary",), vmem_limit_bytes=VMEM_LIMIT, has_side_effects=True),
    )(u, uc, bblk, cblk, atile, dsk, *ride_bufs)
    return res[0], res[1:]


def _s5_backward(u, uc, dy, bblk, cblk, atile, dsk, ride_ins, ride_shapes, ride_copies):
    rows, d = u.shape
    rows_c = uc.shape[0]
    nj = d // S5_CB
    steps, steps_c = rows // N_SEG, rows_c // N_SEG
    rc = _s5_chunk(rows)
    nchunk = rows // rc
    n_ride = len(ride_ins)
    n_sem = 3 * n_ride

    def body(u_ref, uc_ref, dy_ref, b_ref, c_ref, a_ref, d_ref, *rest):
        ride_in = rest[:n_ride]
        du_ref, duc_ref, db_ref, dc_ref, da_ref, dd_ref = rest[n_ride:n_ride + 6]
        ride_out = rest[n_ride + 6:2 * n_ride + 6]
        hbuf, gbuf, hcbuf, gcbuf, send_sems, recv_sems = rest[2 * n_ride + 6:]

        @pl.when(pl.program_id(0) == 0)
        def _():
            for cp in ride_copies(ride_in, ride_out, send_sems, recv_sems):
                cp.start()

        zero = jnp.zeros((1, S5_H), F32)
        db_ref[...] = jnp.zeros_like(db_ref)
        dc_ref[...] = jnp.zeros_like(dc_ref)
        dd_ref[...] = jnp.zeros_like(dd_ref)
        for dr in (0, 1):
            rev = dr == 1
            ar, ai = a_ref[dr, :, 0:S5_H], a_ref[dr, :, S5_H:S5_NS]
            bm, cm = b_ref[dr].astype(MXU_DTYPE), c_ref[dr].astype(MXU_DTYPE)
            hcbuf[...] = _dot(uc_ref[...], bm)
            hin_r, hin_i = _scan(hcbuf, steps_c, ar, ai, zero, zero, rev, True)

            def project(r, _):
                rs = pl.ds(pl.multiple_of(r * rc, rc), rc)
                hbuf[rs, :] = _dot(u_ref[rs, :], bm)
                return 0

            lax.fori_loop(0, nchunk, project, 0)
            _scan(hbuf, steps, ar, ai, hin_r, hin_i, rev, True)

            def readout_bwd(r, _):
                rs = pl.ds(pl.multiple_of(r * rc, rc), rc)
                dyv = dy_ref[rs, :]
                gbuf[rs, :] = _dot(dyv, cm, NT)
                dc_ref[dr] += _dot(hbuf[rs, :], dyv, TN)
                return 0

            lax.fori_loop(0, nchunk, readout_bwd, 0)
            lf_r, lf_i, dar, dai = _adjoint_scan(gbuf, hbuf, steps, ar, ai, zero, zero, hin_r, hin_i, rev)
            gcbuf[...] = jnp.zeros_like(gcbuf)
            _, _, dar_c, dai_c = _adjoint_scan(gcbuf, hcbuf, steps_c, ar, ai, lf_r, lf_i, zero, zero, rev)
            da_ref[dr, :, 0:S5_H] = dar + dar_c
            da_ref[dr, :, S5_H:S5_NS] = dai + dai_c

            def project_bwd(r, _):
                rs = pl.ds(pl.multiple_of(r * rc, rc), rc)
                lam = gbuf[rs, :]
                uv = u_ref[rs, :]
                part = _dot(lam, bm, NT)
                db_ref[dr] += _dot(uv, lam, TN)
                if dr == 0:
                    dyv = dy_ref[rs, :].astype(F32)
                    du_ref[rs, :] = part + dyv * d_ref[...]
                    dd_ref[...] += (dyv * uv.astype(F32)).reshape(rc // SUB, SUB, S5_CB).sum(axis=0)
                else:
                    du_ref[rs, :] += part
                return 0

            lax.fori_loop(0, nchunk, project_bwd, 0)
            lam_c = gcbuf[...]
            part_c = _dot(lam_c, bm, NT)
            db_ref[dr] += _dot(uc_ref[...], lam_c, TN)
            if dr == 0:
                duc_ref[...] = part_c
            else:
                duc_ref[...] += part_c

        @pl.when(pl.program_id(0) == nj - 1)
        def _():
            for cp in ride_copies(ride_in, ride_out, send_sems, recv_sems):
                cp.wait()

    blk = lambda r: pl.BlockSpec((r, S5_CB), lambda j: (0, j))
    res = pl.pallas_call(
        body, name="s5_backward", grid=(nj,),
        in_specs=[
            blk(rows), blk(rows_c), blk(rows),
            pl.BlockSpec((2, None, S5_CB, S5_NS), lambda j: (0, j, 0, 0)),
            pl.BlockSpec((2, None, S5_NS, S5_CB), lambda j: (0, j, 0, 0)),
            pl.BlockSpec((2, None, SUB, S5_NS), lambda j: (0, j, 0, 0)),
            pl.BlockSpec((1, S5_CB), lambda j: (0, j)),
        ] + [ANY_SPEC] * n_ride,
        out_specs=[
            blk(rows), blk(rows_c),
            pl.BlockSpec((2, None, S5_CB, S5_NS), lambda j: (0, j, 0, 0)),
            pl.BlockSpec((2, None, S5_NS, S5_CB), lambda j: (0, j, 0, 0)),
            pl.BlockSpec((2, None, SUB, S5_NS), lambda j: (0, j, 0, 0)),
            pl.BlockSpec((SUB, S5_CB), lambda j: (0, j)),
        ] + [ANY_SPEC] * n_ride,
        out_shape=[
            jax.ShapeDtypeStruct((rows, d), F32), jax.ShapeDtypeStruct((rows_c, d), F32),
            jax.ShapeDtypeStruct(bblk.shape, F32), jax.ShapeDtypeStruct(cblk.shape, F32),
            jax.ShapeDtypeStruct(atile.shape, F32), jax.ShapeDtypeStruct((SUB, d), F32),
        ] + list(ride_shapes),
        scratch_shapes=[pltpu.VMEM((rows, S5_NS), F32), pltpu.VMEM((rows, S5_NS), F32),
                        pltpu.VMEM((rows_c, S5_NS), F32), pltpu.VMEM((rows_c, S5_NS), F32),
                        pltpu.SemaphoreType.DMA((n_sem,)), pltpu.SemaphoreType.DMA((n_sem,))],
        compiler_params=pltpu.CompilerParams(
            dimension_semantics=("arbitrary",), vmem_limit_bytes=VMEM_LIMIT, has_side_effects=True),
    )(u, uc, dy, bblk, cblk, atile, dsk, *ride_ins)
    return res[:6], res[6:]


def _s5_prepare(lam_re, lam_im, log_step, b_re, b_im, c_re, c_im):
    nd, g, p = lam_re.shape
    gb = S5_CB // S5_GROUP
    nj = g // gb
    dt = jnp.exp(log_step)[..., None]
    mag = jnp.exp(lam_re * dt)
    abar_re = mag * jnp.cos(lam_im * dt)
    abar_im = mag * jnp.sin(lam_im * dt)
    nr, ni = abar_re - 1.0, abar_im
    den = lam_re * lam_re + lam_im * lam_im
    fr = (nr * lam_re + ni * lam_im) / den
    fi = (ni * lam_re - nr * lam_im) / den
    bbar_re = fr[..., None] * b_re - fi[..., None] * b_im
    bbar_im = fr[..., None] * b_im + fi[..., None] * b_re
    eye = jnp.eye(gb, dtype=bool)

    def diag_in(w):
        w = w.reshape(nd, nj, gb, p, S5_GROUP).transpose(0, 1, 2, 4, 3)
        w = jnp.where(eye[None, None, :, None, :, None], w[:, :, :, :, None, :], 0.0)
        return w.reshape(nd, nj, gb * S5_GROUP, gb * p)

    def diag_out(w):
        w = w.reshape(nd, nj, gb, S5_GROUP, p).transpose(0, 1, 2, 4, 3)
        w = jnp.where(eye[None, None, :, None, :, None], w[:, :, :, :, None, :], 0.0)
        return w.reshape(nd, nj, gb * p, gb * S5_GROUP)

    bblk = jnp.concatenate([diag_in(bbar_re), diag_in(bbar_im)], axis=-1)
    cblk = jnp.concatenate([diag_out(c_re), -diag_out(c_im)], axis=-2)
    a2 = jnp.concatenate([abar_re.reshape(nd, nj, gb * p), abar_im.reshape(nd, nj, gb * p)], axis=-1)
    atile = jnp.broadcast_to(a2[:, :, None, :], (nd, nj, SUB, 2 * gb * p))
    return bblk, cblk, atile


def _shifted(x, prev_row, next_row):
    n = x.shape[0]
    row = lax.broadcasted_iota(jnp.int32, (SUB, x.shape[1]), 0)
    xp = pltpu.roll(x, 1, 0)
    xn = pltpu.roll(x, n - 1, 0)
    xp = jnp.concatenate([jnp.where(row == 0, prev_row, xp[:SUB]), xp[SUB:]], axis=0)
    xn = jnp.concatenate([xn[:n - SUB], jnp.where(row == SUB - 1, next_row, xn[n - SUB:])], axis=0)
    return xp, xn


def _edge_rows(ref, r0, n, total, group):
    lo = pl.multiple_of(jnp.maximum(r0 - group, 0), group)
    hi = pl.multiple_of(jnp.minimum(r0 + n, total - group), group)
    prev_row = ref[pl.ds(lo, group), :].astype(F32)[group - 1:group] * (r0 > 0).astype(F32)
    next_row = ref[pl.ds(hi, group), :].astype(F32)[0:1] * (r0 + n < total).astype(F32)
    return prev_row, next_row


def _conv_rows(ref, r0, n, total, w_ref, b_ref):
    x = ref[pl.ds(r0, n), :].astype(F32)
    xp, xn = _shifted(x, *_edge_rows(ref, r0, n, total, PACK))
    hc = w_ref[0:1, :] * xp + w_ref[1:2, :] * x + w_ref[2:3, :] * xn + b_ref[...]
    return hc, xp, x, xn


def _conv_specs(rows, f, tc):
    nt = f // tc
    val = lambda r: pl.BlockSpec((r, tc), lambda j: (0, j))
    gate = lambda r: pl.BlockSpec((r, tc), lambda j: (0, j + nt))
    return val, gate


def _conv_swiglu_fwd(name, h, cw, cb):
    rows, f2 = h.shape
    f = f2 // 2
    tc = _tile(f, 256)
    rc = _tile(rows, 256, PACK)
    val, gate = _conv_specs(rows, f, tc)

    def body(hv_ref, hg_ref, wv_ref, wg_ref, bv_ref, bg_ref, a_ref):
        def chunk(r, _):
            r0 = pl.multiple_of(r * rc, rc)
            hv = _conv_rows(hv_ref, r0, rc, rows, wv_ref, bv_ref)[0]
            hg = _conv_rows(hg_ref, r0, rc, rows, wg_ref, bg_ref)[0]
            a_ref[pl.ds(r0, rc), :] = (hg * _sigmoid(hg) * hv).astype(a_ref.dtype)
            return 0

        lax.fori_loop(0, rows // rc, chunk, 0)

    return pl.pallas_call(
        body, name=name, grid=(f // tc,),
        in_specs=[val(rows), gate(rows), val(3), gate(3), val(1), gate(1)],
        out_specs=val(rows), out_shape=jax.ShapeDtypeStruct((rows, f), ACT_DTYPE),
        compiler_params=_params(("parallel",)),
    )(h, h, cw, cw, cb, cb)


def _conv_swiglu_bwd(name, da, h, cw, cb):
    rows, f2 = h.shape
    f = f2 // 2
    tc = _tile(f, 256)
    rc = _tile(rows, 256, PACK)
    val, gate = _conv_specs(rows, f, tc)

    def body(da_ref, hv_ref, hg_ref, wv_ref, wg_ref, bv_ref, bg_ref,
             dhv_ref, dhg_ref, dwv_ref, dwg_ref, dbv_ref, dbg_ref, sv, sg):
        def first(r, carry):
            r0 = pl.multiple_of(r * rc, rc)
            rs = pl.ds(r0, rc)
            hv, vp, vx, vn = _conv_rows(hv_ref, r0, rc, rows, wv_ref, bv_ref)
            hg, gp, gx, gn = _conv_rows(hg_ref, r0, rc, rows, wg_ref, bg_ref)
            d = da_ref[rs, :].astype(F32)
            s = _sigmoid(hg)
            dv = d * (hg * s)
            dg = d * hv * (s * (1.0 + hg * (1.0 - s)))
            sv[rs, :] = dv
            sg[rs, :] = dg
            sums = [dv * vp, dv * vx, dv * vn, dv, dg * gp, dg * gx, dg * gn, dg]
            return tuple(c + jnp.sum(x, axis=0, keepdims=True) for c, x in zip(carry, sums))

        zero = jnp.zeros((1, tc), F32)
        acc = lax.fori_loop(0, rows // rc, first, (zero,) * 8)
        for k in range(3):
            dwv_ref[k:k + 1, :] = acc[k]
            dwg_ref[k:k + 1, :] = acc[4 + k]
        dbv_ref[...] = acc[3]
        dbg_ref[...] = acc[7]

        def second(r, _):
            r0 = pl.multiple_of(r * rc, rc)
            rs = pl.ds(r0, rc)
            for s_ref, w_ref, o_ref in ((sv, wv_ref, dhv_ref), (sg, wg_ref, dhg_ref)):
                x = s_ref[rs, :]
                xp, xn = _shifted(x, *_edge_rows(s_ref, r0, rc, rows, SUB))
                o_ref[rs, :] = (w_ref[0:1, :] * xn + w_ref[1:2, :] * x + w_ref[2:3, :] * xp).astype(o_ref.dtype)
            return 0

        lax.fori_loop(0, rows // rc, second, 0)

    res = pl.pallas_call(
        body, name=name, grid=(f // tc,),
        in_specs=[val(rows), val(rows), gate(rows), val(3), gate(3), val(1), gate(1)],
        out_specs=[val(rows), val(rows), val(3), val(3), val(1), val(1)],
        out_shape=[jax.ShapeDtypeStruct((rows, f), ACT_DTYPE)] * 2
        + [jax.ShapeDtypeStruct((3, f), F32)] * 2 + [jax.ShapeDtypeStruct((1, f), F32)] * 2,
        scratch_shapes=[pltpu.VMEM((rows, tc), F32), pltpu.VMEM((rows, tc), F32)],
        compiler_params=_params(("parallel",)),
    )(da, h, h, cw, cw, cb, cb)
    return res


def _pool_band(name, x, transpose, out_dtype):
    rows, d = x.shape
    ng = len(POOL_WINDOWS)
    ch = d // ng
    tm = _tile(rows, 256, PACK)
    win = tm + 2 * POOL_HALO
    assert win <= rows

    def body(x_ref, o_ref):
        half = lax.shift_left(jnp.int32(1), pl.program_id(0))
        t0 = pl.program_id(1) * tm
        ws = pl.multiple_of(jnp.clip(t0 - POOL_HALO, 0, rows - win), PACK)
        i = t0 + lax.broadcasted_iota(jnp.int32, (tm, win), 0)
        j = ws + lax.broadcasted_iota(jnp.int32, (tm, win), 1)

        def inv_count(t):
            hi = jnp.minimum(t + half - 1, rows - 1)
            lo = jnp.maximum(t - half, 0)
            return 1.0 / (hi - lo + 1).astype(F32)

        xw = x_ref[pl.ds(ws, win), :]
        xt = x_ref[pl.ds(pl.multiple_of(t0, PACK), tm), :].astype(F32)
        if transpose:
            band = (j - half <= i) & (i <= j + half - 1)
            tw = ws + lax.broadcasted_iota(jnp.int32, (win, 1), 0)
            o = _dot(band.astype(MXU_DTYPE), xw.astype(F32) * inv_count(tw)) - xt
        else:
            band = (i - half <= j) & (j <= i + half - 1)
            tt = t0 + lax.broadcasted_iota(jnp.int32, (tm, 1), 0)
            o = _dot(band.astype(MXU_DTYPE), xw) * inv_count(tt) - xt
        o_ref[...] = o.astype(o_ref.dtype)

    return pl.pallas_call(
        body, name=name, grid=(ng, rows // tm),
        in_specs=[pl.BlockSpec((rows, ch), lambda g, i: (0, g))],
        out_specs=pl.BlockSpec((tm, ch), lambda g, i: (i, g)),
        out_shape=jax.ShapeDtypeStruct((rows, d), out_dtype),
        compiler_params=_params(("parallel", "arbitrary")),
    )(x)


def _ada_forward(c16, ada_w, ada_b):
    nl, d, cols = ada_w.shape
    tn = _tile(cols, 512)

    def body(c_ref, w_ref, b_ref, o_ref):
        cv = c_ref[...]
        o_ref[...] = _dot(cv * _sigmoid(cv), w_ref[...]) + b_ref[...]

    return pl.pallas_call(
        body, name="ada_forward", grid=(nl, cols // tn),
        in_specs=[pl.BlockSpec((16, d), lambda l, n: (0, 0)),
                  pl.BlockSpec((None, d, tn), lambda l, n: (l, 0, n)),
                  pl.BlockSpec((None, 1, tn), lambda l, n: (l, 0, n))],
        out_specs=pl.BlockSpec((None, 16, tn), lambda l, n: (l, 0, n)),
        out_shape=jax.ShapeDtypeStruct((nl, 16, cols), F32),
        compiler_params=_params(("parallel", "parallel")),
    )(c16, ada_w, ada_b)


def _ada_backward(c16, dmod, ada_w):
    nl, d, cols = ada_w.shape
    tn = _tile(cols, 512)
    nn = cols // tn

    def body(c_ref, g_ref, w_ref, gw_ref, gc_ref):
        cv = c_ref[...]
        s = _sigmoid(cv)
        gv = g_ref[...]
        gw_ref[...] = _dot(cv * s, gv, TN)
        dcond = _dot(gv, w_ref[...], NT)
        row = lax.broadcasted_iota(jnp.int32, dcond.shape, 0)
        dctx = jnp.sum(jnp.where(row >= 8, dcond * (s * (1.0 + cv * (1.0 - s))), 0.0), axis=0, keepdims=True)

        @pl.when((pl.program_id(0) == 0) & (pl.program_id(1) == 0))
        def _():
            gc_ref[...] = jnp.zeros_like(gc_ref)

        gc_ref[...] += dctx

    return pl.pallas_call(
        body, name="ada_backward", grid=(nl, nn),
        in_specs=[pl.BlockSpec((16, d), lambda l, n: (0, 0)),
                  pl.BlockSpec((None, 16, tn), lambda l, n: (l, 0, n)),
                  pl.BlockSpec((None, d, tn), lambda l, n: (l, 0, n))],
        out_specs=[pl.BlockSpec((None, d, tn), lambda l, n: (l, 0, n)),
                   pl.BlockSpec((1, d), lambda l, n: (0, 0))],
        out_shape=[jax.ShapeDtypeStruct((nl, d, cols), F32), jax.ShapeDtypeStruct((1, d), F32)],
        compiler_params=_params(("arbitrary", "arbitrary")),
    )(c16, dmod, ada_w)


def _row_sum16(name, a):
    nl, _, w = a.shape
    tn = _tile(w, 4096)

    def body(a_ref, o_ref):
        o_ref[...] = jnp.sum(a_ref[...], axis=0, keepdims=True)

    return pl.pallas_call(
        body, name=name, grid=(nl, w // tn),
        in_specs=[pl.BlockSpec((None, 16, tn), lambda l, n: (l, 0, n))],
        out_specs=pl.BlockSpec((None, 1, tn), lambda l, n: (l, 0, n)),
        out_shape=jax.ShapeDtypeStruct((nl, 1, w), F32),
        compiler_params=_params(("parallel", "parallel")),
    )(a)


def _pack(arrays, row_align):
    flat = jnp.concatenate([a.reshape(-1).astype(F32) for a in arrays])
    quantum = row_align * LANE
    padded = -(-flat.shape[0] // quantum) * quantum
    return jnp.pad(flat, (0, padded - flat.shape[0])).reshape(-1, LANE)


def _unpack(packed, shapes):
    flat = packed.reshape(-1)
    out, off = [], 0
    for s in shapes:
        n = math.prod(s)
        out.append(flat[off:off + n].reshape(s))
        off += n
    return out


def _grid_pos_tables(n_tokens, dim):
    quarter = dim // 4
    omega = 1.0 / (POS_BASE ** (jnp.arange(quarter, dtype=F32) / quarter))

    def enc(n):
        ang = jnp.arange(n, dtype=F32).reshape(-1, 1) * omega[None, :]
        return jnp.concatenate([jnp.sin(ang), jnp.cos(ang)], axis=-1)

    return enc(n_tokens // GRID_W), enc(GRID_W)


def _ffn_forward(tag, v, x_in, up4, dn4, layer, cw, cb):
    h = _mm_cols(f"ffn_up_{tag}", v, up4, layer, ACT_DTYPE)
    a = _conv_swiglu_fwd(f"ffn_conv_{tag}", h, cw, cb)
    f = _mm_rows(f"ffn_down_{tag}", a, dn4, layer, F32)
    return h, a, f


def _ffn_backward(tag, dx_out, fb, x_mid, v, h, a, up4, dn4, layer, cw, cb, gate5, gam3, gam2, scale4p1):
    def post(dxo, f, gam, gate):
        dy, dgate, dgam = _postnorm_bwd(dxo, f.astype(F32), gam, gate)
        return (dy,), (dgate, dgam)

    (df,), (dgate5, dgam3) = _rowwise(f"ffn_post_bwd_{tag}", post, dx_out.shape[0],
                                      [(dx_out, False), (fb, False)], [gam3, gate5],
                                      [(dx_out.shape[1], ACT_DTYPE, False)], [dx_out.shape[1]] * 2)
    da = _mm_rows_nt(f"ffn_down_dx_{tag}", df, dn4, layer, ACT_DTYPE)
    g_dn = _mm_rows_tn(f"ffn_down_dw_{tag}", a, df, dn4.shape[-2], ACT_DTYPE)
    dhv, dhg, dwv, dwg, dbv, dbg = _conv_swiglu_bwd(f"ffn_conv_bwd_{tag}", da, h, cw, cb)
    dh = (dhv, dhg)
    dcw = jnp.concatenate([dwv, dwg], axis=1)
    dcb = jnp.concatenate([dbv, dbg], axis=1)
    dv = _mm_cols_nt(f"ffn_up_dx_{tag}", dh, up4, layer, F32)
    g_up = _mm_cols_tn(f"ffn_up_dw_{tag}", v, dh, up4.shape[-1], ACT_DTYPE)

    def pre(dvv, x, dxo, gam, s1):
        dx, dshift, dscale, dgam = _prenorm_bwd(dvv, x, gam, s1)
        return (dxo + dx,), (dshift, dscale, dgam)

    d = dx_out.shape[1]
    (dx_mid,), (dshift3, dscale4, dgam2) = _rowwise(
        f"ffn_pre_bwd_{tag}", pre, dx_out.shape[0], [(dv, False), (x_mid, False), (dx_out, False)],
        [gam2, scale4p1], [(d, F32, False)], [d] * 3)
    return dx_mid, g_up, g_dn, dcw, dcb, (dshift3, dscale4, dgate5), (dgam2, dgam3)


def kernel(x, c, ctx, c_ctx, ada_w, ada_b, norm_g, s5_lam_re, s5_lam_im, s5_log_step, s5_b_re, s5_b_im, s5_c_re, s5_c_im, s5_d, s5_glu_w, pool_w, pool_scale, ffn_up, ffn_conv, ffn_conv_b, ffn_down, loss_target, m_c_ctx, m_ada_w, m_ada_b, m_norm_g, m_s5_lam_re, m_s5_lam_im, m_s5_log_step, m_s5_b_re, m_s5_b_im, m_s5_c_re, m_s5_c_im, m_s5_d, m_s5_glu_w, m_pool_w, m_pool_scale, m_ffn_up, m_ffn_conv, m_ffn_conv_b, m_ffn_down, v_c_ctx, v_ada_w, v_ada_b, v_norm_g, v_s5_lam_re, v_s5_lam_im, v_s5_log_step, v_s5_b_re, v_s5_b_im, v_s5_c_re, v_s5_c_im, v_s5_d, v_s5_glu_w, v_pool_w, v_pool_scale, v_ffn_up, v_ffn_conv, v_ffn_conv_b, v_ffn_down):
    ix, iy, ic = _coords()
    chip = 2 * ix + iy
    me = 2 * chip + ic
    _, rows, d = x.shape
    rows_c = ctx.shape[1]
    nl = ada_w.shape[0]
    assert nl == 2 and s5_glu_w.shape[0] == 1 and pool_w.shape[0] == 1
    a_cols = ada_w.shape[2]
    f2s = ffn_up.shape[2]
    f2 = N_CHIP * f2s
    ds = d // N_CHIP
    ng = len(POOL_WINDOWS)
    ch = d // ng
    ps = pool_w.shape[2]

    core = jnp.reshape(ic, (1,)).astype(jnp.int32)
    chip_id = jnp.reshape(chip, (1,)).astype(jnp.int32)
    pos = jnp.concatenate([core, chip_id])
    shards = {"up": _view2d(ffn_up), "down": _view2d(ffn_down), "glu": s5_glu_w[0], "pool": _view2d(pool_w[0])}
    gather_bufs = [_cast_own_half(f"cast_{n}", w, pos) for n, w in shards.items()]

    c_all = _all_gather8("gather_cond", _pack([c], SUB))
    c_all = c_all.reshape(N_DEV, -1)[:, :d]
    c16 = jnp.concatenate([c_all, jnp.broadcast_to(c_ctx[None, :], (N_DEV, d))], axis=0)
    ada_b_mine = lax.dynamic_slice_in_dim(ada_b, chip * a_cols, a_cols, axis=1)
    mod_part = _ada_forward(c16, ada_w, ada_b_mine[:, None, :])
    narrow_shapes = [mod_part.shape, norm_g.shape, pool_scale.shape, ffn_conv.shape]
    narrow = _all_gather8("gather_narrow", _pack([mod_part, norm_g, pool_scale, ffn_conv], SUB))
    per_chip = [_unpack(narrow[2 * s], narrow_shapes) for s in range(N_CHIP)]
    mod_all = jnp.concatenate([p[0] for p in per_chip], axis=-1)
    gam = jnp.concatenate([p[1] for p in per_chip], axis=-1)
    pscale = jnp.concatenate([p[2] for p in per_chip], axis=-1)
    conv_w = jnp.concatenate([p[3] for p in per_chip], axis=-1)
    mod_mine = lax.dynamic_index_in_dim(mod_all, me, axis=1, keepdims=False)
    mod_ctx = mod_all[0, N_DEV]

    def mods(vec):
        s0, s1, g2, s3, s4, g5 = [vec[k * d:(k + 1) * d][None, :] for k in range(N_MOD)]
        return s0, 1.0 + s1, g2, s3, 1.0 + s4, g5

    m0, m1, mc = mods(mod_mine[0]), mods(mod_mine[1]), mods(mod_ctx)
    gains = [[gam[l, k][None, :] for k in range(4)] for l in range(nl)]
    conv_b = ffn_conv_b[:, None, :]

    row_tab, col_tab = _grid_pos_tables(rows, d)
    per_tile = _tile(rows // N_SEG, 256, SUB) // GRID_W
    rep = SUB // per_tile
    row_tab = jnp.repeat(row_tab, rep, axis=0)

    def init(xv, rt, ct, g0, shift, s1):
        pe_r = jnp.concatenate(
            [jnp.broadcast_to(rt[q * rep:q * rep + 1], (GRID_W, d // 2)) for q in range(per_tile)], axis=0)
        pe_c = jnp.concatenate([ct] * per_tile, axis=0)
        x0 = xv + jnp.concatenate([pe_r, pe_c], axis=1)
        return (x0, _prenorm(x0, g0, shift, s1)), ()

    (x0, u), _ = _rowwise("init", init, rows, [(x[0], False)],
                          [(row_tab, SUB), col_tab, gains[0][0], m0[0], m0[1]],
                          [(d, F32, False), (d, ACT_DTYPE, True)])
    (uc,), _ = _rowwise("ctx_prenorm", lambda cv, g0, shift, s1: ((_prenorm(cv, g0, shift, s1),), ()),
                        rows_c, [(ctx[0], False)], [gains[0][0], mc[0], mc[1]], [(d, ACT_DTYPE, True)])
    s5_params = (s5_lam_re[0], s5_lam_im[0], s5_log_step[0], s5_b_re[0], s5_b_im[0], s5_c_re[0], s5_c_im[0])
    (bblk, cblk, atile), s5_vjp = jax.vjp(_s5_prepare, *s5_params)
    buf_up, buf_dn, buf_glu, buf_pool = gather_bufs
    rows_of = {n: w.shape[0] for n, w in shards.items()}
    y, (buf_up, buf_glu) = _s5_forward(u, uc, bblk, cblk, atile, s5_d, [buf_up, buf_glu],
                                       _gather_ici_plan([rows_of["up"], rows_of["glu"]]))
    up4 = _swap_gathered("gather_up", buf_up, core, rows_of["up"]).reshape((N_CHIP,) + ffn_up.shape)
    glu4 = _swap_gathered("gather_glu", buf_glu, core, rows_of["glu"])
    (z,), _ = _rowwise("gelu", lambda yv: ((_gelu(yv),), ()), rows, [(y, True)], [], [(d, ACT_DTYPE, False)])
    zz = _mm_cols("glu_proj", z, glu4[:, None], 0, ACT_DTYPE)

    def glu_out(zzv, xv, gate2, g1, g2, shift3, s4):
        zf = zzv.astype(F32)
        o = zf[:, :d] * _sigmoid(zf[:, d:])
        x1 = xv + gate2 * (o * _rstd(o) * g1)
        return (x1, _prenorm(x1, g2, shift3, s4)), ()

    (x1, v0), _ = _rowwise("glu_resid", glu_out, rows, [(zz, False), (x0, False)],
                           [m0[2], gains[0][1], gains[0][2], m0[3], m0[4]], [(d, F32, False), (d, ACT_DTYPE, False)])
    h0, (buf_dn, buf_pool) = _mm_cols(
        "ffn_up_l0", v0, up4, 0, ACT_DTYPE,
        ride=([buf_dn, buf_pool], _gather_ici_plan([rows_of["down"], rows_of["pool"]])))
    dn4 = _swap_gathered("gather_down", buf_dn, core, rows_of["down"]).reshape((N_CHIP,) + ffn_down.shape)
    pool_full = _swap_gathered("gather_pool", buf_pool, core, rows_of["pool"])
    pool_full = pool_full.reshape(N_CHIP, ng, ps, ch).transpose(1, 0, 2, 3).reshape(ng, ch, ch)
    a0 = _conv_swiglu_fwd("ffn_conv_l0", h0, conv_w[0], conv_b[0])
    f0 = _mm_rows("ffn_down_l0", a0, dn4, 0, F32)

    def ffn_out(fv, xv, gate5, g3, g0n, shift0, s1):
        x2 = xv + gate5 * (fv * _rstd(fv) * g3)
        return (x2, _prenorm(x2, g0n, shift0, s1), fv), ()

    (x2, u1, fb0), _ = _rowwise("ffn_resid_l0", ffn_out, rows, [(f0, False), (x1, False)],
                                [m0[5], gains[0][3], gains[1][0], m1[0], m1[1]],
                                [(d, F32, False), (d, ACT_DTYPE, False), (d, ACT_DTYPE, False)])

    p1 = _pool_band("pool_band", u1, False, ACT_DTYPE)
    yr = _mm_grp("pool_proj", p1, pool_full, NN, F32)

    def pool_out(yv, xv, ps_, gate2, g1, g2, shift3, s4):
        o = yv * ps_
        x1n = xv + gate2 * (o * _rstd(o) * g1)
        return (x1n, _prenorm(x1n, g2, shift3, s4), yv), ()

    (x3, v1, yb), _ = _rowwise("pool_resid", pool_out, rows, [(yr, False), (x2, False)],
                               [pscale, m1[2], gains[1][1], gains[1][2], m1[3], m1[4]],
                               [(d, F32, False), (d, ACT_DTYPE, False), (d, ACT_DTYPE, False)])
    h1, a1, f1 = _ffn_forward("l1", v1, x3, up4, dn4, 1, conv_w[1], conv_b[1])

    def loss_head(fv, xv, tv, gate5, g3):
        err = xv + gate5 * (fv * _rstd(fv) * g3) - tv
        return (err * (1.0 / d), fv), (err * err,)

    (dx4, fb1), (sq,) = _rowwise("loss_head", loss_head, rows, [(f1, False), (x3, False), (loss_target[0], False)],
                                 [m1[5], gains[1][3]], [(d, F32, False), (d, ACT_DTYPE, False)], [d])
    loss = lax.psum(0.5 * jnp.sum(sq) / d, ("x", "y", "c"))

    dx3, g_up1, g_dn1, dcw1, dcb1, dmod_ffn1, (dgam12, dgam13) = _ffn_backward(
        "l1", dx4, fb1, x3, v1, h1, a1, up4, dn4, 1, conv_w[1], conv_b[1], m1[5], gains[1][3], gains[1][2], m1[4])

    def pool_post(dxo, yv, ps_, g1, gate2):
        yraw = yv.astype(F32)
        dy, dgate, dgam = _postnorm_bwd(dxo, yraw * ps_, g1, gate2)
        return (dy * ps_,), (dgate, dgam, dy * yraw)

    (dyr,), (dgate2_1, dgam11, dpscale) = _rowwise("pool_post_bwd", pool_post, rows, [(dx3, False), (yb, False)],
                                                   [pscale, gains[1][1], m1[2]], [(d, ACT_DTYPE, False)], [d] * 3)
    dp1 = _mm_grp("pool_proj_dx", dyr, pool_full, NT, ACT_DTYPE)
    g_pool = _mm_grp_tn("pool_proj_dw", p1, dyr, ng, ACT_DTYPE)
    du1 = _pool_band("pool_band_bwd", dp1, True, F32)

    def pre_bwd(duv, xv, dxo, g0, s1):
        dx, dshift, dscale, dgam = _prenorm_bwd(duv, xv, g0, s1)
        return (dxo + dx,), (dshift, dscale, dgam)

    (dx2,), (dshift0_1, dscale1_1, dgam10) = _rowwise(
        "pool_pre_bwd", pre_bwd, rows, [(du1, False), (x2, False), (dx3, False)],
        [gains[1][0], m1[1]], [(d, F32, False)], [d] * 3)

    dx1, g_up0, g_dn0, dcw0, dcb0, dmod_ffn0, (dgam02, dgam03) = _ffn_backward(
        "l0", dx2, fb0, x1, v0, h0, a0, up4, dn4, 0, conv_w[0], conv_b[0], m0[5], gains[0][3], gains[0][2], m0[4])

    def glu_post(dxo, zzv, g1, gate2):
        zf = zzv.astype(F32)
        val, s = zf[:, :d], _sigmoid(zf[:, d:])
        do, dgate, dgam = _postnorm_bwd(dxo, val * s, g1, gate2)
        return (jnp.concatenate([do * s, do * val * (s * (1.0 - s))], axis=1),), (dgate, dgam)

    (dzz,), (dgate2_0, dgam01) = _rowwise("glu_post_bwd", glu_post, rows, [(dx1, False), (zz, False)],
                                          [gains[0][1], m0[2]], [(2 * d, ACT_DTYPE, False)], [d] * 2)
    dz = _mm_cols_nt("glu_proj_dx", dzz, glu4[:, None], 0, F32)
    g_glu = _mm_cols_tn("glu_proj_dw", z, dzz, glu4.shape[-1], ACT_DTYPE)
    (dy,), _ = _rowwise("gelu_bwd", lambda dzv, yv: ((dzv * _gelu_grad(yv),), ()), rows,
                        [(dz, False), (y, True)], [], [(d, ACT_DTYPE, True)])
    g_pool4 = g_pool.reshape(ng, N_CHIP, ps, ch).transpose(1, 0, 2, 3).reshape(N_CHIP, ng * ps, ch)
    big = {"up0": g_up0, "up1": g_up1, "dn0": g_dn0, "dn1": g_dn1, "glu": g_glu, "pool": g_pool4}
    pairs = [_pair_sum(f"pair_{n}", g, core) for n, g in big.items()]
    ride_shapes, ride_copies = _chip_exchange_plan(pairs)
    (du0, duc, d_bblk, d_cblk, d_atile, d_dsk), others = _s5_backward(
        u, uc, dy, bblk, cblk, atile, s5_d, pairs, ride_shapes, ride_copies)
    (gx,), (dshift0_0, dscale1_0, dgam00) = _rowwise(
        "s5_pre_bwd", pre_bwd, rows, [(du0, True), (x0, False), (dx1, False)],
        [gains[0][0], m0[1]], [(d, F32, False)], [d] * 3)

    def ctx_bwd(duv, cv, g0, s1):
        _, dshift, dscale, dgam = _prenorm_bwd(duv, cv, g0, s1)
        return (), (dshift, dscale, dgam)

    _, (dshift_c, dscale_c, dgam00c) = _rowwise("ctx_pre_bwd", ctx_bwd, rows_c, [(duc, True), (ctx[0], False)],
                                                [gains[0][0], mc[1]], [], [d] * 3)
    g_s5 = s5_vjp((d_bblk, d_cblk, d_atile))

    zero_d = jnp.zeros((1, d), F32)
    dmod_lat = jnp.stack([
        jnp.concatenate([dshift0_0, dscale1_0, dgate2_0, *dmod_ffn0], axis=1),
        jnp.concatenate([dshift0_1, dscale1_1, dgate2_1, *dmod_ffn1], axis=1)])
    dmod_ctx = jnp.stack([jnp.concatenate([dshift_c, dscale_c] + [zero_d] * 4, axis=1),
                          jnp.zeros((1, N_MOD * d), F32)])
    dmod_shape = (nl, 2, N_MOD * d)
    dmod_all = _all_gather8("gather_dmod", _pack([jnp.concatenate([dmod_lat, dmod_ctx], axis=1)], SUB))
    dmod_all = jnp.stack([_unpack(dmod_all[k], [dmod_shape])[0] for k in range(N_DEV)])
    dmod16 = jnp.concatenate([dmod_all[:, :, 0], dmod_all[:, :, 1]], axis=0).transpose(1, 0, 2)
    dmod16_mine = lax.dynamic_slice_in_dim(dmod16, chip * a_cols, a_cols, axis=2)
    g_ada_w, g_cctx_part = _ada_backward(c16, dmod16_mine, ada_w)
    g_ada_b = _row_sum16("ada_bias_grad", dmod16)[:, 0]

    d_gam = jnp.stack([jnp.concatenate([dgam00 + dgam00c, dgam01, dgam02, dgam03], axis=0),
                       jnp.concatenate([dgam10, dgam11, dgam12, dgam13], axis=0)])
    small = [d_gam, 0.5 * g_cctx_part, *g_s5, jnp.sum(d_dsk, axis=0, keepdims=True), dpscale,
             jnp.stack([dcw0, dcw1]), jnp.stack([dcb0[0], dcb1[0]])]
    small_shapes = [s.shape for s in small]
    packed = _pack(small, N_DEV * SUB)
    summed = _all_reduce8("reduce_small", packed.reshape(N_DEV, -1, LANE)).reshape(-1, LANE)
    (r_gam, r_cctx, r_lam_re, r_lam_im, r_log_step, r_b_re, r_b_im, r_c_re, r_c_im,
     r_dsk, r_pscale, r_conv, r_convb) = _unpack(summed, small_shapes)
    g_norm = lax.dynamic_slice_in_dim(r_gam, chip * ds, ds, axis=2)
    g_pscale = lax.dynamic_slice_in_dim(r_pscale, chip * ds, ds, axis=1)
    g_conv = lax.dynamic_slice_in_dim(r_conv, chip * f2s, f2s, axis=2)

    parts = dict(zip(big, zip(pairs, others)))
    d_rows, dn_rows = ffn_up.shape[1], ffn_down.shape[1]
    g_up = _sum_and_swap("reduce_up0", *parts["up0"], pos, None, nl * d_rows, 0)
    g_up = _sum_and_swap("reduce_up1", *parts["up1"], pos, g_up, nl * d_rows, d_rows).reshape(ffn_up.shape)
    g_dn = _sum_and_swap("reduce_dn0", *parts["dn0"], pos, None, nl * dn_rows, 0)
    g_dn = _sum_and_swap("reduce_dn1", *parts["dn1"], pos, g_dn, nl * dn_rows, dn_rows).reshape(ffn_down.shape)
    g_glu_f = _sum_and_swap("reduce_glu", *parts["glu"], pos, None, d, 0)
    g_pool_f = _sum_and_swap("reduce_pool", *parts["pool"], pos, None, ng * ps, 0)

    grads = {
        "c_ctx": r_cctx[0], "ada_w": g_ada_w, "ada_b": g_ada_b, "norm_g": g_norm,
        "s5_lam_re": r_lam_re[None], "s5_lam_im": r_lam_im[None], "s5_log_step": r_log_step[None],
        "s5_b_re": r_b_re[None], "s5_b_im": r_b_im[None], "s5_c_re": r_c_re[None], "s5_c_im": r_c_im[None],
        "s5_d": r_dsk, "s5_glu_w": g_glu_f[None], "pool_w": g_pool_f.reshape(pool_w.shape),
        "pool_scale": g_pscale, "ffn_up": g_up, "ffn_conv": g_conv, "ffn_conv_b": r_convb, "ffn_down": g_dn,
    }
    weights = {
        "c_ctx": (c_ctx, m_c_ctx, v_c_ctx), "ada_w": (ada_w, m_ada_w, v_ada_w), "ada_b": (ada_b, m_ada_b, v_ada_b),
        "norm_g": (norm_g, m_norm_g, v_norm_g), "s5_lam_re": (s5_lam_re, m_s5_lam_re, v_s5_lam_re),
        "s5_lam_im": (s5_lam_im, m_s5_lam_im, v_s5_lam_im), "s5_log_step": (s5_log_step, m_s5_log_step, v_s5_log_step),
        "s5_b_re": (s5_b_re, m_s5_b_re, v_s5_b_re), "s5_b_im": (s5_b_im, m_s5_b_im, v_s5_b_im),
        "s5_c_re": (s5_c_re, m_s5_c_re, v_s5_c_re), "s5_c_im": (s5_c_im, m_s5_c_im, v_s5_c_im),
        "s5_d": (s5_d, m_s5_d, v_s5_d), "s5_glu_w": (s5_glu_w, m_s5_glu_w, v_s5_glu_w),
        "pool_w": (pool_w, m_pool_w, v_pool_w), "pool_scale": (pool_scale, m_pool_scale, v_pool_scale),
        "ffn_up": (ffn_up, m_ffn_up, v_ffn_up), "ffn_conv": (ffn_conv, m_ffn_conv, v_ffn_conv),
        "ffn_conv_b": (ffn_conv_b, m_ffn_conv_b, v_ffn_conv_b), "ffn_down": (ffn_down, m_ffn_down, v_ffn_down),
    }
    names = list(weights)
    large = ("ada_w", "s5_glu_w", "pool_w", "ffn_up", "ffn_down")
    delta, new_m, new_v = {}, {}, {}
    for n in names:
        w, m, v = weights[n]
        if n == "ada_w":
            delta[n], new_m[n], new_v[n] = _adamw(f"adamw_{n}", w, grads[n], m, v)
        elif n in large:
            delta[n], new_m[n], new_v[n], grads[n] = _adamw(f"adamw_{n}", w, grads[n], m, v, emit_grad=True)
        else:
            shape = w.shape
            view = (1, shape[0]) if w.ndim == 1 else shape
            res = _adamw(f"adamw_{n}", *[t.reshape(view) for t in (w, grads[n], m, v)])
            delta[n], new_m[n], new_v[n] = [t.reshape(shape) for t in res]

    return (loss, gx[None], *[grads[n] for n in names], *[delta[n] for n in names],
            *[new_m[n] for n in names], *[new_v[n] for n in names])
```

```python
import math

import jax
import jax.numpy as jnp
from jax import lax
from jax.experimental import pallas as pl
from jax.experimental.pallas import tpu as pltpu

F32 = jnp.float32
MXU_DTYPE = jnp.bfloat16
ACT_DTYPE = jnp.bfloat16

LANE = 128
SUB = 8
PACK = 16
VMEM_LIMIT = 56 * 1024 * 1024
ELEMWISE_BLOCK = 1 << 18
ROWWISE_ROWS = 512

N_DEV = 8
N_CHIP = 4
N_SEG = SUB
S5_GROUP = 16
S5_STATE = 64
S5_CB = LANE
S5_H = (S5_CB // S5_GROUP) * S5_STATE
S5_NS = 2 * S5_H
POOL_WINDOWS = (2, 4, 8, 16)
POOL_HALO = 16
GRID_W = 64
POS_BASE = 10000.0
RMS_EPS = 1e-6
N_MOD = 6

ADAM_LR = 0.001
ADAM_B1 = 0.9
ADAM_B2 = 0.999
ADAM_EPS = 1e-08
ADAM_WD = 0.01
ADAM_STEP = 10

NN = (((1,), (0,)), ((), ()))
NT = (((1,), (1,)), ((), ()))
TN = (((0,), (0,)), ((), ()))
MESH = pl.DeviceIdType.MESH


def _tile(n, cap, align=LANE):
    best = None
    for t in range(align, min(n, cap) + 1, align):
        if n % t == 0:
            best = t
    return n if best is None else best


def _params(sem=None):
    return pltpu.CompilerParams(dimension_semantics=sem, vmem_limit_bytes=VMEM_LIMIT)


def _dot(a, b, dims=NN):
    return lax.dot_general(a.astype(MXU_DTYPE), b.astype(MXU_DTYPE), dims, preferred_element_type=F32)


def _sigmoid(x):
    return 0.5 * jnp.tanh(0.5 * x) + 0.5


_GELU_C = math.sqrt(2.0 / math.pi)
_GELU_K = 0.044715


def _gelu(x):
    return 0.5 * x * (1.0 + jnp.tanh(_GELU_C * (x + _GELU_K * x * x * x)))


def _gelu_grad(x):
    t = jnp.tanh(_GELU_C * (x + _GELU_K * x * x * x))
    return 0.5 * (1.0 + t) + 0.5 * x * (1.0 - t * t) * _GELU_C * (1.0 + 3.0 * _GELU_K * x * x)


def _rstd(x):
    return lax.rsqrt(jnp.mean(x * x, axis=-1, keepdims=True) + RMS_EPS)


def _norm_bwd(dxh, xh, r):
    return r * (dxh - xh * jnp.mean(dxh * xh, axis=-1, keepdims=True))


def _rowwise(name, fn, rows, tiled, vecs, outs, accs=()):
    seg = rows // N_SEG
    tm = _tile(seg, ROWWISE_ROWS, SUB)
    nt, ntp = rows // tm, seg // tm
    n_t, n_v, n_o, n_a = len(tiled), len(vecs), len(outs), len(accs)

    def spec(width, perm):
        if perm:
            return pl.BlockSpec((tm, width), lambda i: (i % ntp, i // ntp))
        return pl.BlockSpec((tm, width), lambda i: (i, 0))

    args, in_specs = [], []
    for arr, perm in tiled:
        width = arr.shape[-1]
        args.append(arr.reshape(seg, N_SEG * width) if perm else arr)
        in_specs.append(spec(width, perm))
    for v in vecs:
        if isinstance(v, tuple):
            args.append(v[0])
            in_specs.append(pl.BlockSpec((v[1], v[0].shape[1]), lambda i: (i, 0)))
        else:
            args.append(v)
            in_specs.append(pl.BlockSpec(v.shape, lambda i: (0, 0)))
    out_shape, out_specs = [], []
    for width, dtype, perm in outs:
        out_shape.append(jax.ShapeDtypeStruct((seg, N_SEG * width) if perm else (rows, width), dtype))
        out_specs.append(spec(width, perm))
    for width in accs:
        out_shape.append(jax.ShapeDtypeStruct((SUB, width), F32))
        out_specs.append(pl.BlockSpec((SUB, width), lambda i: (0, 0)))

    def body(*refs):
        vals = [r[...] for r in refs[:n_t + n_v]]
        o_refs = refs[n_t + n_v:n_t + n_v + n_o]
        a_refs = refs[n_t + n_v + n_o:]
        o_vals, a_vals = fn(*vals)
        for r, v in zip(o_refs, o_vals):
            r[...] = v.astype(r.dtype)
        if n_a:
            @pl.when(pl.program_id(0) == 0)
            def _():
                for r in a_refs:
                    r[...] = jnp.zeros_like(r)
            for r, v in zip(a_refs, a_vals):
                r[...] += v.reshape(tm // SUB, SUB, v.shape[-1]).sum(axis=0)

    res = pl.pallas_call(
        body, name=name, grid=(nt,), in_specs=in_specs, out_specs=out_specs, out_shape=out_shape,
        compiler_params=_params(("arbitrary",)),
    )(*args)
    res = list(res)
    for k, (width, _, perm) in enumerate(outs):
        if perm:
            res[k] = res[k].reshape(rows, width)
    return res[:n_o], [jnp.sum(a, axis=0, keepdims=True) for a in res[n_o:]]


def _prenorm(x, gam, shift, scale1):
    r = _rstd(x)
    return (x * r) * gam * scale1 + shift


def _prenorm_bwd(du, x, gam, scale1):
    r = _rstd(x)
    xh = x * r
    dxn = du * scale1
    dx = _norm_bwd(dxn * gam, xh, r)
    return dx, du, du * (xh * gam), dxn * xh


def _postnorm_bwd(dxo, y, gam, gate):
    r = _rstd(y)
    yh = y * r
    dyn = dxo * gate
    dy = _norm_bwd(dyn * gam, yh, r)
    return dy, dxo * (yh * gam), dyn * yh


def _matmul(name, a, b, dims, grid, a_spec, b_spec, o_spec, out_shape, out_dtype, acc_shape,
            pair=None, ride=None):
    nk = grid[2]
    ins = list(a if pair and pair[0] == "a" else [a]) + list(b if pair and pair[0] == "b" else [b])
    specs = list(a_spec if pair and pair[0] == "a" else [a_spec]) + list(b_spec if pair and pair[0] == "b" else [b_spec])
    n_in = len(ins)
    bufs, copies = ride if ride else ((), None)
    n_ride = len(bufs)

    def body(*refs):
        in_refs = refs[:n_in]
        o_ref = refs[n_in + n_ride]
        ride_refs = refs[n_in + n_ride + 1:n_in + 2 * n_ride + 1]
        scratch = refs[n_in + 2 * n_ride + 1:]
        pid = [pl.program_id(ax) for ax in range(3)]
        if n_ride:
            send_sems, recv_sems = scratch[-2:]

            @pl.when((pid[0] == 0) & (pid[1] == 0) & (pid[2] == 0))
            def _():
                for cp in copies(ride_refs, ride_refs, send_sems, recv_sems):
                    cp.start()

        vals = [r[...] for r in in_refs]
        if pair:
            first = pid[pair[1]] < pair[2]
            picked = jnp.where(first, vals[0], vals[1]) if pair[0] == "a" else jnp.where(first, vals[1], vals[2])
            vals = [picked, vals[2]] if pair[0] == "a" else [vals[0], picked]
        part = _dot(vals[0], vals[1], dims)
        if nk == 1:
            o_ref[...] = part.astype(o_ref.dtype)
        else:
            acc_ref = scratch[0]

            @pl.when(pid[2] == 0)
            def _():
                acc_ref[...] = part

            @pl.when(pid[2] > 0)
            def _():
                acc_ref[...] += part

            @pl.when(pid[2] == nk - 1)
            def _():
                o_ref[...] = acc_ref[...].astype(o_ref.dtype)

        if n_ride:
            @pl.when((pid[0] == grid[0] - 1) & (pid[1] == grid[1] - 1) & (pid[2] == nk - 1))
            def _():
                for cp in copies(ride_refs, ride_refs, send_sems, recv_sems):
                    cp.wait()

    scratch_shapes = [] if nk == 1 else [pltpu.VMEM(acc_shape, F32)]
    out_shapes = [jax.ShapeDtypeStruct(out_shape, out_dtype)]
    if not n_ride:
        return pl.pallas_call(
            body, name=name, grid=grid, in_specs=specs, out_specs=[o_spec], out_shape=out_shapes,
            scratch_shapes=scratch_shapes, compiler_params=_params(("parallel", "parallel", "arbitrary")),
        )(*ins)[0]
    n_sem = 3 * n_ride
    res = pl.pallas_call(
        body, name=name, grid=grid, in_specs=specs + [ANY_SPEC] * n_ride,
        out_specs=[o_spec] + [ANY_SPEC] * n_ride,
        out_shape=out_shapes + [jax.ShapeDtypeStruct(x.shape, x.dtype) for x in bufs],
        input_output_aliases={n_in + i: 1 + i for i in range(n_ride)},
        scratch_shapes=scratch_shapes + [pltpu.SemaphoreType.DMA((n_sem,)), pltpu.SemaphoreType.DMA((n_sem,))],
        compiler_params=pltpu.CompilerParams(
            dimension_semantics=("arbitrary",) * 3, vmem_limit_bytes=VMEM_LIMIT, has_side_effects=True),
    )(*ins, *bufs)
    return res[0], res[1:]


def _mm_cols(name, a, w4, layer, out_dtype, ride=None):
    m, k = a.shape
    ns = w4.shape[-1]
    tm, tn = _tile(m, 1024), _tile(ns, 1536)
    nps = ns // tn
    return _matmul(
        name, a, w4, NN, (m // tm, N_CHIP * nps, 1),
        pl.BlockSpec((tm, k), lambda i, n, kk: (i, 0)),
        pl.BlockSpec((None, None, k, tn), lambda i, n, kk: (n // nps, layer, 0, n % nps)),
        pl.BlockSpec((tm, tn), lambda i, n, kk: (i, n)),
        (m, N_CHIP * ns), out_dtype, None, ride=ride)


def _halves_specs(g, rows_blk, cols_blk, tiles_half, row_of, col_of):
    if not isinstance(g, tuple):
        return g, pl.BlockSpec((rows_blk, cols_blk), lambda *p: (row_of(p), col_of(p)))
    left = pl.BlockSpec((rows_blk, cols_blk), lambda *p: (row_of(p), jnp.minimum(col_of(p), tiles_half - 1)))
    right = pl.BlockSpec((rows_blk, cols_blk), lambda *p: (row_of(p), jnp.maximum(col_of(p) - tiles_half, 0)))
    return g, (left, right)


def _mm_cols_nt(name, g, w4, layer, out_dtype):
    m = (g[0] if isinstance(g, tuple) else g).shape[0]
    k, ns = w4.shape[-2:]
    tm, tk = _tile(m, 1024), _tile(ns, 1536)
    kps = ns // tk
    half = N_CHIP * kps // 2
    g, g_spec = _halves_specs(g, tm, tk, half, lambda p: p[0], lambda p: p[2])
    return _matmul(
        name, g, w4, NT, (m // tm, 1, N_CHIP * kps), g_spec,
        pl.BlockSpec((None, None, k, tk), lambda i, n, kk: (kk // kps, layer, 0, kk % kps)),
        pl.BlockSpec((tm, k), lambda i, n, kk: (i, 0)),
        (m, k), out_dtype, (tm, k), pair=("a", 2, half) if isinstance(g, tuple) else None)


def _mm_cols_tn(name, a, g, ns, out_dtype):
    m, k = a.shape
    tkm, tmk, tn = _tile(m, 2048), _tile(k, 1024), _tile(ns, 1536)
    nps = ns // tn
    half = N_CHIP * nps // 2
    g, g_spec = _halves_specs(g, tkm, tn, half, lambda p: p[2], lambda p: p[1])
    return _matmul(
        name, a, g, TN, (k // tmk, N_CHIP * nps, m // tkm),
        pl.BlockSpec((tkm, tmk), lambda i, n, kk: (kk, i)), g_spec,
        pl.BlockSpec((None, tmk, tn), lambda i, n, kk: (n // nps, i, n % nps)),
        (N_CHIP, k, ns), out_dtype, (tmk, tn), pair=("b", 1, half) if isinstance(g, tuple) else None)


def _mm_rows(name, a, w4, layer, out_dtype):
    m = a.shape[0]
    rs, n = w4.shape[-2:]
    tm, tk = _tile(m, 1024), _tile(rs, 1536)
    kps = rs // tk
    return _matmul(
        name, a, w4, NN, (m // tm, 1, N_CHIP * kps),
        pl.BlockSpec((tm, tk), lambda i, j, kk: (i, kk)),
        pl.BlockSpec((None, None, tk, n), lambda i, j, kk: (kk // kps, layer, kk % kps, 0)),
        pl.BlockSpec((tm, n), lambda i, j, kk: (i, 0)),
        (m, n), out_dtype, (tm, n))


def _mm_rows_nt(name, g, w4, layer, out_dtype):
    m, n = g.shape
    rs = w4.shape[-2]
    tm, tn = _tile(m, 1024), _tile(rs, 1536)
    nps = rs // tn
    return _matmul(
        name, g, w4, NT, (m // tm, N_CHIP * nps, 1),
        pl.BlockSpec((tm, n), lambda i, j, kk: (i, 0)),
        pl.BlockSpec((None, None, tn, n), lambda i, j, kk: (j // nps, layer, j % nps, 0)),
        pl.BlockSpec((tm, tn), lambda i, j, kk: (i, j)),
        (m, N_CHIP * rs), out_dtype, None)


def _mm_rows_tn(name, a, g, rs, out_dtype):
    m = a.shape[0]
    n = g.shape[1]
    tkm, tmr, tn = _tile(m, 2048), _tile(rs, 1536), _tile(n, 1024)
    mps = rs // tmr
    return _matmul(
        name, a, g, TN, (N_CHIP * mps, n // tn, m // tkm),
        pl.BlockSpec((tkm, tmr), lambda i, j, kk: (kk, i)),
        pl.BlockSpec((tkm, tn), lambda i, j, kk: (kk, j)),
        pl.BlockSpec((None, tmr, tn), lambda i, j, kk: (i // mps, i % mps, j)),
        (N_CHIP, rs, n), out_dtype, (tmr, tn))


def _mm_grp(name, a, w, dims, out_dtype):
    m = a.shape[0]
    ng, ch = w.shape[:2]
    tm = _tile(m, 1024)
    return _matmul(
        name, a, w, dims, (m // tm, ng, 1),
        pl.BlockSpec((tm, ch), lambda i, g, kk: (i, g)),
        pl.BlockSpec((None, ch, ch), lambda i, g, kk: (g, 0, 0)),
        pl.BlockSpec((tm, ch), lambda i, g, kk: (i, g)),
        (m, ng * ch), out_dtype, None)


def _mm_grp_tn(name, a, g, ng, out_dtype):
    m = a.shape[0]
    ch = a.shape[1] // ng
    tk = _tile(m, 2048)
    return _matmul(
        name, a, g, TN, (ng, 1, m // tk),
        pl.BlockSpec((tk, ch), lambda i, j, kk: (kk, i)),
        pl.BlockSpec((tk, ch), lambda i, j, kk: (kk, i)),
        pl.BlockSpec((None, ch, ch), lambda i, j, kk: (i, 0, 0)),
        (ng, ch, ch), out_dtype, (ch, ch))


def _view2d(a):
    return a.reshape(-1, a.shape[-1])


def _elementwise(name, fn, ins, out_dtypes):
    r, c = ins[0].shape
    lanes = -(-c // LANE) * LANE
    tr = _tile(r, max(PACK, ELEMWISE_BLOCK // lanes), PACK)
    spec = pl.BlockSpec((tr, c), lambda i: (i, 0))

    def body(*refs):
        outs = fn(*[x[...] for x in refs[:len(ins)]])
        for o_ref, o in zip(refs[len(ins):], outs):
            o_ref[...] = o.astype(o_ref.dtype)

    return pl.pallas_call(
        body, name=name, grid=(r // tr,), in_specs=[spec] * len(ins), out_specs=[spec] * len(out_dtypes),
        out_shape=[jax.ShapeDtypeStruct((r, c), d) for d in out_dtypes],
        compiler_params=_params(("parallel",)),
    )(*ins)


def _adamw_math(w, g, m, v):
    m = ADAM_B1 * m + (1.0 - ADAM_B1) * g
    v = ADAM_B2 * v + (1.0 - ADAM_B2) * (g * g)
    m_hat = m / (1.0 - ADAM_B1 ** ADAM_STEP)
    v_hat = v / (1.0 - ADAM_B2 ** ADAM_STEP)
    delta = -ADAM_LR * (m_hat / (jnp.sqrt(v_hat) + ADAM_EPS) + ADAM_WD * w)
    return delta, m, v


def _adamw(name, w, g, m, v, emit_grad=False):
    shape = w.shape
    fn = (lambda *t: _adamw_math(*t) + (t[1],)) if emit_grad else _adamw_math
    outs = _elementwise(name, fn, [_view2d(w), _view2d(g), _view2d(m), _view2d(v)], [F32] * (4 if emit_grad else 3))
    return [o.reshape(shape) for o in outs]


def _coords():
    return lax.axis_index("x"), lax.axis_index("y"), lax.axis_index("c")


def _peer(x, y, c, k):
    return (x ^ (k >> 2), y ^ ((k >> 1) & 1), c ^ (k & 1))


def _all_gather8(name, block):
    r = block.shape[0]

    def body(x_ref, out_ref, send_sems, recv_sems):
        x, y, c = _coords()
        me = 4 * x + 2 * y + c
        out_ref[me] = x_ref[...]
        copies = []
        for k in range(1, N_DEV):
            cp = pltpu.make_async_remote_copy(
                src_ref=x_ref, dst_ref=out_ref.at[me], send_sem=send_sems.at[k], recv_sem=recv_sems.at[k],
                device_id=_peer(x, y, c, k), device_id_type=MESH)
            cp.start()
            copies.append(cp)
        for cp in copies:
            cp.wait()

    return pl.pallas_call(
        body, name=name,
        in_specs=[pl.BlockSpec(memory_space=pltpu.VMEM)], out_specs=pl.BlockSpec(memory_space=pltpu.VMEM),
        out_shape=jax.ShapeDtypeStruct((N_DEV, r, LANE), F32),
        scratch_shapes=[pltpu.SemaphoreType.DMA((N_DEV,)), pltpu.SemaphoreType.DMA((N_DEV,))],
        compiler_params=pltpu.CompilerParams(vmem_limit_bytes=VMEM_LIMIT),
    )(block)


def _all_reduce8(name, parts):
    r = parts.shape[1]

    def body(p_ref, out_ref, rbuf, send1, recv1, send2, recv2):
        x, y, c = _coords()
        me = 4 * x + 2 * y + c
        first = []
        for k in range(1, N_DEV):
            px, py, pc = _peer(x, y, c, k)
            cp = pltpu.make_async_remote_copy(
                src_ref=p_ref.at[4 * px + 2 * py + pc], dst_ref=rbuf.at[me], send_sem=send1.at[k],
                recv_sem=recv1.at[k], device_id=(px, py, pc), device_id_type=MESH)
            cp.start()
            first.append(cp)
        rbuf[me] = p_ref[me]
        for cp in first:
            cp.wait()
        acc = rbuf[0]
        for d in range(1, N_DEV):
            acc = acc + rbuf[d]
        out_ref[me] = acc
        second = []
        for k in range(1, N_DEV):
            cp = pltpu.make_async_remote_copy(
                src_ref=out_ref.at[me], dst_ref=out_ref.at[me], send_sem=send2.at[k], recv_sem=recv2.at[k],
                device_id=_peer(x, y, c, k), device_id_type=MESH)
            cp.start()
            second.append(cp)
        for cp in second:
            cp.wait()

    return pl.pallas_call(
        body, name=name,
        in_specs=[pl.BlockSpec(memory_space=pltpu.VMEM)], out_specs=pl.BlockSpec(memory_space=pltpu.VMEM),
        out_shape=jax.ShapeDtypeStruct((N_DEV, r, LANE), F32),
        scratch_shapes=[pltpu.VMEM((N_DEV, r, LANE), F32)] + [pltpu.SemaphoreType.DMA((N_DEV,))] * 4,
        compiler_params=pltpu.CompilerParams(vmem_limit_bytes=VMEM_LIMIT),
    )(parts)


ANY_SPEC = pl.BlockSpec(memory_space=pl.ANY)
COMM_BLOCK_BYTES = 4 << 20


def _staged_call(name, body, core, grid, in_specs, ins, out_shape, scratch, aliases=None):
    return pl.pallas_call(
        body, name=name,
        grid_spec=pltpu.PrefetchScalarGridSpec(
            num_scalar_prefetch=1, grid=grid, in_specs=in_specs, out_specs=ANY_SPEC, scratch_shapes=scratch),
        out_shape=out_shape, input_output_aliases=aliases or {},
        compiler_params=pltpu.CompilerParams(
            dimension_semantics=("arbitrary",) * len(grid), vmem_limit_bytes=VMEM_LIMIT, has_side_effects=True),
    )(core, *ins)


def _rows_tile(rows, cols, itemsize):
    return _tile(rows, max(PACK, COMM_BLOCK_BYTES // (cols * itemsize)), PACK)


def _chip_peer(x, y, c, k):
    return (x ^ (k >> 1), y ^ (k & 1), c)


def _cast_own_half(name, w, pos):
    r, cols = w.shape
    h = r // 2
    tr = _tile(h, max(PACK, ELEMWISE_BLOCK // cols), PACK)
    nb = h // tr

    def body(p_ref, w_ref, o_ref):
        o_ref[...] = w_ref[...].astype(o_ref.dtype)

    return pl.pallas_call(
        body, name=name,
        grid_spec=pltpu.PrefetchScalarGridSpec(
            num_scalar_prefetch=1, grid=(nb,),
            in_specs=[pl.BlockSpec((tr, cols), lambda j, p: (p[0] * nb + j, 0))],
            out_specs=pl.BlockSpec((tr, cols), lambda j, p: ((2 * p[1] + p[0]) * nb + j, 0))),
        out_shape=jax.ShapeDtypeStruct((N_CHIP * r, cols), MXU_DTYPE),
        compiler_params=_params(("parallel",)),
    )(pos, w)


def _gather_ici_plan(shard_rows):
    def copies(refs, _, send_sems, recv_sems):
        x, y, c = _coords()
        out = []
        for i, (ref, r) in enumerate(zip(refs, shard_rows)):
            h = r // 2
            mine = ref.at[pl.ds(pl.multiple_of((2 * x + y) * r + c * h, PACK), h)]
            for k in (1, 2, 3):
                n = 3 * i + k - 1
                out.append(pltpu.make_async_remote_copy(
                    src_ref=mine, dst_ref=mine, send_sem=send_sems.at[n], recv_sem=recv_sems.at[n],
                    device_id=_chip_peer(x, y, c, k), device_id_type=MESH))
        return out

    return copies


def _swap_gathered(name, part, core, r):
    cols = part.shape[1]
    h = r // 2
    full = jax.ShapeDtypeStruct(part.shape, part.dtype)
    tr2 = _rows_tile(h, cols, 2)
    nb2 = h // tr2

    def swap_body(c_ref, mine_ref, full_ref, send_sem, recv_sem):
        s, j = pl.program_id(0), pl.program_id(1)
        x, y, c = _coords()
        dst = full_ref.at[pl.ds(pl.multiple_of(s * r + c * h + j * tr2, PACK), tr2)]
        cp = pltpu.make_async_remote_copy(
            src_ref=mine_ref, dst_ref=dst, send_sem=send_sem, recv_sem=recv_sem,
            device_id=(x, y, 1 - c), device_id_type=MESH)
        cp.start()
        cp.wait_send()

        @pl.when((s == N_CHIP - 1) & (j == nb2 - 1))
        def _():
            landed = full_ref.at[pl.ds(0, N_CHIP * h)]
            pltpu.make_async_remote_copy(
                src_ref=landed, dst_ref=landed, send_sem=send_sem, recv_sem=recv_sem,
                device_id=(x, y, 1 - c), device_id_type=MESH).wait_recv()

    return _staged_call(
        name + "_d2d", swap_body, core, (N_CHIP, nb2),
        [pl.BlockSpec((tr2, cols), lambda s, j, c_ref: ((2 * s + c_ref[0]) * nb2 + j, 0))], [part], full,
        [pltpu.SemaphoreType.DMA(()), pltpu.SemaphoreType.DMA(())], aliases={1: 0}).reshape(N_CHIP, r, cols)


def _pair_sum(name, g, core):
    n, r, cols = g.shape
    h = r // 2
    tr = _rows_tile(h, cols, g.dtype.itemsize)
    nb = h // tr
    half = jax.ShapeDtypeStruct((n, h, cols), g.dtype)

    def send_body(c_ref, g_ref, got_ref, send_sem, recv_sem):
        s, j = pl.program_id(0), pl.program_id(1)
        x, y, c = _coords()
        dst = got_ref.at[pl.ds(pl.multiple_of(s * h + j * tr, PACK), tr)]
        cp = pltpu.make_async_remote_copy(
            src_ref=g_ref, dst_ref=dst, send_sem=send_sem, recv_sem=recv_sem,
            device_id=(x, y, 1 - c), device_id_type=MESH)
        cp.start()
        cp.wait_send()

        @pl.when((s == n - 1) & (j == nb - 1))
        def _():
            pltpu.make_async_remote_copy(
                src_ref=got_ref, dst_ref=got_ref, send_sem=send_sem, recv_sem=recv_sem,
                device_id=(x, y, 1 - c), device_id_type=MESH).wait_recv()

    got = _staged_call(
        name + "_send", send_body, core, (n, nb),
        [pl.BlockSpec((tr, cols), lambda s, j, c_ref: ((2 * s + 1 - c_ref[0]) * nb + j, 0))],
        [g.reshape(n * r, cols)], jax.ShapeDtypeStruct((n * h, cols), g.dtype),
        [pltpu.SemaphoreType.DMA(()), pltpu.SemaphoreType.DMA(())]).reshape(n, h, cols)

    def add_body(c_ref, own_ref, got_ref, o_ref):
        o_ref[...] = (own_ref[...].astype(F32) + got_ref[...].astype(F32)).astype(o_ref.dtype)

    blk = pl.BlockSpec((None, tr, cols), lambda s, j, c_ref: (s, j, 0))
    return pl.pallas_call(
        add_body, name=name + "_add",
        grid_spec=pltpu.PrefetchScalarGridSpec(
            num_scalar_prefetch=1, grid=(n, nb),
            in_specs=[pl.BlockSpec((None, tr, cols), lambda s, j, c_ref: (s, c_ref[0] * nb + j, 0)), blk],
            out_specs=blk),
        out_shape=half, compiler_params=_params(("parallel", "parallel")),
    )(core, g, got)


def _chip_exchange_plan(pairs):
    shapes = [jax.ShapeDtypeStruct((N_CHIP - 1,) + p.shape[1:], p.dtype) for p in pairs]

    def copies(in_refs, out_refs, send_sems, recv_sems):
        x, y, c = _coords()
        out = []
        for i, (src, dst) in enumerate(zip(in_refs, out_refs)):
            for k in (1, 2, 3):
                px, py, pc = _chip_peer(x, y, c, k)
                n = 3 * i + k - 1
                out.append(pltpu.make_async_remote_copy(
                    src_ref=src.at[2 * px + py], dst_ref=dst.at[k - 1], send_sem=send_sems.at[n],
                    recv_sem=recv_sems.at[n], device_id=(px, py, pc), device_id_type=MESH))
        return out

    return shapes, copies


def _sum_and_swap(name, pair, got, pos, into, total_rows, base):
    _, h, cols = pair.shape
    tr = _rows_tile(h, cols, 4)
    nb = h // tr

    def body(p_ref, own_ref, a_ref, b_ref, c_ref, *rest):
        full_ref, red_ref, send_sem, recv_sem, local_sem = rest[-5:]
        j = pl.program_id(0)
        x, y, c = _coords()
        red_ref[...] = ((own_ref[...].astype(F32) + a_ref[...].astype(F32)) + b_ref[...].astype(F32)) + c_ref[...].astype(F32)
        dst = full_ref.at[pl.ds(pl.multiple_of(base + c * h + j * tr, SUB), tr)]
        local = pltpu.make_async_copy(red_ref, dst, local_sem)
        remote = pltpu.make_async_remote_copy(
            src_ref=red_ref, dst_ref=dst, send_sem=send_sem, recv_sem=recv_sem,
            device_id=(x, y, 1 - c), device_id_type=MESH)
        local.start()
        remote.start()
        remote.wait_send()
        local.wait()

        @pl.when(j == nb - 1)
        def _():
            landed = full_ref.at[pl.ds(0, h)]
            pltpu.make_async_remote_copy(
                src_ref=landed, dst_ref=landed, send_sem=send_sem, recv_sem=recv_sem,
                device_id=(x, y, 1 - c), device_id_type=MESH).wait_recv()

    in_specs = [pl.BlockSpec((None, tr, cols), lambda j, p: (p[1], j, 0))]
    in_specs += [pl.BlockSpec((None, tr, cols), (lambda j, p, k=k: (k, j, 0))) for k in range(N_CHIP - 1)]
    ins = [pair, got, got, got]
    aliases = None
    if into is not None:
        in_specs.append(ANY_SPEC)
        ins.append(into)
        aliases = {5: 0}
    return _staged_call(
        name, body, pos, (nb,), in_specs, ins, jax.ShapeDtypeStruct((total_rows, cols), F32),
        [pltpu.VMEM((tr, cols), F32)] + [pltpu.SemaphoreType.DMA(())] * 3, aliases)


def _cmul(ar, ai, br, bi):
    return ar * br - ai * bi, ar * bi + ai * br


def _cpow(ar, ai, n):
    rr, ri = jnp.ones_like(ar), jnp.zeros_like(ai)
    br, bi = ar, ai
    while n:
        if n & 1:
            rr, ri = _cmul(rr, ri, br, bi)
        br, bi = _cmul(br, bi, br, bi)
        n >>= 1
    return rr, ri


def _tile_rows(t):
    if isinstance(t, int):
        return pl.ds(t * SUB, SUB)
    return pl.ds(pl.multiple_of(t * SUB, SUB), SUB)


SCAN_UNROLL = 8


def _unrolled_loop(n, body, carry):
    trips = n // SCAN_UNROLL

    def trip(o, c):
        for k in range(SCAN_UNROLL):
            c = body(o * SCAN_UNROLL + k, c)
        return c

    carry = lax.fori_loop(0, trips, trip, carry)
    for i in range(trips * SCAN_UNROLL, n):
        carry = body(i, carry)
    return carry


def _scan_setup(buf, steps, ar, ai, h0r, h0i, rev):
    def total(i, carry):
        sr, si = carry
        rows = _tile_rows(steps - 1 - i if rev else i)
        pr, pi = _cmul(ar, ai, sr, si)
        return pr + buf[rows, 0:S5_H], pi + buf[rows, S5_H:S5_NS]

    zero = jnp.zeros((SUB, S5_H), F32)
    tot_r, tot_i = _unrolled_loop(steps, total, (zero, zero))
    pw_r, pw_i = _cpow(ar[0:1], ai[0:1], steps)
    row = lax.broadcasted_iota(jnp.int32, (SUB, S5_H), 0)
    cur_r, cur_i = h0r, h0i
    init_r, init_i = zero, zero
    for s in (range(N_SEG - 1, -1, -1) if rev else range(N_SEG)):
        init_r = jnp.where(row == s, cur_r, init_r)
        init_i = jnp.where(row == s, cur_i, init_i)
        nr, ni = _cmul(pw_r, pw_i, cur_r, cur_i)
        cur_r, cur_i = nr + tot_r[s:s + 1], ni + tot_i[s:s + 1]
    return init_r, init_i, cur_r, cur_i


def _scan(buf, steps, ar, ai, h0r, h0i, rev, store):
    init_r, init_i, fin_r, fin_i = _scan_setup(buf, steps, ar, ai, h0r, h0i, rev)
    if store:
        def step(i, carry):
            hr, hi = carry
            rows = _tile_rows(steps - 1 - i if rev else i)
            pr, pi = _cmul(ar, ai, hr, hi)
            hr, hi = pr + buf[rows, 0:S5_H], pi + buf[rows, S5_H:S5_NS]
            buf[rows, 0:S5_H] = hr
            buf[rows, S5_H:S5_NS] = hi
            return hr, hi

        _unrolled_loop(steps, step, (init_r, init_i))
    return fin_r, fin_i


def _adjoint_scan(gbuf, hbuf, steps, ar, ai, l0r, l0i, hin_r, hin_i, rev):
    ci = -ai
    arev = not rev
    init_r, init_i, fin_r, fin_i = _scan_setup(gbuf, steps, ar, ci, l0r, l0i, arev)
    zero = jnp.zeros((SUB, S5_H), F32)

    def update(t, hp_r, hp_i, carry):
        lr, li, dr, di = carry
        rows = _tile_rows(t)
        pr, pi = _cmul(ar, ci, lr, li)
        lr, li = pr + gbuf[rows, 0:S5_H], pi + gbuf[rows, S5_H:S5_NS]
        gbuf[rows, 0:S5_H] = lr
        gbuf[rows, S5_H:S5_NS] = li
        return lr, li, dr + lr * hp_r + li * hp_i, di + li * hp_r - lr * hp_i

    def step(i, carry):
        t = steps - 1 - i if rev is False else i
        prev = _tile_rows(t - 1 if rev is False else t + 1)
        return update(t, hbuf[prev, 0:S5_H], hbuf[prev, S5_H:S5_NS], carry)

    carry = _unrolled_loop(steps - 1, step, (init_r, init_i, zero, zero))
    row = lax.broadcasted_iota(jnp.int32, (SUB, S5_H), 0)
    if rev:
        last, edge, shift, t = _tile_rows(0), N_SEG - 1, SUB - 1, steps - 1
    else:
        last, edge, shift, t = _tile_rows(steps - 1), 0, 1, 0
    hp_r = jnp.where(row == edge, hin_r, pltpu.roll(hbuf[last, 0:S5_H], shift, 0))
    hp_i = jnp.where(row == edge, hin_i, pltpu.roll(hbuf[last, S5_H:S5_NS], shift, 0))
    _, _, dr, di = update(t, hp_r, hp_i, carry)
    return fin_r, fin_i, dr, di


def _s5_chunk(rows):
    return _tile(rows, 512, PACK)


def _s5_forward(u, uc, bblk, cblk, atile, dsk, ride_bufs, ride_copies):
    rows, d = u.shape
    rows_c = uc.shape[0]
    nj = d // S5_CB
    steps, steps_c = rows // N_SEG, rows_c // N_SEG
    rc = _s5_chunk(rows)
    n_ride = len(ride_bufs)
    n_sem = 3 * n_ride

    def body(u_ref, uc_ref, b_ref, c_ref, a_ref, d_ref, *rest):
        y_ref = rest[n_ride]
        ride = rest[n_ride + 1:2 * n_ride + 1]
        buf, bufc, send_sems, recv_sems = rest[2 * n_ride + 1:]

        @pl.when(pl.program_id(0) == 0)
        def _():
            for cp in ride_copies(ride, ride, send_sems, recv_sems):
                cp.start()

        zero = jnp.zeros((1, S5_H), F32)
        for dr in (0, 1):
            rev = dr == 1
            ar, ai = a_ref[dr, :, 0:S5_H], a_ref[dr, :, S5_H:S5_NS]
            bm, cm = b_ref[dr].astype(MXU_DTYPE), c_ref[dr].astype(MXU_DTYPE)
            bufc[...] = _dot(uc_ref[...], bm)
            fin_r, fin_i = _scan(bufc, steps_c, ar, ai, zero, zero, rev, False)

            def project(r, _):
                rs = pl.ds(pl.multiple_of(r * rc, rc), rc)
                buf[rs, :] = _dot(u_ref[rs, :], bm)
                return 0

            lax.fori_loop(0, rows // rc, project, 0)
            _scan(buf, steps, ar, ai, fin_r, fin_i, rev, True)

            def readout(r, _):
                rs = pl.ds(pl.multiple_of(r * rc, rc), rc)
                yv = _dot(buf[rs, :], cm)
                if dr == 0:
                    y_ref[rs, :] = u_ref[rs, :].astype(F32) * d_ref[...] + yv
                else:
                    y_ref[rs, :] += yv
                return 0

            lax.fori_loop(0, rows // rc, readout, 0)

        @pl.when(pl.program_id(0) == nj - 1)
        def _():
            for cp in ride_copies(ride, ride, send_sems, recv_sems):
                cp.wait()

    res = pl.pallas_call(
        body, name="s5_forward", grid=(nj,),
        in_specs=[
            pl.BlockSpec((rows, S5_CB), lambda j: (0, j)),
            pl.BlockSpec((rows_c, S5_CB), lambda j: (0, j)),
            pl.BlockSpec((2, None, S5_CB, S5_NS), lambda j: (0, j, 0, 0)),
            pl.BlockSpec((2, None, S5_NS, S5_CB), lambda j: (0, j, 0, 0)),
            pl.BlockSpec((2, None, SUB, S5_NS), lambda j: (0, j, 0, 0)),
            pl.BlockSpec((1, S5_CB), lambda j: (0, j)),
        ] + [ANY_SPEC] * n_ride,
        out_specs=[pl.BlockSpec((rows, S5_CB), lambda j: (0, j))] + [ANY_SPEC] * n_ride,
        out_shape=[jax.ShapeDtypeStruct((rows, d), F32)]
        + [jax.ShapeDtypeStruct(b.shape, b.dtype) for b in ride_bufs],
        input_output_aliases={6 + i: 1 + i for i in range(n_ride)},
        scratch_shapes=[pltpu.VMEM((rows, S5_NS), F32), pltpu.VMEM((rows_c, S5_NS), F32),
                        pltpu.SemaphoreType.DMA((n_sem,)), pltpu.SemaphoreType.DMA((n_sem,))],
        compiler_params=pltpu.CompilerParams(
            dimension_semantics=("arbitrary",), vmem_limit_bytes=VMEM_LIMIT, has_side_effects=True),
    )(u, uc, bblk, cblk, atile, dsk, *ride_bufs)
    return res[0], res[1:]


def _s5_backward(u, uc, dy, bblk, cblk, atile, dsk, ride_ins, ride_shapes, ride_copies):
    rows, d = u.shape
    rows_c = uc.shape[0]
    nj = d // S5_CB
    steps, steps_c = rows // N_SEG, rows_c // N_SEG
    rc = _s5_chunk(rows)
    nchunk = rows // rc
    n_ride = len(ride_ins)
    n_sem = 3 * n_ride

    def body(u_ref, uc_ref, dy_ref, b_ref, c_ref, a_ref, d_ref, *rest):
        ride_in = rest[:n_ride]
        du_ref, duc_ref, db_ref, dc_ref, da_ref, dd_ref = rest[n_ride:n_ride + 6]
        ride_out = rest[n_ride + 6:2 * n_ride + 6]
        hbuf, gbuf, hcbuf, gcbuf, send_sems, recv_sems = rest[2 * n_ride + 6:]

        @pl.when(pl.program_id(0) == 0)
        def _():
            for cp in ride_copies(ride_in, ride_out, send_sems, recv_sems):
                cp.start()

        zero = jnp.zeros((1, S5_H), F32)
        db_ref[...] = jnp.zeros_like(db_ref)
        dc_ref[...] = jnp.zeros_like(dc_ref)
        dd_ref[...] = jnp.zeros_like(dd_ref)
        for dr in (0, 1):
            rev = dr == 1
            ar, ai = a_ref[dr, :, 0:S5_H], a_ref[dr, :, S5_H:S5_NS]
            bm, cm = b_ref[dr].astype(MXU_DTYPE), c_ref[dr].astype(MXU_DTYPE)
            hcbuf[...] = _dot(uc_ref[...], bm)
            hin_r, hin_i = _scan(hcbuf, steps_c, ar, ai, zero, zero, rev, True)

            def project(r, _):
                rs = pl.ds(pl.multiple_of(r * rc, rc), rc)
                hbuf[rs, :] = _dot(u_ref[rs, :], bm)
                return 0

            lax.fori_loop(0, nchunk, project, 0)
            _scan(hbuf, steps, ar, ai, hin_r, hin_i, rev, True)

            def readout_bwd(r, _):
                rs = pl.ds(pl.multiple_of(r * rc, rc), rc)
                dyv = dy_ref[rs, :]
                gbuf[rs, :] = _dot(dyv, cm, NT)
                dc_ref[dr] += _dot(hbuf[rs, :], dyv, TN)
                return 0

            lax.fori_loop(0, nchunk, readout_bwd, 0)
            lf_r, lf_i, dar, dai = _adjoint_scan(gbuf, hbuf, steps, ar, ai, zero, zero, hin_r, hin_i, rev)
            gcbuf[...] = jnp.zeros_like(gcbuf)
            _, _, dar_c, dai_c = _adjoint_scan(gcbuf, hcbuf, steps_c, ar, ai, lf_r, lf_i, zero, zero, rev)
            da_ref[dr, :, 0:S5_H] = dar + dar_c
            da_ref[dr, :, S5_H:S5_NS] = dai + dai_c

            def project_bwd(r, _):
                rs = pl.ds(pl.multiple_of(r * rc, rc), rc)
                lam = gbuf[rs, :]
                uv = u_ref[rs, :]
                part = _dot(lam, bm, NT)
                db_ref[dr] += _dot(uv, lam, TN)
                if dr == 0:
                    dyv = dy_ref[rs, :].astype(F32)
                    du_ref[rs, :] = part + dyv * d_ref[...]
                    dd_ref[...] += (dyv * uv.astype(F32)).reshape(rc // SUB, SUB, S5_CB).sum(axis=0)
                else:
                    du_ref[rs, :] += part
                return 0

            lax.fori_loop(0, nchunk, project_bwd, 0)
            lam_c = gcbuf[...]
            part_c = _dot(lam_c, bm, NT)
            db_ref[dr] += _dot(uc_ref[...], lam_c, TN)
            if dr == 0:
                duc_ref[...] = part_c
            else:
                duc_ref[...] += part_c

        @pl.when(pl.program_id(0) == nj - 1)
        def _():
            for cp in ride_copies(ride_in, ride_out, send_sems, recv_sems):
                cp.wait()

    blk = lambda r: pl.BlockSpec((r, S5_CB), lambda j: (0, j))
    res = pl.pallas_call(
        body, name="s5_backward", grid=(nj,),
        in_specs=[
            blk(rows), blk(rows_c), blk(rows),
            pl.BlockSpec((2, None, S5_CB, S5_NS), lambda j: (0, j, 0, 0)),
            pl.BlockSpec((2, None, S5_NS, S5_CB), lambda j: (0, j, 0, 0)),
            pl.BlockSpec((2, None, SUB, S5_NS), lambda j: (0, j, 0, 0)),
            pl.BlockSpec((1, S5_CB), lambda j: (0, j)),
        ] + [ANY_SPEC] * n_ride,
        out_specs=[
            blk(rows), blk(rows_c),
            pl.BlockSpec((2, None, S5_CB, S5_NS), lambda j: (0, j, 0, 0)),
            pl.BlockSpec((2, None, S5_NS, S5_CB), lambda j: (0, j, 0, 0)),
            pl.BlockSpec((2, None, SUB, S5_NS), lambda j: (0, j, 0, 0)),
            pl.BlockSpec((SUB, S5_CB), lambda j: (0, j)),
        ] + [ANY_SPEC] * n_ride,
        out_shape=[
            jax.ShapeDtypeStruct((rows, d), F32), jax.ShapeDtypeStruct((rows_c, d), F32),
            jax.ShapeDtypeStruct(bblk.shape, F32), jax.ShapeDtypeStruct(cblk.shape, F32),
            jax.ShapeDtypeStruct(atile.shape, F32), jax.ShapeDtypeStruct((SUB, d), F32),
        ] + list(ride_shapes),
        scratch_shapes=[pltpu.VMEM((rows, S5_NS), F32), pltpu.VMEM((rows, S5_NS), F32),
                        pltpu.VMEM((rows_c, S5_NS), F32), pltpu.VMEM((rows_c, S5_NS), F32),
                        pltpu.SemaphoreType.DMA((n_sem,)), pltpu.SemaphoreType.DMA((n_sem,))],
        compiler_params=pltpu.CompilerParams(
            dimension_semantics=("arbitrary",), vmem_limit_bytes=VMEM_LIMIT, has_side_effects=True),
    )(u, uc, dy, bblk, cblk, atile, dsk, *ride_ins)
    return res[:6], res[6:]


def _s5_prepare(lam_re, lam_im, log_step, b_re, b_im, c_re, c_im):
    nd, g, p = lam_re.shape
    gb = S5_CB // S5_GROUP
    nj = g // gb
    dt = jnp.exp(log_step)[..., None]
    mag = jnp.exp(lam_re * dt)
    abar_re = mag * jnp.cos(lam_im * dt)
    abar_im = mag * jnp.sin(lam_im * dt)
    nr, ni = abar_re - 1.0, abar_im
    den = lam_re * lam_re + lam_im * lam_im
    fr = (nr * lam_re + ni * lam_im) / den
    fi = (ni * lam_re - nr * lam_im) / den
    bbar_re = fr[..., None] * b_re - fi[..., None] * b_im
    bbar_im = fr[..., None] * b_im + fi[..., None] * b_re
    eye = jnp.eye(gb, dtype=bool)

    def diag_in(w):
        w = w.reshape(nd, nj, gb, p, S5_GROUP).transpose(0, 1, 2, 4, 3)
        w = jnp.where(eye[None, None, :, None, :, None], w[:, :, :, :, None, :], 0.0)
        return w.reshape(nd, nj, gb * S5_GROUP, gb * p)

    def diag_out(w):
        w = w.reshape(nd, nj, gb, S5_GROUP, p).transpose(0, 1, 2, 4, 3)
        w = jnp.where(eye[None, None, :, None, :, None], w[:, :, :, :, None, :], 0.0)
        return w.reshape(nd, nj, gb * p, gb * S5_GROUP)

    bblk = jnp.concatenate([diag_in(bbar_re), diag_in(bbar_im)], axis=-1)
    cblk = jnp.concatenate([diag_out(c_re), -diag_out(c_im)], axis=-2)
    a2 = jnp.concatenate([abar_re.reshape(nd, nj, gb * p), abar_im.reshape(nd, nj, gb * p)], axis=-1)
    atile = jnp.broadcast_to(a2[:, :, None, :], (nd, nj, SUB, 2 * gb * p))
    return bblk, cblk, atile


def _shifted(x, prev_row, next_row):
    n = x.shape[0]
    row = lax.broadcasted_iota(jnp.int32, (SUB, x.shape[1]), 0)
    xp = pltpu.roll(x, 1, 0)
    xn = pltpu.roll(x, n - 1, 0)
    xp = jnp.concatenate([jnp.where(row == 0, prev_row, xp[:SUB]), xp[SUB:]], axis=0)
    xn = jnp.concatenate([xn[:n - SUB], jnp.where(row == SUB - 1, next_row, xn[n - SUB:])], axis=0)
    return xp, xn


def _edge_rows(ref, r0, n, total, group):
    lo = pl.multiple_of(jnp.maximum(r0 - group, 0), group)
    hi = pl.multiple_of(jnp.minimum(r0 + n, total - group), group)
    prev_row = ref[pl.ds(lo, group), :].astype(F32)[group - 1:group] * (r0 > 0).astype(F32)
    next_row = ref[pl.ds(hi, group), :].astype(F32)[0:1] * (r0 + n < total).astype(F32)
    return prev_row, next_row


def _conv_rows(ref, r0, n, total, w_ref, b_ref):
    x = ref[pl.ds(r0, n), :].astype(F32)
    xp, xn = _shifted(x, *_edge_rows(ref, r0, n, total, PACK))
    hc = w_ref[0:1, :] * xp + w_ref[1:2, :] * x + w_ref[2:3, :] * xn + b_ref[...]
    return hc, xp, x, xn


def _conv_specs(rows, f, tc):
    nt = f // tc
    val = lambda r: pl.BlockSpec((r, tc), lambda j: (0, j))
    gate = lambda r: pl.BlockSpec((r, tc), lambda j: (0, j + nt))
    return val, gate


def _conv_swiglu_fwd(name, h, cw, cb):
    rows, f2 = h.shape
    f = f2 // 2
    tc = _tile(f, 256)
    rc = _tile(rows, 256, PACK)
    val, gate = _conv_specs(rows, f, tc)

    def body(hv_ref, hg_ref, wv_ref, wg_ref, bv_ref, bg_ref, a_ref):
        def chunk(r, _):
            r0 = pl.multiple_of(r * rc, rc)
            hv = _conv_rows(hv_ref, r0, rc, rows, wv_ref, bv_ref)[0]
            hg = _conv_rows(hg_ref, r0, rc, rows, wg_ref, bg_ref)[0]
            a_ref[pl.ds(r0, rc), :] = (hg * _sigmoid(hg) * hv).astype(a_ref.dtype)
            return 0

        lax.fori_loop(0, rows // rc, chunk, 0)

    return pl.pallas_call(
        body, name=name, grid=(f // tc,),
        in_specs=[val(rows), gate(rows), val(3), gate(3), val(1), gate(1)],
        out_specs=val(rows), out_shape=jax.ShapeDtypeStruct((rows, f), ACT_DTYPE),
        compiler_params=_params(("parallel",)),
    )(h, h, cw, cw, cb, cb)


def _conv_swiglu_bwd(name, da, h, cw, cb):
    rows, f2 = h.shape
    f = f2 // 2
    tc = _tile(f, 256)
    rc = _tile(rows, 256, PACK)
    val, gate = _conv_specs(rows, f, tc)

    def body(da_ref, hv_ref, hg_ref, wv_ref, wg_ref, bv_ref, bg_ref,
             dhv_ref, dhg_ref, dwv_ref, dwg_ref, dbv_ref, dbg_ref, sv, sg):
        def first(r, carry):
            r0 = pl.multiple_of(r * rc, rc)
            rs = pl.ds(r0, rc)
            hv, vp, vx, vn = _conv_rows(hv_ref, r0, rc, rows, wv_ref, bv_ref)
            hg, gp, gx, gn = _conv_rows(hg_ref, r0, rc, rows, wg_ref, bg_ref)
            d = da_ref[rs, :].astype(F32)
            s = _sigmoid(hg)
            dv = d * (hg * s)
            dg = d * hv * (s * (1.0 + hg * (1.0 - s)))
            sv[rs, :] = dv
            sg[rs, :] = dg
            sums = [dv * vp, dv * vx, dv * vn, dv, dg * gp, dg * gx, dg * gn, dg]
            return tuple(c + jnp.sum(x, axis=0, keepdims=True) for c, x in zip(carry, sums))

        zero = jnp.zeros((1, tc), F32)
        acc = lax.fori_loop(0, rows // rc, first, (zero,) * 8)
        for k in range(3):
            dwv_ref[k:k + 1, :] = acc[k]
            dwg_ref[k:k + 1, :] = acc[4 + k]
        dbv_ref[...] = acc[3]
        dbg_ref[...] = acc[7]

        def second(r, _):
            r0 = pl.multiple_of(r * rc, rc)
            rs = pl.ds(r0, rc)
            for s_ref, w_ref, o_ref in ((sv, wv_ref, dhv_ref), (sg, wg_ref, dhg_ref)):
                x = s_ref[rs, :]
                xp, xn = _shifted(x, *_edge_rows(s_ref, r0, rc, rows, SUB))
                o_ref[rs, :] = (w_ref[0:1, :] * xn + w_ref[1:2, :] * x + w_ref[2:3, :] * xp).astype(o_ref.dtype)
            return 0

        lax.fori_loop(0, rows // rc, second, 0)

    res = pl.pallas_call(
        body, name=name, grid=(f // tc,),
        in_specs=[val(rows), val(rows), gate(rows), val(3), gate(3), val(1), gate(1)],
        out_specs=[val(rows), val(rows), val(3), val(3), val(1), val(1)],
        out_shape=[jax.ShapeDtypeStruct((rows, f), ACT_DTYPE)] * 2
        + [jax.ShapeDtypeStruct((3, f), F32)] * 2 + [jax.ShapeDtypeStruct((1, f), F32)] * 2,
        scratch_shapes=[pltpu.VMEM((rows, tc), F32), pltpu.VMEM((rows, tc), F32)],
        compiler_params=_params(("parallel",)),
    )(da, h, h, cw, cw, cb, cb)
    return res


def _pool_band(name, x, transpose, out_dtype):
    rows, d = x.shape
    ng = len(POOL_WINDOWS)
    ch = d // ng
    tm = _tile(rows, 256, PACK)
    win = tm + 2 * POOL_HALO
    assert win <= rows

    def body(x_ref, o_ref):
        half = lax.shift_left(jnp.int32(1), pl.program_id(0))
        t0 = pl.program_id(1) * tm
        ws = pl.multiple_of(jnp.clip(t0 - POOL_HALO, 0, rows - win), PACK)
        i = t0 + lax.broadcasted_iota(jnp.int32, (tm, win), 0)
        j = ws + lax.broadcasted_iota(jnp.int32, (tm, win), 1)

        def inv_count(t):
            hi = jnp.minimum(t + half - 1, rows - 1)
            lo = jnp.maximum(t - half, 0)
            return 1.0 / (hi - lo + 1).astype(F32)

        xw = x_ref[pl.ds(ws, win), :]
        xt = x_ref[pl.ds(pl.multiple_of(t0, PACK), tm), :].astype(F32)
        if transpose:
            band = (j - half <= i) & (i <= j + half - 1)
            tw = ws + lax.broadcasted_iota(jnp.int32, (win, 1), 0)
            o = _dot(band.astype(MXU_DTYPE), xw.astype(F32) * inv_count(tw)) - xt
        else:
            band = (i - half <= j) & (j <= i + half - 1)
            tt = t0 + lax.broadcasted_iota(jnp.int32, (tm, 1), 0)
            o = _dot(band.astype(MXU_DTYPE), xw) * inv_count(tt) - xt
        o_ref[...] = o.astype(o_ref.dtype)

    return pl.pallas_call(
        body, name=name, grid=(ng, rows // tm),
        in_specs=[pl.BlockSpec((rows, ch), lambda g, i: (0, g))],
        out_specs=pl.BlockSpec((tm, ch), lambda g, i: (i, g)),
        out_shape=jax.ShapeDtypeStruct((rows, d), out_dtype),
        compiler_params=_params(("parallel", "arbitrary")),
    )(x)


def _ada_forward(c16, ada_w, ada_b):
    nl, d, cols = ada_w.shape
    tn = _tile(cols, 512)

    def body(c_ref, w_ref, b_ref, o_ref):
        cv = c_ref[...]
        o_ref[...] = _dot(cv * _sigmoid(cv), w_ref[...]) + b_ref[...]

    return pl.pallas_call(
        body, name="ada_forward", grid=(nl, cols // tn),
        in_specs=[pl.BlockSpec((16, d), lambda l, n: (0, 0)),
                  pl.BlockSpec((None, d, tn), lambda l, n: (l, 0, n)),
                  pl.BlockSpec((None, 1, tn), lambda l, n: (l, 0, n))],
        out_specs=pl.BlockSpec((None, 16, tn), lambda l, n: (l, 0, n)),
        out_shape=jax.ShapeDtypeStruct((nl, 16, cols), F32),
        compiler_params=_params(("parallel", "parallel")),
    )(c16, ada_w, ada_b)


def _ada_backward(c16, dmod, ada_w):
    nl, d, cols = ada_w.shape
    tn = _tile(cols, 512)
    nn = cols // tn

    def body(c_ref, g_ref, w_ref, gw_ref, gc_ref):
        cv = c_ref[...]
        s = _sigmoid(cv)
        gv = g_ref[...]
        gw_ref[...] = _dot(cv * s, gv, TN)
        dcond = _dot(gv, w_ref[...], NT)
        row = lax.broadcasted_iota(jnp.int32, dcond.shape, 0)
        dctx = jnp.sum(jnp.where(row >= 8, dcond * (s * (1.0 + cv * (1.0 - s))), 0.0), axis=0, keepdims=True)

        @pl.when((pl.program_id(0) == 0) & (pl.program_id(1) == 0))
        def _():
            gc_ref[...] = jnp.zeros_like(gc_ref)

        gc_ref[...] += dctx

    return pl.pallas_call(
        body, name="ada_backward", grid=(nl, nn),
        in_specs=[pl.BlockSpec((16, d), lambda l, n: (0, 0)),
                  pl.BlockSpec((None, 16, tn), lambda l, n: (l, 0, n)),
                  pl.BlockSpec((None, d, tn), lambda l, n: (l, 0, n))],
        out_specs=[pl.BlockSpec((None, d, tn), lambda l, n: (l, 0, n)),
                   pl.BlockSpec((1, d), lambda l, n: (0, 0))],
        out_shape=[jax.ShapeDtypeStruct((nl, d, cols), F32), jax.ShapeDtypeStruct((1, d), F32)],
        compiler_params=_params(("arbitrary", "arbitrary")),
    )(c16, dmod, ada_w)


def _row_sum16(name, a):
    nl, _, w = a.shape
    tn = _tile(w, 4096)

    def body(a_ref, o_ref):
        o_ref[...] = jnp.sum(a_ref[...], axis=0, keepdims=True)

    return pl.pallas_call(
        body, name=name, grid=(nl, w // tn),
        in_specs=[pl.BlockSpec((None, 16, tn), lambda l, n: (l, 0, n))],
        out_specs=pl.BlockSpec((None, 1, tn), lambda l, n: (l, 0, n)),
        out_shape=jax.ShapeDtypeStruct((nl, 1, w), F32),
        compiler_params=_params(("parallel", "parallel")),
    )(a)


def _pack(arrays, row_align):
    flat = jnp.concatenate([a.reshape(-1).astype(F32) for a in arrays])
    quantum = row_align * LANE
    padded = -(-flat.shape[0] // quantum) * quantum
    return jnp.pad(flat, (0, padded - flat.shape[0])).reshape(-1, LANE)


def _unpack(packed, shapes):
    flat = packed.reshape(-1)
    out, off = [], 0
    for s in shapes:
        n = math.prod(s)
        out.append(flat[off:off + n].reshape(s))
        off += n
    return out


def _grid_pos_tables(n_tokens, dim):
    quarter = dim // 4
    omega = 1.0 / (POS_BASE ** (jnp.arange(quarter, dtype=F32) / quarter))

    def enc(n):
        ang = jnp.arange(n, dtype=F32).reshape(-1, 1) * omega[None, :]
        return jnp.concatenate([jnp.sin(ang), jnp.cos(ang)], axis=-1)

    return enc(n_tokens // GRID_W), enc(GRID_W)


def _ffn_forward(tag, v, x_in, up4, dn4, layer, cw, cb):
    h = _mm_cols(f"ffn_up_{tag}", v, up4, layer, ACT_DTYPE)
    a = _conv_swiglu_fwd(f"ffn_conv_{tag}", h, cw, cb)
    f = _mm_rows(f"ffn_down_{tag}", a, dn4, layer, F32)
    return h, a, f


def _ffn_backward(tag, dx_out, fb, x_mid, v, h, a, up4, dn4, layer, cw, cb, gate5, gam3, gam2, scale4p1):
    def post(dxo, f, gam, gate):
        dy, dgate, dgam = _postnorm_bwd(dxo, f.astype(F32), gam, gate)
        return (dy,), (dgate, dgam)

    (df,), (dgate5, dgam3) = _rowwise(f"ffn_post_bwd_{tag}", post, dx_out.shape[0],
                                      [(dx_out, False), (fb, False)], [gam3, gate5],
                                      [(dx_out.shape[1], ACT_DTYPE, False)], [dx_out.shape[1]] * 2)
    da = _mm_rows_nt(f"ffn_down_dx_{tag}", df, dn4, layer, ACT_DTYPE)
    g_dn = _mm_rows_tn(f"ffn_down_dw_{tag}", a, df, dn4.shape[-2], ACT_DTYPE)
    dhv, dhg, dwv, dwg, dbv, dbg = _conv_swiglu_bwd(f"ffn_conv_bwd_{tag}", da, h, cw, cb)
    dh = (dhv, dhg)
    dcw = jnp.concatenate([dwv, dwg], axis=1)
    dcb = jnp.concatenate([dbv, dbg], axis=1)
    dv = _mm_cols_nt(f"ffn_up_dx_{tag}", dh, up4, layer, F32)
    g_up = _mm_cols_tn(f"ffn_up_dw_{tag}", v, dh, up4.shape[-1], ACT_DTYPE)

    def pre(dvv, x, dxo, gam, s1):
        dx, dshift, dscale, dgam = _prenorm_bwd(dvv, x, gam, s1)
        return (dxo + dx,), (dshift, dscale, dgam)

    d = dx_out.shape[1]
    (dx_mid,), (dshift3, dscale4, dgam2) = _rowwise(
        f"ffn_pre_bwd_{tag}", pre, dx_out.shape[0], [(dv, False), (x_mid, False), (dx_out, False)],
        [gam2, scale4p1], [(d, F32, False)], [d] * 3)
    return dx_mid, g_up, g_dn, dcw, dcb, (dshift3, dscale4, dgate5), (dgam2, dgam3)


def kernel(x, c, ctx, c_ctx, ada_w, ada_b, norm_g, s5_lam_re, s5_lam_im, s5_log_step, s5_b_re, s5_b_im, s5_c_re, s5_c_im, s5_d, s5_glu_w, pool_w, pool_scale, ffn_up, ffn_conv, ffn_conv_b, ffn_down, loss_target, m_c_ctx, m_ada_w, m_ada_b, m_norm_g, m_s5_lam_re, m_s5_lam_im, m_s5_log_step, m_s5_b_re, m_s5_b_im, m_s5_c_re, m_s5_c_im, m_s5_d, m_s5_glu_w, m_pool_w, m_pool_scale, m_ffn_up, m_ffn_conv, m_ffn_conv_b, m_ffn_down, v_c_ctx, v_ada_w, v_ada_b, v_norm_g, v_s5_lam_re, v_s5_lam_im, v_s5_log_step, v_s5_b_re, v_s5_b_im, v_s5_c_re, v_s5_c_im, v_s5_d, v_s5_glu_w, v_pool_w, v_pool_scale, v_ffn_up, v_ffn_conv, v_ffn_conv_b, v_ffn_down):
    ix, iy, ic = _coords()
    chip = 2 * ix + iy
    me = 2 * chip + ic
    _, rows, d = x.shape
    rows_c = ctx.shape[1]
    nl = ada_w.shape[0]
    assert nl == 2 and s5_glu_w.shape[0] == 1 and pool_w.shape[0] == 1
    a_cols = ada_w.shape[2]
    f2s = ffn_up.shape[2]
    f2 = N_CHIP * f2s
    ds = d // N_CHIP
    ng = len(POOL_WINDOWS)
    ch = d // ng
    ps = pool_w.shape[2]

    core = jnp.reshape(ic, (1,)).astype(jnp.int32)
    chip_id = jnp.reshape(chip, (1,)).astype(jnp.int32)
    pos = jnp.concatenate([core, chip_id])
    shards = {"up": _view2d(ffn_up), "down": _view2d(ffn_down), "glu": s5_glu_w[0], "pool": _view2d(pool_w[0])}
    gather_bufs = [_cast_own_half(f"cast_{n}", w, pos) for n, w in shards.items()]

    c_all = _all_gather8("gather_cond", _pack([c], SUB))
    c_all = c_all.reshape(N_DEV, -1)[:, :d]
    c16 = jnp.concatenate([c_all, jnp.broadcast_to(c_ctx[None, :], (N_DEV, d))], axis=0)
    ada_b_mine = lax.dynamic_slice_in_dim(ada_b, chip * a_cols, a_cols, axis=1)
    mod_part = _ada_forward(c16, ada_w, ada_b_mine[:, None, :])
    narrow_shapes = [mod_part.shape, norm_g.shape, pool_scale.shape, ffn_conv.shape]
    narrow = _all_gather8("gather_narrow", _pack([mod_part, norm_g, pool_scale, ffn_conv], SUB))
    per_chip = [_unpack(narrow[2 * s], narrow_shapes) for s in range(N_CHIP)]
    mod_all = jnp.concatenate([p[0] for p in per_chip], axis=-1)
    gam = jnp.concatenate([p[1] for p in per_chip], axis=-1)
    pscale = jnp.concatenate([p[2] for p in per_chip], axis=-1)
    conv_w = jnp.concatenate([p[3] for p in per_chip], axis=-1)
    mod_mine = lax.dynamic_index_in_dim(mod_all, me, axis=1, keepdims=False)
    mod_ctx = mod_all[0, N_DEV]

    def mods(vec):
        s0, s1, g2, s3, s4, g5 = [vec[k * d:(k + 1) * d][None, :] for k in range(N_MOD)]
        return s0, 1.0 + s1, g2, s3, 1.0 + s4, g5

    m0, m1, mc = mods(mod_mine[0]), mods(mod_mine[1]), mods(mod_ctx)
    gains = [[gam[l, k][None, :] for k in range(4)] for l in range(nl)]
    conv_b = ffn_conv_b[:, None, :]

    row_tab, col_tab = _grid_pos_tables(rows, d)
    per_tile = _tile(rows // N_SEG, ROWWISE_ROWS, SUB) // GRID_W
    rep = SUB // per_tile
    row_tab = jnp.repeat(row_tab, rep, axis=0)

    def init(xv, rt, ct, g0, shift, s1):
        pe_r = jnp.concatenate(
            [jnp.broadcast_to(rt[q * rep:q * rep + 1], (GRID_W, d // 2)) for q in range(per_tile)], axis=0)
        pe_c = jnp.concatenate([ct] * per_tile, axis=0)
        x0 = xv + jnp.concatenate([pe_r, pe_c], axis=1)
        return (x0, _prenorm(x0, g0, shift, s1)), ()

    (x0, u), _ = _rowwise("init", init, rows, [(x[0], False)],
                          [(row_tab, SUB), col_tab, gains[0][0], m0[0], m0[1]],
                          [(d, F32, False), (d, ACT_DTYPE, True)])
    (uc,), _ = _rowwise("ctx_prenorm", lambda cv, g0, shift, s1: ((_prenorm(cv, g0, shift, s1),), ()),
                        rows_c, [(ctx[0], False)], [gains[0][0], mc[0], mc[1]], [(d, ACT_DTYPE, True)])
    s5_params = (s5_lam_re[0], s5_lam_im[0], s5_log_step[0], s5_b_re[0], s5_b_im[0], s5_c_re[0], s5_c_im[0])
    (bblk, cblk, atile), s5_vjp = jax.vjp(_s5_prepare, *s5_params)
    buf_up, buf_dn, buf_glu, buf_pool = gather_bufs
    rows_of = {n: w.shape[0] for n, w in shards.items()}
    y, (buf_up, buf_glu) = _s5_forward(u, uc, bblk, cblk, atile, s5_d, [buf_up, buf_glu],
                                       _gather_ici_plan([rows_of["up"], rows_of["glu"]]))
    up4 = _swap_gathered("gather_up", buf_up, core, rows_of["up"]).reshape((N_CHIP,) + ffn_up.shape)
    glu4 = _swap_gathered("gather_glu", buf_glu, core, rows_of["glu"])
    (z,), _ = _rowwise("gelu", lambda yv: ((_gelu(yv),), ()), rows, [(y, True)], [], [(d, ACT_DTYPE, False)])
    zz = _mm_cols("glu_proj", z, glu4[:, None], 0, ACT_DTYPE)

    def glu_out(zzv, xv, gate2, g1, g2, shift3, s4):
        zf = zzv.astype(F32)
        o = zf[:, :d] * _sigmoid(zf[:, d:])
        x1 = xv + gate2 * (o * _rstd(o) * g1)
        return (x1, _prenorm(x1, g2, shift3, s4)), ()

    (x1, v0), _ = _rowwise("glu_resid", glu_out, rows, [(zz, False), (x0, False)],
                           [m0[2], gains[0][1], gains[0][2], m0[3], m0[4]], [(d, F32, False), (d, ACT_DTYPE, False)])
    h0, (buf_dn, buf_pool) = _mm_cols(
        "ffn_up_l0", v0, up4, 0, ACT_DTYPE,
        ride=([buf_dn, buf_pool], _gather_ici_plan([rows_of["down"], rows_of["pool"]])))
    dn4 = _swap_gathered("gather_down", buf_dn, core, rows_of["down"]).reshape((N_CHIP,) + ffn_down.shape)
    pool_full = _swap_gathered("gather_pool", buf_pool, core, rows_of["pool"])
    pool_full = pool_full.reshape(N_CHIP, ng, ps, ch).transpose(1, 0, 2, 3).reshape(ng, ch, ch)
    a0 = _conv_swiglu_fwd("ffn_conv_l0", h0, conv_w[0], conv_b[0])
    f0 = _mm_rows("ffn_down_l0", a0, dn4, 0, F32)

    def ffn_out(fv, xv, gate5, g3, g0n, shift0, s1):
        x2 = xv + gate5 * (fv * _rstd(fv) * g3)
        return (x2, _prenorm(x2, g0n, shift0, s1), fv), ()

    (x2, u1, fb0), _ = _rowwise("ffn_resid_l0", ffn_out, rows, [(f0, False), (x1, False)],
                                [m0[5], gains[0][3], gains[1][0], m1[0], m1[1]],
                                [(d, F32, False), (d, ACT_DTYPE, False), (d, ACT_DTYPE, False)])

    p1 = _pool_band("pool_band", u1, False, ACT_DTYPE)
    yr = _mm_grp("pool_proj", p1, pool_full, NN, F32)

    def pool_out(yv, xv, ps_, gate2, g1, g2, shift3, s4):
        o = yv * ps_
        x1n = xv + gate2 * (o * _rstd(o) * g1)
        return (x1n, _prenorm(x1n, g2, shift3, s4), yv), ()

    (x3, v1, yb), _ = _rowwise("pool_resid", pool_out, rows, [(yr, False), (x2, False)],
                               [pscale, m1[2], gains[1][1], gains[1][2], m1[3], m1[4]],
                               [(d, F32, False), (d, ACT_DTYPE, False), (d, ACT_DTYPE, False)])
    h1, a1, f1 = _ffn_forward("l1", v1, x3, up4, dn4, 1, conv_w[1], conv_b[1])

    def loss_head(fv, xv, tv, gate5, g3):
        err = xv + gate5 * (fv * _rstd(fv) * g3) - tv
        return (err * (1.0 / d), fv), (err * err,)

    (dx4, fb1), (sq,) = _rowwise("loss_head", loss_head, rows, [(f1, False), (x3, False), (loss_target[0], False)],
                                 [m1[5], gains[1][3]], [(d, F32, False), (d, ACT_DTYPE, False)], [d])
    loss = lax.psum(0.5 * jnp.sum(sq) / d, ("x", "y", "c"))

    dx3, g_up1, g_dn1, dcw1, dcb1, dmod_ffn1, (dgam12, dgam13) = _ffn_backward(
        "l1", dx4, fb1, x3, v1, h1, a1, up4, dn4, 1, conv_w[1], conv_b[1], m1[5], gains[1][3], gains[1][2], m1[4])

    def pool_post(dxo, yv, ps_, g1, gate2):
        yraw = yv.astype(F32)
        dy, dgate, dgam = _postnorm_bwd(dxo, yraw * ps_, g1, gate2)
        return (dy * ps_,), (dgate, dgam, dy * yraw)

    (dyr,), (dgate2_1, dgam11, dpscale) = _rowwise("pool_post_bwd", pool_post, rows, [(dx3, False), (yb, False)],
                                                   [pscale, gains[1][1], m1[2]], [(d, ACT_DTYPE, False)], [d] * 3)
    dp1 = _mm_grp("pool_proj_dx", dyr, pool_full, NT, ACT_DTYPE)
    g_pool = _mm_grp_tn("pool_proj_dw", p1, dyr, ng, ACT_DTYPE)
    du1 = _pool_band("pool_band_bwd", dp1, True, F32)

    def pre_bwd(duv, xv, dxo, g0, s1):
        dx, dshift, dscale, dgam = _prenorm_bwd(duv, xv, g0, s1)
        return (dxo + dx,), (dshift, dscale, dgam)

    (dx2,), (dshift0_1, dscale1_1, dgam10) = _rowwise(
        "pool_pre_bwd", pre_bwd, rows, [(du1, False), (x2, False), (dx3, False)],
        [gains[1][0], m1[1]], [(d, F32, False)], [d] * 3)

    dx1, g_up0, g_dn0, dcw0, dcb0, dmod_ffn0, (dgam02, dgam03) = _ffn_backward(
        "l0", dx2, fb0, x1, v0, h0, a0, up4, dn4, 0, conv_w[0], conv_b[0], m0[5], gains[0][3], gains[0][2], m0[4])

    def glu_post(dxo, zzv, g1, gate2):
        zf = zzv.astype(F32)
        val, s = zf[:, :d], _sigmoid(zf[:, d:])
        do, dgate, dgam = _postnorm_bwd(dxo, val * s, g1, gate2)
        return (jnp.concatenate([do * s, do * val * (s * (1.0 - s))], axis=1),), (dgate, dgam)

    (dzz,), (dgate2_0, dgam01) = _rowwise("glu_post_bwd", glu_post, rows, [(dx1, False), (zz, False)],
                                          [gains[0][1], m0[2]], [(2 * d, ACT_DTYPE, False)], [d] * 2)
    dz = _mm_cols_nt("glu_proj_dx", dzz, glu4[:, None], 0, F32)
    g_glu = _mm_cols_tn("glu_proj_dw", z, dzz, glu4.shape[-1], ACT_DTYPE)
    (dy,), _ = _rowwise("gelu_bwd", lambda dzv, yv: ((dzv * _gelu_grad(yv),), ()), rows,
                        [(dz, False), (y, True)], [], [(d, ACT_DTYPE, True)])
    g_pool4 = g_pool.reshape(ng, N_CHIP, ps, ch).transpose(1, 0, 2, 3).reshape(N_CHIP, ng * ps, ch)
    big = {"up0": g_up0, "up1": g_up1, "dn0": g_dn0, "dn1": g_dn1, "glu": g_glu, "pool": g_pool4}
    pairs = [_pair_sum(f"pair_{n}", g, core) for n, g in big.items()]
    ride_shapes, ride_copies = _chip_exchange_plan(pairs)
    (du0, duc, d_bblk, d_cblk, d_atile, d_dsk), others = _s5_backward(
        u, uc, dy, bblk, cblk, atile, s5_d, pairs, ride_shapes, ride_copies)
    (gx,), (dshift0_0, dscale1_0, dgam00) = _rowwise(
        "s5_pre_bwd", pre_bwd, rows, [(du0, True), (x0, False), (dx1, False)],
        [gains[0][0], m0[1]], [(d, F32, False)], [d] * 3)

    def ctx_bwd(duv, cv, g0, s1):
        _, dshift, dscale, dgam = _prenorm_bwd(duv, cv, g0, s1)
        return (), (dshift, dscale, dgam)

    _, (dshift_c, dscale_c, dgam00c) = _rowwise("ctx_pre_bwd", ctx_bwd, rows_c, [(duc, True), (ctx[0], False)],
                                                [gains[0][0], mc[1]], [], [d] * 3)
    g_s5 = s5_vjp((d_bblk, d_cblk, d_atile))

    zero_d = jnp.zeros((1, d), F32)
    dmod_lat = jnp.stack([
        jnp.concatenate([dshift0_0, dscale1_0, dgate2_0, *dmod_ffn0], axis=1),
        jnp.concatenate([dshift0_1, dscale1_1, dgate2_1, *dmod_ffn1], axis=1)])
    dmod_ctx = jnp.stack([jnp.concatenate([dshift_c, dscale_c] + [zero_d] * 4, axis=1),
                          jnp.zeros((1, N_MOD * d), F32)])
    dmod_shape = (nl, 2, N_MOD * d)
    dmod_all = _all_gather8("gather_dmod", _pack([jnp.concatenate([dmod_lat, dmod_ctx], axis=1)], SUB))
    dmod_all = jnp.stack([_unpack(dmod_all[k], [dmod_shape])[0] for k in range(N_DEV)])
    dmod16 = jnp.concatenate([dmod_all[:, :, 0], dmod_all[:, :, 1]], axis=0).transpose(1, 0, 2)
    dmod16_mine = lax.dynamic_slice_in_dim(dmod16, chip * a_cols, a_cols, axis=2)
    g_ada_w, g_cctx_part = _ada_backward(c16, dmod16_mine, ada_w)
    g_ada_b = _row_sum16("ada_bias_grad", dmod16)[:, 0]

    d_gam = jnp.stack([jnp.concatenate([dgam00 + dgam00c, dgam01, dgam02, dgam03], axis=0),
                       jnp.concatenate([dgam10, dgam11, dgam12, dgam13], axis=0)])
    small = [d_gam, 0.5 * g_cctx_part, *g_s5, jnp.sum(d_dsk, axis=0, keepdims=True), dpscale,
             jnp.stack([dcw0, dcw1]), jnp.stack([dcb0[0], dcb1[0]])]
    small_shapes = [s.shape for s in small]
    packed = _pack(small, N_DEV * SUB)
    summed = _all_reduce8("reduce_small", packed.reshape(N_DEV, -1, LANE)).reshape(-1, LANE)
    (r_gam, r_cctx, r_lam_re, r_lam_im, r_log_step, r_b_re, r_b_im, r_c_re, r_c_im,
     r_dsk, r_pscale, r_conv, r_convb) = _unpack(summed, small_shapes)
    g_norm = lax.dynamic_slice_in_dim(r_gam, chip * ds, ds, axis=2)
    g_pscale = lax.dynamic_slice_in_dim(r_pscale, chip * ds, ds, axis=1)
    g_conv = lax.dynamic_slice_in_dim(r_conv, chip * f2s, f2s, axis=2)

    parts = dict(zip(big, zip(pairs, others)))
    d_rows, dn_rows = ffn_up.shape[1], ffn_down.shape[1]
    g_up = _sum_and_swap("reduce_up0", *parts["up0"], pos, None, nl * d_rows, 0)
    g_up = _sum_and_swap("reduce_up1", *parts["up1"], pos, g_up, nl * d_rows, d_rows).reshape(ffn_up.shape)
    g_dn = _sum_and_swap("reduce_dn0", *parts["dn0"], pos, None, nl * dn_rows, 0)
    g_dn = _sum_and_swap("reduce_dn1", *parts["dn1"], pos, g_dn, nl * dn_rows, dn_rows).reshape(ffn_down.shape)
    g_glu_f = _sum_and_swap("reduce_glu", *parts["glu"], pos, None, d, 0)
    g_pool_f = _sum_and_swap("reduce_pool", *parts["pool"], pos, None, ng * ps, 0)

    grads = {
        "c_ctx": r_cctx[0], "ada_w": g_ada_w, "ada_b": g_ada_b, "norm_g": g_norm,
        "s5_lam_re": r_lam_re[None], "s5_lam_im": r_lam_im[None], "s5_log_step": r_log_step[None],
        "s5_b_re": r_b_re[None], "s5_b_im": r_b_im[None], "s5_c_re": r_c_re[None], "s5_c_im": r_c_im[None],
        "s5_d": r_dsk, "s5_glu_w": g_glu_f[None], "pool_w": g_pool_f.reshape(pool_w.shape),
        "pool_scale": g_pscale, "ffn_up": g_up, "ffn_conv": g_conv, "ffn_conv_b": r_convb, "ffn_down": g_dn,
    }
    weights = {
        "c_ctx": (c_ctx, m_c_ctx, v_c_ctx), "ada_w": (ada_w, m_ada_w, v_ada_w), "ada_b": (ada_b, m_ada_b, v_ada_b),
        "norm_g": (norm_g, m_norm_g, v_norm_g), "s5_lam_re": (s5_lam_re, m_s5_lam_re, v_s5_lam_re),
        "s5_lam_im": (s5_lam_im, m_s5_lam_im, v_s5_lam_im), "s5_log_step": (s5_log_step, m_s5_log_step, v_s5_log_step),
        "s5_b_re": (s5_b_re, m_s5_b_re, v_s5_b_re), "s5_b_im": (s5_b_im, m_s5_b_im, v_s5_b_im),
        "s5_c_re": (s5_c_re, m_s5_c_re, v_s5_c_re), "s5_c_im": (s5_c_im, m_s5_c_im, v_s5_c_im),
        "s5_d": (s5_d, m_s5_d, v_s5_d), "s5_glu_w": (s5_glu_w, m_s5_glu_w, v_s5_glu_w),
        "pool_w": (pool_w, m_pool_w, v_pool_w), "pool_scale": (pool_scale, m_pool_scale, v_pool_scale),
        "ffn_up": (ffn_up, m_ffn_up, v_ffn_up), "ffn_conv": (ffn_conv, m_ffn_conv, v_ffn_conv),
        "ffn_conv_b": (ffn_conv_b, m_ffn_conv_b, v_ffn_conv_b), "ffn_down": (ffn_down, m_ffn_down, v_ffn_down),
    }
    names = list(weights)
    large = ("ada_w", "s5_glu_w", "pool_w", "ffn_up", "ffn_down")
    delta, new_m, new_v = {}, {}, {}
    for n in names:
        w, m, v = weights[n]
        if n == "ada_w":
            delta[n], new_m[n], new_v[n] = _adamw(f"adamw_{n}", w, grads[n], m, v)
        elif n in large:
            delta[n], new_m[n], new_v[n], grads[n] = _adamw(f"adamw_{n}", w, grads[n], m, v, emit_grad=True)
        else:
            shape = w.shape
            view = (1, shape[0]) if w.ndim == 1 else shape
            res = _adamw(f"adamw_{n}", *[t.reshape(view) for t in (w, grads[n], m, v)])
            delta[n], new_m[n], new_v[n] = [t.reshape(shape) for t in res]

    return (loss, gx[None], *[grads[n] for n in names], *[delta[n] for n in names],
            *[new_m[n] for n in names], *[new_v[n] for n in names])
```
